```python
import jax
import jax.numpy as jnp
from jax import lax
import numpy as np

D_MODEL = 2048
BATCH = 4
SEQ = 2048
DEPTH = 2
DEC_BATCH = 128
DEC_SEQ = 1
PAST_LEN = 2048
PAGE_SIZE = 128

HEAD_DIM = 128
N_A_LAYERS = DEPTH // 2
N_B_LAYERS = DEPTH - N_A_LAYERS
A_HEADS = 12
A_DK = HEAD_DIM
A_DV = HEAD_DIM
A_CHUNK = 64
A_MIX = A_HEADS * (2 * A_DK + 2 * A_DV)
B_PATTERNS = ((128, 1), (512, 4), (2048, 16))
B_GROUPS = len(B_PATTERNS)
B_SLOTS = 4
B_QHEADS = B_GROUPS * B_SLOTS
B_BLOCK = 128
MAX_WINDOW = max(w for w, _ in B_PATTERNS)
KV_W = B_SLOTS * HEAD_DIM
MEM_LEN = 256
MEM_HEADS = 4
MEM_W = MEM_HEADS * HEAD_DIM
A_IN = A_MIX + MEM_W
A_OUT = A_HEADS * A_DV + MEM_W
B_IN = B_QHEADS * HEAD_DIM + MEM_W
B_OUT = KV_W + MEM_W
ROPE_THETA = 500000.0
ROPE_DIM = HEAD_DIM // 4
N_EXPERTS = 64
N_GROUPS = 8
TOPK_GROUPS = 4
TOP_K = 8
EXPERT_FF = 512
SHARED_FF = 512
ROUTED_SCALE = 2.5
MOE_BLOCK_MAX = 128
MOE_BLOCK_MIN = 8
DEEPNORM_ALPHA = (2 * DEPTH) ** 0.25
DEEPNORM_BETA = (8 * DEPTH) ** -0.25
LN_EPS = 1e-5
RMS_EPS = 1e-6
ATTN_SCALE = HEAD_DIM ** -0.5

kernel_name = 'yoco_hgrn2_dilated_moe_decode_step'


def layer_norm(x, g, b):
    xf = x.astype(jnp.float32)
    mu = jnp.mean(xf, -1, keepdims=True)
    var = jnp.mean(jnp.square(xf - mu), -1, keepdims=True)
    return ((xf - mu) * lax.rsqrt(var + LN_EPS) * g + b).astype(x.dtype)


def post_norm(x, y, g, b):
    return layer_norm(DEEPNORM_ALPHA * x + y.astype(x.dtype), g, b)


def rope_partial(x, pos):
    half = ROPE_DIM // 2
    inv_freq = ROPE_THETA ** (-jnp.arange(0, ROPE_DIM, 2, dtype=jnp.float32) / ROPE_DIM)
    ang = pos.astype(jnp.float32)[:, None] * inv_freq[None, :]
    cos, sin = jnp.cos(ang)[:, None, :], jnp.sin(ang)[:, None, :]
    xr = x[..., :ROPE_DIM].astype(jnp.float32)
    x1, x2 = xr[..., :half], xr[..., half:]
    rot = jnp.concatenate([x1 * cos - x2 * sin, x2 * cos + x1 * sin], -1)
    return jnp.concatenate([rot.astype(x.dtype), x[..., ROPE_DIM:]], -1)


def memory_kv(mem, w):
    b, m, _ = mem.shape
    kv = mem @ w
    return (kv[..., :MEM_W].reshape(b, m, MEM_HEADS, HEAD_DIM),
            kv[..., MEM_W:].reshape(b, m, MEM_HEADS, HEAD_DIM))


def memory_attention(q, mk, mv):
    s = jnp.einsum('bthd,bmhd->bhtm', q, mk).astype(jnp.float32) * ATTN_SCALE
    p = jax.nn.softmax(s, axis=-1).astype(mv.dtype)
    return jnp.einsum('bhtm,bmhd->bthd', p, mv)


def hgrn2_scan(q, k, v, logf, s0):
    b, t, h, _ = q.shape
    c = min(A_CHUNK, t)
    n = -(-t // c)
    pad = n * c - t

    def prep(a):
        a = jnp.pad(a, ((0, 0), (0, pad), (0, 0), (0, 0)))
        return a.reshape(b, n, c, h, a.shape[-1]).transpose(1, 0, 3, 2, 4)

    causal = jnp.tril(jnp.ones((c, c), bool))

    def step(s, inp):
        qc, kc, vc, gc = inp
        g = jnp.cumsum(gc, axis=2)
        diff = g[:, :, :, None, :] - g[:, :, None, :, :]
        decay = jnp.exp(jnp.where(causal[:, :, None], diff, -jnp.inf))
        attn = jnp.einsum('bhtsd,bhsd->bhts', decay * qc[:, :, :, None, :], kc)
        o = jnp.einsum('bhts,bhsv->bhtv', attn, vc) + jnp.einsum('bhtd,bhdv->bhtv', qc * jnp.exp(g), s)
        g_end = g[:, :, -1:, :]
        s_new = jnp.exp(g_end[:, :, 0, :, None]) * s + jnp.einsum('bhsd,bhsv->bhdv', kc * jnp.exp(g_end - g), vc)
        return s_new, o

    s_fin, o = lax.scan(step, s0, (prep(q), prep(k), prep(v), prep(logf)))
    o = o.transpose(1, 0, 3, 2, 4).reshape(b, n * c, h, -1)[:, :t]
    return o, s_fin


def hgrn2_branch(proj, lb, g_norm, s0):
    b, t, _ = proj.shape
    hk, hv = A_HEADS * A_DK, A_HEADS * A_DV
    q = jax.nn.silu(proj[..., :hk].astype(jnp.float32)).reshape(b, t, A_HEADS, A_DK)
    forget = lb + (1.0 - lb) * jax.nn.sigmoid(proj[..., hk:2 * hk].astype(jnp.float32))
    logf = jnp.log(forget).reshape(b, t, A_HEADS, A_DK)
    k = (1.0 - forget).reshape(b, t, A_HEADS, A_DK)
    v = proj[..., 2 * hk:2 * hk + hv].astype(jnp.float32).reshape(b, t, A_HEADS, A_DV)
    gate = proj[..., 2 * hk + hv:].astype(jnp.float32)
    o, s_fin = hgrn2_scan(q, k, v, logf, s0.astype(jnp.float32))
    o = o * lax.rsqrt(jnp.mean(o * o, -1, keepdims=True) + RMS_EPS) * g_norm.astype(jnp.float32)
    o = o.reshape(b, t, hv) * jax.nn.silu(gate)
    return o, s_fin


def hgrn2_layer_mixer(x, mem_k, mem_v, w_in, lb, g_norm, w_out, s0):
    b, t, _ = x.shape
    proj = x @ w_in
    o_h, s_fin = hgrn2_branch(proj[..., :A_MIX], lb, g_norm, s0)
    qm = proj[..., A_MIX:].reshape(b, t, MEM_HEADS, HEAD_DIM)
    o_m = memory_attention(qm, mem_k, mem_v).reshape(b, t, MEM_W)
    y = jnp.concatenate([o_h.astype(x.dtype), o_m.astype(x.dtype)], -1) @ w_out
    return y, s_fin


def shared_kv(h, pos, w_kv):
    b, t, _ = h.shape
    kv = h @ w_kv
    k = rope_partial(kv[..., :KV_W].reshape(b, t, B_SLOTS, HEAD_DIM), pos)
    v = kv[..., KV_W:].reshape(b, t, B_SLOTS, HEAD_DIM)
    return k, v


def dilated_prompt(q, k, v, dil, span):
    b, s, h, d = q.shape
    L = s // dil
    nb = -(-L // B_BLOCK)
    lp = nb * B_BLOCK

    def streams(a):
        a = a.reshape(b, L, dil, h, d).transpose(0, 2, 1, 3, 4).reshape(b * dil, L, h, d)
        a = jnp.pad(a, ((0, 0), (0, lp - L), (0, 0), (0, 0)))
        return a.reshape(b * dil, nb, B_BLOCK, h, d)

    def band(a):
        prev = jnp.pad(a, ((0, 0), (1, 0), (0, 0), (0, 0), (0, 0)))[:, :-1]
        return jnp.concatenate([prev, a], axis=2)

    qb = streams(q)
    kband, vband = band(streams(k)), band(streams(v))
    qi = jnp.arange(B_BLOCK)[:, None]
    ki = jnp.arange(2 * B_BLOCK)[None, :]
    dist = qi + B_BLOCK - ki
    key_u = (jnp.arange(nb) * B_BLOCK)[:, None, None] - B_BLOCK + ki[None]
    mask = ((dist >= 0) & (dist <= span))[None] & (key_u >= 0)
    sc = jnp.einsum('znahd,znchd->znhac', qb, kband).astype(jnp.float32) * ATTN_SCALE
    sc = jnp.where(mask[None, :, None], sc, -jnp.inf)
    m = jnp.max(sc, -1)
    p = jnp.exp(sc - m[..., None])
    l = jnp.sum(p, -1)
    o = jnp.einsum('znhac,znchd->znahd', p.astype(v.dtype), vband).astype(jnp.float32)
    m, l = m.transpose(0, 1, 3, 2), l.transpose(0, 1, 3, 2)
    o = o / l[..., None]

    def positions(a):
        rest = a.shape[4:]
        a = a.reshape(b, dil, lp, h, *rest)[:, :, :L]
        return jnp.moveaxis(a, 1, 2).reshape(b, s, h, *rest)

    return positions(m), positions(l), positions(o)


def dilated_decode(q, k_all, v_all, dil, span, q_rows):
    r = k_all.shape[1]
    rows = q_rows[:, None] - jnp.arange(span + 1)[None, :] * dil
    valid = rows >= 0
    idx = jnp.clip(rows, 0, r - 1)
    kg, vg = k_all[:, idx], v_all[:, idx]
    sc = jnp.einsum('bthd,btjhd->bthj', q, kg).astype(jnp.float32) * ATTN_SCALE
    sc = jnp.where(valid[None, :, None, :], sc, -jnp.inf)
    m = jnp.max(sc, -1)
    p = jnp.exp(sc - m[..., None])
    l = jnp.sum(p, -1)
    o = jnp.einsum('bthj,btjhd->bthd', p.astype(v_all.dtype), vg).astype(jnp.float32) / l[..., None]
    return m, l, o


def combine_dilations(parts):
    m = jnp.stack([pt[0] for pt in parts])
    l = jnp.stack([pt[1] for pt in parts])
    o = jnp.stack([pt[2] for pt in parts])
    w = l * jnp.exp(m - jnp.max(m, 0))
    return jnp.sum(w[..., None] * o, 0) / jnp.sum(w, 0)[..., None]


def dilated_layer_mixer(x, pos, mem_k, mem_v, w_in, w_out, attend):
    b, t, _ = x.shape
    proj = x @ w_in
    qb = rope_partial(proj[..., :B_QHEADS * HEAD_DIM].reshape(b, t, B_QHEADS, HEAD_DIM), pos)
    qb = qb.reshape(b, t, B_GROUPS, B_SLOTS, HEAD_DIM)
    parts = [attend(qb[:, :, g], dil, win // dil) for g, (win, dil) in enumerate(B_PATTERNS)]
    o_b = combine_dilations(parts).reshape(b, t, KV_W).astype(x.dtype)
    qm = proj[..., B_QHEADS * HEAD_DIM:].reshape(b, t, MEM_HEADS, HEAD_DIM)
    o_m = memory_attention(qm, mem_k, mem_v).reshape(b, t, MEM_W).astype(x.dtype)
    return jnp.concatenate([o_b, o_m], -1) @ w_out


def routed_experts(xf, e_idx, gate, w_gate, w_up, w_down):
    n, k = e_idx.shape
    a = n * k
    avg = -(-a // N_EXPERTS)
    blk = min(MOE_BLOCK_MAX, max(MOE_BLOCK_MIN, 1 << (avg - 1).bit_length()))
    n_blocks = -(-a // blk) + N_EXPERTS
    flat_e = e_idx.reshape(-1)
    order = jnp.argsort(flat_e)
    sorted_e = flat_e[order]
    counts = jnp.bincount(flat_e, length=N_EXPERTS)
    starts = jnp.cumsum(counts) - counts
    padded = (counts + blk - 1) // blk * blk
    pad_end = jnp.cumsum(padded)
    pad_start = pad_end - padded
    dest = pad_start[sorted_e] + jnp.arange(a, dtype=jnp.int32) - starts[sorted_e]
    slot_tok = jnp.full((n_blocks * blk,), n, jnp.int32).at[dest].set((order // k).astype(jnp.int32))
    slot_gate = jnp.zeros((n_blocks * blk,), jnp.float32).at[dest].set(gate.reshape(-1)[order].astype(jnp.float32))
    blk_exp = jnp.minimum(jnp.searchsorted(pad_end, jnp.arange(n_blocks, dtype=jnp.int32) * blk, side='right'), N_EXPERTS - 1)
    x_pad = jnp.concatenate([xf, jnp.zeros((1, xf.shape[1]), xf.dtype)], 0)

    def body(acc, inp):
        tok, gw, e = inp
        xb = x_pad[tok]
        hid = jax.nn.silu(xb @ w_gate[e]) * (xb @ w_up[e])
        y = (hid @ w_down[e]).astype(jnp.float32) * gw[:, None]
        return acc.at[tok].add(y), None

    acc, _ = lax.scan(body, jnp.zeros((n + 1, xf.shape[1]), jnp.float32),
                      (slot_tok.reshape(n_blocks, blk), slot_gate.reshape(n_blocks, blk), blk_exp))
    return acc[:n]


def moe_ffn(x, router_w, router_b, w_gate, w_up, w_down, sw_gate, sw_up, sw_down):
    shape = x.shape
    xf = x.reshape(-1, shape[-1])
    n = xf.shape[0]
    scores = jax.nn.sigmoid((xf @ router_w).astype(jnp.float32))
    biased = scores + router_b.astype(jnp.float32)
    grp = lax.top_k(biased.reshape(n, N_GROUPS, N_EXPERTS // N_GROUPS), 2)[0].sum(-1)
    _, g_idx = lax.top_k(grp, TOPK_GROUPS)
    g_mask = jax.nn.one_hot(g_idx, N_GROUPS, dtype=jnp.float32).sum(1) > 0
    e_mask = jnp.repeat(g_mask, N_EXPERTS // N_GROUPS, axis=1)
    _, e_idx = lax.top_k(jnp.where(e_mask, biased, -jnp.inf), TOP_K)
    gate = jnp.take_along_axis(scores, e_idx, axis=1)
    gate = gate / jnp.sum(gate, -1, keepdims=True) * ROUTED_SCALE
    routed = routed_experts(xf, e_idx, gate, w_gate, w_up, w_down)
    shared = (jax.nn.silu(xf @ sw_gate) * (xf @ sw_up)) @ sw_down
    return (routed + shared.astype(jnp.float32)).astype(x.dtype).reshape(shape)


def setup_inputs(seed: int = 0) -> dict:
    key = jax.random.key(seed)
    ks = jax.random.split(key, 26)
    w_buf = min(MAX_WINDOW, PAST_LEN)
    beta = DEEPNORM_BETA

    def nrm(kk, shape, scale):
        return jax.random.normal(kk, shape, jnp.float32) * scale

    return {
        'x_prompt': nrm(ks[0], (BATCH, SEQ, D_MODEL), 1.0),
        'x_sample': nrm(ks[1], (DEC_BATCH, DEC_SEQ, D_MODEL), 1.0),
        'state_hgrn': nrm(ks[2], (N_A_LAYERS, DEC_BATCH, A_HEADS, A_DK, A_DV), 0.3),
        'cache_win_k': nrm(ks[3], (DEC_BATCH, w_buf, B_SLOTS, HEAD_DIM), 1.0),
        'cache_win_v': nrm(ks[4], (DEC_BATCH, w_buf, B_SLOTS, HEAD_DIM), 1.0),
        'cache_mem_k': nrm(ks[5], (DEPTH, DEC_BATCH, MEM_LEN, MEM_HEADS, HEAD_DIM), 1.0),
        'cache_mem_v': nrm(ks[6], (DEPTH, DEC_BATCH, MEM_LEN, MEM_HEADS, HEAD_DIM), 1.0),
        'mem_prompt': nrm(ks[7], (BATCH, MEM_LEN, D_MODEL), 1.0),
        'w_in_a': nrm(ks[8], (N_A_LAYERS, D_MODEL, A_IN), D_MODEL ** -0.5),
        'lb_logits': nrm(ks[9], (N_A_LAYERS + 1, A_HEADS * A_DK), 0.5),
        'g_norm_a': 1.0 + nrm(ks[10], (N_A_LAYERS, A_DV), 0.02),
        'w_out_a': nrm(ks[11], (N_A_LAYERS, A_OUT, D_MODEL), A_OUT ** -0.5 * beta),
        'w_in_b': nrm(ks[12], (N_B_LAYERS, D_MODEL, B_IN), D_MODEL ** -0.5),
        'w_out_b': nrm(ks[13], (N_B_LAYERS, B_OUT, D_MODEL), B_OUT ** -0.5 * beta),
        'w_kv_shared': nrm(ks[14], (D_MODEL, 2 * KV_W), D_MODEL ** -0.5),
        'w_mem_kv': nrm(ks[15], (DEPTH, D_MODEL, 2 * MEM_W), D_MODEL ** -0.5),
        'ln_g': 1.0 + nrm(ks[16], (DEPTH, 2, D_MODEL), 0.02),
        'ln_b': nrm(ks[17], (DEPTH, 2, D_MODEL), 0.02),
        'router_w': nrm(ks[18], (DEPTH, D_MODEL, N_EXPERTS), D_MODEL ** -0.5),
        'router_b': nrm(ks[19], (DEPTH, N_EXPERTS), 0.01),
        'exp_w_gate': nrm(ks[20], (DEPTH, N_EXPERTS, D_MODEL, EXPERT_FF), D_MODEL ** -0.5),
        'exp_w_up': nrm(ks[21], (DEPTH, N_EXPERTS, D_MODEL, EXPERT_FF), D_MODEL ** -0.5),
        'exp_w_down': nrm(ks[22], (DEPTH, N_EXPERTS, EXPERT_FF, D_MODEL), EXPERT_FF ** -0.5 * beta),
        'sh_w_gate': nrm(ks[23], (DEPTH, D_MODEL, SHARED_FF), D_MODEL ** -0.5),
        'sh_w_up': nrm(ks[24], (DEPTH, D_MODEL, SHARED_FF), D_MODEL ** -0.5),
        'sh_w_down': nrm(ks[25], (DEPTH, SHARED_FF, D_MODEL), SHARED_FF ** -0.5 * beta),
    }


def reference(x_prompt, x_sample, state_hgrn, cache_win_k, cache_win_v, cache_mem_k, cache_mem_v, mem_prompt,
              w_in_a, lb_logits, g_norm_a, w_out_a, w_in_b, w_out_b, w_kv_shared, w_mem_kv, ln_g, ln_b,
              router_w, router_b, exp_w_gate, exp_w_up, exp_w_down, sh_w_gate, sh_w_up, sh_w_down):
    bp, sp, _ = x_prompt.shape
    _, ts, _ = x_sample.shape
    pos_p = jnp.arange(sp, dtype=jnp.int32)
    pos_s = PAST_LEN + jnp.arange(ts, dtype=jnp.int32)
    q_rows_s = cache_win_k.shape[1] + jnp.arange(ts, dtype=jnp.int32)
    lower_bounds = jnp.cumsum(jax.nn.softmax(lb_logits.astype(jnp.float32), axis=0), axis=0)
    xp, xs = x_prompt, x_sample
    hgrn_p, hgrn_s, mem_k_p, mem_v_p = [], [], [], []
    for layer in range(DEPTH):
        mk_p, mv_p = memory_kv(mem_prompt, w_mem_kv[layer])
        mem_k_p.append(mk_p)
        mem_v_p.append(mv_p)
        mk_s, mv_s = cache_mem_k[layer], cache_mem_v[layer]
        if layer < N_A_LAYERS:
            a = layer
            s0 = jnp.zeros((bp, A_HEADS, A_DK, A_DV), jnp.float32)
            yp, st_p = hgrn2_layer_mixer(xp, mk_p, mv_p, w_in_a[a], lower_bounds[a], g_norm_a[a], w_out_a[a], s0)
            ys, st_s = hgrn2_layer_mixer(xs, mk_s, mv_s, w_in_a[a], lower_bounds[a], g_norm_a[a], w_out_a[a], state_hgrn[a])
            hgrn_p.append(st_p.astype(x_prompt.dtype))
            hgrn_s.append(st_s.astype(x_sample.dtype))
        else:
            if layer == N_A_LAYERS:
                k_p, v_p = shared_kv(xp, pos_p, w_kv_shared)
                k_s, v_s = shared_kv(xs, pos_s, w_kv_shared)
                k_all = jnp.concatenate([cache_win_k, k_s.astype(cache_win_k.dtype)], axis=1)
                v_all = jnp.concatenate([cache_win_v, v_s.astype(cache_win_v.dtype)], axis=1)
            bl = layer - N_A_LAYERS
            yp = dilated_layer_mixer(xp, pos_p, mk_p, mv_p, w_in_b[bl], w_out_b[bl],
                                     lambda q, dil, span: dilated_prompt(q, k_p, v_p, dil, span))
            ys = dilated_layer_mixer(xs, pos_s, mk_s, mv_s, w_in_b[bl], w_out_b[bl],
                                     lambda q, dil, span: dilated_decode(q, k_all, v_all, dil, span, q_rows_s))
        xp = post_norm(xp, yp, ln_g[layer, 0], ln_b[layer, 0])
        xs = post_norm(xs, ys, ln_g[layer, 0], ln_b[layer, 0])
        moe_w = (router_w[layer], router_b[layer], exp_w_gate[layer], exp_w_up[layer], exp_w_down[layer],
                 sh_w_gate[layer], sh_w_up[layer], sh_w_down[layer])
        xp = post_norm(xp, moe_ffn(xp, *moe_w), ln_g[layer, 1], ln_b[layer, 1])
        xs = post_norm(xs, moe_ffn(xs, *moe_w), ln_g[layer, 1], ln_b[layer, 1])
    w_p = min(MAX_WINDOW, sp)
    hgrn_prompt = jnp.stack(hgrn_p)
    hgrn_sample = jnp.stack(hgrn_s)
    mem_k_prompt = jnp.stack(mem_k_p)
    mem_v_prompt = jnp.stack(mem_v_p)
    return (xp, xs, hgrn_prompt, hgrn_sample, k_p[:, sp - w_p:], v_p[:, sp - w_p:], k_s, v_s, mem_k_prompt, mem_v_prompt)
```

```python
import functools

import jax
import jax.numpy as jnp
from jax import lax
from jax.experimental import pallas as pl
from jax.experimental.pallas import tpu as pltpu

F32 = jnp.float32
BF16 = jnp.bfloat16

HEAD_DIM = 128
A_HEADS = 12
A_CHUNK = 64
A_SUB = 16
B_PATTERNS = ((128, 1), (512, 4), (2048, 16))
B_SLOTS = 4
B_GROUPS = len(B_PATTERNS)
B_QHEADS = B_GROUPS * B_SLOTS
B_BLOCK = 128
MEM_HEADS = 4
MEM_W = MEM_HEADS * HEAD_DIM
KV_W = B_SLOTS * HEAD_DIM
ROPE_THETA = 500000.0
ROPE_DIM = HEAD_DIM // 4
N_EXPERTS = 64
N_GROUPS = 8
TOPK_GROUPS = 4
TOP_K = 8
ROUTED_SCALE = 2.5
LN_EPS = 1e-5
RMS_EPS = 1e-6
ATTN_SCALE = HEAD_DIM ** -0.5
PAST_LEN = 2048

V7X_VMEM_LIMIT_BYTES = 56 * 1024 * 1024
LANES = 128
NEG_BIG = -1e30


def _cparams(sem, vmem=None):
    return pltpu.CompilerParams(dimension_semantics=sem, vmem_limit_bytes=vmem)


def _dot(a, b):
    return jnp.dot(a, b, preferred_element_type=F32)


def _dot_nt(a, b):
    return lax.dot_general(a, b, (((1,), (1,)), ((), ())), preferred_element_type=F32)


def _sigmoid(x):
    return 1.0 / (1.0 + jnp.exp(-x))


def _proj_kernel(*refs, n_lhs):
    x_refs = refs[:n_lhs]
    w_refs = refs[n_lhs:2 * n_lhs]
    o_ref = refs[2 * n_lhs]
    wb_refs = refs[2 * n_lhs + 1:]

    @pl.when(pl.program_id(1) == 0)
    def _():
        for w_ref, wb_ref in zip(w_refs, wb_refs):
            wb_ref[...] = w_ref[...].astype(BF16)

    acc = None
    for x_ref, wb_ref in zip(x_refs, wb_refs):
        d = _dot(x_ref[...].astype(BF16), wb_ref[...])
        acc = d if acc is None else acc + d
    o_ref[...] = acc.astype(o_ref.dtype)


def _proj(lhs, w, layer, out_dtype, tm, tn, name):
    m = lhs[0].shape[0]
    n = w.shape[2]
    koff = 0
    in_specs, w_specs, scratch = [], [], []
    for x in lhs:
        k = x.shape[1]
        assert koff % k == 0 and m % tm == 0 and n % tn == 0
        in_specs.append(pl.BlockSpec((tm, k), lambda j, i: (i, 0)))
        w_specs.append(pl.BlockSpec((None, k, tn), functools.partial(lambda j, i, kb: (layer, kb, j), kb=koff // k)))
        scratch.append(pltpu.VMEM((k, tn), BF16))
        koff += k
    assert koff == w.shape[1]
    return pl.pallas_call(
        functools.partial(_proj_kernel, n_lhs=len(lhs)),
        grid=(n // tn, m // tm),
        in_specs=in_specs + w_specs,
        out_specs=pl.BlockSpec((tm, tn), lambda j, i: (i, j)),
        out_shape=jax.ShapeDtypeStruct((m, n), out_dtype),
        scratch_shapes=scratch,
        compiler_params=_cparams(("arbitrary", "arbitrary"), V7X_VMEM_LIMIT_BYTES),
        name=name,
    )(*lhs, *([w] * len(lhs)))


def _postnorm_kernel(*refs, n_y, alpha):
    x_ref = refs[0]
    y_refs = refs[1:1 + n_y]
    g_ref, b_ref, of_ref, ob_ref = refs[1 + n_y:]
    y = y_refs[0][...].astype(F32)
    for r in y_refs[1:]:
        y = y + r[...].astype(F32)
    z = alpha * x_ref[...] + y
    mu = jnp.mean(z, axis=-1, keepdims=True)
    zc = z - mu
    var = jnp.mean(zc * zc, axis=-1, keepdims=True)
    out = zc * lax.rsqrt(var + LN_EPS) * g_ref[...] + b_ref[...]
    of_ref[...] = out
    ob_ref[...] = out.astype(BF16)


def _postnorm(x, ys, g, b, alpha, tm, name):
    m, d = x.shape
    row = pl.BlockSpec((tm, d), lambda i: (i, 0))
    vec = pl.BlockSpec((1, d), lambda i: (0, 0))
    return pl.pallas_call(
        functools.partial(_postnorm_kernel, n_y=len(ys), alpha=alpha),
        grid=(m // tm,),
        in_specs=[row] * (1 + len(ys)) + [vec, vec],
        out_specs=[row, row],
        out_shape=[jax.ShapeDtypeStruct((m, d), F32), jax.ShapeDtypeStruct((m, d), BF16)],
        compiler_params=_cparams(("parallel",)),
        name=name,
    )(x, *ys, g.reshape(1, d), b.reshape(1, d))


def _hgrn_prompt_kernel(q_ref, f_ref, v_ref, gate_ref, lb_ref, gn_ref, o_ref, s_ref, st_ref, *, seq):
    c = A_CHUNK
    n_chunks = seq // c
    n_sub = c // A_SUB
    lb = lb_ref[...]
    gn = gn_ref[...]
    row = lax.broadcasted_iota(jnp.int32, (c, c), 0)
    col = lax.broadcasted_iota(jnp.int32, (c, c), 1)
    tril = jnp.where(row >= col, 1.0, 0.0).astype(BF16)
    same_sub = (row // A_SUB) == (col // A_SUB)
    diag_dist = jnp.where(same_sub, row - col, -1)
    off_mask = col < (row // A_SUB) * A_SUB
    st_ref[...] = jnp.zeros_like(st_ref)

    def chunk(ci, carry):
        r0 = pl.multiple_of(ci * c, c)
        qp = q_ref[pl.ds(r0, c), :]
        fp = f_ref[pl.ds(r0, c), :]
        v = v_ref[pl.ds(r0, c), :]
        gp = gate_ref[pl.ds(r0, c), :]
        st = st_ref[...]
        q = qp * _sigmoid(qp)
        forget = lb + (1.0 - lb) * _sigmoid(fp)
        logf = jnp.log(forget)
        k = 1.0 - forget
        hi = logf.astype(BF16)
        r1 = logf - hi.astype(F32)
        mid = r1.astype(BF16)
        lo = (r1 - mid.astype(F32)).astype(BF16)
        g = _dot(tril, hi) + _dot(tril, mid) + _dot(tril, lo)
        v_b = v.astype(BF16)
        o = _dot_nt((q * jnp.exp(g)).astype(BF16), st.astype(BF16))
        rows = [jnp.zeros((A_SUB, c), F32)]
        for i in range(1, n_sub):
            gref = g[i * A_SUB:i * A_SUB + 1, :]
            qt = q[i * A_SUB:(i + 1) * A_SUB, :] * jnp.exp(g[i * A_SUB:(i + 1) * A_SUB, :] - gref)
            kt = k * jnp.exp(jnp.minimum(gref - g, 0.0))
            rows.append(_dot_nt(qt.astype(BF16), kt.astype(BF16)))
        a = jnp.where(off_mask, jnp.concatenate(rows, axis=0), 0.0)
        for d in range(A_SUB):
            kr = k if d == 0 else pltpu.roll(k, d, axis=0)
            gr = g if d == 0 else pltpu.roll(g, d, axis=0)
            x = q * kr * jnp.exp(jnp.minimum(g - gr, 0.0))
            a = jnp.where(diag_dist == d, jnp.sum(x, axis=-1, keepdims=True), a)
        o = o + _dot(a.astype(BF16), v_b)
        gend = g[c - 1:c, :]
        kt_end = k * jnp.exp(gend - g)
        st_ref[...] = jnp.exp(gend) * st + _dot(v_b.T, kt_end.astype(BF16))
        o = o * lax.rsqrt(jnp.mean(o * o, axis=-1, keepdims=True) + RMS_EPS) * gn
        o_ref[pl.ds(r0, c), :] = (o * (gp * _sigmoid(gp))).astype(o_ref.dtype)
        return carry

    lax.fori_loop(0, n_chunks, chunk, 0)
    s_ref[...] = st_ref[...].T


def _hgrn_prompt(proj, lb, g_norm, n_batch, seq):
    h = A_HEADS
    blk = lambda off: pl.BlockSpec((seq, HEAD_DIM), functools.partial(lambda b, hh, off: (b, off + hh), off=off))
    return pl.pallas_call(
        functools.partial(_hgrn_prompt_kernel, seq=seq),
        grid=(n_batch, h),
        in_specs=[blk(0), blk(h), blk(2 * h), blk(3 * h),
                  pl.BlockSpec((None, 1, HEAD_DIM), lambda b, hh: (hh, 0, 0)),
                  pl.BlockSpec((1, HEAD_DIM), lambda b, hh: (0, 0))],
        out_specs=[pl.BlockSpec((seq, HEAD_DIM), lambda b, hh: (b, hh)),
                   pl.BlockSpec((None, None, HEAD_DIM, HEAD_DIM), lambda b, hh: (b, hh, 0, 0))],
        out_shape=[jax.ShapeDtypeStruct((n_batch * seq, h * HEAD_DIM), BF16),
                   jax.ShapeDtypeStruct((n_batch, h, HEAD_DIM, HEAD_DIM), F32)],
        scratch_shapes=[pltpu.VMEM((HEAD_DIM, HEAD_DIM), F32)],
        compiler_params=_cparams(("parallel", "parallel")),
        name="hgrn_prompt",
    )(proj, proj, proj, proj, lb.reshape(h, 1, HEAD_DIM), g_norm.reshape(1, HEAD_DIM))


def _softmax_av(s, v):
    m = jnp.max(s, axis=-1, keepdims=True)
    p = jnp.exp(s - m)
    l = jnp.sum(p, axis=-1, keepdims=True)
    return _dot(p.astype(BF16), v) / l


def _mem_attn_kernel(q_ref, k_ref, v_ref, o_ref):
    for h in range(MEM_HEADS):
        sl = slice(h * HEAD_DIM, (h + 1) * HEAD_DIM)
        s = _dot_nt(q_ref[:, sl].astype(BF16), k_ref[:, sl].astype(BF16)) * ATTN_SCALE
        o_ref[:, sl] = _softmax_av(s, v_ref[:, sl].astype(BF16)).astype(o_ref.dtype)


def _mem_attn_prompt(proj, q_col, kv, n_batch, seq, tq):
    mem_len = kv.shape[0] // n_batch
    nq = seq // tq
    return pl.pallas_call(
        _mem_attn_kernel,
        grid=(n_batch, nq),
        in_specs=[pl.BlockSpec((tq, MEM_W), lambda b, i: (b * nq + i, q_col)),
                  pl.BlockSpec((mem_len, MEM_W), lambda b, i: (b, 0)),
                  pl.BlockSpec((mem_len, MEM_W), lambda b, i: (b, 1))],
        out_specs=pl.BlockSpec((tq, MEM_W), lambda b, i: (b * nq + i, 0)),
        out_shape=jax.ShapeDtypeStruct((n_batch * seq, MEM_W), BF16),
        compiler_params=_cparams(("parallel", "parallel")),
        name="mem_attn_prompt",
    )(proj, kv, kv)


def _rope_tables(pos):
    half = ROPE_DIM // 2
    inv_freq = ROPE_THETA ** (-jnp.arange(0, ROPE_DIM, 2, dtype=F32) / ROPE_DIM)
    ang = pos.astype(F32)[:, None] * inv_freq[None, :]
    cos, sin = jnp.cos(ang), jnp.sin(ang)
    n = pos.shape[0]
    one = jnp.ones((n, HEAD_DIM - ROPE_DIM), F32)
    zero = jnp.zeros((n, HEAD_DIM - half), F32)
    c = jnp.concatenate([cos, cos, one], axis=1)
    s_up = jnp.concatenate([-sin, zero], axis=1)
    s_dn = jnp.concatenate([jnp.zeros((n, half), F32), sin, zero[:, half:]], axis=1)
    return c, s_up, s_dn


def _rope_kernel(x_ref, c_ref, su_ref, sd_ref, o_ref, *, n_heads):
    half = ROPE_DIM // 2
    c, su, sd = c_ref[...], su_ref[...], sd_ref[...]
    for h in range(n_heads):
        sl = slice(h * HEAD_DIM, (h + 1) * HEAD_DIM)
        x = x_ref[:, sl]
        up = pltpu.roll(x, HEAD_DIM - half, axis=1)
        dn = pltpu.roll(x, half, axis=1)
        o_ref[:, sl] = (x * c + up * su + dn * sd).astype(o_ref.dtype)


def _rope(x, n_heads, tables, out_dtype, tm, name):
    m = x.shape[0]
    w = n_heads * HEAD_DIM
    tab = pl.BlockSpec((tm, HEAD_DIM), lambda i: (i, 0))
    return pl.pallas_call(
        functools.partial(_rope_kernel, n_heads=n_heads),
        grid=(m // tm,),
        in_specs=[pl.BlockSpec((tm, w), lambda i: (i, 0)), tab, tab, tab],
        out_specs=pl.BlockSpec((tm, w), lambda i: (i, 0)),
        out_shape=jax.ShapeDtypeStruct((m, w), out_dtype),
        compiler_params=_cparams(("parallel",)),
        name=name,
    )(x, *tables)


def _dil_prompt_kernel(q_ref, k_ref, v_ref, o_ref, kb_ref, vb_ref):
    qb = pl.program_id(1)
    blk = B_BLOCK

    @pl.when(qb == 0)
    def _():
        kb_ref[...] = k_ref[...].astype(BF16)
        vb_ref[...] = v_ref[...].astype(BF16)

    base = (lax.broadcasted_iota(jnp.int32, (blk, blk), 0) - lax.broadcasted_iota(jnp.int32, (blk, blk), 1))
    for h in range(B_SLOTS):
        hsl = slice(h * HEAD_DIM, (h + 1) * HEAD_DIM)
        carry = (jnp.full((blk, 1), NEG_BIG, F32), jnp.zeros((blk, 1), F32), jnp.zeros((blk, HEAD_DIM), F32))
        for g, (win, dil) in enumerate(B_PATTERNS):
            qh = g * B_SLOTS + h
            q = q_ref[:, qh * HEAD_DIM:(qh + 1) * HEAD_DIM]

            def body(kb, carry, q=q, win=win, dil=dil, hsl=hsl):
                m, l, acc = carry
                r0 = pl.multiple_of(kb * blk, blk)
                s = _dot_nt(q, kb_ref[pl.ds(r0, blk), hsl]) * ATTN_SCALE
                dist = base + (qb - kb) * blk
                if dil > 1:
                    dist = jnp.where((dist & (dil - 1)) != 0, -1, dist)
                valid = dist.astype(jnp.uint32) <= jnp.uint32(win)
                m_new = jnp.maximum(m, jnp.max(jnp.where(valid, s, NEG_BIG), axis=-1, keepdims=True))
                p = jnp.where(valid, jnp.exp(s - m_new), 0.0)
                alpha = jnp.exp(m - m_new)
                l = alpha * l + jnp.sum(p, axis=-1, keepdims=True)
                acc = alpha * acc + _dot(p.astype(BF16), vb_ref[pl.ds(r0, blk), hsl])
                return m_new, l, acc

            carry = lax.fori_loop(jnp.maximum(qb - win // blk, 0), qb + 1, body, carry)
        o_ref[:, hsl] = (carry[2] / carry[1]).astype(o_ref.dtype)


def _dil_attn_prompt(q, k, kv, n_batch, seq):
    nq = seq // B_BLOCK
    return pl.pallas_call(
        _dil_prompt_kernel,
        grid=(n_batch, nq),
        in_specs=[pl.BlockSpec((B_BLOCK, B_QHEADS * HEAD_DIM), lambda b, i: (b * nq + i, 0)),
                  pl.BlockSpec((seq, KV_W), lambda b, i: (b, 0)),
                  pl.BlockSpec((seq, KV_W), lambda b, i: (b, 1))],
        out_specs=pl.BlockSpec((B_BLOCK, KV_W), lambda b, i: (b * nq + i, 0)),
        out_shape=jax.ShapeDtypeStruct((n_batch * seq, KV_W), BF16),
        scratch_shapes=[pltpu.VMEM((seq, KV_W), BF16), pltpu.VMEM((seq, KV_W), BF16)],
        compiler_params=_cparams(("arbitrary", "arbitrary"), V7X_VMEM_LIMIT_BYTES),
        name="dil_attn_prompt",
    )(q, k, kv)


def _decode_scores(q, k):
    return jnp.sum(k * q[None], axis=-1, keepdims=True) * ATTN_SCALE


def _decode_attend(parts):
    m = None
    for s, _ in parts:
        sm = jnp.max(s, axis=0)
        m = sm if m is None else jnp.maximum(m, sm)
    l = jnp.zeros_like(m)
    acc = None
    for s, v in parts:
        p = jnp.exp(s - m[None])
        l = l + jnp.sum(p, axis=0)
        pv = jnp.sum(p * v, axis=0)
        acc = pv if acc is None else acc + pv
    return acc / l


def _mem_decode_kernel(q_ref, k_ref, v_ref, o_ref):
    o_ref[...] = _decode_attend([(_decode_scores(q_ref[...], k_ref[...]), v_ref[...])]).astype(o_ref.dtype)


def _mem_attn_decode(q, cache_k, cache_v, layer):
    ns, nh, hd = q.shape
    mem_len = cache_k.shape[2]
    cspec = pl.BlockSpec((None, None, mem_len, nh, hd), lambda b: (layer, b, 0, 0, 0))
    qspec = pl.BlockSpec((None, nh, hd), lambda b: (b, 0, 0))
    return pl.pallas_call(
        _mem_decode_kernel,
        grid=(ns,),
        in_specs=[qspec, cspec, cspec],
        out_specs=qspec,
        out_shape=jax.ShapeDtypeStruct((ns, nh, hd), BF16),
        compiler_params=_cparams(("parallel",)),
        name="mem_attn_decode",
    )(q, cache_k, cache_v)


def _dil_decode_kernel(q_ref, kn_ref, vn_ref, *refs):
    o_ref = refs[-1]
    kn = kn_ref[...][None]
    vn = vn_ref[...][None]
    parts = []
    for g in range(B_GROUPS):
        q = q_ref[g].astype(F32)
        parts.append((_decode_scores(q, refs[2 * g][...]), refs[2 * g + 1][...]))
        parts.append((_decode_scores(q, kn), vn))
    o_ref[...] = _decode_attend(parts).astype(o_ref.dtype)


def _dil_attn_decode(q, k_new, v_new, cache_k, cache_v):
    ns, w_buf, ns_slots, hd = cache_k.shape
    in_specs = [pl.BlockSpec((None, B_GROUPS, ns_slots, hd), lambda b: (b, 0, 0, 0)),
                pl.BlockSpec((None, ns_slots, hd), lambda b: (b, 0, 0)),
                pl.BlockSpec((None, ns_slots, hd), lambda b: (b, 0, 0))]
    args = [q, k_new, v_new]
    for win, dil in B_PATTERNS:
        span = win // dil
        assert w_buf % dil == 0 and (w_buf // dil) % span == 0
        view = (ns, w_buf // dil, dil, ns_slots, hd)
        last = w_buf // dil // span - 1
        spec = pl.BlockSpec((None, span, None, ns_slots, hd), functools.partial(lambda b, last: (b, last, 0, 0, 0), last=last))
        in_specs += [spec, spec]
        args += [cache_k.reshape(view), cache_v.reshape(view)]
    return pl.pallas_call(
        _dil_decode_kernel,
        grid=(ns,),
        in_specs=in_specs,
        out_specs=pl.BlockSpec((None, ns_slots, hd), lambda b: (b, 0, 0)),
        out_shape=jax.ShapeDtypeStruct((ns, ns_slots, hd), BF16),
        compiler_params=_cparams(("parallel",)),
        name="dil_attn_decode",
    )(*args)


def _hgrn_step_kernel(p_ref, s_ref, lb_ref, gn_ref, o_ref, so_ref, rows_ref):
    h = A_HEADS
    hk = h * HEAD_DIM
    gn = gn_ref[...]
    rows_ref[...] = jnp.zeros_like(rows_ref)
    for i in range(h):
        sl = slice(i * HEAD_DIM, (i + 1) * HEAD_DIM)
        qp = p_ref[:, sl]
        fp = p_ref[:, hk + i * HEAD_DIM:hk + (i + 1) * HEAD_DIM]
        lb = lb_ref[i:i + 1, :]
        forget = lb + (1.0 - lb) * _sigmoid(fp)
        rows_ref[3 * i:3 * i + 1, :] = qp * _sigmoid(qp)
        rows_ref[3 * i + 1:3 * i + 2, :] = forget
        rows_ref[3 * i + 2:3 * i + 3, :] = 1.0 - forget
    cols = rows_ref[...].T
    for i in range(h):
        sl = slice(i * HEAD_DIM, (i + 1) * HEAD_DIM)
        v = p_ref[:, 2 * hk + i * HEAD_DIM:2 * hk + (i + 1) * HEAD_DIM]
        gp = p_ref[:, 3 * hk + i * HEAD_DIM:3 * hk + (i + 1) * HEAD_DIM]
        q_c = cols[:, 3 * i:3 * i + 1]
        f_c = cols[:, 3 * i + 1:3 * i + 2]
        k_c = cols[:, 3 * i + 2:3 * i + 3]
        s_new = f_c * s_ref[i] + k_c * v
        so_ref[i] = s_new
        o = jnp.sum(q_c * s_new, axis=0, keepdims=True)
        o = o * lax.rsqrt(jnp.mean(o * o, axis=-1, keepdims=True) + RMS_EPS) * gn
        o_ref[:, sl] = (o * (gp * _sigmoid(gp))).astype(o_ref.dtype)


def _hgrn_step(proj, state, layer, lb, g_norm):
    ns = proj.shape[0]
    h = A_HEADS
    sspec = pl.BlockSpec((None, h, HEAD_DIM, HEAD_DIM), lambda b: (b, 0, 0, 0))
    return pl.pallas_call(
        _hgrn_step_kernel,
        grid=(ns,),
        in_specs=[pl.BlockSpec((None, 1, proj.shape[2]), lambda b: (b, 0, 0)),
                  pl.BlockSpec((None, None, h, HEAD_DIM, HEAD_DIM), lambda b: (layer, b, 0, 0, 0)),
                  pl.BlockSpec((h, HEAD_DIM), lambda b: (0, 0)),
                  pl.BlockSpec((1, HEAD_DIM), lambda b: (0, 0))],
        out_specs=[pl.BlockSpec((None, 1, h * HEAD_DIM), lambda b: (b, 0, 0)), sspec],
        out_shape=[jax.ShapeDtypeStruct((ns, 1, h * HEAD_DIM), BF16),
                   jax.ShapeDtypeStruct((ns, h, HEAD_DIM, HEAD_DIM), F32)],
        scratch_shapes=[pltpu.VMEM((HEAD_DIM, HEAD_DIM), F32)],
        compiler_params=_cparams(("parallel",)),
        name="hgrn_step",
    )(proj, state, lb.reshape(h, HEAD_DIM), g_norm.reshape(1, HEAD_DIM))


def _ffn_kernel(be_ref, nb_ref, x_ref, wg_ref, wu_ref, wd_ref, o_ref, wgb_ref, wub_ref, wdb_ref):
    i = pl.program_id(0)

    @pl.when(i < nb_ref[0])
    def _():
        @pl.when((i == 0) | (be_ref[i] != be_ref[jnp.maximum(i - 1, 0)]))
        def _():
            wgb_ref[...] = wg_ref[...].astype(BF16)
            wub_ref[...] = wu_ref[...].astype(BF16)
            wdb_ref[...] = wd_ref[...].astype(BF16)

        x = x_ref[...]
        hg = _dot(x, wgb_ref[...])
        hu = _dot(x, wub_ref[...])
        hid = hg * _sigmoid(hg) * hu
        o_ref[...] = _dot(hid.astype(BF16), wdb_ref[...]).astype(o_ref.dtype)

    @pl.when(i >= nb_ref[0])
    def _():
        o_ref[...] = jnp.zeros_like(o_ref)


def _expert_ffn(x, blk_exp, n_used, w_gate, w_up, w_down, layer, tm, name):
    r, d = x.shape
    ff = w_gate.shape[3]
    return pl.pallas_call(
        _ffn_kernel,
        grid_spec=pltpu.PrefetchScalarGridSpec(
            num_scalar_prefetch=2,
            grid=(r // tm,),
            in_specs=[pl.BlockSpec((tm, d), lambda i, be, nb: (i, 0)),
                      pl.BlockSpec((None, None, d, ff), lambda i, be, nb: (layer, be[i], 0, 0)),
                      pl.BlockSpec((None, None, d, ff), lambda i, be, nb: (layer, be[i], 0, 0)),
                      pl.BlockSpec((None, None, ff, d), lambda i, be, nb: (layer, be[i], 0, 0))],
            out_specs=pl.BlockSpec((tm, d), lambda i, be, nb: (i, 0)),
            scratch_shapes=[pltpu.VMEM((d, ff), BF16), pltpu.VMEM((d, ff), BF16), pltpu.VMEM((ff, d), BF16)],
        ),
        out_shape=jax.ShapeDtypeStruct((r, d), BF16),
        compiler_params=_cparams(("arbitrary",), V7X_VMEM_LIMIT_BYTES),
        name=name,
    )(blk_exp, n_used, x, w_gate, w_up, w_down)


def _router_kernel(x_ref, w_ref, o_ref):
    x = x_ref[...]
    w = w_ref[...]
    xh = x.astype(BF16)
    xl = (x - xh.astype(F32)).astype(BF16)
    wh = w.astype(BF16)
    wl = (w - wh.astype(F32)).astype(BF16)
    o_ref[...] = _dot(xh, wh) + (_dot(xh, wl) + _dot(xl, wh))


def _router_logits(x, w, tm):
    m, d = x.shape
    e = w.shape[1]
    return pl.pallas_call(
        _router_kernel,
        grid=(m // tm,),
        in_specs=[pl.BlockSpec((tm, d), lambda i: (i, 0)), pl.BlockSpec((d, e), lambda i: (0, 0))],
        out_specs=pl.BlockSpec((tm, e), lambda i: (i, 0)),
        out_shape=jax.ShapeDtypeStruct((m, e), F32),
        compiler_params=_cparams(("parallel",)),
        name="router_logits",
    )(x, w)


def _route(logits, router_b):
    n = logits.shape[0]
    scores = jax.nn.sigmoid(logits)
    biased = scores + router_b.astype(F32)
    grp = lax.top_k(biased.reshape(n, N_GROUPS, N_EXPERTS // N_GROUPS), 2)[0].sum(-1)
    _, g_idx = lax.top_k(grp, TOPK_GROUPS)
    g_mask = jax.nn.one_hot(g_idx, N_GROUPS, dtype=F32).sum(1) > 0
    e_mask = jnp.repeat(g_mask, N_EXPERTS // N_GROUPS, axis=1)
    _, e_idx = lax.top_k(jnp.where(e_mask, biased, -jnp.inf), TOP_K)
    gate = jnp.take_along_axis(scores, e_idx, axis=1)
    gate = gate / jnp.sum(gate, -1, keepdims=True) * ROUTED_SCALE
    return e_idx, gate


def _dispatch_plan(e_idx, tm):
    n, k = e_idx.shape
    a = n * k
    n_blocks = -(-a // tm) + N_EXPERTS
    flat_e = e_idx.reshape(-1)
    onehot = (flat_e[:, None] == jnp.arange(N_EXPERTS, dtype=flat_e.dtype)[None, :]).astype(jnp.int32)
    csum = jnp.cumsum(onehot, axis=0)
    rank = jnp.take_along_axis(csum, flat_e[:, None], axis=1)[:, 0] - 1
    counts = csum[-1]
    padded = (counts + tm - 1) // tm * tm
    pad_end = jnp.cumsum(padded)
    pad_start = pad_end - padded
    dest = (pad_start[flat_e] + rank).astype(jnp.int32)
    slot_tok = jnp.full((n_blocks * tm,), n, jnp.int32).at[dest].set(jnp.arange(a, dtype=jnp.int32) // k)
    blk_exp = jnp.minimum(jnp.searchsorted(pad_end, jnp.arange(n_blocks, dtype=jnp.int32) * tm, side='right'),
                          N_EXPERTS - 1).astype(jnp.int32)
    n_used = (pad_end[-1:] // tm).astype(jnp.int32)
    return slot_tok, dest.reshape(n, k), blk_exp, n_used


MOE_BLOCK = 256
ROW_TILE = 640


def _moe(xf, xb, layer, router_w, router_b, w_gate, w_up, w_down, sw_gate, sw_up, sw_down):
    n, d = xf.shape
    logits = _router_logits(xf, router_w[layer], ROW_TILE // 2)
    e_idx, gate = _route(logits, router_b[layer])
    slot_tok, slot_of, blk_exp, n_used = _dispatch_plan(e_idx, MOE_BLOCK)
    x_pad = jnp.concatenate([xb, jnp.zeros((1, d), xb.dtype)], axis=0)
    x_sorted = jnp.take(x_pad, slot_tok, axis=0)
    y_sorted = _expert_ffn(x_sorted, blk_exp, n_used, w_gate, w_up, w_down, layer, MOE_BLOCK, "routed_ffn")
    routed = jnp.sum(jnp.take(y_sorted, slot_of, axis=0).astype(F32) * gate[:, :, None], axis=1)
    n_sh = n // ROW_TILE
    shared = _expert_ffn(xb, jnp.zeros((n_sh,), jnp.int32), jnp.full((1,), n_sh, jnp.int32),
                         sw_gate[:, None], sw_up[:, None], sw_down[:, None], layer, ROW_TILE, "shared_ffn")
    return routed, shared


def kernel(x_prompt, x_sample, state_hgrn, cache_win_k, cache_win_v, cache_mem_k, cache_mem_v, mem_prompt,
           w_in_a, lb_logits, g_norm_a, w_out_a, w_in_b, w_out_b, w_kv_shared, w_mem_kv, ln_g, ln_b,
           router_w, router_b, exp_w_gate, exp_w_up, exp_w_down, sh_w_gate, sh_w_up, sh_w_down):
    bp, sp, d = x_prompt.shape
    ns = x_sample.shape[0]
    assert x_sample.shape[1] == 1
    depth = ln_g.shape[0]
    n_a = w_in_a.shape[0]
    alpha = (2 * depth) ** 0.25
    n_p = bp * sp
    mem_len = mem_prompt.shape[1]
    a_mix = 4 * A_HEADS * HEAD_DIM

    xf = jnp.concatenate([x_prompt.reshape(n_p, d), x_sample.reshape(ns, d)], axis=0)
    xb = xf.astype(BF16)
    lower_bounds = jnp.cumsum(jax.nn.softmax(lb_logits.astype(F32), axis=0), axis=0)
    mem_flat = mem_prompt.reshape(bp * mem_len, d)
    pos_all = jnp.concatenate([jnp.tile(jnp.arange(sp, dtype=jnp.int32), bp),
                               jnp.full((ns,), PAST_LEN, jnp.int32)])
    tables = _rope_tables(pos_all)

    hgrn_p, hgrn_s, mem_k_p, mem_v_p = [], [], [], []
    for layer in range(depth):
        kvm = _proj([mem_flat], w_mem_kv, layer, F32, mem_len, 512, "mem_kv")
        mem_k_p.append(kvm[:, :MEM_W].reshape(bp, mem_len, MEM_HEADS, HEAD_DIM))
        mem_v_p.append(kvm[:, MEM_W:].reshape(bp, mem_len, MEM_HEADS, HEAD_DIM))
        if layer < n_a:
            a = layer
            proj = _proj([xb], w_in_a, a, F32, ROW_TILE, 512, "proj_in_a")
            o_x_p, st_p = _hgrn_prompt(proj, lower_bounds[a], g_norm_a[a], bp, sp)
            o_m_p = _mem_attn_prompt(proj, a_mix // MEM_W, kvm, bp, sp, 512)
            proj_s = proj[n_p:]
            o_x_s, st_s = _hgrn_step(proj_s.reshape(ns, 1, -1), state_hgrn, a, lower_bounds[a], g_norm_a[a])
            o_m_s = _mem_attn_decode(proj_s[:, a_mix:].reshape(ns, MEM_HEADS, HEAD_DIM), cache_mem_k, cache_mem_v, layer)
            hgrn_p.append(st_p)
            hgrn_s.append(st_s)
            w_out, w_out_layer = w_out_a, a
        else:
            bl = layer - n_a
            if layer == n_a:
                kv = _proj([xb], w_kv_shared[None], 0, F32, ROW_TILE, 512, "proj_kv")
                k_r = _rope(kv, B_SLOTS, tables, F32, ROW_TILE, "rope_k")
                k_p = k_r[:n_p].reshape(bp, sp, B_SLOTS, HEAD_DIM)
                v_p = kv[:n_p, KV_W:].reshape(bp, sp, B_SLOTS, HEAD_DIM)
                k_s = k_r[n_p:].reshape(ns, B_SLOTS, HEAD_DIM)
                v_s = kv[n_p:, KV_W:].reshape(ns, B_SLOTS, HEAD_DIM)
            proj = _proj([xb], w_in_b, bl, F32, ROW_TILE, 512, "proj_in_b")
            q_r = _rope(proj, B_QHEADS, tables, BF16, ROW_TILE, "rope_q")
            o_x_p = _dil_attn_prompt(q_r, k_r, kv, bp, sp)
            o_m_p = _mem_attn_prompt(proj, B_QHEADS * HEAD_DIM // MEM_W, kvm, bp, sp, 512)
            q_s = q_r[n_p:].astype(F32).reshape(ns, B_GROUPS, B_SLOTS, HEAD_DIM)
            o_x_s = _dil_attn_decode(q_s, k_s, v_s, cache_win_k, cache_win_v)
            o_m_s = _mem_attn_decode(proj[n_p:, B_QHEADS * HEAD_DIM:].reshape(ns, MEM_HEADS, HEAD_DIM),
                                     cache_mem_k, cache_mem_v, layer)
            w_out, w_out_layer = w_out_b, bl
        o_x = jnp.concatenate([o_x_p, o_x_s.reshape(ns, -1)], axis=0)
        o_m = jnp.concatenate([o_m_p, o_m_s.reshape(ns, -1)], axis=0)
        y = _proj([o_x, o_m], w_out, w_out_layer, F32, ROW_TILE, 512, "proj_out")
        xf, xb = _postnorm(xf, [y], ln_g[layer, 0], ln_b[layer, 0], alpha, ROW_TILE // 2, "postnorm_mix")
        routed, shared = _moe(xf, xb, layer, router_w, router_b, exp_w_gate, exp_w_up, exp_w_down,
                              sh_w_gate, sh_w_up, sh_w_down)
        xf, xb = _postnorm(xf, [routed, shared], ln_g[layer, 1], ln_b[layer, 1], alpha, ROW_TILE // 2, "postnorm_moe")

    w_p = min(max(w for w, _ in B_PATTERNS), sp)
    return (xf[:n_p].reshape(bp, sp, d), xf[n_p:].reshape(ns, 1, d),
            jnp.stack(hgrn_p), jnp.stack(hgrn_s),
            k_p[:, sp - w_p:], v_p[:, sp - w_p:],
            k_s.reshape(ns, 1, B_SLOTS, HEAD_DIM), v_s.reshape(ns, 1, B_SLOTS, HEAD_DIM),
            jnp.stack(mem_k_p), jnp.stack(mem_v_p))
```

```python
import functools

import jax
import jax.numpy as jnp
from jax import lax
from jax.experimental import pallas as pl
from jax.experimental.pallas import tpu as pltpu

F32 = jnp.float32
BF16 = jnp.bfloat16

HEAD_DIM = 128
A_HEADS = 12
A_CHUNK = 64
A_SUB = 16
B_PATTERNS = ((128, 1), (512, 4), (2048, 16))
B_SLOTS = 4
B_GROUPS = len(B_PATTERNS)
B_QHEADS = B_GROUPS * B_SLOTS
B_BLOCK = 128
MEM_HEADS = 4
MEM_W = MEM_HEADS * HEAD_DIM
KV_W = B_SLOTS * HEAD_DIM
ROPE_THETA = 500000.0
ROPE_DIM = HEAD_DIM // 4
N_EXPERTS = 64
N_GROUPS = 8
TOPK_GROUPS = 4
TOP_K = 8
ROUTED_SCALE = 2.5
LN_EPS = 1e-5
RMS_EPS = 1e-6
ATTN_SCALE = HEAD_DIM ** -0.5
PAST_LEN = 2048

V7X_VMEM_LIMIT_BYTES = 56 * 1024 * 1024
LANES = 128
NEG_BIG = -1e30


def _cparams(sem, vmem=None):
    return pltpu.CompilerParams(dimension_semantics=sem, vmem_limit_bytes=vmem)


def _dot(a, b):
    return jnp.dot(a, b, preferred_element_type=F32)


def _dot_nt(a, b):
    return lax.dot_general(a, b, (((1,), (1,)), ((), ())), preferred_element_type=F32)


def _sigmoid(x):
    return 1.0 / (1.0 + jnp.exp(-x))


def _proj_kernel(*refs, n_lhs):
    x_refs = refs[:n_lhs]
    w_refs = refs[n_lhs:2 * n_lhs]
    o_ref = refs[2 * n_lhs]
    wb_refs = refs[2 * n_lhs + 1:]

    @pl.when(pl.program_id(1) == 0)
    def _():
        for w_ref, wb_ref in zip(w_refs, wb_refs):
            wb_ref[...] = w_ref[...].astype(BF16)

    acc = None
    for x_ref, wb_ref in zip(x_refs, wb_refs):
        d = _dot(x_ref[...].astype(BF16), wb_ref[...])
        acc = d if acc is None else acc + d
    o_ref[...] = acc.astype(o_ref.dtype)


def _proj(lhs, w, layer, out_dtype, tm, tn, name):
    m = lhs[0].shape[0]
    n = w.shape[2]
    koff = 0
    in_specs, w_specs, scratch = [], [], []
    for x in lhs:
        k = x.shape[1]
        assert koff % k == 0 and m % tm == 0 and n % tn == 0
        in_specs.append(pl.BlockSpec((tm, k), lambda j, i: (i, 0)))
        w_specs.append(pl.BlockSpec((None, k, tn), functools.partial(lambda j, i, kb: (layer, kb, j), kb=koff // k)))
        scratch.append(pltpu.VMEM((k, tn), BF16))
        koff += k
    assert koff == w.shape[1]
    return pl.pallas_call(
        functools.partial(_proj_kernel, n_lhs=len(lhs)),
        grid=(n // tn, m // tm),
        in_specs=in_specs + w_specs,
        out_specs=pl.BlockSpec((tm, tn), lambda j, i: (i, j)),
        out_shape=jax.ShapeDtypeStruct((m, n), out_dtype),
        scratch_shapes=scratch,
        compiler_params=_cparams(("arbitrary", "arbitrary"), V7X_VMEM_LIMIT_BYTES),
        name=name,
    )(*lhs, *([w] * len(lhs)))


def _postnorm_kernel(*refs, n_y, alpha):
    x_ref = refs[0]
    y_refs = refs[1:1 + n_y]
    g_ref, b_ref, of_ref, ob_ref = refs[1 + n_y:]
    y = y_refs[0][...].astype(F32)
    for r in y_refs[1:]:
        y = y + r[...].astype(F32)
    z = alpha * x_ref[...] + y
    mu = jnp.mean(z, axis=-1, keepdims=True)
    zc = z - mu
    var = jnp.mean(zc * zc, axis=-1, keepdims=True)
    out = zc * lax.rsqrt(var + LN_EPS) * g_ref[...] + b_ref[...]
    of_ref[...] = out
    ob_ref[...] = out.astype(BF16)


def _postnorm(x, ys, g, b, alpha, tm, name):
    m, d = x.shape
    row = pl.BlockSpec((tm, d), lambda i: (i, 0))
    vec = pl.BlockSpec((1, d), lambda i: (0, 0))
    return pl.pallas_call(
        functools.partial(_postnorm_kernel, n_y=len(ys), alpha=alpha),
        grid=(m // tm,),
        in_specs=[row] * (1 + len(ys)) + [vec, vec],
        out_specs=[row, row],
        out_shape=[jax.ShapeDtypeStruct((m, d), F32), jax.ShapeDtypeStruct((m, d), BF16)],
        compiler_params=_cparams(("parallel",)),
        name=name,
    )(x, *ys, g.reshape(1, d), b.reshape(1, d))


def _hgrn_prompt_kernel(q_ref, f_ref, v_ref, gate_ref, lb_ref, gn_ref, o_ref, s_ref, st_ref, *, seq):
    c = A_CHUNK
    n_chunks = seq // c
    n_sub = c // A_SUB
    lb = lb_ref[...]
    gn = gn_ref[...]
    row = lax.broadcasted_iota(jnp.int32, (c, c), 0)
    col = lax.broadcasted_iota(jnp.int32, (c, c), 1)
    tril = jnp.where(row >= col, 1.0, 0.0).astype(BF16)
    same_sub = (row // A_SUB) == (col // A_SUB)
    diag_dist = jnp.where(same_sub, row - col, -1)
    off_mask = col < (row // A_SUB) * A_SUB
    st_ref[...] = jnp.zeros_like(st_ref)

    def chunk(ci, carry):
        r0 = pl.multiple_of(ci * c, c)
        qp = q_ref[pl.ds(r0, c), :]
        fp = f_ref[pl.ds(r0, c), :]
        v = v_ref[pl.ds(r0, c), :]
        gp = gate_ref[pl.ds(r0, c), :]
        st = st_ref[...]
        q = qp * _sigmoid(qp)
        forget = lb + (1.0 - lb) * _sigmoid(fp)
        logf = jnp.log(forget)
        k = 1.0 - forget
        hi = logf.astype(BF16)
        r1 = logf - hi.astype(F32)
        mid = r1.astype(BF16)
        lo = (r1 - mid.astype(F32)).astype(BF16)
        g = _dot(tril, hi) + _dot(tril, mid) + _dot(tril, lo)
        v_b = v.astype(BF16)
        o = _dot_nt((q * jnp.exp(g)).astype(BF16), st.astype(BF16))
        rows = [jnp.zeros((A_SUB, c), F32)]
        for i in range(1, n_sub):
            gref = g[i * A_SUB:i * A_SUB + 1, :]
            qt = q[i * A_SUB:(i + 1) * A_SUB, :] * jnp.exp(g[i * A_SUB:(i + 1) * A_SUB, :] - gref)
            kt = k * jnp.exp(jnp.minimum(gref - g, 0.0))
            rows.append(_dot_nt(qt.astype(BF16), kt.astype(BF16)))
        a = jnp.where(off_mask, jnp.concatenate(rows, axis=0), 0.0)
        for d in range(A_SUB):
            kr = k if d == 0 else pltpu.roll(k, d, axis=0)
            gr = g if d == 0 else pltpu.roll(g, d, axis=0)
            x = q * kr * jnp.exp(jnp.minimum(g - gr, 0.0))
            a = jnp.where(diag_dist == d, jnp.sum(x, axis=-1, keepdims=True), a)
        o = o + _dot(a.astype(BF16), v_b)
        gend = g[c - 1:c, :]
        kt_end = k * jnp.exp(gend - g)
        st_ref[...] = jnp.exp(gend) * st + _dot(v_b.T, kt_end.astype(BF16))
        o = o * lax.rsqrt(jnp.mean(o * o, axis=-1, keepdims=True) + RMS_EPS) * gn
        o_ref[pl.ds(r0, c), :] = (o * (gp * _sigmoid(gp))).astype(o_ref.dtype)
        return carry

    lax.fori_loop(0, n_chunks, chunk, 0)
    s_ref[...] = st_ref[...].T


def _hgrn_prompt(proj, lb, g_norm, n_batch, seq):
    h = A_HEADS
    blk = lambda off: pl.BlockSpec((seq, HEAD_DIM), functools.partial(lambda b, hh, off: (b, off + hh), off=off))
    return pl.pallas_call(
        functools.partial(_hgrn_prompt_kernel, seq=seq),
        grid=(n_batch, h),
        in_specs=[blk(0), blk(h), blk(2 * h), blk(3 * h),
                  pl.BlockSpec((None, 1, HEAD_DIM), lambda b, hh: (hh, 0, 0)),
                  pl.BlockSpec((1, HEAD_DIM), lambda b, hh: (0, 0))],
        out_specs=[pl.BlockSpec((seq, HEAD_DIM), lambda b, hh: (b, hh)),
                   pl.BlockSpec((None, None, HEAD_DIM, HEAD_DIM), lambda b, hh: (b, hh, 0, 0))],
        out_shape=[jax.ShapeDtypeStruct((n_batch * seq, h * HEAD_DIM), BF16),
                   jax.ShapeDtypeStruct((n_batch, h, HEAD_DIM, HEAD_DIM), F32)],
        scratch_shapes=[pltpu.VMEM((HEAD_DIM, HEAD_DIM), F32)],
        compiler_params=_cparams(("parallel", "parallel")),
        name="hgrn_prompt",
    )(proj, proj, proj, proj, lb.reshape(h, 1, HEAD_DIM), g_norm.reshape(1, HEAD_DIM))


def _softmax_av(s, v):
    m = jnp.max(s, axis=-1, keepdims=True)
    p = jnp.exp(s - m)
    l = jnp.sum(p, axis=-1, keepdims=True)
    return _dot(p.astype(BF16), v) / l


def _mem_attn_kernel(q_ref, k_ref, v_ref, o_ref):
    for h in range(MEM_HEADS):
        sl = slice(h * HEAD_DIM, (h + 1) * HEAD_DIM)
        s = _dot_nt(q_ref[:, sl].astype(BF16), k_ref[:, sl].astype(BF16)) * ATTN_SCALE
        o_ref[:, sl] = _softmax_av(s, v_ref[:, sl].astype(BF16)).astype(o_ref.dtype)


def _mem_attn_prompt(proj, q_col, kv, n_batch, seq, tq):
    mem_len = kv.shape[0] // n_batch
    nq = seq // tq
    return pl.pallas_call(
        _mem_attn_kernel,
        grid=(n_batch, nq),
        in_specs=[pl.BlockSpec((tq, MEM_W), lambda b, i: (b * nq + i, q_col)),
                  pl.BlockSpec((mem_len, MEM_W), lambda b, i: (b, 0)),
                  pl.BlockSpec((mem_len, MEM_W), lambda b, i: (b, 1))],
        out_specs=pl.BlockSpec((tq, MEM_W), lambda b, i: (b * nq + i, 0)),
        out_shape=jax.ShapeDtypeStruct((n_batch * seq, MEM_W), BF16),
        compiler_params=_cparams(("parallel", "parallel")),
        name="mem_attn_prompt",
    )(proj, kv, kv)


def _rope_tables(pos):
    half = ROPE_DIM // 2
    inv_freq = ROPE_THETA ** (-jnp.arange(0, ROPE_DIM, 2, dtype=F32) / ROPE_DIM)
    ang = pos.astype(F32)[:, None] * inv_freq[None, :]
    cos, sin = jnp.cos(ang), jnp.sin(ang)
    n = pos.shape[0]
    one = jnp.ones((n, HEAD_DIM - ROPE_DIM), F32)
    zero = jnp.zeros((n, HEAD_DIM - half), F32)
    c = jnp.concatenate([cos, cos, one], axis=1)
    s_up = jnp.concatenate([-sin, zero], axis=1)
    s_dn = jnp.concatenate([jnp.zeros((n, half), F32), sin, zero[:, half:]], axis=1)
    return c, s_up, s_dn


def _rope_kernel(x_ref, c_ref, su_ref, sd_ref, o_ref, *, n_heads):
    half = ROPE_DIM // 2
    c, su, sd = c_ref[...], su_ref[...], sd_ref[...]
    for h in range(n_heads):
        sl = slice(h * HEAD_DIM, (h + 1) * HEAD_DIM)
        x = x_ref[:, sl]
        up = pltpu.roll(x, HEAD_DIM - half, axis=1)
        dn = pltpu.roll(x, half, axis=1)
        o_ref[:, sl] = (x * c + up * su + dn * sd).astype(o_ref.dtype)


def _rope(x, n_heads, tables, out_dtype, tm, name):
    m = x.shape[0]
    w = n_heads * HEAD_DIM
    tab = pl.BlockSpec((tm, HEAD_DIM), lambda i: (i, 0))
    return pl.pallas_call(
        functools.partial(_rope_kernel, n_heads=n_heads),
        grid=(m // tm,),
        in_specs=[pl.BlockSpec((tm, w), lambda i: (i, 0)), tab, tab, tab],
        out_specs=pl.BlockSpec((tm, w), lambda i: (i, 0)),
        out_shape=jax.ShapeDtypeStruct((m, w), out_dtype),
        compiler_params=_cparams(("parallel",)),
        name=name,
    )(x, *tables)


def _dil_prompt_kernel(*refs, seq):
    q_refs = refs[:B_GROUPS]
    k_ref, v_ref, o_ref, m_ref, l_ref, acc_ref = refs[B_GROUPS:]
    blk = B_BLOCK
    base = (lax.broadcasted_iota(jnp.int32, (blk, blk), 0) - lax.broadcasted_iota(jnp.int32, (blk, blk), 1))
    for g, (win, dil) in enumerate(B_PATTERNS):
        span = win // dil
        n_blk = seq // dil // blk
        assert span <= blk

        def body(it, carry, g=g, dil=dil, span=span, n_blk=n_blk):
            r = it % dil
            n = it // dil

            def rows(nn):
                start = nn * (blk * dil) + r
                return pl.ds(pl.multiple_of(start, blk), blk) if dil == 1 else pl.ds(start, blk, stride=dil)

            own = rows(n)
            q = q_refs[g][own, :].astype(BF16)
            s = jnp.where(base >= 0, _dot_nt(q, k_ref[own, :].astype(BF16)) * ATTN_SCALE, NEG_BIG)
            vv = v_ref[own, :].astype(BF16)
            if n_blk > 1:
                prev = rows(jnp.maximum(n - 1, 0))
                reach = jnp.where(n > 0, span - blk, -blk - 1)
                s_prev = jnp.where(base <= reach, _dot_nt(q, k_ref[prev, :].astype(BF16)) * ATTN_SCALE, NEG_BIG)
                s = jnp.concatenate([s_prev, s], axis=1)
                vv = jnp.concatenate([v_ref[prev, :].astype(BF16), vv], axis=0)
            m_b = jnp.max(s, axis=-1, keepdims=True)
            p = jnp.exp(s - m_b)
            l_b = jnp.sum(p, axis=-1, keepdims=True)
            acc_b = _dot(p.astype(BF16), vv)
            if g == 0:
                m_ref[own, :] = jnp.broadcast_to(m_b, (blk, HEAD_DIM))
                l_ref[own, :] = jnp.broadcast_to(l_b, (blk, HEAD_DIM))
                acc_ref[own, :] = acc_b
            else:
                m_old = m_ref[own, :]
                m_new = jnp.maximum(m_old, m_b)
                a_old = jnp.exp(m_old - m_new)
                a_b = jnp.exp(m_b - m_new)
                m_ref[own, :] = m_new
                l_ref[own, :] = l_ref[own, :] * a_old + l_b * a_b
                acc_ref[own, :] = acc_ref[own, :] * a_old + acc_b * a_b
            return carry

        lax.fori_loop(0, dil * n_blk, body, 0)
    o_ref[...] = (acc_ref[...] / l_ref[...]).astype(o_ref.dtype)


def _dil_attn_prompt(q, k, kv, n_batch, seq):
    col = lambda c0: pl.BlockSpec((seq, HEAD_DIM), functools.partial(lambda b, h, c0: (b, c0 + h), c0=c0))
    return pl.pallas_call(
        functools.partial(_dil_prompt_kernel, seq=seq),
        grid=(n_batch, B_SLOTS),
        in_specs=[col(g * B_SLOTS) for g in range(B_GROUPS)] + [col(0), col(B_SLOTS)],
        out_specs=col(0),
        out_shape=jax.ShapeDtypeStruct((n_batch * seq, KV_W), BF16),
        scratch_shapes=[pltpu.VMEM((seq, HEAD_DIM), F32)] * 3,
        compiler_params=_cparams(("parallel", "parallel"), V7X_VMEM_LIMIT_BYTES),
        name="dil_attn_prompt",
    )(*([q] * B_GROUPS), k, kv)


def _decode_scores(q, k):
    return jnp.sum(k * q[None], axis=-1, keepdims=True) * ATTN_SCALE


def _decode_attend(parts):
    m = None
    for s, _ in parts:
        sm = jnp.max(s, axis=0)
        m = sm if m is None else jnp.maximum(m, sm)
    l = jnp.zeros_like(m)
    acc = None
    for s, v in parts:
        p = jnp.exp(s - m[None])
        l = l + jnp.sum(p, axis=0)
        pv = jnp.sum(p * v, axis=0)
        acc = pv if acc is None else acc + pv
    return acc / l


def _mem_decode_kernel(q_ref, k_ref, v_ref, o_ref):
    o_ref[...] = _decode_attend([(_decode_scores(q_ref[...], k_ref[...]), v_ref[...])]).astype(o_ref.dtype)


def _mem_attn_decode(q, cache_k, cache_v, layer):
    ns, nh, hd = q.shape
    mem_len = cache_k.shape[2]
    cspec = pl.BlockSpec((None, None, mem_len, nh, hd), lambda b: (layer, b, 0, 0, 0))
    qspec = pl.BlockSpec((None, nh, hd), lambda b: (b, 0, 0))
    return pl.pallas_call(
        _mem_decode_kernel,
        grid=(ns,),
        in_specs=[qspec, cspec, cspec],
        out_specs=qspec,
        out_shape=jax.ShapeDtypeStruct((ns, nh, hd), BF16),
        compiler_params=_cparams(("parallel",)),
        name="mem_attn_decode",
    )(q, cache_k, cache_v)


def _dil_decode_kernel(q_ref, kn_ref, vn_ref, *refs):
    o_ref = refs[-1]
    kn = kn_ref[...][None]
    vn = vn_ref[...][None]
    parts = []
    for g in range(B_GROUPS):
        q = q_ref[g].astype(F32)
        parts.append((_decode_scores(q, refs[2 * g][...]), refs[2 * g + 1][...]))
        parts.append((_decode_scores(q, kn), vn))
    o_ref[...] = _decode_attend(parts).astype(o_ref.dtype)


def _dil_attn_decode(q, k_new, v_new, cache_k, cache_v):
    ns, w_buf, ns_slots, hd = cache_k.shape
    in_specs = [pl.BlockSpec((None, B_GROUPS, ns_slots, hd), lambda b: (b, 0, 0, 0)),
                pl.BlockSpec((None, ns_slots, hd), lambda b: (b, 0, 0)),
                pl.BlockSpec((None, ns_slots, hd), lambda b: (b, 0, 0))]
    args = [q, k_new, v_new]
    for win, dil in B_PATTERNS:
        span = win // dil
        assert w_buf % dil == 0 and (w_buf // dil) % span == 0
        view = (ns, w_buf // dil, dil, ns_slots, hd)
        last = w_buf // dil // span - 1
        spec = pl.BlockSpec((None, span, None, ns_slots, hd), functools.partial(lambda b, last: (b, last, 0, 0, 0), last=last))
        in_specs += [spec, spec]
        args += [cache_k.reshape(view), cache_v.reshape(view)]
    return pl.pallas_call(
        _dil_decode_kernel,
        grid=(ns,),
        in_specs=in_specs,
        out_specs=pl.BlockSpec((None, ns_slots, hd), lambda b: (b, 0, 0)),
        out_shape=jax.ShapeDtypeStruct((ns, ns_slots, hd), BF16),
        compiler_params=_cparams(("parallel",)),
        name="dil_attn_decode",
    )(*args)


def _hgrn_step_kernel(p_ref, s_ref, lb_ref, gn_ref, o_ref, so_ref, rows_ref):
    h = A_HEADS
    hk = h * HEAD_DIM
    gn = gn_ref[...]
    rows_ref[...] = jnp.zeros_like(rows_ref)
    for i in range(h):
        sl = slice(i * HEAD_DIM, (i + 1) * HEAD_DIM)
        qp = p_ref[:, sl]
        fp = p_ref[:, hk + i * HEAD_DIM:hk + (i + 1) * HEAD_DIM]
        lb = lb_ref[i:i + 1, :]
        forget = lb + (1.0 - lb) * _sigmoid(fp)
        rows_ref[3 * i:3 * i + 1, :] = qp * _sigmoid(qp)
        rows_ref[3 * i + 1:3 * i + 2, :] = forget
        rows_ref[3 * i + 2:3 * i + 3, :] = 1.0 - forget
    cols = rows_ref[...].T
    for i in range(h):
        sl = slice(i * HEAD_DIM, (i + 1) * HEAD_DIM)
        v = p_ref[:, 2 * hk + i * HEAD_DIM:2 * hk + (i + 1) * HEAD_DIM]
        gp = p_ref[:, 3 * hk + i * HEAD_DIM:3 * hk + (i + 1) * HEAD_DIM]
        q_c = cols[:, 3 * i:3 * i + 1]
        f_c = cols[:, 3 * i + 1:3 * i + 2]
        k_c = cols[:, 3 * i + 2:3 * i + 3]
        s_new = f_c * s_ref[i] + k_c * v
        so_ref[i] = s_new
        o = jnp.sum(q_c * s_new, axis=0, keepdims=True)
        o = o * lax.rsqrt(jnp.mean(o * o, axis=-1, keepdims=True) + RMS_EPS) * gn
        o_ref[:, sl] = (o * (gp * _sigmoid(gp))).astype(o_ref.dtype)


def _hgrn_step(proj, state, layer, lb, g_norm):
    ns = proj.shape[0]
    h = A_HEADS
    sspec = pl.BlockSpec((None, h, HEAD_DIM, HEAD_DIM), lambda b: (b, 0, 0, 0))
    return pl.pallas_call(
        _hgrn_step_kernel,
        grid=(ns,),
        in_specs=[pl.BlockSpec((None, 1, proj.shape[2]), lambda b: (b, 0, 0)),
                  pl.BlockSpec((None, None, h, HEAD_DIM, HEAD_DIM), lambda b: (layer, b, 0, 0, 0)),
                  pl.BlockSpec((h, HEAD_DIM), lambda b: (0, 0)),
                  pl.BlockSpec((1, HEAD_DIM), lambda b: (0, 0))],
        out_specs=[pl.BlockSpec((None, 1, h * HEAD_DIM), lambda b: (b, 0, 0)), sspec],
        out_shape=[jax.ShapeDtypeStruct((ns, 1, h * HEAD_DIM), BF16),
                   jax.ShapeDtypeStruct((ns, h, HEAD_DIM, HEAD_DIM), F32)],
        scratch_shapes=[pltpu.VMEM((HEAD_DIM, HEAD_DIM), F32)],
        compiler_params=_cparams(("parallel",)),
        name="hgrn_step",
    )(proj, state, lb.reshape(h, HEAD_DIM), g_norm.reshape(1, HEAD_DIM))


def _ffn_kernel(be_ref, nb_ref, x_ref, wg_ref, wu_ref, wd_ref, o_ref, wgb_ref, wub_ref, wdb_ref):
    i = pl.program_id(0)

    @pl.when(i < nb_ref[0])
    def _():
        @pl.when((i == 0) | (be_ref[i] != be_ref[jnp.maximum(i - 1, 0)]))
        def _():
            wgb_ref[...] = wg_ref[...].astype(BF16)
            wub_ref[...] = wu_ref[...].astype(BF16)
            wdb_ref[...] = wd_ref[...].astype(BF16)

        x = x_ref[...]
        hg = _dot(x, wgb_ref[...])
        hu = _dot(x, wub_ref[...])
        hid = hg * _sigmoid(hg) * hu
        o_ref[...] = _dot(hid.astype(BF16), wdb_ref[...]).astype(o_ref.dtype)

    @pl.when(i >= nb_ref[0])
    def _():
        o_ref[...] = jnp.zeros_like(o_ref)


def _expert_ffn(x, blk_exp, n_used, w_gate, w_up, w_down, layer, tm, name):
    r, d = x.shape
    ff = w_gate.shape[3]
    return pl.pallas_call(
        _ffn_kernel,
        grid_spec=pltpu.PrefetchScalarGridSpec(
            num_scalar_prefetch=2,
            grid=(r // tm,),
            in_specs=[pl.BlockSpec((tm, d), lambda i, be, nb: (i, 0)),
                      pl.BlockSpec((None, None, d, ff), lambda i, be, nb: (layer, be[i], 0, 0)),
                      pl.BlockSpec((None, None, d, ff), lambda i, be, nb: (layer, be[i], 0, 0)),
                      pl.BlockSpec((None, None, ff, d), lambda i, be, nb: (layer, be[i], 0, 0))],
            out_specs=pl.BlockSpec((tm, d), lambda i, be, nb: (i, 0)),
            scratch_shapes=[pltpu.VMEM((d, ff), BF16), pltpu.VMEM((d, ff), BF16), pltpu.VMEM((ff, d), BF16)],
        ),
        out_shape=jax.ShapeDtypeStruct((r, d), BF16),
        compiler_params=_cparams(("arbitrary",), V7X_VMEM_LIMIT_BYTES),
        name=name,
    )(blk_exp, n_used, x, w_gate, w_up, w_down)


def _first_argmax(val, idx, sentinel):
    m = jnp.max(val, axis=0, keepdims=True)
    i = jnp.min(jnp.where(val == m, idx, sentinel), axis=0, keepdims=True)
    return m, i


def _route_kernel(x_ref, wt_ref, b_ref, e_ref, g_ref, c_ref):
    x = x_ref[...]
    w = wt_ref[...]
    t = x.shape[0]
    xh = x.astype(BF16)
    xl = (x - xh.astype(F32)).astype(BF16)
    wh = w.astype(BF16)
    wl = (w - wh.astype(F32)).astype(BF16)
    logits = _dot_nt(wh, xh) + (_dot_nt(wl, xh) + _dot_nt(wh, xl))
    scores = _sigmoid(logits)
    biased = scores + b_ref[...]
    gs = N_EXPERTS // N_GROUPS
    neg = -jnp.inf
    eid = lax.broadcasted_iota(jnp.int32, (N_EXPERTS, t), 0)
    sub = lax.broadcasted_iota(jnp.int32, (gs, t), 0)
    grow = lax.broadcasted_iota(jnp.int32, (N_GROUPS, t), 0)
    grp = jnp.zeros((N_GROUPS, t), F32)
    for g in range(N_GROUPS):
        bg = biased[g * gs:(g + 1) * gs]
        m1, i1 = _first_argmax(bg, sub, gs)
        m2 = jnp.max(jnp.where(sub == i1, neg, bg), axis=0, keepdims=True)
        grp = jnp.where(grow == g, m1 + m2, grp)
    chosen = jnp.zeros((N_GROUPS, t), F32)
    for _ in range(TOPK_GROUPS):
        _, gi = _first_argmax(grp, grow, N_GROUPS)
        hit = grow == gi
        chosen = jnp.where(hit, 1.0, chosen)
        grp = jnp.where(hit, neg, grp)
    chosen_e = jnp.concatenate([jnp.broadcast_to(chosen[g:g + 1], (gs, t)) for g in range(N_GROUPS)], axis=0)
    masked = jnp.where(chosen_e > 0.0, biased, neg)
    krow = lax.broadcasted_iota(jnp.int32, (TOP_K, t), 0)
    e_out = jnp.zeros((TOP_K, t), jnp.int32)
    g_out = jnp.zeros((TOP_K, t), F32)
    member = jnp.zeros((N_EXPERTS, t), F32)
    for k in range(TOP_K):
        _, idx = _first_argmax(masked, eid, N_EXPERTS)
        hit = eid == idx
        gk = jnp.sum(jnp.where(hit, scores, 0.0), axis=0, keepdims=True)
        masked = jnp.where(hit, neg, masked)
        member = jnp.where(hit, 1.0, member)
        e_out = jnp.where(krow == k, idx, e_out)
        g_out = jnp.where(krow == k, gk, g_out)
    g_out = g_out / jnp.sum(g_out, axis=0, keepdims=True) * ROUTED_SCALE
    e_ref[...] = e_out
    g_ref[...] = g_out
    c_ref[...] = jnp.sum(member, axis=1, keepdims=True).astype(jnp.int32)


def _route(x, router_w, layer, router_b, tm):
    n, d = x.shape
    e = router_w.shape[2]
    wt = jnp.swapaxes(router_w, 1, 2)
    return pl.pallas_call(
        _route_kernel,
        grid=(n // tm,),
        in_specs=[pl.BlockSpec((tm, d), lambda i: (i, 0)),
                  pl.BlockSpec((None, e, d), lambda i: (layer, 0, 0)),
                  pl.BlockSpec((e, 1), lambda i: (0, 0))],
        out_specs=[pl.BlockSpec((TOP_K, tm), lambda i: (0, i)),
                   pl.BlockSpec((TOP_K, tm), lambda i: (0, i)),
                   pl.BlockSpec((None, e, 1), lambda i: (i, 0, 0))],
        out_shape=[jax.ShapeDtypeStruct((TOP_K, n), jnp.int32), jax.ShapeDtypeStruct((TOP_K, n), F32),
                   jax.ShapeDtypeStruct((n // tm, e, 1), jnp.int32)],
        compiler_params=_cparams(("parallel",), V7X_VMEM_LIMIT_BYTES),
        name="route",
    )(x, wt, router_b[layer].astype(F32).reshape(e, 1))


def _slots_kernel(e_ref, base_ref, o_ref):
    e = e_ref[...]
    t = e.shape[1]
    eid = lax.broadcasted_iota(jnp.int32, (N_EXPERTS, t), 0)
    member = jnp.zeros((N_EXPERTS, t), F32)
    for k in range(TOP_K):
        member = jnp.where(eid == e[k:k + 1], 1.0, member)
    before = (lax.broadcasted_iota(jnp.int32, (t, t), 0) < lax.broadcasted_iota(jnp.int32, (t, t), 1))
    rank = _dot(member.astype(BF16), jnp.where(before, 1.0, 0.0).astype(BF16))
    slot = rank + base_ref[...].astype(F32)
    krow = lax.broadcasted_iota(jnp.int32, (TOP_K, t), 0)
    out = jnp.zeros((TOP_K, t), F32)
    for k in range(TOP_K):
        sk = jnp.sum(jnp.where(eid == e[k:k + 1], slot, 0.0), axis=0, keepdims=True)
        out = jnp.where(krow == k, sk, out)
    o_ref[...] = out.astype(jnp.int32)


def _dispatch_plan(e_idx, counts, tm, tile):
    k, n = e_idx.shape
    n_blocks = -(-(n * k) // tm) + N_EXPERTS
    counts = counts.reshape(n // tile, N_EXPERTS)
    total = jnp.sum(counts, axis=0)
    padded = (total + tm - 1) // tm * tm
    pad_end = jnp.cumsum(padded)
    tile_base = (pad_end - padded)[None, :] + jnp.cumsum(counts, axis=0) - counts
    slot_of = pl.pallas_call(
        _slots_kernel,
        grid=(n // tile,),
        in_specs=[pl.BlockSpec((k, tile), lambda i: (0, i)),
                  pl.BlockSpec((None, N_EXPERTS, 1), lambda i: (i, 0, 0))],
        out_specs=pl.BlockSpec((k, tile), lambda i: (0, i)),
        out_shape=jax.ShapeDtypeStruct((k, n), jnp.int32),
        compiler_params=_cparams(("parallel",)),
        name="slots",
    )(e_idx, tile_base.astype(jnp.int32).reshape(n // tile, N_EXPERTS, 1))
    blk_exp = jnp.minimum(jnp.searchsorted(pad_end, jnp.arange(n_blocks, dtype=jnp.int32) * tm, side='right'),
                          N_EXPERTS - 1).astype(jnp.int32)
    n_used = (pad_end[-1:] // tm).astype(jnp.int32)
    return slot_of, blk_exp, n_used, n_blocks


MOE_BLOCK = 256
ROW_TILE = 640


def _moe_postnorm_kernel(x_ref, y_ref, gate_ref, sh_ref, g_ref, b_ref, of_ref, ob_ref, *, alpha):
    gate = gate_ref[...]
    y = sh_ref[...].astype(F32)
    for k in range(TOP_K):
        y = y + gate[:, k:k + 1] * y_ref[k].astype(F32)
    z = alpha * x_ref[...] + y
    mu = jnp.mean(z, axis=-1, keepdims=True)
    zc = z - mu
    var = jnp.mean(zc * zc, axis=-1, keepdims=True)
    out = zc * lax.rsqrt(var + LN_EPS) * g_ref[...] + b_ref[...]
    of_ref[...] = out
    ob_ref[...] = out.astype(BF16)


def _moe_postnorm(x, y_tok, gate, shared, g, b, alpha, tm):
    m, d = x.shape
    row = pl.BlockSpec((tm, d), lambda i: (i, 0))
    vec = pl.BlockSpec((1, d), lambda i: (0, 0))
    return pl.pallas_call(
        functools.partial(_moe_postnorm_kernel, alpha=alpha),
        grid=(m // tm,),
        in_specs=[row, pl.BlockSpec((TOP_K, tm, d), lambda i: (0, i, 0)),
                  pl.BlockSpec((tm, TOP_K), lambda i: (i, 0)), row, vec, vec],
        out_specs=[row, row],
        out_shape=[jax.ShapeDtypeStruct((m, d), F32), jax.ShapeDtypeStruct((m, d), BF16)],
        compiler_params=_cparams(("parallel",), V7X_VMEM_LIMIT_BYTES),
        name="postnorm_moe",
    )(x, y_tok, gate, shared, g.reshape(1, d), b.reshape(1, d))


def _moe(xf, xb, layer, router_w, router_b, w_gate, w_up, w_down, sw_gate, sw_up, sw_down, ln_g, ln_b, alpha):
    n, d = xf.shape
    e_idx, gate, counts = _route(xf, router_w, layer, router_b, ROW_TILE)
    slot_of, blk_exp, n_used, n_blocks = _dispatch_plan(e_idx, counts, MOE_BLOCK, ROW_TILE)
    flat_slot = slot_of.reshape(-1)
    tok = jnp.tile(jnp.arange(n, dtype=jnp.int32), TOP_K)
    slot_tok = jnp.full((n_blocks * MOE_BLOCK,), n, jnp.int32).at[flat_slot].set(tok)
    x_pad = jnp.concatenate([xb, jnp.zeros((1, d), xb.dtype)], axis=0)
    x_sorted = jnp.take(x_pad, slot_tok, axis=0)
    y_sorted = _expert_ffn(x_sorted, blk_exp, n_used, w_gate, w_up, w_down, layer, MOE_BLOCK, "routed_ffn")
    y_tok = jnp.take(y_sorted, flat_slot, axis=0).reshape(TOP_K, n, d)
    n_sh = n // ROW_TILE
    shared = _expert_ffn(xb, jnp.zeros((n_sh,), jnp.int32), jnp.full((1,), n_sh, jnp.int32),
                         sw_gate[:, None], sw_up[:, None], sw_down[:, None], layer, ROW_TILE, "shared_ffn")
    return _moe_postnorm(xf, y_tok, gate.T, shared, ln_g, ln_b, alpha, ROW_TILE // 4)


def kernel(x_prompt, x_sample, state_hgrn, cache_win_k, cache_win_v, cache_mem_k, cache_mem_v, mem_prompt,
           w_in_a, lb_logits, g_norm_a, w_out_a, w_in_b, w_out_b, w_kv_shared, w_mem_kv, ln_g, ln_b,
           router_w, router_b, exp_w_gate, exp_w_up, exp_w_down, sh_w_gate, sh_w_up, sh_w_down):
    bp, sp, d = x_prompt.shape
    ns = x_sample.shape[0]
    assert x_sample.shape[1] == 1
    depth = ln_g.shape[0]
    n_a = w_in_a.shape[0]
    alpha = (2 * depth) ** 0.25
    n_p = bp * sp
    mem_len = mem_prompt.shape[1]
    a_mix = 4 * A_HEADS * HEAD_DIM

    xf = jnp.concatenate([x_prompt.reshape(n_p, d), x_sample.reshape(ns, d)], axis=0)
    xb = xf.astype(BF16)
    lower_bounds = jnp.cumsum(jax.nn.softmax(lb_logits.astype(F32), axis=0), axis=0)
    mem_flat = mem_prompt.reshape(bp * mem_len, d)
    pos_all = jnp.concatenate([jnp.tile(jnp.arange(sp, dtype=jnp.int32), bp),
                               jnp.full((ns,), PAST_LEN, jnp.int32)])
    tables = _rope_tables(pos_all)

    hgrn_p, hgrn_s, mem_k_p, mem_v_p = [], [], [], []
    for layer in range(depth):
        kvm = _proj([mem_flat], w_mem_kv, layer, F32, mem_len, 512, "mem_kv")
        mem_k_p.append(kvm[:, :MEM_W].reshape(bp, mem_len, MEM_HEADS, HEAD_DIM))
        mem_v_p.append(kvm[:, MEM_W:].reshape(bp, mem_len, MEM_HEADS, HEAD_DIM))
        if layer < n_a:
            a = layer
            proj = _proj([xb], w_in_a, a, F32, ROW_TILE, 512, "proj_in_a")
            o_x_p, st_p = _hgrn_prompt(proj, lower_bounds[a], g_norm_a[a], bp, sp)
            o_m_p = _mem_attn_prompt(proj, a_mix // MEM_W, kvm, bp, sp, 512)
            proj_s = proj[n_p:]
            o_x_s, st_s = _hgrn_step(proj_s.reshape(ns, 1, -1), state_hgrn, a, lower_bounds[a], g_norm_a[a])
            o_m_s = _mem_attn_decode(proj_s[:, a_mix:].reshape(ns, MEM_HEADS, HEAD_DIM), cache_mem_k, cache_mem_v, layer)
            hgrn_p.append(st_p)
            hgrn_s.append(st_s)
            w_out, w_out_layer = w_out_a, a
        else:
            bl = layer - n_a
            if layer == n_a:
                kv = _proj([xb], w_kv_shared[None], 0, F32, ROW_TILE, 512, "proj_kv")
                k_r = _rope(kv, B_SLOTS, tables, F32, ROW_TILE, "rope_k")
                k_p = k_r[:n_p].reshape(bp, sp, B_SLOTS, HEAD_DIM)
                v_p = kv[:n_p, KV_W:].reshape(bp, sp, B_SLOTS, HEAD_DIM)
                k_s = k_r[n_p:].reshape(ns, B_SLOTS, HEAD_DIM)
                v_s = kv[n_p:, KV_W:].reshape(ns, B_SLOTS, HEAD_DIM)
            proj = _proj([xb], w_in_b, bl, F32, ROW_TILE, 512, "proj_in_b")
            q_r = _rope(proj, B_QHEADS, tables, F32, ROW_TILE, "rope_q")
            o_x_p = _dil_attn_prompt(q_r, k_r, kv, bp, sp)
            o_m_p = _mem_attn_prompt(proj, B_QHEADS * HEAD_DIM // MEM_W, kvm, bp, sp, 512)
            q_s = q_r[n_p:].astype(F32).reshape(ns, B_GROUPS, B_SLOTS, HEAD_DIM)
            o_x_s = _dil_attn_decode(q_s, k_s, v_s, cache_win_k, cache_win_v)
            o_m_s = _mem_attn_decode(proj[n_p:, B_QHEADS * HEAD_DIM:].reshape(ns, MEM_HEADS, HEAD_DIM),
                                     cache_mem_k, cache_mem_v, layer)
            w_out, w_out_layer = w_out_b, bl
        o_x = jnp.concatenate([o_x_p, o_x_s.reshape(ns, -1)], axis=0)
        o_m = jnp.concatenate([o_m_p, o_m_s.reshape(ns, -1)], axis=0)
        y = _proj([o_x, o_m], w_out, w_out_layer, F32, ROW_TILE, 512, "proj_out")
        xf, xb = _postnorm(xf, [y], ln_g[layer, 0], ln_b[layer, 0], alpha, ROW_TILE // 2, "postnorm_mix")
        xf, xb = _moe(xf, xb, layer, router_w, router_b, exp_w_gate, exp_w_up, exp_w_down,
                      sh_w_gate, sh_w_up, sh_w_down, ln_g[layer, 1], ln_b[layer, 1], alpha)

    w_p = min(max(w for w, _ in B_PATTERNS), sp)
    return (xf[:n_p].reshape(bp, sp, d), xf[n_p:].reshape(ns, 1, d),
            jnp.stack(hgrn_p), jnp.stack(hgrn_s),
            k_p[:, sp - w_p:], v_p[:, sp - w_p:],
            k_s.reshape(ns, 1, B_SLOTS, HEAD_DIM), v_s.reshape(ns, 1, B_SLOTS, HEAD_DIM),
            jnp.stack(mem_k_p), jnp.stack(mem_v_p))
```

```python
import functools

import jax
import jax.numpy as jnp
from jax import lax
from jax.experimental import pallas as pl
from jax.experimental.pallas import tpu as pltpu
from jax.experimental.pallas import tpu_sc as plsc

F32 = jnp.float32
BF16 = jnp.bfloat16

HEAD_DIM = 128
A_HEADS = 12
A_CHUNK = 64
A_SUB = 16
B_PATTERNS = ((128, 1), (512, 4), (2048, 16))
B_SLOTS = 4
B_GROUPS = len(B_PATTERNS)
B_QHEADS = B_GROUPS * B_SLOTS
B_BLOCK = 128
MEM_HEADS = 4
MEM_W = MEM_HEADS * HEAD_DIM
KV_W = B_SLOTS * HEAD_DIM
ROPE_THETA = 500000.0
ROPE_DIM = HEAD_DIM // 4
N_EXPERTS = 64
N_GROUPS = 8
TOPK_GROUPS = 4
TOP_K = 8
ROUTED_SCALE = 2.5
LN_EPS = 1e-5
RMS_EPS = 1e-6
ATTN_SCALE = HEAD_DIM ** -0.5
PAST_LEN = 2048

V7X_VMEM_LIMIT_BYTES = 56 * 1024 * 1024
LANES = 128
NEG_BIG = -1e30


def _cparams(sem, vmem=None):
    return pltpu.CompilerParams(dimension_semantics=sem, vmem_limit_bytes=vmem)


def _dot(a, b):
    return jnp.dot(a, b, preferred_element_type=F32)


def _dot_nt(a, b):
    return lax.dot_general(a, b, (((1,), (1,)), ((), ())), preferred_element_type=F32)


def _sigmoid(x):
    return 1.0 / (1.0 + jnp.exp(-x))


def _proj_kernel(*refs, n_lhs):
    x_refs = refs[:n_lhs]
    w_refs = refs[n_lhs:2 * n_lhs]
    o_ref = refs[2 * n_lhs]
    wb_refs = refs[2 * n_lhs + 1:]

    @pl.when(pl.program_id(1) == 0)
    def _():
        for w_ref, wb_ref in zip(w_refs, wb_refs):
            wb_ref[...] = w_ref[...].astype(BF16)

    acc = None
    for x_ref, wb_ref in zip(x_refs, wb_refs):
        d = _dot(x_ref[...].astype(BF16), wb_ref[...])
        acc = d if acc is None else acc + d
    o_ref[...] = acc.astype(o_ref.dtype)


def _proj(lhs, w, layer, out_dtype, tm, tn, name):
    m = lhs[0].shape[0]
    n = w.shape[2]
    koff = 0
    in_specs, w_specs, scratch = [], [], []
    for x in lhs:
        k = x.shape[1]
        assert koff % k == 0 and m % tm == 0 and n % tn == 0
        in_specs.append(pl.BlockSpec((tm, k), lambda j, i: (i, 0)))
        w_specs.append(pl.BlockSpec((None, k, tn), functools.partial(lambda j, i, kb: (layer, kb, j), kb=koff // k)))
        scratch.append(pltpu.VMEM((k, tn), BF16))
        koff += k
    assert koff == w.shape[1]
    return pl.pallas_call(
        functools.partial(_proj_kernel, n_lhs=len(lhs)),
        grid=(n // tn, m // tm),
        in_specs=in_specs + w_specs,
        out_specs=pl.BlockSpec((tm, tn), lambda j, i: (i, j)),
        out_shape=jax.ShapeDtypeStruct((m, n), out_dtype),
        scratch_shapes=scratch,
        compiler_params=_cparams(("arbitrary", "arbitrary"), V7X_VMEM_LIMIT_BYTES),
        name=name,
    )(*lhs, *([w] * len(lhs)))


def _pack_halves(x):
    c = x.shape[1] // 2
    lo = pltpu.bitcast(x[:, :c].astype(BF16).astype(F32), jnp.int32)
    hi = pltpu.bitcast(x[:, c:].astype(BF16).astype(F32), jnp.int32)
    return hi | lax.shift_right_logical(lo, 16)


def _unpack_halves(w):
    lo = pltpu.bitcast(lax.shift_left(w, 16), F32)
    hi = pltpu.bitcast(w & jnp.int32(-65536), F32)
    return lo, hi


def _layer_norm(z, g, b):
    mu = jnp.mean(z, axis=-1, keepdims=True)
    zc = z - mu
    var = jnp.mean(zc * zc, axis=-1, keepdims=True)
    return zc * lax.rsqrt(var + LN_EPS) * g + b


def _postnorm_kernel(x_ref, y_ref, g_ref, b_ref, of_ref, op_ref, *, alpha):
    out = _layer_norm(alpha * x_ref[...] + y_ref[...], g_ref[...], b_ref[...])
    of_ref[...] = out
    op_ref[...] = _pack_halves(out)


def _postnorm(x, y, g, b, alpha, tm, name):
    m, d = x.shape
    row = pl.BlockSpec((tm, d), lambda i: (i, 0))
    vec = pl.BlockSpec((1, d), lambda i: (0, 0))
    return pl.pallas_call(
        functools.partial(_postnorm_kernel, alpha=alpha),
        grid=(m // tm,),
        in_specs=[row, row, vec, vec],
        out_specs=[row, pl.BlockSpec((tm, d // 2), lambda i: (i, 0))],
        out_shape=[jax.ShapeDtypeStruct((m, d), F32), jax.ShapeDtypeStruct((m, d // 2), jnp.int32)],
        compiler_params=_cparams(("parallel",)),
        name=name,
    )(x, y, g.reshape(1, d), b.reshape(1, d))


def _hgrn_prompt_kernel(q_ref, f_ref, v_ref, gate_ref, lb_ref, gn_ref, o_ref, s_ref, st_ref, *, seq, heads):
    c = A_CHUNK
    n_chunks = seq // c
    n_sub = c // A_SUB
    gn = gn_ref[...]
    row = lax.broadcasted_iota(jnp.int32, (c, c), 0)
    col = lax.broadcasted_iota(jnp.int32, (c, c), 1)
    tril = jnp.where(row >= col, 1.0, 0.0).astype(BF16)
    same_sub = (row // A_SUB) == (col // A_SUB)
    diag_dist = jnp.where(same_sub, row - col, -1)
    off_mask = col < (row // A_SUB) * A_SUB
    st_ref[...] = jnp.zeros_like(st_ref)

    def chunk(ci, carry):
        for hh in range(heads):
            head_chunk(ci, hh)
        return carry

    def head_chunk(ci, hh):
        r0 = pl.multiple_of(ci * c, c)
        hsl = slice(hh * HEAD_DIM, (hh + 1) * HEAD_DIM)
        lb = lb_ref[hh]
        qp = q_ref[pl.ds(r0, c), hsl]
        fp = f_ref[pl.ds(r0, c), hsl]
        v = v_ref[pl.ds(r0, c), hsl]
        gp = gate_ref[pl.ds(r0, c), hsl]
        st = st_ref[hh]
        q = qp * _sigmoid(qp)
        forget = lb + (1.0 - lb) * _sigmoid(fp)
        logf = jnp.log2(forget)
        k = 1.0 - forget
        hi = logf.astype(BF16)
        r1 = logf - hi.astype(F32)
        mid = r1.astype(BF16)
        lo = (r1 - mid.astype(F32)).astype(BF16)
        g = _dot(tril, hi) + _dot(tril, mid) + _dot(tril, lo)
        v_b = v.astype(BF16)
        o = _dot_nt((q * jnp.exp2(g)).astype(BF16), st.astype(BF16))
        rows = [jnp.zeros((A_SUB, c), F32)]
        for i in range(1, n_sub):
            gref = g[i * A_SUB:i * A_SUB + 1, :]
            qt = q[i * A_SUB:(i + 1) * A_SUB, :] * jnp.exp2(g[i * A_SUB:(i + 1) * A_SUB, :] - gref)
            kt = k * jnp.exp2(jnp.minimum(gref - g, 0.0))
            rows.append(_dot_nt(qt.astype(BF16), kt.astype(BF16)))
        a = jnp.where(off_mask, jnp.concatenate(rows, axis=0), 0.0)
        for d in range(A_SUB):
            kr = k if d == 0 else pltpu.roll(k, d, axis=0)
            gr = g if d == 0 else pltpu.roll(g, d, axis=0)
            x = q * kr * jnp.exp2(g - gr)
            a = jnp.where(diag_dist == d, jnp.sum(x, axis=-1, keepdims=True), a)
        o = o + _dot(a.astype(BF16), v_b)
        gend = g[c - 1:c, :]
        kt_end = k * jnp.exp2(gend - g)
        st_ref[hh] = jnp.exp2(gend) * st + _dot(v_b.T, kt_end.astype(BF16))
        o = o * lax.rsqrt(jnp.mean(o * o, axis=-1, keepdims=True) + RMS_EPS) * gn
        o_ref[pl.ds(r0, c), hsl] = (o * (gp * _sigmoid(gp))).astype(o_ref.dtype)

    lax.fori_loop(0, n_chunks, chunk, 0)
    for hh in range(heads):
        s_ref[hh] = st_ref[hh].T


A_HEADS_PER_STEP = 4


def _hgrn_prompt(proj, lb, g_norm, n_batch, seq):
    h = A_HEADS
    hp = A_HEADS_PER_STEP
    assert h % hp == 0
    ng = h // hp
    w = hp * HEAD_DIM
    blk = lambda off: pl.BlockSpec((seq, w), functools.partial(lambda b, hg, off: (b, off + hg), off=off))
    return pl.pallas_call(
        functools.partial(_hgrn_prompt_kernel, seq=seq, heads=hp),
        grid=(n_batch, ng),
        in_specs=[blk(0), blk(ng), blk(2 * ng), blk(3 * ng),
                  pl.BlockSpec((hp, 1, HEAD_DIM), lambda b, hg: (hg, 0, 0)),
                  pl.BlockSpec((1, HEAD_DIM), lambda b, hg: (0, 0))],
        out_specs=[pl.BlockSpec((seq, w), lambda b, hg: (b, hg)),
                   pl.BlockSpec((None, hp, HEAD_DIM, HEAD_DIM), lambda b, hg: (b, hg, 0, 0))],
        out_shape=[jax.ShapeDtypeStruct((n_batch * seq, h * HEAD_DIM), BF16),
                   jax.ShapeDtypeStruct((n_batch, h, HEAD_DIM, HEAD_DIM), F32)],
        scratch_shapes=[pltpu.VMEM((hp, HEAD_DIM, HEAD_DIM), F32)],
        compiler_params=_cparams(("parallel", "parallel"), V7X_VMEM_LIMIT_BYTES),
        name="hgrn_prompt",
    )(proj, proj, proj, proj, lb.reshape(h, 1, HEAD_DIM), g_norm.reshape(1, HEAD_DIM))


def _softmax_av(s, v):
    m = jnp.max(s, axis=-1, keepdims=True)
    p = jnp.exp(s - m)
    l = jnp.sum(p, axis=-1, keepdims=True)
    return _dot(p.astype(BF16), v) / l


def _mem_attn_kernel(q_ref, k_ref, v_ref, o_ref):
    for h in range(MEM_HEADS):
        sl = slice(h * HEAD_DIM, (h + 1) * HEAD_DIM)
        s = _dot_nt(q_ref[:, sl].astype(BF16), k_ref[:, sl].astype(BF16)) * ATTN_SCALE
        o_ref[:, sl] = _softmax_av(s, v_ref[:, sl].astype(BF16)).astype(o_ref.dtype)


def _mem_attn_prompt(proj, q_col, kv, n_batch, seq, tq):
    mem_len = kv.shape[0] // n_batch
    nq = seq // tq
    return pl.pallas_call(
        _mem_attn_kernel,
        grid=(n_batch, nq),
        in_specs=[pl.BlockSpec((tq, MEM_W), lambda b, i: (b * nq + i, q_col)),
                  pl.BlockSpec((mem_len, MEM_W), lambda b, i: (b, 0)),
                  pl.BlockSpec((mem_len, MEM_W), lambda b, i: (b, 1))],
        out_specs=pl.BlockSpec((tq, MEM_W), lambda b, i: (b * nq + i, 0)),
        out_shape=jax.ShapeDtypeStruct((n_batch * seq, MEM_W), BF16),
        compiler_params=_cparams(("parallel", "parallel")),
        name="mem_attn_prompt",
    )(proj, kv, kv)


def _rope_tables(pos):
    half = ROPE_DIM // 2
    inv_freq = ROPE_THETA ** (-jnp.arange(0, ROPE_DIM, 2, dtype=F32) / ROPE_DIM)
    ang = pos.astype(F32)[:, None] * inv_freq[None, :]
    cos, sin = jnp.cos(ang), jnp.sin(ang)
    n = pos.shape[0]
    one = jnp.ones((n, HEAD_DIM - ROPE_DIM), F32)
    zero = jnp.zeros((n, HEAD_DIM - half), F32)
    c = jnp.concatenate([cos, cos, one], axis=1)
    s_up = jnp.concatenate([-sin, zero], axis=1)
    s_dn = jnp.concatenate([jnp.zeros((n, half), F32), sin, zero[:, half:]], axis=1)
    return c, s_up, s_dn


def _rope_kernel(x_ref, c_ref, su_ref, sd_ref, o_ref, *, n_heads):
    half = ROPE_DIM // 2
    c, su, sd = c_ref[...], su_ref[...], sd_ref[...]
    for h in range(n_heads):
        sl = slice(h * HEAD_DIM, (h + 1) * HEAD_DIM)
        x = x_ref[:, sl]
        up = pltpu.roll(x, HEAD_DIM - half, axis=1)
        dn = pltpu.roll(x, half, axis=1)
        o_ref[:, sl] = (x * c + up * su + dn * sd).astype(o_ref.dtype)


def _rope(x, n_heads, tables, out_dtype, tm, name):
    m = x.shape[0]
    w = n_heads * HEAD_DIM
    tab = pl.BlockSpec((tm, HEAD_DIM), lambda i: (i, 0))
    return pl.pallas_call(
        functools.partial(_rope_kernel, n_heads=n_heads),
        grid=(m // tm,),
        in_specs=[pl.BlockSpec((tm, w), lambda i: (i, 0)), tab, tab, tab],
        out_specs=pl.BlockSpec((tm, w), lambda i: (i, 0)),
        out_shape=jax.ShapeDtypeStruct((m, w), out_dtype),
        compiler_params=_cparams(("parallel",)),
        name=name,
    )(x, *tables)


def _dil_prompt_kernel(*refs, seq):
    q_refs = refs[:B_GROUPS]
    k_ref, v_ref, o_ref, m_ref, l_ref, acc_ref = refs[B_GROUPS:]
    blk = B_BLOCK
    base = (lax.broadcasted_iota(jnp.int32, (blk, blk), 0) - lax.broadcasted_iota(jnp.int32, (blk, blk), 1))
    for g, (win, dil) in enumerate(B_PATTERNS):
        span = win // dil
        n_blk = seq // dil // blk
        assert span <= blk

        def body(it, carry, g=g, dil=dil, span=span, n_blk=n_blk):
            r = it % dil
            n = it // dil

            def rows(nn):
                start = nn * (blk * dil) + r
                return pl.ds(pl.multiple_of(start, blk), blk) if dil == 1 else pl.ds(start, blk, stride=dil)

            own = rows(n)
            q = q_refs[g][own, :].astype(BF16)
            s = jnp.where(base >= 0, _dot_nt(q, k_ref[own, :].astype(BF16)) * ATTN_SCALE, NEG_BIG)
            vv = v_ref[own, :].astype(BF16)
            if n_blk > 1:
                prev = rows(jnp.maximum(n - 1, 0))
                reach = jnp.where(n > 0, span - blk, -blk - 1)
                s_prev = jnp.where(base <= reach, _dot_nt(q, k_ref[prev, :].astype(BF16)) * ATTN_SCALE, NEG_BIG)
                s = jnp.concatenate([s_prev, s], axis=1)
                vv = jnp.concatenate([v_ref[prev, :].astype(BF16), vv], axis=0)
            m_b = jnp.max(s, axis=-1, keepdims=True)
            p = jnp.exp(s - m_b)
            l_b = jnp.sum(p, axis=-1, keepdims=True)
            acc_b = _dot(p.astype(BF16), vv)
            if g == 0:
                m_ref[own, :] = jnp.broadcast_to(m_b, (blk, HEAD_DIM))
                l_ref[own, :] = jnp.broadcast_to(l_b, (blk, HEAD_DIM))
                acc_ref[own, :] = acc_b
            else:
                m_old = m_ref[own, :]
                m_new = jnp.maximum(m_old, m_b)
                a_old = jnp.exp(m_old - m_new)
                a_b = jnp.exp(m_b - m_new)
                m_ref[own, :] = m_new
                l_ref[own, :] = l_ref[own, :] * a_old + l_b * a_b
                acc_ref[own, :] = acc_ref[own, :] * a_old + acc_b * a_b
            return carry

        lax.fori_loop(0, dil * n_blk, body, 0)
    o_ref[...] = (acc_ref[...] / l_ref[...]).astype(o_ref.dtype)


def _dil_attn_prompt(q, k, kv, n_batch, seq):
    col = lambda c0: pl.BlockSpec((seq, HEAD_DIM), functools.partial(lambda b, h, c0: (b, c0 + h), c0=c0))
    return pl.pallas_call(
        functools.partial(_dil_prompt_kernel, seq=seq),
        grid=(n_batch, B_SLOTS),
        in_specs=[col(g * B_SLOTS) for g in range(B_GROUPS)] + [col(0), col(B_SLOTS)],
        out_specs=col(0),
        out_shape=jax.ShapeDtypeStruct((n_batch * seq, KV_W), BF16),
        scratch_shapes=[pltpu.VMEM((seq, HEAD_DIM), F32)] * 3,
        compiler_params=_cparams(("parallel", "parallel"), V7X_VMEM_LIMIT_BYTES),
        name="dil_attn_prompt",
    )(*([q] * B_GROUPS), k, kv)


def _decode_scores(q, k):
    return jnp.sum(k * q[None], axis=-1, keepdims=True) * ATTN_SCALE


def _decode_attend(parts):
    m = None
    for s, _ in parts:
        sm = jnp.max(s, axis=0)
        m = sm if m is None else jnp.maximum(m, sm)
    l = jnp.zeros_like(m)
    acc = None
    for s, v in parts:
        p = jnp.exp(s - m[None])
        l = l + jnp.sum(p, axis=0)
        pv = jnp.sum(p * v, axis=0)
        acc = pv if acc is None else acc + pv
    return acc / l


def _mem_decode_kernel(q_ref, k_ref, v_ref, o_ref):
    o_ref[...] = _decode_attend([(_decode_scores(q_ref[...], k_ref[...]), v_ref[...])]).astype(o_ref.dtype)


def _mem_attn_decode(q, cache_k, cache_v, layer):
    ns, nh, hd = q.shape
    mem_len = cache_k.shape[2]
    cspec = pl.BlockSpec((None, None, mem_len, nh, hd), lambda b: (layer, b, 0, 0, 0))
    qspec = pl.BlockSpec((None, nh, hd), lambda b: (b, 0, 0))
    return pl.pallas_call(
        _mem_decode_kernel,
        grid=(ns,),
        in_specs=[qspec, cspec, cspec],
        out_specs=qspec,
        out_shape=jax.ShapeDtypeStruct((ns, nh, hd), BF16),
        compiler_params=_cparams(("parallel",)),
        name="mem_attn_decode",
    )(q, cache_k, cache_v)


def _dil_decode_kernel(q_ref, kn_ref, vn_ref, *refs):
    o_ref = refs[-1]
    kn = kn_ref[...][None]
    vn = vn_ref[...][None]
    parts = []
    for g in range(B_GROUPS):
        q = q_ref[g].astype(F32)
        parts.append((_decode_scores(q, refs[2 * g][...]), refs[2 * g + 1][...]))
        parts.append((_decode_scores(q, kn), vn))
    o_ref[...] = _decode_attend(parts).astype(o_ref.dtype)


def _dil_attn_decode(q, k_new, v_new, cache_k, cache_v):
    ns, w_buf, ns_slots, hd = cache_k.shape
    in_specs = [pl.BlockSpec((None, B_GROUPS, ns_slots, hd), lambda b: (b, 0, 0, 0)),
                pl.BlockSpec((None, ns_slots, hd), lambda b: (b, 0, 0)),
                pl.BlockSpec((None, ns_slots, hd), lambda b: (b, 0, 0))]
    args = [q, k_new, v_new]
    for win, dil in B_PATTERNS:
        span = win // dil
        assert w_buf % dil == 0 and (w_buf // dil) % span == 0
        view = (ns, w_buf // dil, dil, ns_slots, hd)
        last = w_buf // dil // span - 1
        spec = pl.BlockSpec((None, span, None, ns_slots, hd), functools.partial(lambda b, last: (b, last, 0, 0, 0), last=last))
        in_specs += [spec, spec]
        args += [cache_k.reshape(view), cache_v.reshape(view)]
    return pl.pallas_call(
        _dil_decode_kernel,
        grid=(ns,),
        in_specs=in_specs,
        out_specs=pl.BlockSpec((None, ns_slots, hd), lambda b: (b, 0, 0)),
        out_shape=jax.ShapeDtypeStruct((ns, ns_slots, hd), BF16),
        compiler_params=_cparams(("parallel",)),
        name="dil_attn_decode",
    )(*args)


def _hgrn_step_kernel(p_ref, s_ref, lb_ref, gn_ref, o_ref, so_ref, rows_ref):
    h = A_HEADS
    hk = h * HEAD_DIM
    gn = gn_ref[...]
    rows_ref[...] = jnp.zeros_like(rows_ref)
    for i in range(h):
        sl = slice(i * HEAD_DIM, (i + 1) * HEAD_DIM)
        qp = p_ref[:, sl]
        fp = p_ref[:, hk + i * HEAD_DIM:hk + (i + 1) * HEAD_DIM]
        lb = lb_ref[i:i + 1, :]
        forget = lb + (1.0 - lb) * _sigmoid(fp)
        rows_ref[3 * i:3 * i + 1, :] = qp * _sigmoid(qp)
        rows_ref[3 * i + 1:3 * i + 2, :] = forget
        rows_ref[3 * i + 2:3 * i + 3, :] = 1.0 - forget
    cols = rows_ref[...].T
    for i in range(h):
        sl = slice(i * HEAD_DIM, (i + 1) * HEAD_DIM)
        v = p_ref[:, 2 * hk + i * HEAD_DIM:2 * hk + (i + 1) * HEAD_DIM]
        gp = p_ref[:, 3 * hk + i * HEAD_DIM:3 * hk + (i + 1) * HEAD_DIM]
        q_c = cols[:, 3 * i:3 * i + 1]
        f_c = cols[:, 3 * i + 1:3 * i + 2]
        k_c = cols[:, 3 * i + 2:3 * i + 3]
        s_new = f_c * s_ref[i] + k_c * v
        so_ref[i] = s_new
        o = jnp.sum(q_c * s_new, axis=0, keepdims=True)
        o = o * lax.rsqrt(jnp.mean(o * o, axis=-1, keepdims=True) + RMS_EPS) * gn
        o_ref[:, sl] = (o * (gp * _sigmoid(gp))).astype(o_ref.dtype)


def _hgrn_step(proj, state, layer, lb, g_norm):
    ns = proj.shape[0]
    h = A_HEADS
    sspec = pl.BlockSpec((None, h, HEAD_DIM, HEAD_DIM), lambda b: (b, 0, 0, 0))
    return pl.pallas_call(
        _hgrn_step_kernel,
        grid=(ns,),
        in_specs=[pl.BlockSpec((None, 1, proj.shape[2]), lambda b: (b, 0, 0)),
                  pl.BlockSpec((None, None, h, HEAD_DIM, HEAD_DIM), lambda b: (layer, b, 0, 0, 0)),
                  pl.BlockSpec((h, HEAD_DIM), lambda b: (0, 0)),
                  pl.BlockSpec((1, HEAD_DIM), lambda b: (0, 0))],
        out_specs=[pl.BlockSpec((None, 1, h * HEAD_DIM), lambda b: (b, 0, 0)), sspec],
        out_shape=[jax.ShapeDtypeStruct((ns, 1, h * HEAD_DIM), BF16),
                   jax.ShapeDtypeStruct((ns, h, HEAD_DIM, HEAD_DIM), F32)],
        scratch_shapes=[pltpu.VMEM((HEAD_DIM, HEAD_DIM), F32)],
        compiler_params=_cparams(("parallel",)),
        name="hgrn_step",
    )(proj, state, lb.reshape(h, HEAD_DIM), g_norm.reshape(1, HEAD_DIM))


def _ffn_kernel(be_ref, nv_ref, x_ref, wg_ref, wu_ref, wd_ref, o_ref, wgb_ref, wub_ref, wdb_ref):
    i = pl.program_id(0)
    n_valid = nv_ref[i]

    @pl.when(n_valid > 0)
    def _():
        @pl.when((i == 0) | (be_ref[i] != be_ref[jnp.maximum(i - 1, 0)]))
        def _():
            wgb_ref[...] = wg_ref[...].astype(BF16)
            wub_ref[...] = wu_ref[...].astype(BF16)
            wdb_ref[...] = wd_ref[...].astype(BF16)

        lo, hi = _unpack_halves(x_ref[...])
        c = lo.shape[1]
        keep = lax.broadcasted_iota(jnp.int32, lo.shape, 0) < n_valid
        lo = jnp.where(keep, lo, 0.0).astype(BF16)
        hi = jnp.where(keep, hi, 0.0).astype(BF16)
        hg = _dot(lo, wgb_ref[:c, :]) + _dot(hi, wgb_ref[c:, :])
        hu = _dot(lo, wub_ref[:c, :]) + _dot(hi, wub_ref[c:, :])
        hid = hg * _sigmoid(hg) * hu
        o_ref[...] = _pack_halves(_dot(hid.astype(BF16), wdb_ref[...]))

    @pl.when(n_valid <= 0)
    def _():
        o_ref[...] = jnp.zeros_like(o_ref)


def _expert_ffn(x, blk_exp, blk_valid, w_gate, w_up, w_down, layer, tm, name):
    r = x.shape[0]
    d, ff = w_gate.shape[2:]
    return pl.pallas_call(
        _ffn_kernel,
        grid_spec=pltpu.PrefetchScalarGridSpec(
            num_scalar_prefetch=2,
            grid=(r // tm,),
            in_specs=[pl.BlockSpec((tm, d // 2), lambda i, be, nv: (i, 0)),
                      pl.BlockSpec((None, None, d, ff), lambda i, be, nv: (layer, be[i], 0, 0)),
                      pl.BlockSpec((None, None, d, ff), lambda i, be, nv: (layer, be[i], 0, 0)),
                      pl.BlockSpec((None, None, ff, d), lambda i, be, nv: (layer, be[i], 0, 0))],
            out_specs=pl.BlockSpec((tm, d // 2), lambda i, be, nv: (i, 0)),
            scratch_shapes=[pltpu.VMEM((d, ff), BF16), pltpu.VMEM((d, ff), BF16), pltpu.VMEM((ff, d), BF16)],
        ),
        out_shape=jax.ShapeDtypeStruct((r, d // 2), jnp.int32),
        compiler_params=_cparams(("arbitrary",), V7X_VMEM_LIMIT_BYTES),
        name=name,
    )(blk_exp, blk_valid, x, w_gate, w_up, w_down)


def _first_argmax(val, idx, sentinel):
    m = jnp.max(val, axis=0, keepdims=True)
    i = jnp.min(jnp.where(val == m, idx, sentinel), axis=0, keepdims=True)
    return m, i


def _route_kernel(x_ref, wt_ref, b_ref, e_ref, g_ref, c_ref):
    x = x_ref[...]
    w = wt_ref[...]
    t = x.shape[0]
    xh = x.astype(BF16)
    xl = (x - xh.astype(F32)).astype(BF16)
    wh = w.astype(BF16)
    wl = (w - wh.astype(F32)).astype(BF16)
    logits = _dot_nt(wh, xh) + (_dot_nt(wl, xh) + _dot_nt(wh, xl))
    scores = _sigmoid(logits)
    biased = scores + b_ref[...]
    gs = N_EXPERTS // N_GROUPS
    neg = -jnp.inf
    eid = lax.broadcasted_iota(jnp.int32, (N_EXPERTS, t), 0)
    sub = lax.broadcasted_iota(jnp.int32, (gs, t), 0)
    grow = lax.broadcasted_iota(jnp.int32, (N_GROUPS, t), 0)
    grp = jnp.zeros((N_GROUPS, t), F32)
    for g in range(N_GROUPS):
        bg = biased[g * gs:(g + 1) * gs]
        m1, i1 = _first_argmax(bg, sub, gs)
        m2 = jnp.max(jnp.where(sub == i1, neg, bg), axis=0, keepdims=True)
        grp = jnp.where(grow == g, m1 + m2, grp)
    chosen = jnp.zeros((N_GROUPS, t), F32)
    for _ in range(TOPK_GROUPS):
        _, gi = _first_argmax(grp, grow, N_GROUPS)
        hit = grow == gi
        chosen = jnp.where(hit, 1.0, chosen)
        grp = jnp.where(hit, neg, grp)
    chosen_e = jnp.concatenate([jnp.broadcast_to(chosen[g:g + 1], (gs, t)) for g in range(N_GROUPS)], axis=0)
    masked = jnp.where(chosen_e > 0.0, biased, neg)
    krow = lax.broadcasted_iota(jnp.int32, (TOP_K, t), 0)
    e_out = jnp.zeros((TOP_K, t), jnp.int32)
    g_out = jnp.zeros((TOP_K, t), F32)
    member = jnp.zeros((N_EXPERTS, t), F32)
    for k in range(TOP_K):
        _, idx = _first_argmax(masked, eid, N_EXPERTS)
        hit = eid == idx
        gk = jnp.sum(jnp.where(hit, scores, 0.0), axis=0, keepdims=True)
        masked = jnp.where(hit, neg, masked)
        member = jnp.where(hit, 1.0, member)
        e_out = jnp.where(krow == k, idx, e_out)
        g_out = jnp.where(krow == k, gk, g_out)
    g_out = g_out / jnp.sum(g_out, axis=0, keepdims=True) * ROUTED_SCALE
    e_ref[...] = e_out
    g_ref[...] = g_out
    c_ref[...] = jnp.sum(member, axis=1, keepdims=True).astype(jnp.int32)


def _route(x, router_w, layer, router_b, tm):
    n, d = x.shape
    e = router_w.shape[2]
    wt = jnp.swapaxes(router_w, 1, 2)
    return pl.pallas_call(
        _route_kernel,
        grid=(n // tm,),
        in_specs=[pl.BlockSpec((tm, d), lambda i: (i, 0)),
                  pl.BlockSpec((None, e, d), lambda i: (layer, 0, 0)),
                  pl.BlockSpec((e, 1), lambda i: (0, 0))],
        out_specs=[pl.BlockSpec((TOP_K, tm), lambda i: (0, i)),
                   pl.BlockSpec((TOP_K, tm), lambda i: (0, i)),
                   pl.BlockSpec((None, e, 1), lambda i: (i, 0, 0))],
        out_shape=[jax.ShapeDtypeStruct((TOP_K, n), jnp.int32), jax.ShapeDtypeStruct((TOP_K, n), F32),
                   jax.ShapeDtypeStruct((n // tm, e, 1), jnp.int32)],
        compiler_params=_cparams(("parallel",), V7X_VMEM_LIMIT_BYTES),
        name="route",
    )(x, wt, router_b[layer].astype(F32).reshape(e, 1))


def _slots_kernel(e_ref, base_ref, o_ref):
    e = e_ref[...]
    t = e.shape[1]
    eid = lax.broadcasted_iota(jnp.int32, (N_EXPERTS, t), 0)
    member = jnp.zeros((N_EXPERTS, t), F32)
    for k in range(TOP_K):
        member = jnp.where(eid == e[k:k + 1], 1.0, member)
    before = (lax.broadcasted_iota(jnp.int32, (t, t), 0) < lax.broadcasted_iota(jnp.int32, (t, t), 1))
    rank = _dot(member.astype(BF16), jnp.where(before, 1.0, 0.0).astype(BF16))
    slot = rank + base_ref[...].astype(F32)
    krow = lax.broadcasted_iota(jnp.int32, (TOP_K, t), 0)
    out = jnp.zeros((TOP_K, t), F32)
    for k in range(TOP_K):
        sk = jnp.sum(jnp.where(eid == e[k:k + 1], slot, 0.0), axis=0, keepdims=True)
        out = jnp.where(krow == k, sk, out)
    o_ref[...] = out.astype(jnp.int32)


def _dispatch_plan(e_idx, counts, tm, tile):
    k, n = e_idx.shape
    n_blocks = -(-(n * k) // tm) + N_EXPERTS
    counts = counts.reshape(n // tile, N_EXPERTS)
    total = jnp.sum(counts, axis=0)
    padded = (total + tm - 1) // tm * tm
    pad_end = jnp.cumsum(padded)
    tile_base = (pad_end - padded)[None, :] + jnp.cumsum(counts, axis=0) - counts
    slot_of = pl.pallas_call(
        _slots_kernel,
        grid=(n // tile,),
        in_specs=[pl.BlockSpec((k, tile), lambda i: (0, i)),
                  pl.BlockSpec((None, N_EXPERTS, 1), lambda i: (i, 0, 0))],
        out_specs=pl.BlockSpec((k, tile), lambda i: (0, i)),
        out_shape=jax.ShapeDtypeStruct((k, n), jnp.int32),
        compiler_params=_cparams(("parallel",)),
        name="slots",
    )(e_idx, tile_base.astype(jnp.int32).reshape(n // tile, N_EXPERTS, 1))
    blk_row0 = jnp.arange(n_blocks, dtype=jnp.int32) * tm
    blk_exp = jnp.minimum(jnp.sum((pad_end[None, :] <= blk_row0[:, None]).astype(jnp.int32), axis=1), N_EXPERTS - 1)
    blk_valid = jnp.clip((pad_end - padded + total)[blk_exp] - blk_row0, 0, tm)
    return slot_of, blk_exp.astype(jnp.int32), blk_valid.astype(jnp.int32), n_blocks


V7X_SC_CORES = 2
V7X_SC_SUBCORES = 16
SC_WORKERS = V7X_SC_CORES * V7X_SC_SUBCORES
SC_CHUNK = 32


def _sc_mesh():
    return plsc.VectorSubcoreMesh(core_axis_name="c", subcore_axis_name="s",
                                  num_cores=V7X_SC_CORES, num_subcores=V7X_SC_SUBCORES)


def _sc_worker_id():
    return lax.axis_index("s") * V7X_SC_CORES + lax.axis_index("c")


def _sc_gather_rows(table, slot_of):
    k, n = slot_of.shape
    w = table.shape[1]
    n_rows = k * n
    per_worker = n_rows // (SC_WORKERS * SC_CHUNK)
    assert per_worker * SC_WORKERS * SC_CHUNK == n_rows
    idx = slot_of.reshape(SC_WORKERS, per_worker, SC_CHUNK)

    def body(table_hbm, idx_hbm, out_hbm, idx_v, rows_v, sem):
        wid = _sc_worker_id()
        pltpu.sync_copy(idx_hbm.at[wid], idx_v)

        @pl.loop(0, per_worker)
        def _(c):
            pltpu.async_copy(table_hbm.at[idx_v.at[c]], rows_v, sem).wait()
            row0 = pl.multiple_of((wid * per_worker + c) * SC_CHUNK, SC_CHUNK)
            pltpu.sync_copy(rows_v, out_hbm.at[pl.ds(row0, SC_CHUNK)])

    return pl.kernel(
        body, out_type=jax.ShapeDtypeStruct((n_rows, w), table.dtype), mesh=_sc_mesh(),
        scratch_types=[pltpu.VMEM((per_worker, SC_CHUNK), jnp.int32), pltpu.VMEM((SC_CHUNK, w), table.dtype),
                       pltpu.SemaphoreType.DMA],
        name="sc_gather_rows",
    )(table, idx)


def _sc_scatter_rows(x, slot_of, n_slots):
    k, n = slot_of.shape
    w = x.shape[1]
    n_chunks = n // SC_CHUNK
    assert n_chunks * SC_CHUNK == n
    idx = slot_of.reshape(k, n_chunks, SC_CHUNK).transpose(1, 0, 2)
    rounds = -(-n_chunks // SC_WORKERS)

    def body(x_hbm, idx_hbm, out_hbm, idx_v, rows_v, sem):
        wid = _sc_worker_id()

        @pl.loop(0, rounds)
        def _(j):
            q = j * SC_WORKERS + wid

            @pl.when(q < n_chunks)
            def _():
                pltpu.sync_copy(idx_hbm.at[q], idx_v)
                pltpu.sync_copy(x_hbm.at[pl.ds(pl.multiple_of(q * SC_CHUNK, SC_CHUNK), SC_CHUNK)], rows_v)
                for kk in range(k):
                    pltpu.async_copy(rows_v, out_hbm.at[idx_v.at[kk]], sem).wait()

    return pl.kernel(
        body, out_type=jax.ShapeDtypeStruct((n_slots, w), x.dtype), mesh=_sc_mesh(),
        scratch_types=[pltpu.VMEM((k, SC_CHUNK), jnp.int32), pltpu.VMEM((SC_CHUNK, w), x.dtype),
                       pltpu.SemaphoreType.DMA],
        name="sc_scatter_rows",
    )(x, idx)


MOE_BLOCK = 256
ROW_TILE = 640


def _moe_postnorm_kernel(x_ref, y_ref, gate_ref, sh_ref, g_ref, b_ref, of_ref, ob_ref, *, alpha):
    gate = gate_ref[...]
    lo, hi = _unpack_halves(sh_ref[...])
    for k in range(TOP_K):
        lo_k, hi_k = _unpack_halves(y_ref[k])
        lo = lo + gate[:, k:k + 1] * lo_k
        hi = hi + gate[:, k:k + 1] * hi_k
    z = alpha * x_ref[...] + jnp.concatenate([lo, hi], axis=1)
    out = _layer_norm(z, g_ref[...], b_ref[...])
    of_ref[...] = out
    ob_ref[...] = out.astype(BF16)


def _moe_postnorm(x, y_tok, gate, shared, g, b, alpha, tm):
    m, d = x.shape
    row = pl.BlockSpec((tm, d), lambda i: (i, 0))
    vec = pl.BlockSpec((1, d), lambda i: (0, 0))
    return pl.pallas_call(
        functools.partial(_moe_postnorm_kernel, alpha=alpha),
        grid=(m // tm,),
        in_specs=[row, pl.BlockSpec((TOP_K, tm, d // 2), lambda i: (0, i, 0)),
                  pl.BlockSpec((tm, TOP_K), lambda i: (i, 0)), pl.BlockSpec((tm, d // 2), lambda i: (i, 0)), vec, vec],
        out_specs=[row, row],
        out_shape=[jax.ShapeDtypeStruct((m, d), F32), jax.ShapeDtypeStruct((m, d), BF16)],
        compiler_params=_cparams(("parallel",), V7X_VMEM_LIMIT_BYTES),
        name="postnorm_moe",
    )(x, y_tok, gate, shared, g.reshape(1, d), b.reshape(1, d))


def _moe(xf, xp, layer, router_w, router_b, w_gate, w_up, w_down, sw_gate, sw_up, sw_down, ln_g, ln_b, alpha):
    n, d = xf.shape
    e_idx, gate, counts = _route(xf, router_w, layer, router_b, ROW_TILE)
    slot_of, blk_exp, blk_valid, n_blocks = _dispatch_plan(e_idx, counts, MOE_BLOCK, ROW_TILE)
    x_sorted = _sc_scatter_rows(xp, slot_of, n_blocks * MOE_BLOCK)
    y_sorted = _expert_ffn(x_sorted, blk_exp, blk_valid, w_gate, w_up, w_down, layer, MOE_BLOCK, "routed_ffn")
    y_tok = _sc_gather_rows(y_sorted, slot_of).reshape(TOP_K, n, d // 2)
    n_sh = n // ROW_TILE
    shared = _expert_ffn(xp, jnp.zeros((n_sh,), jnp.int32), jnp.full((n_sh,), ROW_TILE, jnp.int32),
                         sw_gate[:, None], sw_up[:, None], sw_down[:, None], layer, ROW_TILE, "shared_ffn")
    return _moe_postnorm(xf, y_tok, gate.T, shared, ln_g, ln_b, alpha, ROW_TILE // 4)


def kernel(x_prompt, x_sample, state_hgrn, cache_win_k, cache_win_v, cache_mem_k, cache_mem_v, mem_prompt,
           w_in_a, lb_logits, g_norm_a, w_out_a, w_in_b, w_out_b, w_kv_shared, w_mem_kv, ln_g, ln_b,
           router_w, router_b, exp_w_gate, exp_w_up, exp_w_down, sh_w_gate, sh_w_up, sh_w_down):
    bp, sp, d = x_prompt.shape
    ns = x_sample.shape[0]
    assert x_sample.shape[1] == 1
    depth = ln_g.shape[0]
    n_a = w_in_a.shape[0]
    alpha = (2 * depth) ** 0.25
    n_p = bp * sp
    mem_len = mem_prompt.shape[1]
    a_mix = 4 * A_HEADS * HEAD_DIM

    xf = jnp.concatenate([x_prompt.reshape(n_p, d), x_sample.reshape(ns, d)], axis=0)
    xb = xf.astype(BF16)
    lower_bounds = jnp.cumsum(jax.nn.softmax(lb_logits.astype(F32), axis=0), axis=0)
    mem_flat = mem_prompt.reshape(bp * mem_len, d)
    pos_all = jnp.concatenate([jnp.tile(jnp.arange(sp, dtype=jnp.int32), bp),
                               jnp.full((ns,), PAST_LEN, jnp.int32)])
    tables = _rope_tables(pos_all)

    hgrn_p, hgrn_s, mem_k_p, mem_v_p = [], [], [], []
    for layer in range(depth):
        kvm = _proj([mem_flat], w_mem_kv, layer, F32, mem_len, 512, "mem_kv")
        mem_k_p.append(kvm[:, :MEM_W].reshape(bp, mem_len, MEM_HEADS, HEAD_DIM))
        mem_v_p.append(kvm[:, MEM_W:].reshape(bp, mem_len, MEM_HEADS, HEAD_DIM))
        if layer < n_a:
            a = layer
            proj = _proj([xb], w_in_a, a, F32, ROW_TILE, 512, "proj_in_a")
            o_x_p, st_p = _hgrn_prompt(proj, lower_bounds[a], g_norm_a[a], bp, sp)
            o_m_p = _mem_attn_prompt(proj, a_mix // MEM_W, kvm, bp, sp, 512)
            proj_s = proj[n_p:]
            o_x_s, st_s = _hgrn_step(proj_s.reshape(ns, 1, -1), state_hgrn, a, lower_bounds[a], g_norm_a[a])
            o_m_s = _mem_attn_decode(proj_s[:, a_mix:].reshape(ns, MEM_HEADS, HEAD_DIM), cache_mem_k, cache_mem_v, layer)
            hgrn_p.append(st_p)
            hgrn_s.append(st_s)
            w_out, w_out_layer = w_out_a, a
        else:
            bl = layer - n_a
            if layer == n_a:
                kv = _proj([xb], w_kv_shared[None], 0, F32, ROW_TILE, 512, "proj_kv")
                k_r = _rope(kv, B_SLOTS, tables, F32, ROW_TILE, "rope_k")
                k_p = k_r[:n_p].reshape(bp, sp, B_SLOTS, HEAD_DIM)
                v_p = kv[:n_p, KV_W:].reshape(bp, sp, B_SLOTS, HEAD_DIM)
                k_s = k_r[n_p:].reshape(ns, B_SLOTS, HEAD_DIM)
                v_s = kv[n_p:, KV_W:].reshape(ns, B_SLOTS, HEAD_DIM)
            proj = _proj([xb], w_in_b, bl, F32, ROW_TILE, 512, "proj_in_b")
            q_r = _rope(proj, B_QHEADS, tables, F32, ROW_TILE, "rope_q")
            o_x_p = _dil_attn_prompt(q_r, k_r, kv, bp, sp)
            o_m_p = _mem_attn_prompt(proj, B_QHEADS * HEAD_DIM // MEM_W, kvm, bp, sp, 512)
            q_s = q_r[n_p:].astype(F32).reshape(ns, B_GROUPS, B_SLOTS, HEAD_DIM)
            o_x_s = _dil_attn_decode(q_s, k_s, v_s, cache_win_k, cache_win_v)
            o_m_s = _mem_attn_decode(proj[n_p:, B_QHEADS * HEAD_DIM:].reshape(ns, MEM_HEADS, HEAD_DIM),
                                     cache_mem_k, cache_mem_v, layer)
            w_out, w_out_layer = w_out_b, bl
        o_x = jnp.concatenate([o_x_p, o_x_s.reshape(ns, -1)], axis=0)
        o_m = jnp.concatenate([o_m_p, o_m_s.reshape(ns, -1)], axis=0)
        y = _proj([o_x, o_m], w_out, w_out_layer, F32, ROW_TILE, 512, "proj_out")
        xf, xp = _postnorm(xf, y, ln_g[layer, 0], ln_b[layer, 0], alpha, ROW_TILE // 2, "postnorm_mix")
        xf, xb = _moe(xf, xp, layer, router_w, router_b, exp_w_gate, exp_w_up, exp_w_down,
                      sh_w_gate, sh_w_up, sh_w_down, ln_g[layer, 1], ln_b[layer, 1], alpha)

    w_p = min(max(w for w, _ in B_PATTERNS), sp)
    return (xf[:n_p].reshape(bp, sp, d), xf[n_p:].reshape(ns, 1, d),
            jnp.stack(hgrn_p), jnp.stack(hgrn_s),
            k_p[:, sp - w_p:], v_p[:, sp - w_p:],
            k_s.reshape(ns, 1, B_SLOTS, HEAD_DIM), v_s.reshape(ns, 1, B_SLOTS, HEAD_DIM),
            jnp.stack(mem_k_p), jnp.stack(mem_v_p))
```

```python
import functools

import jax
import jax.numpy as jnp
from jax import lax
from jax.experimental import pallas as pl
from jax.experimental.pallas import tpu as pltpu
from jax.experimental.pallas import tpu_sc as plsc

F32 = jnp.float32
BF16 = jnp.bfloat16

HEAD_DIM = 128
A_HEADS = 12
A_CHUNK = 64
A_SUB = 16
B_PATTERNS = ((128, 1), (512, 4), (2048, 16))
B_SLOTS = 4
B_GROUPS = len(B_PATTERNS)
B_QHEADS = B_GROUPS * B_SLOTS
B_BLOCK = 128
MEM_HEADS = 4
MEM_W = MEM_HEADS * HEAD_DIM
KV_W = B_SLOTS * HEAD_DIM
ROPE_THETA = 500000.0
ROPE_DIM = HEAD_DIM // 4
N_EXPERTS = 64
N_GROUPS = 8
TOPK_GROUPS = 4
TOP_K = 8
ROUTED_SCALE = 2.5
LN_EPS = 1e-5
RMS_EPS = 1e-6
ATTN_SCALE = HEAD_DIM ** -0.5
PAST_LEN = 2048

V7X_VMEM_LIMIT_BYTES = 56 * 1024 * 1024
LANES = 128
NEG_BIG = -1e30


def _cparams(sem, vmem=None):
    return pltpu.CompilerParams(dimension_semantics=sem, vmem_limit_bytes=vmem)


def _dot(a, b):
    return jnp.dot(a, b, preferred_element_type=F32)


def _dot_nt(a, b):
    return lax.dot_general(a, b, (((1,), (1,)), ((), ())), preferred_element_type=F32)


def _sigmoid(x):
    return 1.0 / (1.0 + jnp.exp(-x))


def _proj_kernel(*refs, n_lhs):
    x_refs = refs[:n_lhs]
    w_refs = refs[n_lhs:2 * n_lhs]
    o_ref = refs[2 * n_lhs]
    wb_refs = refs[2 * n_lhs + 1:]

    @pl.when(pl.program_id(1) == 0)
    def _():
        for w_ref, wb_ref in zip(w_refs, wb_refs):
            wb_ref[...] = w_ref[...].astype(BF16)

    acc = None
    for x_ref, wb_ref in zip(x_refs, wb_refs):
        d = _dot(x_ref[...].astype(BF16), wb_ref[...])
        acc = d if acc is None else acc + d
    o_ref[...] = acc.astype(o_ref.dtype)


def _proj(lhs, w, layer, out_dtype, tm, tn, name):
    m = lhs[0].shape[0]
    n = w.shape[2]
    koff = 0
    in_specs, w_specs, scratch = [], [], []
    for x in lhs:
        k = x.shape[1]
        assert koff % k == 0 and m % tm == 0 and n % tn == 0
        in_specs.append(pl.BlockSpec((tm, k), lambda j, i: (i, 0)))
        w_specs.append(pl.BlockSpec((None, k, tn), functools.partial(lambda j, i, kb: (layer, kb, j), kb=koff // k)))
        scratch.append(pltpu.VMEM((k, tn), BF16))
        koff += k
    assert koff == w.shape[1]
    return pl.pallas_call(
        functools.partial(_proj_kernel, n_lhs=len(lhs)),
        grid=(n // tn, m // tm),
        in_specs=in_specs + w_specs,
        out_specs=pl.BlockSpec((tm, tn), lambda j, i: (i, j)),
        out_shape=jax.ShapeDtypeStruct((m, n), out_dtype),
        scratch_shapes=scratch,
        compiler_params=_cparams(("arbitrary", "arbitrary"), V7X_VMEM_LIMIT_BYTES),
        name=name,
    )(*lhs, *([w] * len(lhs)))


def _pack_halves(x):
    c = x.shape[1] // 2
    lo = pltpu.bitcast(x[:, :c].astype(BF16).astype(F32), jnp.int32)
    hi = pltpu.bitcast(x[:, c:].astype(BF16).astype(F32), jnp.int32)
    return hi | lax.shift_right_logical(lo, 16)


def _unpack_halves(w):
    lo = pltpu.bitcast(lax.shift_left(w, 16), F32)
    hi = pltpu.bitcast(w & jnp.int32(-65536), F32)
    return lo, hi


def _layer_norm(z, g, b):
    mu = jnp.mean(z, axis=-1, keepdims=True)
    zc = z - mu
    var = jnp.mean(zc * zc, axis=-1, keepdims=True)
    return zc * lax.rsqrt(var + LN_EPS) * g + b


PART_TILE = 128


def _part_specs(n_p, block):
    tiles_p = n_p // PART_TILE
    rest = (0,) * (len(block) - 1)
    return (pl.BlockSpec(block, lambda i: (jnp.minimum(i, tiles_p - 1),) + rest),
            pl.BlockSpec(block, lambda i: (jnp.maximum(i - tiles_p, 0),) + rest))


def _on_part(tiles_p, fn):
    i = pl.program_id(0)
    pl.when(i < tiles_p)(functools.partial(fn, 0))
    pl.when(i >= tiles_p)(functools.partial(fn, 1))


def _postnorm_kernel(xp_ref, xs_ref, y_ref, g_ref, b_ref, of_ref, op_ref, *, alpha, tiles_p):
    def part(which):
        x_ref = (xp_ref, xs_ref)[which]
        out = _layer_norm(alpha * x_ref[...] + y_ref[...], g_ref[...], b_ref[...])
        of_ref[...] = out
        op_ref[...] = _pack_halves(out)

    _on_part(tiles_p, part)


def _postnorm(x_p, x_s, y, g, b, alpha, name):
    n_p, d = x_p.shape
    m = n_p + x_s.shape[0]
    row = pl.BlockSpec((PART_TILE, d), lambda i: (i, 0))
    vec = pl.BlockSpec((1, d), lambda i: (0, 0))
    part_p, part_s = _part_specs(n_p, (PART_TILE, d))
    return pl.pallas_call(
        functools.partial(_postnorm_kernel, alpha=alpha, tiles_p=n_p // PART_TILE),
        grid=(m // PART_TILE,),
        in_specs=[part_p, part_s, row, vec, vec],
        out_specs=[row, pl.BlockSpec((PART_TILE, d // 2), lambda i: (i, 0))],
        out_shape=[jax.ShapeDtypeStruct((m, d), F32), jax.ShapeDtypeStruct((m, d // 2), jnp.int32)],
        compiler_params=_cparams(("arbitrary",)),
        name=name,
    )(x_p, x_s, y, g.reshape(1, d), b.reshape(1, d))


def _hgrn_prompt_kernel(q_ref, f_ref, v_ref, gate_ref, lb_ref, gn_ref, os_ref, o_ref, s_ref, st_ref,
                        *, seq, heads, n_batch):
    _prompt_then_sample(n_batch, os_ref, o_ref,
                        functools.partial(_hgrn_sequence, q_ref, f_ref, v_ref, gate_ref, lb_ref, gn_ref,
                                          o_ref, s_ref, st_ref, seq=seq, heads=heads), axis=1)


def _hgrn_sequence(q_ref, f_ref, v_ref, gate_ref, lb_ref, gn_ref, o_ref, s_ref, st_ref, *, seq, heads):
    c = A_CHUNK
    n_chunks = seq // c
    n_sub = c // A_SUB
    gn = gn_ref[...]
    row = lax.broadcasted_iota(jnp.int32, (c, c), 0)
    col = lax.broadcasted_iota(jnp.int32, (c, c), 1)
    tril = jnp.where(row >= col, 1.0, 0.0).astype(BF16)
    same_sub = (row // A_SUB) == (col // A_SUB)
    diag_dist = jnp.where(same_sub, row - col, -1)
    off_mask = col < (row // A_SUB) * A_SUB
    st_ref[...] = jnp.zeros_like(st_ref)

    def chunk(ci, carry):
        for hh in range(heads):
            head_chunk(ci, hh)
        return carry

    def head_chunk(ci, hh):
        r0 = pl.multiple_of(ci * c, c)
        hsl = slice(hh * HEAD_DIM, (hh + 1) * HEAD_DIM)
        lb = lb_ref[hh]
        qp = q_ref[pl.ds(r0, c), hsl]
        fp = f_ref[pl.ds(r0, c), hsl]
        v = v_ref[pl.ds(r0, c), hsl]
        gp = gate_ref[pl.ds(r0, c), hsl]
        st = st_ref[hh]
        q = qp * _sigmoid(qp)
        forget = lb + (1.0 - lb) * _sigmoid(fp)
        logf = jnp.log2(forget)
        k = 1.0 - forget
        hi = logf.astype(BF16)
        r1 = logf - hi.astype(F32)
        mid = r1.astype(BF16)
        lo = (r1 - mid.astype(F32)).astype(BF16)
        g = _dot(tril, hi) + _dot(tril, mid) + _dot(tril, lo)
        v_b = v.astype(BF16)
        o = _dot_nt((q * jnp.exp2(g)).astype(BF16), st.astype(BF16))
        rows = [jnp.zeros((A_SUB, c), F32)]
        for i in range(1, n_sub):
            gref = g[i * A_SUB:i * A_SUB + 1, :]
            qt = q[i * A_SUB:(i + 1) * A_SUB, :] * jnp.exp2(g[i * A_SUB:(i + 1) * A_SUB, :] - gref)
            kt = k * jnp.exp2(jnp.minimum(gref - g, 0.0))
            rows.append(_dot_nt(qt.astype(BF16), kt.astype(BF16)))
        a = jnp.where(off_mask, jnp.concatenate(rows, axis=0), 0.0)
        for d in range(A_SUB):
            kr = k if d == 0 else pltpu.roll(k, d, axis=0)
            gr = g if d == 0 else pltpu.roll(g, d, axis=0)
            x = q * kr * jnp.exp2(g - gr)
            a = jnp.where(diag_dist == d, jnp.sum(x, axis=-1, keepdims=True), a)
        o = o + _dot(a.astype(BF16), v_b)
        gend = g[c - 1:c, :]
        kt_end = k * jnp.exp2(gend - g)
        st_ref[hh] = jnp.exp2(gend) * st + _dot(v_b.T, kt_end.astype(BF16))
        o = o * lax.rsqrt(jnp.mean(o * o, axis=-1, keepdims=True) + RMS_EPS) * gn
        o_ref[pl.ds(r0, c), hsl] = (o * (gp * _sigmoid(gp))).astype(o_ref.dtype)

    lax.fori_loop(0, n_chunks, chunk, 0)
    for hh in range(heads):
        s_ref[hh] = st_ref[hh].T


A_HEADS_PER_STEP = 4


def _hgrn_prompt(proj, lb, g_norm, o_sample, n_batch, seq):
    h = A_HEADS
    hp = A_HEADS_PER_STEP
    ns = o_sample.shape[0]
    assert h % hp == 0 and ns <= seq and proj.shape[0] == n_batch * seq + ns
    ng = h // hp
    w = hp * HEAD_DIM
    last = n_batch - 1
    blk = lambda off: pl.BlockSpec((seq, w), functools.partial(lambda hg, b, off: (jnp.minimum(b, last), off + hg), off=off))
    return pl.pallas_call(
        functools.partial(_hgrn_prompt_kernel, seq=seq, heads=hp, n_batch=n_batch),
        grid=(ng, n_batch + 1),
        in_specs=[blk(0), blk(ng), blk(2 * ng), blk(3 * ng),
                  pl.BlockSpec((hp, 1, HEAD_DIM), lambda hg, b: (hg, 0, 0)),
                  pl.BlockSpec((1, HEAD_DIM), lambda hg, b: (0, 0)),
                  pl.BlockSpec((ns, w), lambda hg, b: (0, hg))],
        out_specs=[pl.BlockSpec((seq, w), lambda hg, b: (b, hg)),
                   pl.BlockSpec((None, hp, HEAD_DIM, HEAD_DIM), lambda hg, b: (jnp.minimum(b, last), hg, 0, 0))],
        out_shape=[jax.ShapeDtypeStruct((proj.shape[0], h * HEAD_DIM), BF16),
                   jax.ShapeDtypeStruct((n_batch, h, HEAD_DIM, HEAD_DIM), F32)],
        scratch_shapes=[pltpu.VMEM((hp, HEAD_DIM, HEAD_DIM), F32)],
        compiler_params=_cparams(("arbitrary", "arbitrary"), V7X_VMEM_LIMIT_BYTES),
        name="hgrn_prompt",
    )(proj, proj, proj, proj, lb.reshape(h, 1, HEAD_DIM), g_norm.reshape(1, HEAD_DIM), o_sample)


def _softmax_av(s, v):
    m = jnp.max(s, axis=-1, keepdims=True)
    p = jnp.exp(s - m)
    l = jnp.sum(p, axis=-1, keepdims=True)
    return _dot(p.astype(BF16), v) / l


def _prompt_then_sample(n_prompt_steps, os_ref, o_ref, prompt_step, axis=0):
    i = pl.program_id(axis)
    pl.when(i < n_prompt_steps)(prompt_step)

    @pl.when(i >= n_prompt_steps)
    def _():
        o_ref[:os_ref.shape[0], :] = os_ref[...]


def _mem_attn_kernel(q_ref, k_ref, v_ref, os_ref, o_ref, *, n_steps):
    def step():
        for h in range(MEM_HEADS):
            sl = slice(h * HEAD_DIM, (h + 1) * HEAD_DIM)
            s = _dot_nt(q_ref[:, sl].astype(BF16), k_ref[:, sl].astype(BF16)) * ATTN_SCALE
            o_ref[:, sl] = _softmax_av(s, v_ref[:, sl].astype(BF16)).astype(o_ref.dtype)

    _prompt_then_sample(n_steps, os_ref, o_ref, step)


def _mem_attn_prompt(proj, q_col, kv, o_sample, n_batch, seq, tq):
    mem_len = kv.shape[0] // n_batch
    nq = seq // tq
    n_steps = n_batch * nq
    ns = o_sample.shape[0]
    assert ns <= tq and proj.shape[0] == n_batch * seq + ns
    last = n_steps - 1
    return pl.pallas_call(
        functools.partial(_mem_attn_kernel, n_steps=n_steps),
        grid=(n_steps + 1,),
        in_specs=[pl.BlockSpec((tq, MEM_W), lambda i: (jnp.minimum(i, last), q_col)),
                  pl.BlockSpec((mem_len, MEM_W), lambda i: (jnp.minimum(i, last) // nq, 0)),
                  pl.BlockSpec((mem_len, MEM_W), lambda i: (jnp.minimum(i, last) // nq, 1)),
                  pl.BlockSpec((ns, MEM_W), lambda i: (0, 0))],
        out_specs=pl.BlockSpec((tq, MEM_W), lambda i: (i, 0)),
        out_shape=jax.ShapeDtypeStruct((proj.shape[0], MEM_W), BF16),
        compiler_params=_cparams(("arbitrary",)),
        name="mem_attn_prompt",
    )(proj, kv, kv, o_sample)


def _rope_tables(pos):
    half = ROPE_DIM // 2
    inv_freq = ROPE_THETA ** (-jnp.arange(0, ROPE_DIM, 2, dtype=F32) / ROPE_DIM)
    ang = pos.astype(F32)[:, None] * inv_freq[None, :]
    cos, sin = jnp.cos(ang), jnp.sin(ang)
    n = pos.shape[0]
    one = jnp.ones((n, HEAD_DIM - ROPE_DIM), F32)
    zero = jnp.zeros((n, HEAD_DIM - half), F32)
    c = jnp.concatenate([cos, cos, one], axis=1)
    s_up = jnp.concatenate([-sin, zero], axis=1)
    s_dn = jnp.concatenate([jnp.zeros((n, half), F32), sin, zero[:, half:]], axis=1)
    return c, s_up, s_dn


def _rope_kernel(x_ref, c_ref, su_ref, sd_ref, o_ref, *, n_heads):
    c, su, sd = c_ref[...], su_ref[...], sd_ref[...]
    for h in range(n_heads):
        sl = slice(h * HEAD_DIM, (h + 1) * HEAD_DIM)
        o_ref[:, sl] = _rope_head(x_ref[:, sl], c, su, sd).astype(o_ref.dtype)


def _rope_head(x, c, su, sd):
    half = ROPE_DIM // 2
    up = pltpu.roll(x, HEAD_DIM - half, axis=1)
    dn = pltpu.roll(x, half, axis=1)
    return x * c + up * su + dn * sd


def _kv_kernel(kv_ref, c_ref, su_ref, sd_ref, k_ref, kp_ref, ks_ref, vp_ref, vs_ref, *, tiles_p):
    c, su, sd = c_ref[...], su_ref[...], sd_ref[...]

    def part(which):
        k4_ref, v4_ref = ((kp_ref, vp_ref), (ks_ref, vs_ref))[which]
        for h in range(B_SLOTS):
            sl = slice(h * HEAD_DIM, (h + 1) * HEAD_DIM)
            k = _rope_head(kv_ref[:, sl], c, su, sd)
            k_ref[:, sl] = k
            k4_ref[:, h, :] = k
            v4_ref[:, h, :] = kv_ref[:, KV_W + h * HEAD_DIM:KV_W + (h + 1) * HEAD_DIM]

    _on_part(tiles_p, part)


def _shared_kv(kv, tables, n_p):
    m = kv.shape[0]
    tm = PART_TILE
    tab = pl.BlockSpec((tm, HEAD_DIM), lambda i: (i, 0))
    part_p, part_s = _part_specs(n_p, (tm, B_SLOTS, HEAD_DIM))
    cache = lambda rows: jax.ShapeDtypeStruct((rows, B_SLOTS, HEAD_DIM), F32)
    return pl.pallas_call(
        functools.partial(_kv_kernel, tiles_p=n_p // tm),
        grid=(m // tm,),
        in_specs=[pl.BlockSpec((tm, 2 * KV_W), lambda i: (i, 0)), tab, tab, tab],
        out_specs=[pl.BlockSpec((tm, KV_W), lambda i: (i, 0)), part_p, part_s, part_p, part_s],
        out_shape=[jax.ShapeDtypeStruct((m, KV_W), F32), cache(n_p), cache(m - n_p), cache(n_p), cache(m - n_p)],
        compiler_params=_cparams(("arbitrary",)),
        name="shared_kv",
    )(kv, *tables)


def _rope(x, n_heads, tables, out_dtype, tm, name):
    m = x.shape[0]
    w = n_heads * HEAD_DIM
    tab = pl.BlockSpec((tm, HEAD_DIM), lambda i: (i, 0))
    return pl.pallas_call(
        functools.partial(_rope_kernel, n_heads=n_heads),
        grid=(m // tm,),
        in_specs=[pl.BlockSpec((tm, w), lambda i: (i, 0)), tab, tab, tab],
        out_specs=pl.BlockSpec((tm, w), lambda i: (i, 0)),
        out_shape=jax.ShapeDtypeStruct((m, w), out_dtype),
        compiler_params=_cparams(("parallel",)),
        name=name,
    )(x, *tables)


def _dil_prompt_kernel(*refs, seq, n_batch):
    os_ref, o_ref = refs[B_GROUPS + 2:B_GROUPS + 4]
    _prompt_then_sample(n_batch, os_ref, o_ref, functools.partial(_dil_sequence, *refs, seq=seq), axis=1)


def _dil_sequence(*refs, seq):
    q_refs = refs[:B_GROUPS]
    k_ref, v_ref, _, o_ref, m_ref, l_ref, acc_ref = refs[B_GROUPS:]
    blk = B_BLOCK
    base = (lax.broadcasted_iota(jnp.int32, (blk, blk), 0) - lax.broadcasted_iota(jnp.int32, (blk, blk), 1))
    for g, (win, dil) in enumerate(B_PATTERNS):
        span = win // dil
        n_blk = seq // dil // blk
        assert span <= blk

        def body(it, carry, g=g, dil=dil, span=span, n_blk=n_blk):
            r = it % dil
            n = it // dil

            def rows(nn):
                start = nn * (blk * dil) + r
                return pl.ds(pl.multiple_of(start, blk), blk) if dil == 1 else pl.ds(start, blk, stride=dil)

            own = rows(n)
            q = q_refs[g][own, :].astype(BF16)
            s = jnp.where(base >= 0, _dot_nt(q, k_ref[own, :].astype(BF16)) * ATTN_SCALE, NEG_BIG)
            vv = v_ref[own, :].astype(BF16)
            if n_blk > 1:
                prev = rows(jnp.maximum(n - 1, 0))
                reach = jnp.where(n > 0, span - blk, -blk - 1)
                s_prev = jnp.where(base <= reach, _dot_nt(q, k_ref[prev, :].astype(BF16)) * ATTN_SCALE, NEG_BIG)
                s = jnp.concatenate([s_prev, s], axis=1)
                vv = jnp.concatenate([v_ref[prev, :].astype(BF16), vv], axis=0)
            m_b = jnp.max(s, axis=-1, keepdims=True)
            p = jnp.exp(s - m_b)
            l_b = jnp.sum(p, axis=-1, keepdims=True)
            acc_b = _dot(p.astype(BF16), vv)
            if g == 0:
                m_ref[own, :] = jnp.broadcast_to(m_b, (blk, HEAD_DIM))
                l_ref[own, :] = jnp.broadcast_to(l_b, (blk, HEAD_DIM))
                acc_ref[own, :] = acc_b
            else:
                m_old = m_ref[own, :]
                m_new = jnp.maximum(m_old, m_b)
                a_old = jnp.exp(m_old - m_new)
                a_b = jnp.exp(m_b - m_new)
                m_ref[own, :] = m_new
                l_ref[own, :] = l_ref[own, :] * a_old + l_b * a_b
                acc_ref[own, :] = acc_ref[own, :] * a_old + acc_b * a_b
            return carry

        lax.fori_loop(0, dil * n_blk, body, 0)
    o_ref[...] = (acc_ref[...] / l_ref[...]).astype(o_ref.dtype)


def _dil_attn_prompt(q, k, kv, o_sample, n_batch, seq):
    ns = o_sample.shape[0]
    assert ns <= seq and q.shape[0] == n_batch * seq + ns
    last = n_batch - 1
    col = lambda c0: pl.BlockSpec((seq, HEAD_DIM), functools.partial(lambda h, b, c0: (jnp.minimum(b, last), c0 + h), c0=c0))
    return pl.pallas_call(
        functools.partial(_dil_prompt_kernel, seq=seq, n_batch=n_batch),
        grid=(B_SLOTS, n_batch + 1),
        in_specs=[col(g * B_SLOTS) for g in range(B_GROUPS)] + [col(0), col(B_SLOTS),
                                                                 pl.BlockSpec((ns, HEAD_DIM), lambda h, b: (0, h))],
        out_specs=pl.BlockSpec((seq, HEAD_DIM), lambda h, b: (b, h)),
        out_shape=jax.ShapeDtypeStruct((q.shape[0], KV_W), BF16),
        scratch_shapes=[pltpu.VMEM((seq, HEAD_DIM), F32)] * 3,
        compiler_params=_cparams(("arbitrary", "arbitrary"), V7X_VMEM_LIMIT_BYTES),
        name="dil_attn_prompt",
    )(*([q] * B_GROUPS), k, kv, o_sample)


def _decode_scores(q, k):
    return jnp.sum(k * q[None], axis=-1, keepdims=True) * ATTN_SCALE


def _decode_attend(parts):
    m = None
    for s, _ in parts:
        sm = jnp.max(s, axis=0)
        m = sm if m is None else jnp.maximum(m, sm)
    l = jnp.zeros_like(m)
    acc = None
    for s, v in parts:
        p = jnp.exp(s - m[None])
        l = l + jnp.sum(p, axis=0)
        pv = jnp.sum(p * v, axis=0)
        acc = pv if acc is None else acc + pv
    return acc / l


MEM_DECODE_SEQS = 4


def _mem_decode_kernel(q_ref, k_ref, v_ref, o_ref):
    n_seq, rows, _ = k_ref.shape
    nh = q_ref.shape[1]
    own = (lax.broadcasted_iota(jnp.int32, (nh, rows), 1) % nh) == lax.broadcasted_iota(jnp.int32, (nh, rows), 0)
    for i in range(n_seq):
        s = _dot_nt(q_ref[i].astype(BF16), k_ref[i].astype(BF16)) * ATTN_SCALE
        o_ref[i] = _softmax_av(jnp.where(own, s, NEG_BIG), v_ref[i].astype(BF16)).astype(o_ref.dtype)


def _mem_attn_decode(q, cache_k, cache_v, layer):
    ns, nh, hd = q.shape
    n_layers, _, mem_len = cache_k.shape[:3]
    g = MEM_DECODE_SEQS
    assert ns % g == 0
    rows = mem_len * nh
    cspec = pl.BlockSpec((None, g, rows, hd), lambda b: (layer, b, 0, 0))
    qspec = pl.BlockSpec((g, nh, hd), lambda b: (b, 0, 0))
    return pl.pallas_call(
        _mem_decode_kernel,
        grid=(ns // g,),
        in_specs=[qspec, cspec, cspec],
        out_specs=qspec,
        out_shape=jax.ShapeDtypeStruct((ns, nh, hd), BF16),
        compiler_params=_cparams(("parallel",)),
        name="mem_attn_decode",
    )(q, cache_k.reshape(n_layers, ns, rows, hd), cache_v.reshape(n_layers, ns, rows, hd))


def _dil_decode_kernel(q_ref, kn_ref, vn_ref, *refs):
    o_ref = refs[-1]
    kn = kn_ref[...][None]
    vn = vn_ref[...][None]
    parts = []
    for g in range(B_GROUPS):
        q = q_ref[g].astype(F32)
        parts.append((_decode_scores(q, refs[2 * g][...]), refs[2 * g + 1][...]))
        parts.append((_decode_scores(q, kn), vn))
    o_ref[...] = _decode_attend(parts).astype(o_ref.dtype)


def _dil_attn_decode(q, k_new, v_new, cache_k, cache_v):
    ns, w_buf, ns_slots, hd = cache_k.shape
    in_specs = [pl.BlockSpec((None, B_GROUPS, ns_slots, hd), lambda b: (b, 0, 0, 0)),
                pl.BlockSpec((None, ns_slots, hd), lambda b: (b, 0, 0)),
                pl.BlockSpec((None, ns_slots, hd), lambda b: (b, 0, 0))]
    args = [q, k_new, v_new]
    for win, dil in B_PATTERNS:
        span = win // dil
        assert w_buf % dil == 0 and (w_buf // dil) % span == 0
        view = (ns, w_buf // dil, dil, ns_slots, hd)
        last = w_buf // dil // span - 1
        spec = pl.BlockSpec((None, span, None, ns_slots, hd), functools.partial(lambda b, last: (b, last, 0, 0, 0), last=last))
        in_specs += [spec, spec]
        args += [cache_k.reshape(view), cache_v.reshape(view)]
    return pl.pallas_call(
        _dil_decode_kernel,
        grid=(ns,),
        in_specs=in_specs,
        out_specs=pl.BlockSpec((None, ns_slots, hd), lambda b: (b, 0, 0)),
        out_shape=jax.ShapeDtypeStruct((ns, ns_slots, hd), BF16),
        compiler_params=_cparams(("parallel",)),
        name="dil_attn_decode",
    )(*args)


def _hgrn_step_kernel(p_ref, s_ref, lb_ref, gn_ref, o_ref, so_ref, rows_ref):
    h = A_HEADS
    hk = h * HEAD_DIM
    gn = gn_ref[...]
    rows_ref[...] = jnp.zeros_like(rows_ref)
    for i in range(h):
        fp = p_ref[:, hk + i * HEAD_DIM:hk + (i + 1) * HEAD_DIM]
        lb = lb_ref[i:i + 1, :]
        rows_ref[i:i + 1, :] = lb + (1.0 - lb) * _sigmoid(fp)
    cols = rows_ref[...].T
    first = lax.broadcasted_iota(jnp.int32, (8, HEAD_DIM), 0) == 0
    for i in range(h):
        sl = slice(i * HEAD_DIM, (i + 1) * HEAD_DIM)
        qp = p_ref[:, sl]
        v = p_ref[:, 2 * hk + i * HEAD_DIM:2 * hk + (i + 1) * HEAD_DIM]
        gp = p_ref[:, 3 * hk + i * HEAD_DIM:3 * hk + (i + 1) * HEAD_DIM]
        k = 1.0 - rows_ref[i:i + 1, :]
        k8 = jnp.where(first, k, 0.0).astype(BF16)
        v8 = jnp.broadcast_to(v, (8, HEAD_DIM)).astype(BF16)
        kv = lax.dot_general(k8, v8, (((0,), (0,)), ((), ())), preferred_element_type=F32)
        s_new = cols[:, i:i + 1] * s_ref[i] + kv
        so_ref[i] = s_new
        q8 = jnp.broadcast_to(qp * _sigmoid(qp), (8, HEAD_DIM)).astype(BF16)
        o = _dot(q8, s_new.astype(BF16))[0:1]
        o = o * lax.rsqrt(jnp.mean(o * o, axis=-1, keepdims=True) + RMS_EPS) * gn
        o_ref[:, sl] = (o * (gp * _sigmoid(gp))).astype(o_ref.dtype)


def _hgrn_step(proj, state, layer, lb, g_norm):
    ns = proj.shape[0]
    h = A_HEADS
    sspec = pl.BlockSpec((None, h, HEAD_DIM, HEAD_DIM), lambda b: (b, 0, 0, 0))
    return pl.pallas_call(
        _hgrn_step_kernel,
        grid=(ns,),
        in_specs=[pl.BlockSpec((None, 1, proj.shape[2]), lambda b: (b, 0, 0)),
                  pl.BlockSpec((None, None, h, HEAD_DIM, HEAD_DIM), lambda b: (layer, b, 0, 0, 0)),
                  pl.BlockSpec((h, HEAD_DIM), lambda b: (0, 0)),
                  pl.BlockSpec((1, HEAD_DIM), lambda b: (0, 0))],
        out_specs=[pl.BlockSpec((None, 1, h * HEAD_DIM), lambda b: (b, 0, 0)), sspec],
        out_shape=[jax.ShapeDtypeStruct((ns, 1, h * HEAD_DIM), BF16),
                   jax.ShapeDtypeStruct((ns, h, HEAD_DIM, HEAD_DIM), F32)],
        scratch_shapes=[pltpu.VMEM((HEAD_DIM, HEAD_DIM), F32)],
        compiler_params=_cparams(("parallel",)),
        name="hgrn_step",
    )(proj, state, lb.reshape(h, HEAD_DIM), g_norm.reshape(1, HEAD_DIM))


def _ffn_kernel(be_ref, nv_ref, x_ref, wg_ref, wu_ref, wd_ref, o_ref, wgb_ref, wub_ref, wdb_ref):
    i = pl.program_id(0)
    n_valid = nv_ref[i]

    @pl.when(n_valid > 0)
    def _():
        @pl.when((i == 0) | (be_ref[i] != be_ref[jnp.maximum(i - 1, 0)]))
        def _():
            wgb_ref[...] = wg_ref[...].astype(BF16)
            wub_ref[...] = wu_ref[...].astype(BF16)
            wdb_ref[...] = wd_ref[...].astype(BF16)

        lo, hi = _unpack_halves(x_ref[...])
        c = lo.shape[1]
        keep = lax.broadcasted_iota(jnp.int32, lo.shape, 0) < n_valid
        lo = jnp.where(keep, lo, 0.0).astype(BF16)
        hi = jnp.where(keep, hi, 0.0).astype(BF16)
        hg = _dot(lo, wgb_ref[:c, :]) + _dot(hi, wgb_ref[c:, :])
        hu = _dot(lo, wub_ref[:c, :]) + _dot(hi, wub_ref[c:, :])
        hid = hg * _sigmoid(hg) * hu
        o_ref[...] = _pack_halves(_dot(hid.astype(BF16), wdb_ref[...]))

    @pl.when(n_valid <= 0)
    def _():
        o_ref[...] = jnp.zeros_like(o_ref)


def _expert_ffn(x, blk_exp, blk_valid, w_gate, w_up, w_down, layer, tm, name):
    r = x.shape[0]
    d, ff = w_gate.shape[2:]
    return pl.pallas_call(
        _ffn_kernel,
        grid_spec=pltpu.PrefetchScalarGridSpec(
            num_scalar_prefetch=2,
            grid=(r // tm,),
            in_specs=[pl.BlockSpec((tm, d // 2), lambda i, be, nv: (i, 0)),
                      pl.BlockSpec((None, None, d, ff), lambda i, be, nv: (layer, be[i], 0, 0)),
                      pl.BlockSpec((None, None, d, ff), lambda i, be, nv: (layer, be[i], 0, 0)),
                      pl.BlockSpec((None, None, ff, d), lambda i, be, nv: (layer, be[i], 0, 0))],
            out_specs=pl.BlockSpec((tm, d // 2), lambda i, be, nv: (i, 0)),
            scratch_shapes=[pltpu.VMEM((d, ff), BF16), pltpu.VMEM((d, ff), BF16), pltpu.VMEM((ff, d), BF16)],
        ),
        out_shape=jax.ShapeDtypeStruct((r, d // 2), jnp.int32),
        compiler_params=_cparams(("arbitrary",), V7X_VMEM_LIMIT_BYTES),
        name=name,
    )(blk_exp, blk_valid, x, w_gate, w_up, w_down)


def _first_argmax(val, idx, sentinel):
    m = jnp.max(val, axis=0, keepdims=True)
    i = jnp.min(jnp.where(val == m, idx, sentinel), axis=0, keepdims=True)
    return m, i


def _route_kernel(x_ref, wt_ref, b_ref, e_ref, g_ref, c_ref):
    x = x_ref[...]
    w = wt_ref[...]
    t = x.shape[0]
    xh = x.astype(BF16)
    xl = (x - xh.astype(F32)).astype(BF16)
    wh = w.astype(BF16)
    wl = (w - wh.astype(F32)).astype(BF16)
    logits = _dot_nt(wh, xh) + (_dot_nt(wl, xh) + _dot_nt(wh, xl))
    scores = _sigmoid(logits)
    biased = scores + b_ref[...]
    gs = N_EXPERTS // N_GROUPS
    neg = -jnp.inf
    eid = lax.broadcasted_iota(jnp.int32, (N_EXPERTS, t), 0)
    sub = lax.broadcasted_iota(jnp.int32, (gs, t), 0)
    grow = lax.broadcasted_iota(jnp.int32, (N_GROUPS, t), 0)
    grp = jnp.zeros((N_GROUPS, t), F32)
    for g in range(N_GROUPS):
        bg = biased[g * gs:(g + 1) * gs]
        m1, i1 = _first_argmax(bg, sub, gs)
        m2 = jnp.max(jnp.where(sub == i1, neg, bg), axis=0, keepdims=True)
        grp = jnp.where(grow == g, m1 + m2, grp)
    chosen = jnp.zeros((N_GROUPS, t), F32)
    for _ in range(TOPK_GROUPS):
        _, gi = _first_argmax(grp, grow, N_GROUPS)
        hit = grow == gi
        chosen = jnp.where(hit, 1.0, chosen)
        grp = jnp.where(hit, neg, grp)
    chosen_e = jnp.concatenate([jnp.broadcast_to(chosen[g:g + 1], (gs, t)) for g in range(N_GROUPS)], axis=0)
    masked = jnp.where(chosen_e > 0.0, biased, neg)
    krow = lax.broadcasted_iota(jnp.int32, (TOP_K, t), 0)
    e_out = jnp.zeros((TOP_K, t), jnp.int32)
    g_out = jnp.zeros((TOP_K, t), F32)
    member = jnp.zeros((N_EXPERTS, t), F32)
    for k in range(TOP_K):
        _, idx = _first_argmax(masked, eid, N_EXPERTS)
        hit = eid == idx
        gk = jnp.sum(jnp.where(hit, scores, 0.0), axis=0, keepdims=True)
        masked = jnp.where(hit, neg, masked)
        member = jnp.where(hit, 1.0, member)
        e_out = jnp.where(krow == k, idx, e_out)
        g_out = jnp.where(krow == k, gk, g_out)
    g_out = g_out / jnp.sum(g_out, axis=0, keepdims=True) * ROUTED_SCALE
    e_ref[...] = e_out
    g_ref[...] = g_out
    c_ref[...] = jnp.sum(member, axis=1, keepdims=True).astype(jnp.int32)


def _route(x, router_w, layer, router_b, tm):
    n, d = x.shape
    e = router_w.shape[2]
    wt = jnp.swapaxes(router_w, 1, 2)
    return pl.pallas_call(
        _route_kernel,
        grid=(n // tm,),
        in_specs=[pl.BlockSpec((tm, d), lambda i: (i, 0)),
                  pl.BlockSpec((None, e, d), lambda i: (layer, 0, 0)),
                  pl.BlockSpec((e, 1), lambda i: (0, 0))],
        out_specs=[pl.BlockSpec((TOP_K, tm), lambda i: (0, i)),
                   pl.BlockSpec((TOP_K, tm), lambda i: (0, i)),
                   pl.BlockSpec((None, e, 1), lambda i: (i, 0, 0))],
        out_shape=[jax.ShapeDtypeStruct((TOP_K, n), jnp.int32), jax.ShapeDtypeStruct((TOP_K, n), F32),
                   jax.ShapeDtypeStruct((n // tm, e, 1), jnp.int32)],
        compiler_params=_cparams(("parallel",), V7X_VMEM_LIMIT_BYTES),
        name="route",
    )(x, wt, router_b[layer].astype(F32).reshape(e, 1))


def _slots_kernel(e_ref, base_ref, o_ref):
    e = e_ref[...]
    t = e.shape[1]
    eid = lax.broadcasted_iota(jnp.int32, (N_EXPERTS, t), 0)
    member = jnp.zeros((N_EXPERTS, t), F32)
    for k in range(TOP_K):
        member = jnp.where(eid == e[k:k + 1], 1.0, member)
    before = (lax.broadcasted_iota(jnp.int32, (t, t), 0) < lax.broadcasted_iota(jnp.int32, (t, t), 1))
    rank = _dot(member.astype(BF16), jnp.where(before, 1.0, 0.0).astype(BF16))
    slot = rank + base_ref[...].astype(F32)
    krow = lax.broadcasted_iota(jnp.int32, (TOP_K, t), 0)
    out = jnp.zeros((TOP_K, t), F32)
    for k in range(TOP_K):
        sk = jnp.sum(jnp.where(eid == e[k:k + 1], slot, 0.0), axis=0, keepdims=True)
        out = jnp.where(krow == k, sk, out)
    o_ref[...] = out.astype(jnp.int32)


def _dispatch_plan(e_idx, counts, tm, tile):
    k, n = e_idx.shape
    n_blocks = -(-(n * k) // tm) + N_EXPERTS
    counts = counts.reshape(n // tile, N_EXPERTS)
    total = jnp.sum(counts, axis=0)
    padded = (total + tm - 1) // tm * tm
    pad_end = jnp.cumsum(padded)
    tile_base = (pad_end - padded)[None, :] + jnp.cumsum(counts, axis=0) - counts
    slot_of = pl.pallas_call(
        _slots_kernel,
        grid=(n // tile,),
        in_specs=[pl.BlockSpec((k, tile), lambda i: (0, i)),
                  pl.BlockSpec((None, N_EXPERTS, 1), lambda i: (i, 0, 0))],
        out_specs=pl.BlockSpec((k, tile), lambda i: (0, i)),
        out_shape=jax.ShapeDtypeStruct((k, n), jnp.int32),
        compiler_params=_cparams(("parallel",)),
        name="slots",
    )(e_idx, tile_base.astype(jnp.int32).reshape(n // tile, N_EXPERTS, 1))
    blk_row0 = jnp.arange(n_blocks, dtype=jnp.int32) * tm
    blk_exp = jnp.minimum(jnp.sum((pad_end[None, :] <= blk_row0[:, None]).astype(jnp.int32), axis=1), N_EXPERTS - 1)
    real_end = pad_end - padded + total
    is_exp = blk_exp[:, None] == jnp.arange(N_EXPERTS, dtype=blk_exp.dtype)[None, :]
    blk_valid = jnp.clip(jnp.sum(jnp.where(is_exp, real_end[None, :], 0), axis=1) - blk_row0, 0, tm)
    return slot_of, blk_exp.astype(jnp.int32), blk_valid.astype(jnp.int32), n_blocks


V7X_SC_CORES = 2
V7X_SC_SUBCORES = 16
SC_WORKERS = V7X_SC_CORES * V7X_SC_SUBCORES
SC_CHUNK = 32
SC_GATHER_CHUNK = 40


def _sc_mesh():
    return plsc.VectorSubcoreMesh(core_axis_name="c", subcore_axis_name="s",
                                  num_cores=V7X_SC_CORES, num_subcores=V7X_SC_SUBCORES)


def _sc_worker_id():
    return lax.axis_index("s") * V7X_SC_CORES + lax.axis_index("c")


def _sc_gather_rows(table, slot_of):
    k, n = slot_of.shape
    w = table.shape[1]
    n_rows = k * n
    ch = SC_GATHER_CHUNK
    per_worker = n_rows // (SC_WORKERS * ch)
    assert per_worker * SC_WORKERS * ch == n_rows and per_worker % 2 == 0
    idx = slot_of.reshape(SC_WORKERS, per_worker, ch)

    def body(table_hbm, idx_hbm, out_hbm, idx_v, rows_v, sems):
        wid = _sc_worker_id()
        pltpu.sync_copy(idx_hbm.at[wid], idx_v)

        def gather(c, slot):
            return pltpu.make_async_copy(table_hbm.at[idx_v.at[c]], rows_v.at[slot], sems.at[slot])

        gather(0, 0).start()

        @pl.loop(0, per_worker, step=2)
        def _(c0):
            for slot in range(2):
                c = c0 + slot
                gather(c, slot).wait()

                @pl.when(c + 1 < per_worker)
                def _():
                    gather(c + 1, 1 - slot).start()

                row0 = pl.multiple_of((wid * per_worker + c) * ch, 8)
                pltpu.sync_copy(rows_v.at[slot], out_hbm.at[pl.ds(row0, ch)])

    return pl.kernel(
        body, out_type=jax.ShapeDtypeStruct((n_rows, w), table.dtype), mesh=_sc_mesh(),
        scratch_types=[pltpu.VMEM((per_worker, ch), jnp.int32), pltpu.VMEM((2, ch, w), table.dtype),
                       pltpu.SemaphoreType.DMA((2,))],
        name="sc_gather_rows",
    )(table, idx)


def _sc_scatter_rows(x, slot_of, n_slots):
    k, n = slot_of.shape
    w = x.shape[1]
    n_chunks = n // SC_CHUNK
    assert n_chunks * SC_CHUNK == n
    idx = slot_of.reshape(k, n_chunks, SC_CHUNK).transpose(1, 0, 2)
    rounds = -(-n_chunks // SC_WORKERS)

    def body(x_hbm, idx_hbm, out_hbm, idx_v, rows_v, sem):
        wid = _sc_worker_id()

        @pl.loop(0, rounds)
        def _(j):
            q = j * SC_WORKERS + wid

            @pl.when(q < n_chunks)
            def _():
                pltpu.sync_copy(idx_hbm.at[q], idx_v)
                pltpu.sync_copy(x_hbm.at[pl.ds(pl.multiple_of(q * SC_CHUNK, SC_CHUNK), SC_CHUNK)], rows_v)
                copies = [pltpu.make_async_copy(rows_v, out_hbm.at[idx_v.at[kk]], sem) for kk in range(k)]
                for cp in copies:
                    cp.start()
                for cp in copies:
                    cp.wait()

    return pl.kernel(
        body, out_type=jax.ShapeDtypeStruct((n_slots, w), x.dtype), mesh=_sc_mesh(),
        scratch_types=[pltpu.VMEM((k, SC_CHUNK), jnp.int32), pltpu.VMEM((SC_CHUNK, w), x.dtype),
                       pltpu.SemaphoreType.DMA],
        name="sc_scatter_rows",
    )(x, idx)


MOE_BLOCK = 384
ROW_TILE = 640


def _moe_postnorm_kernel(x_ref, y_ref, gate_ref, sh_ref, g_ref, b_ref, ofp_ref, ofs_ref, ob_ref, *, alpha, tiles_p):
    def part(which):
        of_ref = (ofp_ref, ofs_ref)[which]
        gate = gate_ref[...]
        lo, hi = _unpack_halves(sh_ref[...])
        for k in range(TOP_K):
            lo_k, hi_k = _unpack_halves(y_ref[k])
            lo = lo + gate[:, k:k + 1] * lo_k
            hi = hi + gate[:, k:k + 1] * hi_k
        z = alpha * x_ref[...] + jnp.concatenate([lo, hi], axis=1)
        out = _layer_norm(z, g_ref[...], b_ref[...])
        of_ref[...] = out
        ob_ref[...] = out.astype(BF16)

    _on_part(tiles_p, part)


def _moe_postnorm(x, n_p, y_tok, gate, shared, g, b, alpha):
    m, d = x.shape
    tm = PART_TILE
    row = pl.BlockSpec((tm, d), lambda i: (i, 0))
    vec = pl.BlockSpec((1, d), lambda i: (0, 0))
    part_p, part_s = _part_specs(n_p, (tm, d))
    return pl.pallas_call(
        functools.partial(_moe_postnorm_kernel, alpha=alpha, tiles_p=n_p // tm),
        grid=(m // tm,),
        in_specs=[row, pl.BlockSpec((TOP_K, tm, d // 2), lambda i: (0, i, 0)),
                  pl.BlockSpec((tm, TOP_K), lambda i: (i, 0)), pl.BlockSpec((tm, d // 2), lambda i: (i, 0)), vec, vec],
        out_specs=[part_p, part_s, row],
        out_shape=[jax.ShapeDtypeStruct((n_p, d), F32), jax.ShapeDtypeStruct((m - n_p, d), F32),
                   jax.ShapeDtypeStruct((m, d), BF16)],
        compiler_params=_cparams(("arbitrary",), V7X_VMEM_LIMIT_BYTES),
        name="postnorm_moe",
    )(x, y_tok, gate, shared, g.reshape(1, d), b.reshape(1, d))


def _moe(xf, n_p, xp, layer, router_w, router_b, w_gate, w_up, w_down, sw_gate, sw_up, sw_down, ln_g, ln_b, alpha):
    n, d = xf.shape
    e_idx, gate, counts = _route(xf, router_w, layer, router_b, ROW_TILE)
    slot_of, blk_exp, blk_valid, n_blocks = _dispatch_plan(e_idx, counts, MOE_BLOCK, ROW_TILE)
    x_sorted = _sc_scatter_rows(xp, slot_of, n_blocks * MOE_BLOCK)
    y_sorted = _expert_ffn(x_sorted, blk_exp, blk_valid, w_gate, w_up, w_down, layer, MOE_BLOCK, "routed_ffn")
    y_tok = _sc_gather_rows(y_sorted, slot_of).reshape(TOP_K, n, d // 2)
    n_sh = n // ROW_TILE
    shared = _expert_ffn(xp, jnp.zeros((n_sh,), jnp.int32), jnp.full((n_sh,), ROW_TILE, jnp.int32),
                         sw_gate[:, None], sw_up[:, None], sw_down[:, None], layer, ROW_TILE, "shared_ffn")
    return _moe_postnorm(xf, n_p, y_tok, gate.T, shared, ln_g, ln_b, alpha)


def kernel(x_prompt, x_sample, state_hgrn, cache_win_k, cache_win_v, cache_mem_k, cache_mem_v, mem_prompt,
           w_in_a, lb_logits, g_norm_a, w_out_a, w_in_b, w_out_b, w_kv_shared, w_mem_kv, ln_g, ln_b,
           router_w, router_b, exp_w_gate, exp_w_up, exp_w_down, sh_w_gate, sh_w_up, sh_w_down):
    bp, sp, d = x_prompt.shape
    ns = x_sample.shape[0]
    assert x_sample.shape[1] == 1
    depth = ln_g.shape[0]
    n_a = w_in_a.shape[0]
    alpha = (2 * depth) ** 0.25
    n_p = bp * sp
    mem_len = mem_prompt.shape[1]
    a_mix = 4 * A_HEADS * HEAD_DIM

    xf_p, xf_s = x_prompt.reshape(n_p, d), x_sample.reshape(ns, d)
    xb = jnp.concatenate([xf_p.astype(BF16), xf_s.astype(BF16)], axis=0)
    lower_bounds = jnp.cumsum(jax.nn.softmax(lb_logits.astype(F32), axis=0), axis=0)
    mem_flat = mem_prompt.reshape(bp * mem_len, d)
    pos_all = jnp.concatenate([jnp.tile(jnp.arange(sp, dtype=jnp.int32), bp),
                               jnp.full((ns,), PAST_LEN, jnp.int32)])
    tables = _rope_tables(pos_all)

    hgrn_p, hgrn_s, mem_k_p, mem_v_p = [], [], [], []
    for layer in range(depth):
        kvm = _proj([mem_flat], w_mem_kv, layer, F32, 2 * mem_len, 1024, "mem_kv")
        mem_k_p.append(kvm[:, :MEM_W].reshape(bp, mem_len, MEM_HEADS, HEAD_DIM))
        mem_v_p.append(kvm[:, MEM_W:].reshape(bp, mem_len, MEM_HEADS, HEAD_DIM))
        if layer < n_a:
            a = layer
            proj = _proj([xb], w_in_a, a, F32, ROW_TILE // 2, 1664, "proj_in_a")
            proj_s = proj[n_p:]
            o_x_s, st_s = _hgrn_step(proj_s.reshape(ns, 1, -1), state_hgrn, a, lower_bounds[a], g_norm_a[a])
            o_m_s = _mem_attn_decode(proj_s[:, a_mix:].reshape(ns, MEM_HEADS, HEAD_DIM), cache_mem_k, cache_mem_v, layer)
            o_x, st_p = _hgrn_prompt(proj, lower_bounds[a], g_norm_a[a], o_x_s.reshape(ns, -1), bp, sp)
            o_m = _mem_attn_prompt(proj, a_mix // MEM_W, kvm, o_m_s.reshape(ns, -1), bp, sp, 512)
            hgrn_p.append(st_p)
            hgrn_s.append(st_s)
            w_out, w_out_layer = w_out_a, a
        else:
            bl = layer - n_a
            if layer == n_a:
                kv = _proj([xb], w_kv_shared[None], 0, F32, ROW_TILE, 1024, "proj_kv")
                k_r, k_p, k_s, v_p, v_s = _shared_kv(kv, tables, n_p)
            proj = _proj([xb], w_in_b, bl, F32, ROW_TILE, 1024, "proj_in_b")
            q_r = _rope(proj, B_QHEADS, tables, F32, ROW_TILE, "rope_q")
            q_s = q_r[n_p:].reshape(ns, B_GROUPS, B_SLOTS, HEAD_DIM)
            o_x_s = _dil_attn_decode(q_s, k_s, v_s, cache_win_k, cache_win_v)
            o_m_s = _mem_attn_decode(proj[n_p:, B_QHEADS * HEAD_DIM:].reshape(ns, MEM_HEADS, HEAD_DIM),
                                     cache_mem_k, cache_mem_v, layer)
            o_x = _dil_attn_prompt(q_r, k_r, kv, o_x_s.reshape(ns, -1), bp, sp)
            o_m = _mem_attn_prompt(proj, B_QHEADS * HEAD_DIM // MEM_W, kvm, o_m_s.reshape(ns, -1), bp, sp, 512)
            w_out, w_out_layer = w_out_b, bl
        y = _proj([o_x, o_m], w_out, w_out_layer, F32, ROW_TILE, 1024, "proj_out")
        xf, xp = _postnorm(xf_p, xf_s, y, ln_g[layer, 0], ln_b[layer, 0], alpha, "postnorm_mix")
        xf_p, xf_s, xb = _moe(xf, n_p, xp, layer, router_w, router_b, exp_w_gate, exp_w_up, exp_w_down,
                              sh_w_gate, sh_w_up, sh_w_down, ln_g[layer, 1], ln_b[layer, 1], alpha)

    w_p = min(max(w for w, _ in B_PATTERNS), sp)
    k_p = k_p.reshape(bp, sp, B_SLOTS, HEAD_DIM)
    v_p = v_p.reshape(bp, sp, B_SLOTS, HEAD_DIM)
    return (xf_p.reshape(bp, sp, d), xf_s.reshape(ns, 1, d),
            jnp.stack(hgrn_p), jnp.stack(hgrn_s),
            k_p[:, sp - w_p:], v_p[:, sp - w_p:],
            k_s.reshape(ns, 1, B_SLOTS, HEAD_DIM), v_s.reshape(ns, 1, B_SLOTS, HEAD_DIM),
            jnp.stack(mem_k_p), jnp.stack(mem_v_p))
```

```python
import functools

import jax
import jax.numpy as jnp
from jax import lax
from jax.experimental import pallas as pl
from jax.experimental.pallas import tpu as pltpu
from jax.experimental.pallas import tpu_sc as plsc

F32 = jnp.float32
BF16 = jnp.bfloat16

HEAD_DIM = 128
A_HEADS = 12
A_CHUNK = 64
A_SUB = 16
B_PATTERNS = ((128, 1), (512, 4), (2048, 16))
B_SLOTS = 4
B_GROUPS = len(B_PATTERNS)
B_QHEADS = B_GROUPS * B_SLOTS
B_BLOCK = 128
MEM_HEADS = 4
MEM_W = MEM_HEADS * HEAD_DIM
KV_W = B_SLOTS * HEAD_DIM
ROPE_THETA = 500000.0
ROPE_DIM = HEAD_DIM // 4
N_EXPERTS = 64
N_GROUPS = 8
TOPK_GROUPS = 4
TOP_K = 8
ROUTED_SCALE = 2.5
LN_EPS = 1e-5
RMS_EPS = 1e-6
ATTN_SCALE = HEAD_DIM ** -0.5
PAST_LEN = 2048

V7X_VMEM_LIMIT_BYTES = 56 * 1024 * 1024
LANES = 128
NEG_BIG = -1e30


def _cparams(sem, vmem=None):
    return pltpu.CompilerParams(dimension_semantics=sem, vmem_limit_bytes=vmem)


def _dot(a, b):
    return jnp.dot(a, b, preferred_element_type=F32)


def _dot_nt(a, b):
    return lax.dot_general(a, b, (((1,), (1,)), ((), ())), preferred_element_type=F32)


def _sigmoid(x):
    return 1.0 / (1.0 + jnp.exp(-x))


def _proj_kernel(*refs, n_lhs):
    x_refs = refs[:n_lhs]
    w_refs = refs[n_lhs:2 * n_lhs]
    o_ref = refs[2 * n_lhs]
    wb_refs = refs[2 * n_lhs + 1:]

    @pl.when(pl.program_id(1) == 0)
    def _():
        for w_ref, wb_ref in zip(w_refs, wb_refs):
            wb_ref[...] = w_ref[...].astype(BF16)

    acc = None
    for x_ref, wb_ref in zip(x_refs, wb_refs):
        d = _dot(x_ref[...].astype(BF16), wb_ref[...])
        acc = d if acc is None else acc + d
    o_ref[...] = acc.astype(o_ref.dtype)


def _proj(lhs, w, layer, out_dtype, tm, tn, name):
    m = lhs[0].shape[0]
    n = w.shape[2]
    koff = 0
    in_specs, w_specs, scratch = [], [], []
    for x in lhs:
        k = x.shape[1]
        assert koff % k == 0 and m % tm == 0 and n % tn == 0
        in_specs.append(pl.BlockSpec((tm, k), lambda j, i: (i, 0)))
        w_specs.append(pl.BlockSpec((None, k, tn), functools.partial(lambda j, i, kb: (layer, kb, j), kb=koff // k)))
        scratch.append(pltpu.VMEM((k, tn), BF16))
        koff += k
    assert koff == w.shape[1]
    return pl.pallas_call(
        functools.partial(_proj_kernel, n_lhs=len(lhs)),
        grid=(n // tn, m // tm),
        in_specs=in_specs + w_specs,
        out_specs=pl.BlockSpec((tm, tn), lambda j, i: (i, j)),
        out_shape=jax.ShapeDtypeStruct((m, n), out_dtype),
        scratch_shapes=scratch,
        compiler_params=_cparams(("arbitrary", "arbitrary"), V7X_VMEM_LIMIT_BYTES),
        name=name,
    )(*lhs, *([w] * len(lhs)))


def _pack_halves(x):
    c = x.shape[1] // 2
    lo = pltpu.bitcast(x[:, :c].astype(BF16).astype(F32), jnp.int32)
    hi = pltpu.bitcast(x[:, c:].astype(BF16).astype(F32), jnp.int32)
    return hi | lax.shift_right_logical(lo, 16)


def _unpack_halves(w):
    lo = pltpu.bitcast(lax.shift_left(w, 16), F32)
    hi = pltpu.bitcast(w & jnp.int32(-65536), F32)
    return lo, hi


def _layer_norm(z, g, b):
    mu = jnp.mean(z, axis=-1, keepdims=True)
    zc = z - mu
    var = jnp.mean(zc * zc, axis=-1, keepdims=True)
    return zc * lax.rsqrt(var + LN_EPS) * g + b


PART_TILE = 128


def _part_specs(n_p, block):
    tiles_p = n_p // PART_TILE
    rest = (0,) * (len(block) - 1)
    return (pl.BlockSpec(block, lambda i: (jnp.minimum(i, tiles_p - 1),) + rest),
            pl.BlockSpec(block, lambda i: (jnp.maximum(i - tiles_p, 0),) + rest))


def _on_part(tiles_p, fn):
    i = pl.program_id(0)
    pl.when(i < tiles_p)(functools.partial(fn, 0))
    pl.when(i >= tiles_p)(functools.partial(fn, 1))


def _postnorm_kernel(xp_ref, xs_ref, y_ref, g_ref, b_ref, of_ref, op_ref, *, alpha, tiles_p):
    def part(which):
        x_ref = (xp_ref, xs_ref)[which]
        out = _layer_norm(alpha * x_ref[...] + y_ref[...], g_ref[...], b_ref[...])
        of_ref[...] = out
        op_ref[...] = _pack_halves(out)

    _on_part(tiles_p, part)


def _postnorm(x_p, x_s, y, g, b, alpha, name):
    n_p, d = x_p.shape
    m = n_p + x_s.shape[0]
    row = pl.BlockSpec((PART_TILE, d), lambda i: (i, 0))
    vec = pl.BlockSpec((1, d), lambda i: (0, 0))
    part_p, part_s = _part_specs(n_p, (PART_TILE, d))
    return pl.pallas_call(
        functools.partial(_postnorm_kernel, alpha=alpha, tiles_p=n_p // PART_TILE),
        grid=(m // PART_TILE,),
        in_specs=[part_p, part_s, row, vec, vec],
        out_specs=[row, pl.BlockSpec((PART_TILE, d // 2), lambda i: (i, 0))],
        out_shape=[jax.ShapeDtypeStruct((m, d), F32), jax.ShapeDtypeStruct((m, d // 2), jnp.int32)],
        compiler_params=_cparams(("arbitrary",)),
        name=name,
    )(x_p, x_s, y, g.reshape(1, d), b.reshape(1, d))


def _hgrn_prompt_kernel(q_ref, f_ref, v_ref, gate_ref, lb_ref, gn_ref, os_ref, o_ref, s_ref, st_ref,
                        *, seq, heads, n_batch):
    _prompt_then_sample(n_batch, os_ref, o_ref,
                        functools.partial(_hgrn_sequence, q_ref, f_ref, v_ref, gate_ref, lb_ref, gn_ref,
                                          o_ref, s_ref, st_ref, seq=seq, heads=heads), axis=1)


def _hgrn_sequence(q_ref, f_ref, v_ref, gate_ref, lb_ref, gn_ref, o_ref, s_ref, st_ref, *, seq, heads):
    c = A_CHUNK
    n_chunks = seq // c
    n_sub = c // A_SUB
    gn = gn_ref[...]
    row = lax.broadcasted_iota(jnp.int32, (c, c), 0)
    col = lax.broadcasted_iota(jnp.int32, (c, c), 1)
    tril = jnp.where(row >= col, 1.0, 0.0).astype(BF16)
    same_sub = (row // A_SUB) == (col // A_SUB)
    diag_dist = jnp.where(same_sub, row - col, -1)
    off_mask = col < (row // A_SUB) * A_SUB
    st_ref[...] = jnp.zeros_like(st_ref)

    def chunk(ci, carry):
        for hh in range(heads):
            head_chunk(ci, hh)
        return carry

    def head_chunk(ci, hh):
        r0 = pl.multiple_of(ci * c, c)
        hsl = slice(hh * HEAD_DIM, (hh + 1) * HEAD_DIM)
        lb = lb_ref[hh]
        qp = q_ref[pl.ds(r0, c), hsl]
        fp = f_ref[pl.ds(r0, c), hsl]
        v = v_ref[pl.ds(r0, c), hsl]
        gp = gate_ref[pl.ds(r0, c), hsl]
        st = st_ref[hh]
        q = qp * _sigmoid(qp)
        forget = lb + (1.0 - lb) * _sigmoid(fp)
        logf = jnp.log2(forget)
        k = 1.0 - forget
        hi = logf.astype(BF16)
        r1 = logf - hi.astype(F32)
        mid = r1.astype(BF16)
        lo = (r1 - mid.astype(F32)).astype(BF16)
        g = _dot(tril, hi) + _dot(tril, mid) + _dot(tril, lo)
        v_b = v.astype(BF16)
        o = _dot_nt((q * jnp.exp2(g)).astype(BF16), st.astype(BF16))
        rows = [jnp.zeros((A_SUB, c), F32)]
        for i in range(1, n_sub):
            gref = g[i * A_SUB:i * A_SUB + 1, :]
            qt = q[i * A_SUB:(i + 1) * A_SUB, :] * jnp.exp2(g[i * A_SUB:(i + 1) * A_SUB, :] - gref)
            kt = k * jnp.exp2(jnp.minimum(gref - g, 0.0))
            rows.append(_dot_nt(qt.astype(BF16), kt.astype(BF16)))
        a = jnp.where(off_mask, jnp.concatenate(rows, axis=0), 0.0)
        for d in range(A_SUB):
            kr = k if d == 0 else pltpu.roll(k, d, axis=0)
            gr = g if d == 0 else pltpu.roll(g, d, axis=0)
            x = q * kr * jnp.exp2(g - gr)
            a = jnp.where(diag_dist == d, jnp.sum(x, axis=-1, keepdims=True), a)
        o = o + _dot(a.astype(BF16), v_b)
        gend = g[c - 1:c, :]
        kt_end = k * jnp.exp2(gend - g)
        st_ref[hh] = jnp.exp2(gend) * st + _dot(v_b.T, kt_end.astype(BF16))
        o = o * lax.rsqrt(jnp.mean(o * o, axis=-1, keepdims=True) + RMS_EPS) * gn
        o_ref[pl.ds(r0, c), hsl] = (o * (gp * _sigmoid(gp))).astype(o_ref.dtype)

    lax.fori_loop(0, n_chunks, chunk, 0)
    for hh in range(heads):
        s_ref[hh] = st_ref[hh].T


A_HEADS_PER_STEP = 4


def _hgrn_prompt(proj, lb, g_norm, o_sample, n_batch, seq):
    h = A_HEADS
    hp = A_HEADS_PER_STEP
    ns = o_sample.shape[0]
    assert h % hp == 0 and ns <= seq and proj.shape[0] == n_batch * seq + ns
    ng = h // hp
    w = hp * HEAD_DIM
    last = n_batch - 1
    blk = lambda off: pl.BlockSpec((seq, w), functools.partial(lambda hg, b, off: (jnp.minimum(b, last), off + hg), off=off))
    return pl.pallas_call(
        functools.partial(_hgrn_prompt_kernel, seq=seq, heads=hp, n_batch=n_batch),
        grid=(ng, n_batch + 1),
        in_specs=[blk(0), blk(ng), blk(2 * ng), blk(3 * ng),
                  pl.BlockSpec((hp, 1, HEAD_DIM), lambda hg, b: (hg, 0, 0)),
                  pl.BlockSpec((1, HEAD_DIM), lambda hg, b: (0, 0)),
                  pl.BlockSpec((ns, w), lambda hg, b: (0, hg))],
        out_specs=[pl.BlockSpec((seq, w), lambda hg, b: (b, hg)),
                   pl.BlockSpec((None, hp, HEAD_DIM, HEAD_DIM), lambda hg, b: (jnp.minimum(b, last), hg, 0, 0))],
        out_shape=[jax.ShapeDtypeStruct((proj.shape[0], h * HEAD_DIM), BF16),
                   jax.ShapeDtypeStruct((n_batch, h, HEAD_DIM, HEAD_DIM), F32)],
        scratch_shapes=[pltpu.VMEM((hp, HEAD_DIM, HEAD_DIM), F32)],
        compiler_params=_cparams(("arbitrary", "arbitrary"), V7X_VMEM_LIMIT_BYTES),
        name="hgrn_prompt",
    )(proj, proj, proj, proj, lb.reshape(h, 1, HEAD_DIM), g_norm.reshape(1, HEAD_DIM), o_sample)


def _softmax_av(s, v):
    m = jnp.max(s, axis=-1, keepdims=True)
    p = jnp.exp(s - m)
    l = jnp.sum(p, axis=-1, keepdims=True)
    return _dot(p.astype(BF16), v) / l


def _prompt_then_sample(n_prompt_steps, os_ref, o_ref, prompt_step, axis=0):
    i = pl.program_id(axis)
    pl.when(i < n_prompt_steps)(prompt_step)

    @pl.when(i >= n_prompt_steps)
    def _():
        o_ref[:os_ref.shape[0], :] = os_ref[...]


def _mem_attn_kernel(q_ref, k_ref, v_ref, os_ref, o_ref, *, n_steps):
    def step():
        for h in range(MEM_HEADS):
            sl = slice(h * HEAD_DIM, (h + 1) * HEAD_DIM)
            s = _dot_nt(q_ref[:, sl].astype(BF16), k_ref[:, sl].astype(BF16)) * ATTN_SCALE
            o_ref[:, sl] = _softmax_av(s, v_ref[:, sl].astype(BF16)).astype(o_ref.dtype)

    _prompt_then_sample(n_steps, os_ref, o_ref, step)


def _mem_attn_prompt(proj, q_col, kv, o_sample, n_batch, seq, tq):
    mem_len = kv.shape[0] // n_batch
    nq = seq // tq
    n_steps = n_batch * nq
    ns = o_sample.shape[0]
    assert ns <= tq and proj.shape[0] == n_batch * seq + ns
    last = n_steps - 1
    return pl.pallas_call(
        functools.partial(_mem_attn_kernel, n_steps=n_steps),
        grid=(n_steps + 1,),
        in_specs=[pl.BlockSpec((tq, MEM_W), lambda i: (jnp.minimum(i, last), q_col)),
                  pl.BlockSpec((mem_len, MEM_W), lambda i: (jnp.minimum(i, last) // nq, 0)),
                  pl.BlockSpec((mem_len, MEM_W), lambda i: (jnp.minimum(i, last) // nq, 1)),
                  pl.BlockSpec((ns, MEM_W), lambda i: (0, 0))],
        out_specs=pl.BlockSpec((tq, MEM_W), lambda i: (i, 0)),
        out_shape=jax.ShapeDtypeStruct((proj.shape[0], MEM_W), BF16),
        compiler_params=_cparams(("arbitrary",)),
        name="mem_attn_prompt",
    )(proj, kv, kv, o_sample)


def _rope_tables(pos):
    half = ROPE_DIM // 2
    inv_freq = ROPE_THETA ** (-jnp.arange(0, ROPE_DIM, 2, dtype=F32) / ROPE_DIM)
    ang = pos.astype(F32)[:, None] * inv_freq[None, :]
    cos, sin = jnp.cos(ang), jnp.sin(ang)
    n = pos.shape[0]
    one = jnp.ones((n, HEAD_DIM - ROPE_DIM), F32)
    zero = jnp.zeros((n, HEAD_DIM - half), F32)
    c = jnp.concatenate([cos, cos, one], axis=1)
    s_up = jnp.concatenate([-sin, zero], axis=1)
    s_dn = jnp.concatenate([jnp.zeros((n, half), F32), sin, zero[:, half:]], axis=1)
    return c, s_up, s_dn


def _rope_kernel(x_ref, c_ref, su_ref, sd_ref, o_ref, *, n_heads):
    c, su, sd = c_ref[...], su_ref[...], sd_ref[...]
    for h in range(n_heads):
        sl = slice(h * HEAD_DIM, (h + 1) * HEAD_DIM)
        o_ref[:, sl] = _rope_head(x_ref[:, sl], c, su, sd).astype(o_ref.dtype)


def _rope_head(x, c, su, sd):
    half = ROPE_DIM // 2
    up = pltpu.roll(x, HEAD_DIM - half, axis=1)
    dn = pltpu.roll(x, half, axis=1)
    return x * c + up * su + dn * sd


def _kv_kernel(kv_ref, c_ref, su_ref, sd_ref, k_ref, kp_ref, ks_ref, vp_ref, vs_ref, *, tiles_p):
    c, su, sd = c_ref[...], su_ref[...], sd_ref[...]

    def part(which):
        k4_ref, v4_ref = ((kp_ref, vp_ref), (ks_ref, vs_ref))[which]
        for h in range(B_SLOTS):
            sl = slice(h * HEAD_DIM, (h + 1) * HEAD_DIM)
            k = _rope_head(kv_ref[:, sl], c, su, sd)
            k_ref[:, sl] = k
            k4_ref[:, h, :] = k
            v4_ref[:, h, :] = kv_ref[:, KV_W + h * HEAD_DIM:KV_W + (h + 1) * HEAD_DIM]

    _on_part(tiles_p, part)


def _shared_kv(kv, tables, n_p):
    m = kv.shape[0]
    tm = PART_TILE
    tab = pl.BlockSpec((tm, HEAD_DIM), lambda i: (i, 0))
    part_p, part_s = _part_specs(n_p, (tm, B_SLOTS, HEAD_DIM))
    cache = lambda rows: jax.ShapeDtypeStruct((rows, B_SLOTS, HEAD_DIM), F32)
    return pl.pallas_call(
        functools.partial(_kv_kernel, tiles_p=n_p // tm),
        grid=(m // tm,),
        in_specs=[pl.BlockSpec((tm, 2 * KV_W), lambda i: (i, 0)), tab, tab, tab],
        out_specs=[pl.BlockSpec((tm, KV_W), lambda i: (i, 0)), part_p, part_s, part_p, part_s],
        out_shape=[jax.ShapeDtypeStruct((m, KV_W), F32), cache(n_p), cache(m - n_p), cache(n_p), cache(m - n_p)],
        compiler_params=_cparams(("arbitrary",)),
        name="shared_kv",
    )(kv, *tables)


def _rope(x, n_heads, tables, out_dtype, tm, name):
    m = x.shape[0]
    w = n_heads * HEAD_DIM
    tab = pl.BlockSpec((tm, HEAD_DIM), lambda i: (i, 0))
    return pl.pallas_call(
        functools.partial(_rope_kernel, n_heads=n_heads),
        grid=(m // tm,),
        in_specs=[pl.BlockSpec((tm, w), lambda i: (i, 0)), tab, tab, tab],
        out_specs=pl.BlockSpec((tm, w), lambda i: (i, 0)),
        out_shape=jax.ShapeDtypeStruct((m, w), out_dtype),
        compiler_params=_cparams(("parallel",)),
        name=name,
    )(x, *tables)


DIL_BATCH = 8


def _dil_prompt_kernel(*refs, seq, n_batch):
    os_ref, o_ref = refs[B_GROUPS + 2:B_GROUPS + 4]
    _prompt_then_sample(n_batch, os_ref, o_ref, functools.partial(_dil_sequence, *refs, seq=seq), axis=1)


def _dil_sequence(*refs, seq):
    q_refs = refs[:B_GROUPS]
    k_ref, v_ref, _, o_ref, m_ref, l_ref, acc_ref = refs[B_GROUPS:]
    blk = B_BLOCK
    base = (lax.broadcasted_iota(jnp.int32, (blk, blk), 0) - lax.broadcasted_iota(jnp.int32, (blk, blk), 1))
    for g, (win, dil) in enumerate(B_PATTERNS):
        span = win // dil
        n_blk = seq // dil // blk
        assert span <= blk

        def body(it, carry, g=g, dil=dil, span=span, n_blk=n_blk):
            own, prev, prev_ok = [], [], []
            for j in range(DIL_BATCH):
                t = it * DIL_BATCH + j
                r, n = t % dil, t // dil

                def rows(nn, r=r):
                    start = nn * (blk * dil) + r
                    return pl.ds(pl.multiple_of(start, blk), blk) if dil == 1 else pl.ds(start, blk, stride=dil)

                own.append(rows(n))
                prev.append(rows(jnp.maximum(n - 1, 0)))
                prev_ok.append(base <= jnp.where(n > 0, span - blk, -blk - 1))
            load = lambda ref, idx: jnp.stack([ref[i, :].astype(BF16) for i in idx])
            scores = lambda a, b: jnp.einsum('bqd,bkd->bqk', a, b, preferred_element_type=F32) * ATTN_SCALE
            weighted = lambda p, v: jnp.einsum('bqk,bkd->bqd', p.astype(BF16), v, preferred_element_type=F32)
            q = load(q_refs[g], own)
            s = jnp.where((base >= 0)[None], scores(q, load(k_ref, own)), NEG_BIG)
            m_b = jnp.max(s, axis=-1, keepdims=True)
            if n_blk > 1:
                s_prev = jnp.where(jnp.stack(prev_ok), scores(q, load(k_ref, prev)), NEG_BIG)
                m_b = jnp.maximum(m_b, jnp.max(s_prev, axis=-1, keepdims=True))
            p = jnp.exp(s - m_b)
            l_b = jnp.sum(p, axis=-1, keepdims=True)
            acc_b = weighted(p, load(v_ref, own))
            if n_blk > 1:
                p_prev = jnp.exp(s_prev - m_b)
                l_b = l_b + jnp.sum(p_prev, axis=-1, keepdims=True)
                acc_b = acc_b + weighted(p_prev, load(v_ref, prev))
            for j, idx in enumerate(own):
                if g == 0:
                    m_ref[idx, :] = jnp.broadcast_to(m_b[j], (blk, HEAD_DIM))
                    l_ref[idx, :] = jnp.broadcast_to(l_b[j], (blk, HEAD_DIM))
                    acc_ref[idx, :] = acc_b[j]
                else:
                    m_old = m_ref[idx, :]
                    m_new = jnp.maximum(m_old, m_b[j])
                    a_old = jnp.exp(m_old - m_new)
                    a_b = jnp.exp(m_b[j] - m_new)
                    m_ref[idx, :] = m_new
                    l_ref[idx, :] = l_ref[idx, :] * a_old + l_b[j] * a_b
                    acc_ref[idx, :] = acc_ref[idx, :] * a_old + acc_b[j] * a_b
            return carry

        assert (dil * n_blk) % DIL_BATCH == 0
        lax.fori_loop(0, dil * n_blk // DIL_BATCH, body, 0)
    o_ref[...] = (acc_ref[...] / l_ref[...]).astype(o_ref.dtype)


def _dil_attn_prompt(q, k, kv, o_sample, n_batch, seq):
    ns = o_sample.shape[0]
    assert ns <= seq and q.shape[0] == n_batch * seq + ns
    last = n_batch - 1
    col = lambda c0: pl.BlockSpec((seq, HEAD_DIM), functools.partial(lambda h, b, c0: (jnp.minimum(b, last), c0 + h), c0=c0))
    return pl.pallas_call(
        functools.partial(_dil_prompt_kernel, seq=seq, n_batch=n_batch),
        grid=(B_SLOTS, n_batch + 1),
        in_specs=[col(g * B_SLOTS) for g in range(B_GROUPS)] + [col(0), col(B_SLOTS),
                                                                 pl.BlockSpec((ns, HEAD_DIM), lambda h, b: (0, h))],
        out_specs=pl.BlockSpec((seq, HEAD_DIM), lambda h, b: (b, h)),
        out_shape=jax.ShapeDtypeStruct((q.shape[0], KV_W), BF16),
        scratch_shapes=[pltpu.VMEM((seq, HEAD_DIM), F32)] * 3,
        compiler_params=_cparams(("arbitrary", "arbitrary"), V7X_VMEM_LIMIT_BYTES),
        name="dil_attn_prompt",
    )(*([q] * B_GROUPS), k, kv, o_sample)


def _decode_scores(q, k):
    return jnp.sum(k * q[None], axis=-1, keepdims=True) * ATTN_SCALE


def _decode_attend(parts):
    m = None
    for s, _ in parts:
        sm = jnp.max(s, axis=0)
        m = sm if m is None else jnp.maximum(m, sm)
    l = jnp.zeros_like(m)
    acc = None
    for s, v in parts:
        p = jnp.exp(s - m[None])
        l = l + jnp.sum(p, axis=0)
        pv = jnp.sum(p * v, axis=0)
        acc = pv if acc is None else acc + pv
    return acc / l


MEM_DECODE_SEQS = 4


def _mem_decode_kernel(q_ref, k_ref, v_ref, o_ref):
    n_seq, rows, _ = k_ref.shape
    nh = q_ref.shape[1]
    own = (lax.broadcasted_iota(jnp.int32, (nh, rows), 1) % nh) == lax.broadcasted_iota(jnp.int32, (nh, rows), 0)
    for i in range(n_seq):
        s = _dot_nt(q_ref[i].astype(BF16), k_ref[i].astype(BF16)) * ATTN_SCALE
        o_ref[i] = _softmax_av(jnp.where(own, s, NEG_BIG), v_ref[i].astype(BF16)).astype(o_ref.dtype)


def _mem_attn_decode(q, cache_k, cache_v, layer):
    ns, nh, hd = q.shape
    n_layers, _, mem_len = cache_k.shape[:3]
    g = MEM_DECODE_SEQS
    assert ns % g == 0
    rows = mem_len * nh
    cspec = pl.BlockSpec((None, g, rows, hd), lambda b: (layer, b, 0, 0))
    qspec = pl.BlockSpec((g, nh, hd), lambda b: (b, 0, 0))
    return pl.pallas_call(
        _mem_decode_kernel,
        grid=(ns // g,),
        in_specs=[qspec, cspec, cspec],
        out_specs=qspec,
        out_shape=jax.ShapeDtypeStruct((ns, nh, hd), BF16),
        compiler_params=_cparams(("parallel",)),
        name="mem_attn_decode",
    )(q, cache_k.reshape(n_layers, ns, rows, hd), cache_v.reshape(n_layers, ns, rows, hd))


def _dil_decode_kernel(q_ref, kn_ref, vn_ref, *refs):
    o_ref = refs[-1]
    kn = kn_ref[...][None]
    vn = vn_ref[...][None]
    parts = []
    for g in range(B_GROUPS):
        q = q_ref[g].astype(F32)
        parts.append((_decode_scores(q, refs[2 * g][...]), refs[2 * g + 1][...]))
        parts.append((_decode_scores(q, kn), vn))
    o_ref[...] = _decode_attend(parts).astype(o_ref.dtype)


def _dil_attn_decode(q, k_new, v_new, cache_k, cache_v):
    ns, w_buf, ns_slots, hd = cache_k.shape
    in_specs = [pl.BlockSpec((None, B_GROUPS, ns_slots, hd), lambda b: (b, 0, 0, 0)),
                pl.BlockSpec((None, ns_slots, hd), lambda b: (b, 0, 0)),
                pl.BlockSpec((None, ns_slots, hd), lambda b: (b, 0, 0))]
    args = [q, k_new, v_new]
    for win, dil in B_PATTERNS:
        span = win // dil
        assert w_buf % dil == 0 and (w_buf // dil) % span == 0
        view = (ns, w_buf // dil, dil, ns_slots, hd)
        last = w_buf // dil // span - 1
        spec = pl.BlockSpec((None, span, None, ns_slots, hd), functools.partial(lambda b, last: (b, last, 0, 0, 0), last=last))
        in_specs += [spec, spec]
        args += [cache_k.reshape(view), cache_v.reshape(view)]
    return pl.pallas_call(
        _dil_decode_kernel,
        grid=(ns,),
        in_specs=in_specs,
        out_specs=pl.BlockSpec((None, ns_slots, hd), lambda b: (b, 0, 0)),
        out_shape=jax.ShapeDtypeStruct((ns, ns_slots, hd), BF16),
        compiler_params=_cparams(("parallel",)),
        name="dil_attn_decode",
    )(*args)


def _hgrn_step_kernel(p_ref, s_ref, lb_ref, gn_ref, o_ref, so_ref, rows_ref):
    h = A_HEADS
    hk = h * HEAD_DIM
    gn = gn_ref[...]
    rows_ref[...] = jnp.zeros_like(rows_ref)
    for i in range(h):
        fp = p_ref[:, hk + i * HEAD_DIM:hk + (i + 1) * HEAD_DIM]
        lb = lb_ref[i:i + 1, :]
        rows_ref[i:i + 1, :] = lb + (1.0 - lb) * _sigmoid(fp)
    cols = rows_ref[...].T
    first = lax.broadcasted_iota(jnp.int32, (8, HEAD_DIM), 0) == 0
    for i in range(h):
        sl = slice(i * HEAD_DIM, (i + 1) * HEAD_DIM)
        qp = p_ref[:, sl]
        v = p_ref[:, 2 * hk + i * HEAD_DIM:2 * hk + (i + 1) * HEAD_DIM]
        gp = p_ref[:, 3 * hk + i * HEAD_DIM:3 * hk + (i + 1) * HEAD_DIM]
        k = 1.0 - rows_ref[i:i + 1, :]
        k8 = jnp.where(first, k, 0.0).astype(BF16)
        v8 = jnp.broadcast_to(v, (8, HEAD_DIM)).astype(BF16)
        kv = lax.dot_general(k8, v8, (((0,), (0,)), ((), ())), preferred_element_type=F32)
        s_new = cols[:, i:i + 1] * s_ref[i] + kv
        so_ref[i] = s_new
        q8 = jnp.broadcast_to(qp * _sigmoid(qp), (8, HEAD_DIM)).astype(BF16)
        o = _dot(q8, s_new.astype(BF16))[0:1]
        o = o * lax.rsqrt(jnp.mean(o * o, axis=-1, keepdims=True) + RMS_EPS) * gn
        o_ref[:, sl] = (o * (gp * _sigmoid(gp))).astype(o_ref.dtype)


def _hgrn_step(proj, state, layer, lb, g_norm):
    ns = proj.shape[0]
    h = A_HEADS
    sspec = pl.BlockSpec((None, h, HEAD_DIM, HEAD_DIM), lambda b: (b, 0, 0, 0))
    return pl.pallas_call(
        _hgrn_step_kernel,
        grid=(ns,),
        in_specs=[pl.BlockSpec((None, 1, proj.shape[2]), lambda b: (b, 0, 0)),
                  pl.BlockSpec((None, None, h, HEAD_DIM, HEAD_DIM), lambda b: (layer, b, 0, 0, 0)),
                  pl.BlockSpec((h, HEAD_DIM), lambda b: (0, 0)),
                  pl.BlockSpec((1, HEAD_DIM), lambda b: (0, 0))],
        out_specs=[pl.BlockSpec((None, 1, h * HEAD_DIM), lambda b: (b, 0, 0)), sspec],
        out_shape=[jax.ShapeDtypeStruct((ns, 1, h * HEAD_DIM), BF16),
                   jax.ShapeDtypeStruct((ns, h, HEAD_DIM, HEAD_DIM), F32)],
        scratch_shapes=[pltpu.VMEM((HEAD_DIM, HEAD_DIM), F32)],
        compiler_params=_cparams(("parallel",)),
        name="hgrn_step",
    )(proj, state, lb.reshape(h, HEAD_DIM), g_norm.reshape(1, HEAD_DIM))


def _ffn_kernel(be_ref, nv_ref, od_ref, nu_ref, x_ref, wg_ref, wu_ref, wd_ref, o_ref, wgb_ref, wub_ref, wdb_ref, *, sub):
    i = pl.program_id(0)
    n_valid = nv_ref[i]

    @pl.when((n_valid > 0) & ((i == 0) | (be_ref[i] != be_ref[jnp.maximum(i - 1, 0)])))
    def _():
        wgb_ref[...] = wg_ref[...].astype(BF16)
        wub_ref[...] = wu_ref[...].astype(BF16)
        wdb_ref[...] = wd_ref[...].astype(BF16)

    for j in range(x_ref.shape[0] // sub):
        rows = slice(j * sub, (j + 1) * sub)

        @pl.when(n_valid > j * sub)
        def _(j=j, rows=rows):
            lo, hi = _unpack_halves(x_ref[rows, :])
            c = lo.shape[1]
            keep = lax.broadcasted_iota(jnp.int32, lo.shape, 0) < n_valid - j * sub
            lo = jnp.where(keep, lo, 0.0).astype(BF16)
            hi = jnp.where(keep, hi, 0.0).astype(BF16)
            hg = _dot(lo, wgb_ref[:c, :]) + _dot(hi, wgb_ref[c:, :])
            hu = _dot(lo, wub_ref[:c, :]) + _dot(hi, wub_ref[c:, :])
            hid = hg * _sigmoid(hg) * hu
            o_ref[rows, :] = _pack_halves(_dot(hid.astype(BF16), wdb_ref[...]))

        @pl.when(n_valid <= j * sub)
        def _(rows=rows):
            o_ref[rows, :] = jnp.zeros((sub, o_ref.shape[1]), o_ref.dtype)


def _expert_ffn(x, blk_exp, blk_valid, blk_order, n_used, w_gate, w_up, w_down, layer, tm, sub, name):
    r = x.shape[0]
    d, ff = w_gate.shape[2:]
    assert tm % sub == 0
    w_spec = lambda shape: pl.BlockSpec((None, None) + shape, lambda i, be, nv, od, nu: (layer, be[i], 0, 0))
    return pl.pallas_call(
        functools.partial(_ffn_kernel, sub=sub),
        grid_spec=pltpu.PrefetchScalarGridSpec(
            num_scalar_prefetch=4,
            grid=(r // tm,),
            in_specs=[pl.BlockSpec((tm, d // 2), lambda i, be, nv, od, nu: (od[jnp.minimum(i, nu[0] - 1)], 0)),
                      w_spec((d, ff)), w_spec((d, ff)), w_spec((ff, d))],
            out_specs=pl.BlockSpec((tm, d // 2), lambda i, be, nv, od, nu: (od[i], 0)),
            scratch_shapes=[pltpu.VMEM((d, ff), BF16), pltpu.VMEM((d, ff), BF16), pltpu.VMEM((ff, d), BF16)],
        ),
        out_shape=jax.ShapeDtypeStruct((r, d // 2), jnp.int32),
        compiler_params=_cparams(("arbitrary",), V7X_VMEM_LIMIT_BYTES),
        name=name,
    )(blk_exp, blk_valid, blk_order, n_used, x, w_gate, w_up, w_down)


def _first_argmax(val, idx, sentinel):
    m = jnp.max(val, axis=0, keepdims=True)
    i = jnp.min(jnp.where(val == m, idx, sentinel), axis=0, keepdims=True)
    return m, i


def _route_kernel(x_ref, wt_ref, b_ref, e_ref, g_ref, c_ref):
    x = x_ref[...]
    w = wt_ref[...]
    t = x.shape[0]
    xh = x.astype(BF16)
    xl = (x - xh.astype(F32)).astype(BF16)
    wh = w.astype(BF16)
    wl = (w - wh.astype(F32)).astype(BF16)
    logits = _dot_nt(wh, xh) + (_dot_nt(wl, xh) + _dot_nt(wh, xl))
    scores = _sigmoid(logits)
    biased = scores + b_ref[...]
    gs = N_EXPERTS // N_GROUPS
    neg = -jnp.inf
    eid = lax.broadcasted_iota(jnp.int32, (N_EXPERTS, t), 0)
    sub = lax.broadcasted_iota(jnp.int32, (gs, t), 0)
    grow = lax.broadcasted_iota(jnp.int32, (N_GROUPS, t), 0)
    grp = jnp.zeros((N_GROUPS, t), F32)
    for g in range(N_GROUPS):
        bg = biased[g * gs:(g + 1) * gs]
        m1, i1 = _first_argmax(bg, sub, gs)
        m2 = jnp.max(jnp.where(sub == i1, neg, bg), axis=0, keepdims=True)
        grp = jnp.where(grow == g, m1 + m2, grp)
    chosen = jnp.zeros((N_GROUPS, t), F32)
    for _ in range(TOPK_GROUPS):
        _, gi = _first_argmax(grp, grow, N_GROUPS)
        hit = grow == gi
        chosen = jnp.where(hit, 1.0, chosen)
        grp = jnp.where(hit, neg, grp)
    chosen_e = jnp.concatenate([jnp.broadcast_to(chosen[g:g + 1], (gs, t)) for g in range(N_GROUPS)], axis=0)
    masked = jnp.where(chosen_e > 0.0, biased, neg)
    krow = lax.broadcasted_iota(jnp.int32, (TOP_K, t), 0)
    e_out = jnp.zeros((TOP_K, t), jnp.int32)
    g_out = jnp.zeros((TOP_K, t), F32)
    member = jnp.zeros((N_EXPERTS, t), F32)
    for k in range(TOP_K):
        _, idx = _first_argmax(masked, eid, N_EXPERTS)
        hit = eid == idx
        gk = jnp.sum(jnp.where(hit, scores, 0.0), axis=0, keepdims=True)
        masked = jnp.where(hit, neg, masked)
        member = jnp.where(hit, 1.0, member)
        e_out = jnp.where(krow == k, idx, e_out)
        g_out = jnp.where(krow == k, gk, g_out)
    g_out = g_out / jnp.sum(g_out, axis=0, keepdims=True) * ROUTED_SCALE
    e_ref[...] = e_out
    g_ref[...] = g_out
    c_ref[...] = jnp.sum(member, axis=1, keepdims=True).astype(jnp.int32)


def _route(x, router_w, layer, router_b, tm):
    n, d = x.shape
    e = router_w.shape[2]
    wt = jnp.swapaxes(router_w, 1, 2)
    return pl.pallas_call(
        _route_kernel,
        grid=(n // tm,),
        in_specs=[pl.BlockSpec((tm, d), lambda i: (i, 0)),
                  pl.BlockSpec((None, e, d), lambda i: (layer, 0, 0)),
                  pl.BlockSpec((e, 1), lambda i: (0, 0))],
        out_specs=[pl.BlockSpec((TOP_K, tm), lambda i: (0, i)),
                   pl.BlockSpec((TOP_K, tm), lambda i: (0, i)),
                   pl.BlockSpec((None, e, 1), lambda i: (i, 0, 0))],
        out_shape=[jax.ShapeDtypeStruct((TOP_K, n), jnp.int32), jax.ShapeDtypeStruct((TOP_K, n), F32),
                   jax.ShapeDtypeStruct((n // tm, e, 1), jnp.int32)],
        compiler_params=_cparams(("parallel",), V7X_VMEM_LIMIT_BYTES),
        name="route",
    )(x, wt, router_b[layer].astype(F32).reshape(e, 1))


def _slots_kernel(e_ref, base_ref, o_ref):
    e = e_ref[...]
    t = e.shape[1]
    eid = lax.broadcasted_iota(jnp.int32, (N_EXPERTS, t), 0)
    member = jnp.zeros((N_EXPERTS, t), F32)
    for k in range(TOP_K):
        member = jnp.where(eid == e[k:k + 1], 1.0, member)
    before = (lax.broadcasted_iota(jnp.int32, (t, t), 0) < lax.broadcasted_iota(jnp.int32, (t, t), 1))
    rank = _dot(member.astype(BF16), jnp.where(before, 1.0, 0.0).astype(BF16))
    slot = rank + base_ref[...].astype(F32)
    krow = lax.broadcasted_iota(jnp.int32, (TOP_K, t), 0)
    out = jnp.zeros((TOP_K, t), F32)
    for k in range(TOP_K):
        sk = jnp.sum(jnp.where(eid == e[k:k + 1], slot, 0.0), axis=0, keepdims=True)
        out = jnp.where(krow == k, sk, out)
    o_ref[...] = out.astype(jnp.int32)


def _dispatch_plan(e_idx, counts, tm, tile):
    k, n = e_idx.shape
    n_blocks = -(-(n * k) // tm) + N_EXPERTS
    counts = counts.reshape(n // tile, N_EXPERTS)
    total = jnp.sum(counts, axis=0)
    padded = (total + tm - 1) // tm * tm
    pad_end = jnp.cumsum(padded)
    tile_base = (pad_end - padded)[None, :] + jnp.cumsum(counts, axis=0) - counts
    slot_of = pl.pallas_call(
        _slots_kernel,
        grid=(n // tile,),
        in_specs=[pl.BlockSpec((k, tile), lambda i: (0, i)),
                  pl.BlockSpec((None, N_EXPERTS, 1), lambda i: (i, 0, 0))],
        out_specs=pl.BlockSpec((k, tile), lambda i: (0, i)),
        out_shape=jax.ShapeDtypeStruct((k, n), jnp.int32),
        compiler_params=_cparams(("parallel",)),
        name="slots",
    )(e_idx, tile_base.astype(jnp.int32).reshape(n // tile, N_EXPERTS, 1))
    pos = jnp.arange(n_blocks, dtype=jnp.int32)
    blk_exp = jnp.minimum(jnp.sum((pad_end[None, :] <= (pos * tm)[:, None]).astype(jnp.int32), axis=1), N_EXPERTS - 1)
    is_exp = blk_exp[:, None] == jnp.arange(N_EXPERTS, dtype=blk_exp.dtype)[None, :]
    per_blk = lambda v: jnp.sum(jnp.where(is_exp, v[None, :], 0), axis=1)
    n_used = jnp.maximum(pad_end[-1:] // tm, 1).astype(jnp.int32)
    first, count = per_blk((pad_end - padded) // tm), jnp.maximum(per_blk(padded // tm), 1)
    blk_order = jnp.where(pos < n_used[0], first + (pos - first + count - 1) % count, pos)
    blk_valid = jnp.clip(per_blk(pad_end - padded + total) - blk_order * tm, 0, tm)
    return (slot_of, blk_exp.astype(jnp.int32), blk_valid.astype(jnp.int32), blk_order.astype(jnp.int32), n_used,
            n_blocks)


V7X_SC_CORES = 2
V7X_SC_SUBCORES = 16
SC_WORKERS = V7X_SC_CORES * V7X_SC_SUBCORES
SC_CHUNK = 32
SC_GATHER_CHUNK = 40


def _sc_mesh():
    return plsc.VectorSubcoreMesh(core_axis_name="c", subcore_axis_name="s",
                                  num_cores=V7X_SC_CORES, num_subcores=V7X_SC_SUBCORES)


def _sc_worker_id():
    return lax.axis_index("s") * V7X_SC_CORES + lax.axis_index("c")


def _sc_gather_rows(table, slot_of):
    k, n = slot_of.shape
    w = table.shape[1]
    n_rows = k * n
    ch = SC_GATHER_CHUNK
    per_worker = n_rows // (SC_WORKERS * ch)
    assert per_worker * SC_WORKERS * ch == n_rows and per_worker % 2 == 0
    idx = slot_of.reshape(SC_WORKERS, per_worker, ch)

    def body(table_hbm, idx_hbm, out_hbm, idx_v, rows_v, sems):
        wid = _sc_worker_id()
        pltpu.sync_copy(idx_hbm.at[wid], idx_v)

        def gather(c, slot):
            return pltpu.make_async_copy(table_hbm.at[idx_v.at[c]], rows_v.at[slot], sems.at[slot])

        gather(0, 0).start()

        @pl.loop(0, per_worker, step=2)
        def _(c0):
            for slot in range(2):
                c = c0 + slot
                gather(c, slot).wait()

                @pl.when(c + 1 < per_worker)
                def _():
                    gather(c + 1, 1 - slot).start()

                row0 = pl.multiple_of((wid * per_worker + c) * ch, 8)
                pltpu.sync_copy(rows_v.at[slot], out_hbm.at[pl.ds(row0, ch)])

    return pl.kernel(
        body, out_type=jax.ShapeDtypeStruct((n_rows, w), table.dtype), mesh=_sc_mesh(),
        scratch_types=[pltpu.VMEM((per_worker, ch), jnp.int32), pltpu.VMEM((2, ch, w), table.dtype),
                       pltpu.SemaphoreType.DMA((2,))],
        name="sc_gather_rows",
    )(table, idx)


def _sc_scatter_rows(x, slot_of, n_slots):
    k, n = slot_of.shape
    w = x.shape[1]
    n_chunks = n // SC_CHUNK
    assert n_chunks * SC_CHUNK == n
    idx = slot_of.reshape(k, n_chunks, SC_CHUNK).transpose(1, 0, 2)
    rounds = -(-n_chunks // SC_WORKERS)

    def body(x_hbm, idx_hbm, out_hbm, idx_v, rows_v, sem):
        wid = _sc_worker_id()

        @pl.loop(0, rounds)
        def _(j):
            q = j * SC_WORKERS + wid

            @pl.when(q < n_chunks)
            def _():
                pltpu.sync_copy(idx_hbm.at[q], idx_v)
                pltpu.sync_copy(x_hbm.at[pl.ds(pl.multiple_of(q * SC_CHUNK, SC_CHUNK), SC_CHUNK)], rows_v)
                copies = [pltpu.make_async_copy(rows_v, out_hbm.at[idx_v.at[kk]], sem) for kk in range(k)]
                for cp in copies:
                    cp.start()
                for cp in copies:
                    cp.wait()

    return pl.kernel(
        body, out_type=jax.ShapeDtypeStruct((n_slots, w), x.dtype), mesh=_sc_mesh(),
        scratch_types=[pltpu.VMEM((k, SC_CHUNK), jnp.int32), pltpu.VMEM((SC_CHUNK, w), x.dtype),
                       pltpu.SemaphoreType.DMA],
        name="sc_scatter_rows",
    )(x, idx)


MOE_BLOCK = 768
MOE_SUB = 384
ROW_TILE = 640


def _moe_postnorm_kernel(x_ref, y_ref, gate_ref, sh_ref, g_ref, b_ref, ofp_ref, ofs_ref, ob_ref, *, alpha, tiles_p):
    def part(which):
        of_ref = (ofp_ref, ofs_ref)[which]
        gate = gate_ref[...]
        lo, hi = _unpack_halves(sh_ref[...])
        for k in range(TOP_K):
            lo_k, hi_k = _unpack_halves(y_ref[k])
            lo = lo + gate[:, k:k + 1] * lo_k
            hi = hi + gate[:, k:k + 1] * hi_k
        z = alpha * x_ref[...] + jnp.concatenate([lo, hi], axis=1)
        out = _layer_norm(z, g_ref[...], b_ref[...])
        of_ref[...] = out
        ob_ref[...] = out.astype(BF16)

    _on_part(tiles_p, part)


def _moe_postnorm(x, n_p, y_tok, gate, shared, g, b, alpha):
    m, d = x.shape
    tm = PART_TILE
    row = pl.BlockSpec((tm, d), lambda i: (i, 0))
    vec = pl.BlockSpec((1, d), lambda i: (0, 0))
    part_p, part_s = _part_specs(n_p, (tm, d))
    return pl.pallas_call(
        functools.partial(_moe_postnorm_kernel, alpha=alpha, tiles_p=n_p // tm),
        grid=(m // tm,),
        in_specs=[row, pl.BlockSpec((TOP_K, tm, d // 2), lambda i: (0, i, 0)),
                  pl.BlockSpec((tm, TOP_K), lambda i: (i, 0)), pl.BlockSpec((tm, d // 2), lambda i: (i, 0)), vec, vec],
        out_specs=[part_p, part_s, row],
        out_shape=[jax.ShapeDtypeStruct((n_p, d), F32), jax.ShapeDtypeStruct((m - n_p, d), F32),
                   jax.ShapeDtypeStruct((m, d), BF16)],
        compiler_params=_cparams(("arbitrary",), V7X_VMEM_LIMIT_BYTES),
        name="postnorm_moe",
    )(x, y_tok, gate, shared, g.reshape(1, d), b.reshape(1, d))


def _moe(xf, n_p, xp, layer, router_w, router_b, w_gate, w_up, w_down, sw_gate, sw_up, sw_down, ln_g, ln_b, alpha):
    n, d = xf.shape
    e_idx, gate, counts = _route(xf, router_w, layer, router_b, ROW_TILE)
    slot_of, blk_exp, blk_valid, blk_order, n_used, n_blocks = _dispatch_plan(e_idx, counts, MOE_BLOCK, ROW_TILE)
    x_sorted = _sc_scatter_rows(xp, slot_of, n_blocks * MOE_BLOCK)
    y_sorted = _expert_ffn(x_sorted, blk_exp, blk_valid, blk_order, n_used, w_gate, w_up, w_down, layer,
                           MOE_BLOCK, MOE_SUB, "routed_ffn")
    y_tok = _sc_gather_rows(y_sorted, slot_of).reshape(TOP_K, n, d // 2)
    n_sh = n // ROW_TILE
    shared = _expert_ffn(xp, jnp.zeros((n_sh,), jnp.int32), jnp.full((n_sh,), ROW_TILE, jnp.int32),
                         jnp.arange(n_sh, dtype=jnp.int32), jnp.full((1,), n_sh, jnp.int32),
                         sw_gate[:, None], sw_up[:, None], sw_down[:, None], layer, ROW_TILE, ROW_TILE, "shared_ffn")
    return _moe_postnorm(xf, n_p, y_tok, gate.T, shared, ln_g, ln_b, alpha)


def kernel(x_prompt, x_sample, state_hgrn, cache_win_k, cache_win_v, cache_mem_k, cache_mem_v, mem_prompt,
           w_in_a, lb_logits, g_norm_a, w_out_a, w_in_b, w_out_b, w_kv_shared, w_mem_kv, ln_g, ln_b,
           router_w, router_b, exp_w_gate, exp_w_up, exp_w_down, sh_w_gate, sh_w_up, sh_w_down):
    bp, sp, d = x_prompt.shape
    ns = x_sample.shape[0]
    assert x_sample.shape[1] == 1
    depth = ln_g.shape[0]
    n_a = w_in_a.shape[0]
    alpha = (2 * depth) ** 0.25
    n_p = bp * sp
    mem_len = mem_prompt.shape[1]
    a_mix = 4 * A_HEADS * HEAD_DIM

    xf_p, xf_s = x_prompt.reshape(n_p, d), x_sample.reshape(ns, d)
    xb = jnp.concatenate([xf_p.astype(BF16), xf_s.astype(BF16)], axis=0)
    lower_bounds = jnp.cumsum(jax.nn.softmax(lb_logits.astype(F32), axis=0), axis=0)
    mem_flat = mem_prompt.reshape(bp * mem_len, d)
    pos_all = jnp.concatenate([jnp.tile(jnp.arange(sp, dtype=jnp.int32), bp),
                               jnp.full((ns,), PAST_LEN, jnp.int32)])
    tables = _rope_tables(pos_all)

    hgrn_p, hgrn_s, mem_k_p, mem_v_p = [], [], [], []
    for layer in range(depth):
        kvm = _proj([mem_flat], w_mem_kv, layer, F32, 2 * mem_len, 1024, "mem_kv")
        mem_k_p.append(kvm[:, :MEM_W].reshape(bp, mem_len, MEM_HEADS, HEAD_DIM))
        mem_v_p.append(kvm[:, MEM_W:].reshape(bp, mem_len, MEM_HEADS, HEAD_DIM))
        if layer < n_a:
            a = layer
            proj = _proj([xb], w_in_a, a, F32, ROW_TILE // 2, 1664, "proj_in_a")
            proj_s = proj[n_p:]
            o_x_s, st_s = _hgrn_step(proj_s.reshape(ns, 1, -1), state_hgrn, a, lower_bounds[a], g_norm_a[a])
            o_m_s = _mem_attn_decode(proj_s[:, a_mix:].reshape(ns, MEM_HEADS, HEAD_DIM), cache_mem_k, cache_mem_v, layer)
            o_x, st_p = _hgrn_prompt(proj, lower_bounds[a], g_norm_a[a], o_x_s.reshape(ns, -1), bp, sp)
            o_m = _mem_attn_prompt(proj, a_mix // MEM_W, kvm, o_m_s.reshape(ns, -1), bp, sp, 512)
            hgrn_p.append(st_p)
            hgrn_s.append(st_s)
            w_out, w_out_layer = w_out_a, a
        else:
            bl = layer - n_a
            if layer == n_a:
                kv = _proj([xb], w_kv_shared[None], 0, F32, ROW_TILE, 1024, "proj_kv")
                k_r, k_p, k_s, v_p, v_s = _shared_kv(kv, tables, n_p)
            proj = _proj([xb], w_in_b, bl, F32, ROW_TILE, 1024, "proj_in_b")
            q_r = _rope(proj, B_QHEADS, tables, F32, ROW_TILE, "rope_q")
            q_s = q_r[n_p:].reshape(ns, B_GROUPS, B_SLOTS, HEAD_DIM)
            o_x_s = _dil_attn_decode(q_s, k_s, v_s, cache_win_k, cache_win_v)
            o_m_s = _mem_attn_decode(proj[n_p:, B_QHEADS * HEAD_DIM:].reshape(ns, MEM_HEADS, HEAD_DIM),
                                     cache_mem_k, cache_mem_v, layer)
            o_x = _dil_attn_prompt(q_r, k_r, kv, o_x_s.reshape(ns, -1), bp, sp)
            o_m = _mem_attn_prompt(proj, B_QHEADS * HEAD_DIM // MEM_W, kvm, o_m_s.reshape(ns, -1), bp, sp, 512)
            w_out, w_out_layer = w_out_b, bl
        y = _proj([o_x, o_m], w_out, w_out_layer, F32, ROW_TILE, 1024, "proj_out")
        xf, xp = _postnorm(xf_p, xf_s, y, ln_g[layer, 0], ln_b[layer, 0], alpha, "postnorm_mix")
        xf_p, xf_s, xb = _moe(xf, n_p, xp, layer, router_w, router_b, exp_w_gate, exp_w_up, exp_w_down,
                              sh_w_gate, sh_w_up, sh_w_down, ln_g[layer, 1], ln_b[layer, 1], alpha)

    w_p = min(max(w for w, _ in B_PATTERNS), sp)
    k_p = k_p.reshape(bp, sp, B_SLOTS, HEAD_DIM)
    v_p = v_p.reshape(bp, sp, B_SLOTS, HEAD_DIM)
    return (xf_p.reshape(bp, sp, d), xf_s.reshape(ns, 1, d),
            jnp.stack(hgrn_p), jnp.stack(hgrn_s),
            k_p[:, sp - w_p:], v_p[:, sp - w_p:],
            k_s.reshape(ns, 1, B_SLOTS, HEAD_DIM), v_s.reshape(ns, 1, B_SLOTS, HEAD_DIM),
            jnp.stack(mem_k_p), jnp.stack(mem_v_p))
```

```python
import functools

import jax
import jax.numpy as jnp
from jax import lax
from jax.experimental import pallas as pl
from jax.experimental.pallas import tpu as pltpu
from jax.experimental.pallas import tpu_sc as plsc

F32 = jnp.float32
BF16 = jnp.bfloat16

HEAD_DIM = 128
A_HEADS = 12
A_CHUNK = 64
A_SUB = 16
B_PATTERNS = ((128, 1), (512, 4), (2048, 16))
B_SLOTS = 4
B_GROUPS = len(B_PATTERNS)
B_QHEADS = B_GROUPS * B_SLOTS
B_BLOCK = 128
MEM_HEADS = 4
MEM_W = MEM_HEADS * HEAD_DIM
KV_W = B_SLOTS * HEAD_DIM
ROPE_THETA = 500000.0
ROPE_DIM = HEAD_DIM // 4
N_EXPERTS = 64
N_GROUPS = 8
TOPK_GROUPS = 4
TOP_K = 8
ROUTED_SCALE = 2.5
LN_EPS = 1e-5
RMS_EPS = 1e-6
ATTN_SCALE = HEAD_DIM ** -0.5
PAST_LEN = 2048

V7X_VMEM_LIMIT_BYTES = 56 * 1024 * 1024
LANES = 128
NEG_BIG = -1e30


def _cparams(sem, vmem=None):
    return pltpu.CompilerParams(dimension_semantics=sem, vmem_limit_bytes=vmem)


def _dot(a, b):
    return jnp.dot(a, b, preferred_element_type=F32)


def _dot_nt(a, b):
    return lax.dot_general(a, b, (((1,), (1,)), ((), ())), preferred_element_type=F32)


def _sigmoid(x):
    return 1.0 / (1.0 + jnp.exp(-x))


def _proj_kernel(*refs, n_lhs):
    x_refs = refs[:n_lhs]
    w_refs = refs[n_lhs:2 * n_lhs]
    o_ref = refs[2 * n_lhs]
    wb_refs = refs[2 * n_lhs + 1:]

    @pl.when(pl.program_id(1) == 0)
    def _():
        for w_ref, wb_ref in zip(w_refs, wb_refs):
            wb_ref[...] = w_ref[...].astype(BF16)

    acc = None
    for x_ref, wb_ref in zip(x_refs, wb_refs):
        d = _dot(x_ref[...].astype(BF16), wb_ref[...])
        acc = d if acc is None else acc + d
    o_ref[...] = acc.astype(o_ref.dtype)


def _proj(lhs, w, layer, out_dtype, tm, tn, name):
    m = lhs[0].shape[0]
    n = w.shape[2]
    koff = 0
    in_specs, w_specs, scratch = [], [], []
    for x in lhs:
        k = x.shape[1]
        assert koff % k == 0 and m % tm == 0 and n % tn == 0
        in_specs.append(pl.BlockSpec((tm, k), lambda j, i: (i, 0)))
        w_specs.append(pl.BlockSpec((None, k, tn), functools.partial(lambda j, i, kb: (layer, kb, j), kb=koff // k)))
        scratch.append(pltpu.VMEM((k, tn), BF16))
        koff += k
    assert koff == w.shape[1]
    return pl.pallas_call(
        functools.partial(_proj_kernel, n_lhs=len(lhs)),
        grid=(n // tn, m // tm),
        in_specs=in_specs + w_specs,
        out_specs=pl.BlockSpec((tm, tn), lambda j, i: (i, j)),
        out_shape=jax.ShapeDtypeStruct((m, n), out_dtype),
        scratch_shapes=scratch,
        compiler_params=_cparams(("arbitrary", "arbitrary"), V7X_VMEM_LIMIT_BYTES),
        name=name,
    )(*lhs, *([w] * len(lhs)))


def _pack_halves(x):
    c = x.shape[1] // 2
    lo = pltpu.bitcast(x[:, :c].astype(BF16).astype(F32), jnp.int32)
    hi = pltpu.bitcast(x[:, c:].astype(BF16).astype(F32), jnp.int32)
    return hi | lax.shift_right_logical(lo, 16)


def _unpack_halves(w):
    lo = pltpu.bitcast(lax.shift_left(w, 16), F32)
    hi = pltpu.bitcast(w & jnp.int32(-65536), F32)
    return lo, hi


def _layer_norm(z, g, b):
    mu = jnp.mean(z, axis=-1, keepdims=True)
    zc = z - mu
    var = jnp.mean(zc * zc, axis=-1, keepdims=True)
    return zc * lax.rsqrt(var + LN_EPS) * g + b


PART_TILE = 128


def _part_specs(n_p, block):
    tiles_p = n_p // PART_TILE
    rest = (0,) * (len(block) - 1)
    return (pl.BlockSpec(block, lambda i: (jnp.minimum(i, tiles_p - 1),) + rest),
            pl.BlockSpec(block, lambda i: (jnp.maximum(i - tiles_p, 0),) + rest))


def _on_part(tiles_p, fn):
    i = pl.program_id(0)
    pl.when(i < tiles_p)(functools.partial(fn, 0))
    pl.when(i >= tiles_p)(functools.partial(fn, 1))


def _postnorm_kernel(xp_ref, xs_ref, y_ref, g_ref, b_ref, of_ref, op_ref, *, alpha, tiles_p):
    def part(which):
        x_ref = (xp_ref, xs_ref)[which]
        out = _layer_norm(alpha * x_ref[...] + y_ref[...], g_ref[...], b_ref[...])
        of_ref[...] = out
        op_ref[...] = _pack_halves(out)

    _on_part(tiles_p, part)


def _postnorm(x_p, x_s, y, g, b, alpha, name):
    n_p, d = x_p.shape
    m = n_p + x_s.shape[0]
    row = pl.BlockSpec((PART_TILE, d), lambda i: (i, 0))
    vec = pl.BlockSpec((1, d), lambda i: (0, 0))
    part_p, part_s = _part_specs(n_p, (PART_TILE, d))
    return pl.pallas_call(
        functools.partial(_postnorm_kernel, alpha=alpha, tiles_p=n_p // PART_TILE),
        grid=(m // PART_TILE,),
        in_specs=[part_p, part_s, row, vec, vec],
        out_specs=[row, pl.BlockSpec((PART_TILE, d // 2), lambda i: (i, 0))],
        out_shape=[jax.ShapeDtypeStruct((m, d), F32), jax.ShapeDtypeStruct((m, d // 2), jnp.int32)],
        compiler_params=_cparams(("arbitrary",)),
        name=name,
    )(x_p, x_s, y, g.reshape(1, d), b.reshape(1, d))


def _hgrn_prompt_kernel(q_ref, f_ref, v_ref, gate_ref, lb_ref, gn_ref, os_ref, o_ref, s_ref, st_ref,
                        *, seq, heads, n_batch):
    _prompt_then_sample(n_batch, os_ref, o_ref,
                        functools.partial(_hgrn_sequence, q_ref, f_ref, v_ref, gate_ref, lb_ref, gn_ref,
                                          o_ref, s_ref, st_ref, seq=seq, heads=heads), axis=1)


def _hgrn_sequence(q_ref, f_ref, v_ref, gate_ref, lb_ref, gn_ref, o_ref, s_ref, st_ref, *, seq, heads):
    c = A_CHUNK
    n_chunks = seq // c
    n_sub = c // A_SUB
    gn = gn_ref[...]
    row = lax.broadcasted_iota(jnp.int32, (c, c), 0)
    col = lax.broadcasted_iota(jnp.int32, (c, c), 1)
    tril = jnp.where(row >= col, 1.0, 0.0).astype(BF16)
    same_sub = (row // A_SUB) == (col // A_SUB)
    diag_dist = jnp.where(same_sub, row - col, -1)
    off_mask = col < (row // A_SUB) * A_SUB
    st_ref[...] = jnp.zeros_like(st_ref)

    def chunk(ci, carry):
        for hh in range(heads):
            head_chunk(ci, hh)
        return carry

    def head_chunk(ci, hh):
        r0 = pl.multiple_of(ci * c, c)
        hsl = slice(hh * HEAD_DIM, (hh + 1) * HEAD_DIM)
        lb = lb_ref[hh]
        qp = q_ref[pl.ds(r0, c), hsl]
        fp = f_ref[pl.ds(r0, c), hsl]
        v = v_ref[pl.ds(r0, c), hsl]
        gp = gate_ref[pl.ds(r0, c), hsl]
        st = st_ref[hh]
        q = qp * _sigmoid(qp)
        forget = lb + (1.0 - lb) * _sigmoid(fp)
        logf = jnp.log2(forget)
        k = 1.0 - forget
        hi = logf.astype(BF16)
        r1 = logf - hi.astype(F32)
        mid = r1.astype(BF16)
        lo = (r1 - mid.astype(F32)).astype(BF16)
        g = _dot(tril, hi) + _dot(tril, mid) + _dot(tril, lo)
        v_b = v.astype(BF16)
        o = _dot_nt((q * jnp.exp2(g)).astype(BF16), st.astype(BF16))
        rows = [jnp.zeros((A_SUB, c), F32)]
        for i in range(1, n_sub):
            gref = g[i * A_SUB:i * A_SUB + 1, :]
            qt = q[i * A_SUB:(i + 1) * A_SUB, :] * jnp.exp2(g[i * A_SUB:(i + 1) * A_SUB, :] - gref)
            kt = k * jnp.exp2(jnp.minimum(gref - g, 0.0))
            rows.append(_dot_nt(qt.astype(BF16), kt.astype(BF16)))
        a = jnp.where(off_mask, jnp.concatenate(rows, axis=0), 0.0)
        for d in range(A_SUB):
            kr = k if d == 0 else pltpu.roll(k, d, axis=0)
            gr = g if d == 0 else pltpu.roll(g, d, axis=0)
            x = q * kr * jnp.exp2(g - gr)
            a = jnp.where(diag_dist == d, jnp.sum(x, axis=-1, keepdims=True), a)
        o = o + _dot(a.astype(BF16), v_b)
        gend = g[c - 1:c, :]
        kt_end = k * jnp.exp2(gend - g)
        st_ref[hh] = jnp.exp2(gend) * st + _dot(v_b.T, kt_end.astype(BF16))
        o = o * lax.rsqrt(jnp.mean(o * o, axis=-1, keepdims=True) + RMS_EPS) * gn
        o_ref[pl.ds(r0, c), hsl] = (o * (gp * _sigmoid(gp))).astype(o_ref.dtype)

    lax.fori_loop(0, n_chunks, chunk, 0)
    for hh in range(heads):
        s_ref[hh] = st_ref[hh].T


A_HEADS_PER_STEP = 4


def _hgrn_prompt(proj, lb, g_norm, o_sample, n_batch, seq):
    h = A_HEADS
    hp = A_HEADS_PER_STEP
    ns = o_sample.shape[0]
    assert h % hp == 0 and ns <= seq and proj.shape[0] == n_batch * seq + ns
    ng = h // hp
    w = hp * HEAD_DIM
    last = n_batch - 1
    blk = lambda off: pl.BlockSpec((seq, w), functools.partial(lambda hg, b, off: (jnp.minimum(b, last), off + hg), off=off))
    return pl.pallas_call(
        functools.partial(_hgrn_prompt_kernel, seq=seq, heads=hp, n_batch=n_batch),
        grid=(ng, n_batch + 1),
        in_specs=[blk(0), blk(ng), blk(2 * ng), blk(3 * ng),
                  pl.BlockSpec((hp, 1, HEAD_DIM), lambda hg, b: (hg, 0, 0)),
                  pl.BlockSpec((1, HEAD_DIM), lambda hg, b: (0, 0)),
                  pl.BlockSpec((ns, w), lambda hg, b: (0, hg))],
        out_specs=[pl.BlockSpec((seq, w), lambda hg, b: (b, hg)),
                   pl.BlockSpec((None, hp, HEAD_DIM, HEAD_DIM), lambda hg, b: (jnp.minimum(b, last), hg, 0, 0))],
        out_shape=[jax.ShapeDtypeStruct((proj.shape[0], h * HEAD_DIM), BF16),
                   jax.ShapeDtypeStruct((n_batch, h, HEAD_DIM, HEAD_DIM), F32)],
        scratch_shapes=[pltpu.VMEM((hp, HEAD_DIM, HEAD_DIM), F32)],
        compiler_params=_cparams(("arbitrary", "arbitrary"), V7X_VMEM_LIMIT_BYTES),
        name="hgrn_prompt",
    )(proj, proj, proj, proj, lb.reshape(h, 1, HEAD_DIM), g_norm.reshape(1, HEAD_DIM), o_sample)


def _softmax_av(s, v):
    m = jnp.max(s, axis=-1, keepdims=True)
    p = jnp.exp(s - m)
    l = jnp.sum(p, axis=-1, keepdims=True)
    return _dot(p.astype(BF16), v) / l


def _prompt_then_sample(n_prompt_steps, os_ref, o_ref, prompt_step, axis=0):
    i = pl.program_id(axis)
    pl.when(i < n_prompt_steps)(prompt_step)

    @pl.when(i >= n_prompt_steps)
    def _():
        o_ref[:os_ref.shape[0], :] = os_ref[...]


def _mem_attn_kernel(q_ref, k_ref, v_ref, os_ref, o_ref, *, n_steps):
    def step():
        for h in range(MEM_HEADS):
            sl = slice(h * HEAD_DIM, (h + 1) * HEAD_DIM)
            s = _dot_nt(q_ref[:, sl].astype(BF16), k_ref[:, sl].astype(BF16)) * ATTN_SCALE
            o_ref[:, sl] = _softmax_av(s, v_ref[:, sl].astype(BF16)).astype(o_ref.dtype)

    _prompt_then_sample(n_steps, os_ref, o_ref, step)


def _mem_attn_prompt(proj, q_col, kv, o_sample, n_batch, seq, tq):
    mem_len = kv.shape[0] // n_batch
    nq = seq // tq
    n_steps = n_batch * nq
    ns = o_sample.shape[0]
    assert ns <= tq and proj.shape[0] == n_batch * seq + ns
    last = n_steps - 1
    return pl.pallas_call(
        functools.partial(_mem_attn_kernel, n_steps=n_steps),
        grid=(n_steps + 1,),
        in_specs=[pl.BlockSpec((tq, MEM_W), lambda i: (jnp.minimum(i, last), q_col)),
                  pl.BlockSpec((mem_len, MEM_W), lambda i: (jnp.minimum(i, last) // nq, 0)),
                  pl.BlockSpec((mem_len, MEM_W), lambda i: (jnp.minimum(i, last) // nq, 1)),
                  pl.BlockSpec((ns, MEM_W), lambda i: (0, 0))],
        out_specs=pl.BlockSpec((tq, MEM_W), lambda i: (i, 0)),
        out_shape=jax.ShapeDtypeStruct((proj.shape[0], MEM_W), BF16),
        compiler_params=_cparams(("arbitrary",)),
        name="mem_attn_prompt",
    )(proj, kv, kv, o_sample)


def _rope_tables(pos):
    half = ROPE_DIM // 2
    inv_freq = ROPE_THETA ** (-jnp.arange(0, ROPE_DIM, 2, dtype=F32) / ROPE_DIM)
    ang = pos.astype(F32)[:, None] * inv_freq[None, :]
    cos, sin = jnp.cos(ang), jnp.sin(ang)
    n = pos.shape[0]
    one = jnp.ones((n, HEAD_DIM - ROPE_DIM), F32)
    zero = jnp.zeros((n, HEAD_DIM - half), F32)
    c = jnp.concatenate([cos, cos, one], axis=1)
    s_up = jnp.concatenate([-sin, zero], axis=1)
    s_dn = jnp.concatenate([jnp.zeros((n, half), F32), sin, zero[:, half:]], axis=1)
    return c, s_up, s_dn


def _rope_kernel(x_ref, c_ref, su_ref, sd_ref, o_ref, *, n_heads):
    c, su, sd = c_ref[...], su_ref[...], sd_ref[...]
    for h in range(n_heads):
        sl = slice(h * HEAD_DIM, (h + 1) * HEAD_DIM)
        o_ref[:, sl] = _rope_head(x_ref[:, sl], c, su, sd).astype(o_ref.dtype)


def _rope_head(x, c, su, sd):
    half = ROPE_DIM // 2
    up = pltpu.roll(x, HEAD_DIM - half, axis=1)
    dn = pltpu.roll(x, half, axis=1)
    return x * c + up * su + dn * sd


def _kv_kernel(kv_ref, c_ref, su_ref, sd_ref, k_ref, kp_ref, ks_ref, vp_ref, vs_ref, *, tiles_p):
    c, su, sd = c_ref[...], su_ref[...], sd_ref[...]

    def part(which):
        k4_ref, v4_ref = ((kp_ref, vp_ref), (ks_ref, vs_ref))[which]
        for h in range(B_SLOTS):
            sl = slice(h * HEAD_DIM, (h + 1) * HEAD_DIM)
            k = _rope_head(kv_ref[:, sl], c, su, sd)
            k_ref[:, sl] = k
            k4_ref[:, h, :] = k
            v4_ref[:, h, :] = kv_ref[:, KV_W + h * HEAD_DIM:KV_W + (h + 1) * HEAD_DIM]

    _on_part(tiles_p, part)


def _shared_kv(kv, tables, n_p):
    m = kv.shape[0]
    tm = PART_TILE
    tab = pl.BlockSpec((tm, HEAD_DIM), lambda i: (i, 0))
    part_p, part_s = _part_specs(n_p, (tm, B_SLOTS, HEAD_DIM))
    cache = lambda rows: jax.ShapeDtypeStruct((rows, B_SLOTS, HEAD_DIM), F32)
    return pl.pallas_call(
        functools.partial(_kv_kernel, tiles_p=n_p // tm),
        grid=(m // tm,),
        in_specs=[pl.BlockSpec((tm, 2 * KV_W), lambda i: (i, 0)), tab, tab, tab],
        out_specs=[pl.BlockSpec((tm, KV_W), lambda i: (i, 0)), part_p, part_s, part_p, part_s],
        out_shape=[jax.ShapeDtypeStruct((m, KV_W), F32), cache(n_p), cache(m - n_p), cache(n_p), cache(m - n_p)],
        compiler_params=_cparams(("arbitrary",)),
        name="shared_kv",
    )(kv, *tables)


def _rope(x, n_heads, tables, out_dtype, tm, name):
    m = x.shape[0]
    w = n_heads * HEAD_DIM
    tab = pl.BlockSpec((tm, HEAD_DIM), lambda i: (i, 0))
    return pl.pallas_call(
        functools.partial(_rope_kernel, n_heads=n_heads),
        grid=(m // tm,),
        in_specs=[pl.BlockSpec((tm, w), lambda i: (i, 0)), tab, tab, tab],
        out_specs=pl.BlockSpec((tm, w), lambda i: (i, 0)),
        out_shape=jax.ShapeDtypeStruct((m, w), out_dtype),
        compiler_params=_cparams(("parallel",)),
        name=name,
    )(x, *tables)


DIL_BATCH = 8


def _dil_prompt_kernel(*refs, seq, n_batch):
    os_ref, o_ref = refs[B_GROUPS + 2:B_GROUPS + 4]
    _prompt_then_sample(n_batch, os_ref, o_ref, functools.partial(_dil_sequence, *refs, seq=seq), axis=1)


def _dil_sequence(*refs, seq):
    q_refs = refs[:B_GROUPS]
    k_ref, v_ref, _, o_ref, m_ref, l_ref, acc_ref = refs[B_GROUPS:]
    blk = B_BLOCK
    base = (lax.broadcasted_iota(jnp.int32, (blk, blk), 0) - lax.broadcasted_iota(jnp.int32, (blk, blk), 1))
    for g, (win, dil) in enumerate(B_PATTERNS):
        span = win // dil
        n_blk = seq // dil // blk
        assert span <= blk

        def body(it, carry, g=g, dil=dil, span=span, n_blk=n_blk):
            own, prev, prev_ok = [], [], []
            for j in range(DIL_BATCH):
                t = it * DIL_BATCH + j
                r, n = t % dil, t // dil

                def rows(nn, r=r):
                    start = nn * (blk * dil) + r
                    return pl.ds(pl.multiple_of(start, blk), blk) if dil == 1 else pl.ds(start, blk, stride=dil)

                own.append(rows(n))
                prev.append(rows(jnp.maximum(n - 1, 0)))
                prev_ok.append(base <= jnp.where(n > 0, span - blk, -blk - 1))
            load = lambda ref, idx: jnp.stack([ref[i, :].astype(BF16) for i in idx])
            scores = lambda a, b: jnp.einsum('bqd,bkd->bqk', a, b, preferred_element_type=F32) * ATTN_SCALE
            weighted = lambda p, v: jnp.einsum('bqk,bkd->bqd', p.astype(BF16), v, preferred_element_type=F32)
            q = load(q_refs[g], own)
            s = jnp.where((base >= 0)[None], scores(q, load(k_ref, own)), NEG_BIG)
            m_b = jnp.max(s, axis=-1, keepdims=True)
            if n_blk > 1:
                s_prev = jnp.where(jnp.stack(prev_ok), scores(q, load(k_ref, prev)), NEG_BIG)
                m_b = jnp.maximum(m_b, jnp.max(s_prev, axis=-1, keepdims=True))
            p = jnp.exp(s - m_b)
            l_b = jnp.sum(p, axis=-1, keepdims=True)
            acc_b = weighted(p, load(v_ref, own))
            if n_blk > 1:
                p_prev = jnp.exp(s_prev - m_b)
                l_b = l_b + jnp.sum(p_prev, axis=-1, keepdims=True)
                acc_b = acc_b + weighted(p_prev, load(v_ref, prev))
            for j, idx in enumerate(own):
                if g == 0:
                    m_ref[idx, :] = jnp.broadcast_to(m_b[j], (blk, HEAD_DIM))
                    l_ref[idx, :] = jnp.broadcast_to(l_b[j], (blk, HEAD_DIM))
                    acc_ref[idx, :] = acc_b[j]
                else:
                    m_old = m_ref[idx, :]
                    m_new = jnp.maximum(m_old, m_b[j])
                    a_old = jnp.exp(m_old - m_new)
                    a_b = jnp.exp(m_b[j] - m_new)
                    m_ref[idx, :] = m_new
                    l_ref[idx, :] = l_ref[idx, :] * a_old + l_b[j] * a_b
                    acc_ref[idx, :] = acc_ref[idx, :] * a_old + acc_b[j] * a_b
            return carry

        assert (dil * n_blk) % DIL_BATCH == 0
        lax.fori_loop(0, dil * n_blk // DIL_BATCH, body, 0)
    o_ref[...] = (acc_ref[...] / l_ref[...]).astype(o_ref.dtype)


def _dil_attn_prompt(q, k, kv, o_sample, n_batch, seq):
    ns = o_sample.shape[0]
    assert ns <= seq and q.shape[0] == n_batch * seq + ns
    last = n_batch - 1
    col = lambda c0: pl.BlockSpec((seq, HEAD_DIM), functools.partial(lambda h, b, c0: (jnp.minimum(b, last), c0 + h), c0=c0))
    return pl.pallas_call(
        functools.partial(_dil_prompt_kernel, seq=seq, n_batch=n_batch),
        grid=(B_SLOTS, n_batch + 1),
        in_specs=[col(g * B_SLOTS) for g in range(B_GROUPS)] + [col(0), col(B_SLOTS),
                                                                 pl.BlockSpec((ns, HEAD_DIM), lambda h, b: (0, h))],
        out_specs=pl.BlockSpec((seq, HEAD_DIM), lambda h, b: (b, h)),
        out_shape=jax.ShapeDtypeStruct((q.shape[0], KV_W), BF16),
        scratch_shapes=[pltpu.VMEM((seq, HEAD_DIM), F32)] * 3,
        compiler_params=_cparams(("arbitrary", "arbitrary"), V7X_VMEM_LIMIT_BYTES),
        name="dil_attn_prompt",
    )(*([q] * B_GROUPS), k, kv, o_sample)


MEM_DECODE_SEQS = 4


def _mem_decode_kernel(q_ref, k_ref, v_ref, o_ref):
    n_seq, rows, _ = k_ref.shape
    nh = q_ref.shape[1]
    own = (lax.broadcasted_iota(jnp.int32, (nh, rows), 1) % nh) == lax.broadcasted_iota(jnp.int32, (nh, rows), 0)
    for i in range(n_seq):
        s = _dot_nt(q_ref[i].astype(BF16), k_ref[i].astype(BF16)) * ATTN_SCALE
        o_ref[i] = _softmax_av(jnp.where(own, s, NEG_BIG), v_ref[i].astype(BF16)).astype(o_ref.dtype)


def _mem_attn_decode(q, cache_k, cache_v, layer):
    ns, nh, hd = q.shape
    n_layers, _, mem_len = cache_k.shape[:3]
    g = MEM_DECODE_SEQS
    assert ns % g == 0
    rows = mem_len * nh
    cspec = pl.BlockSpec((None, g, rows, hd), lambda b: (layer, b, 0, 0))
    qspec = pl.BlockSpec((g, nh, hd), lambda b: (b, 0, 0))
    return pl.pallas_call(
        _mem_decode_kernel,
        grid=(ns // g,),
        in_specs=[qspec, cspec, cspec],
        out_specs=qspec,
        out_shape=jax.ShapeDtypeStruct((ns, nh, hd), BF16),
        compiler_params=_cparams(("parallel",)),
        name="mem_attn_decode",
    )(q, cache_k.reshape(n_layers, ns, rows, hd), cache_v.reshape(n_layers, ns, rows, hd))


DIL_DECODE_SEQS = 2
SUBLANES = 8


def _dil_decode_kernel(q_ref, kn_ref, vn_ref, *refs):
    o_ref = refs[-1]
    n_slots = kn_ref.shape[1]
    for i in range(q_ref.shape[0]):
        kn, vn = kn_ref[i], vn_ref[i]
        scores, values = [], []
        for g in range(B_GROUPS):
            k_ref, v_ref = refs[2 * g], refs[2 * g + 1]
            rows = k_ref.shape[1] * (k_ref.shape[2] if len(k_ref.shape) == 4 else 1)
            period = k_ref.shape[2] if len(k_ref.shape) == 4 else n_slots
            k = k_ref[i].reshape(rows, HEAD_DIM).astype(BF16)
            v = v_ref[i].reshape(rows, HEAD_DIM).astype(BF16)
            q = q_ref[i, g]
            own = (lax.broadcasted_iota(jnp.int32, (n_slots, rows), 1) % period
                   == lax.broadcasted_iota(jnp.int32, (n_slots, rows), 0))
            scores.append(jnp.where(own, _dot_nt(q.astype(BF16), k) * ATTN_SCALE, NEG_BIG))
            values.append(v)
            scores.append(jnp.sum(q * kn, axis=-1, keepdims=True) * ATTN_SCALE)
            values.append(None)
        m = functools.reduce(jnp.maximum, [jnp.max(s, axis=-1, keepdims=True) for s in scores])
        l = jnp.zeros_like(m)
        acc = jnp.zeros((n_slots, HEAD_DIM), F32)
        for s, v in zip(scores, values):
            p = jnp.exp(s - m)
            l = l + jnp.sum(p, axis=-1, keepdims=True)
            acc = acc + (p * vn if v is None else _dot(p.astype(BF16), v))
        o_ref[i] = (acc / l).astype(o_ref.dtype)


def _dil_attn_decode(q, k_new, v_new, cache_k, cache_v):
    ns, w_buf, ns_slots, hd = cache_k.shape
    n_seq = DIL_DECODE_SEQS
    assert ns % n_seq == 0
    in_specs = [pl.BlockSpec((n_seq, B_GROUPS, ns_slots, hd), lambda b: (b, 0, 0, 0)),
                pl.BlockSpec((n_seq, ns_slots, hd), lambda b: (b, 0, 0)),
                pl.BlockSpec((n_seq, ns_slots, hd), lambda b: (b, 0, 0))]
    args = [q, k_new, v_new]
    for win, dil in B_PATTERNS:
        span = win // dil
        assert w_buf % dil == 0 and (w_buf // dil) % span == 0
        last = w_buf // dil // span - 1
        if dil == 1:
            view = (ns, w_buf * ns_slots, hd)
            spec = pl.BlockSpec((n_seq, span * ns_slots, hd), functools.partial(lambda b, last: (b, last, 0), last=last))
        else:
            assert dil * ns_slots >= SUBLANES and ns_slots <= SUBLANES
            view = (ns, w_buf // dil, dil * ns_slots, hd)
            spec = pl.BlockSpec((n_seq, span, SUBLANES, hd), functools.partial(lambda b, last: (b, last, 0, 0), last=last))
        in_specs += [spec, spec]
        args += [cache_k.reshape(view), cache_v.reshape(view)]
    return pl.pallas_call(
        _dil_decode_kernel,
        grid=(ns // n_seq,),
        in_specs=in_specs,
        out_specs=pl.BlockSpec((n_seq, ns_slots, hd), lambda b: (b, 0, 0)),
        out_shape=jax.ShapeDtypeStruct((ns, ns_slots, hd), BF16),
        compiler_params=_cparams(("parallel",)),
        name="dil_attn_decode",
    )(*args)


HGRN_STEP_SEQS = 1


def _hgrn_step_kernel(p_ref, s_ref, lb_ref, gn_ref, o_ref, so_ref, rows_ref):
    h = A_HEADS
    hk = h * HEAD_DIM
    gn = gn_ref[...]
    first = lax.broadcasted_iota(jnp.int32, (8, HEAD_DIM), 0) == 0
    for b in range(p_ref.shape[0]):
        rows_ref[b] = jnp.zeros(rows_ref.shape[1:], F32)
        for i in range(h):
            fp = p_ref[b, :, hk + i * HEAD_DIM:hk + (i + 1) * HEAD_DIM]
            lb = lb_ref[i:i + 1, :]
            rows_ref[b, i:i + 1, :] = lb + (1.0 - lb) * _sigmoid(fp)
        cols = rows_ref[b].T
        for i in range(h):
            sl = slice(i * HEAD_DIM, (i + 1) * HEAD_DIM)
            qp = p_ref[b, :, sl]
            v = p_ref[b, :, 2 * hk + i * HEAD_DIM:2 * hk + (i + 1) * HEAD_DIM]
            gp = p_ref[b, :, 3 * hk + i * HEAD_DIM:3 * hk + (i + 1) * HEAD_DIM]
            k = 1.0 - rows_ref[b, i:i + 1, :]
            k8 = jnp.where(first, k, 0.0).astype(BF16)
            v8 = jnp.broadcast_to(v, (8, HEAD_DIM)).astype(BF16)
            kv = lax.dot_general(k8, v8, (((0,), (0,)), ((), ())), preferred_element_type=F32)
            s_new = cols[:, i:i + 1] * s_ref[b, i] + kv
            so_ref[b, i] = s_new
            q8 = jnp.broadcast_to(qp * _sigmoid(qp), (8, HEAD_DIM)).astype(BF16)
            o = _dot(q8, s_new.astype(BF16))[0:1]
            o = o * lax.rsqrt(jnp.mean(o * o, axis=-1, keepdims=True) + RMS_EPS) * gn
            o_ref[b, :, sl] = (o * (gp * _sigmoid(gp))).astype(o_ref.dtype)


def _hgrn_step(proj, state, layer, lb, g_norm):
    ns = proj.shape[0]
    h = A_HEADS
    g = HGRN_STEP_SEQS
    assert ns % g == 0
    sspec = pl.BlockSpec((g, h, HEAD_DIM, HEAD_DIM), lambda b: (b, 0, 0, 0))
    return pl.pallas_call(
        _hgrn_step_kernel,
        grid=(ns // g,),
        in_specs=[pl.BlockSpec((g, 1, proj.shape[2]), lambda b: (b, 0, 0)),
                  pl.BlockSpec((None, g, h, HEAD_DIM, HEAD_DIM), lambda b: (layer, b, 0, 0, 0)),
                  pl.BlockSpec((h, HEAD_DIM), lambda b: (0, 0)),
                  pl.BlockSpec((1, HEAD_DIM), lambda b: (0, 0))],
        out_specs=[pl.BlockSpec((g, 1, h * HEAD_DIM), lambda b: (b, 0, 0)), sspec],
        out_shape=[jax.ShapeDtypeStruct((ns, 1, h * HEAD_DIM), BF16),
                   jax.ShapeDtypeStruct((ns, h, HEAD_DIM, HEAD_DIM), F32)],
        scratch_shapes=[pltpu.VMEM((g, HEAD_DIM, HEAD_DIM), F32)],
        compiler_params=_cparams(("parallel",)),
        name="hgrn_step",
    )(proj, state, lb.reshape(h, HEAD_DIM), g_norm.reshape(1, HEAD_DIM))


def _ffn_kernel(be_ref, nv_ref, od_ref, nu_ref, x_ref, wg_ref, wu_ref, wd_ref, o_ref, wgb_ref, wub_ref, wdb_ref, *, sub):
    i = pl.program_id(0)
    n_valid = nv_ref[i]

    @pl.when((n_valid > 0) & ((i == 0) | (be_ref[i] != be_ref[jnp.maximum(i - 1, 0)])))
    def _():
        wgb_ref[...] = wg_ref[...].astype(BF16)
        wub_ref[...] = wu_ref[...].astype(BF16)
        wdb_ref[...] = wd_ref[...].astype(BF16)

    for j in range(x_ref.shape[0] // sub):
        rows = slice(j * sub, (j + 1) * sub)

        @pl.when(n_valid > j * sub)
        def _(j=j, rows=rows):
            lo, hi = _unpack_halves(x_ref[rows, :])
            c = lo.shape[1]
            keep = lax.broadcasted_iota(jnp.int32, lo.shape, 0) < n_valid - j * sub
            lo = jnp.where(keep, lo, 0.0).astype(BF16)
            hi = jnp.where(keep, hi, 0.0).astype(BF16)
            ff = wgb_ref.shape[1]
            y = None
            for part in range(FFN_COL_SPLIT):
                cs = slice(part * ff // FFN_COL_SPLIT, (part + 1) * ff // FFN_COL_SPLIT)
                hg = _dot(lo, wgb_ref[:c, cs]) + _dot(hi, wgb_ref[c:, cs])
                hu = _dot(lo, wub_ref[:c, cs]) + _dot(hi, wub_ref[c:, cs])
                hid = (hg * _sigmoid(hg) * hu).astype(BF16)
                yp = _dot(hid, wdb_ref[cs, :])
                y = yp if y is None else y + yp
            o_ref[rows, :] = _pack_halves(y)

        @pl.when(n_valid <= j * sub)
        def _(rows=rows):
            o_ref[rows, :] = jnp.zeros((sub, o_ref.shape[1]), o_ref.dtype)


def _expert_ffn(x, blk_exp, blk_valid, blk_order, n_used, w_gate, w_up, w_down, layer, tm, sub, name):
    r = x.shape[0]
    d, ff = w_gate.shape[2:]
    assert tm % sub == 0
    w_spec = lambda shape: pl.BlockSpec((None, None) + shape, lambda i, be, nv, od, nu: (layer, be[i], 0, 0))
    return pl.pallas_call(
        functools.partial(_ffn_kernel, sub=sub),
        grid_spec=pltpu.PrefetchScalarGridSpec(
            num_scalar_prefetch=4,
            grid=(r // tm,),
            in_specs=[pl.BlockSpec((tm, d // 2), lambda i, be, nv, od, nu: (od[jnp.minimum(i, nu[0] - 1)], 0)),
                      w_spec((d, ff)), w_spec((d, ff)), w_spec((ff, d))],
            out_specs=pl.BlockSpec((tm, d // 2), lambda i, be, nv, od, nu: (od[i], 0)),
            scratch_shapes=[pltpu.VMEM((d, ff), BF16), pltpu.VMEM((d, ff), BF16), pltpu.VMEM((ff, d), BF16)],
        ),
        out_shape=jax.ShapeDtypeStruct((r, d // 2), jnp.int32),
        compiler_params=_cparams(("arbitrary",), V7X_VMEM_LIMIT_BYTES),
        name=name,
    )(blk_exp, blk_valid, blk_order, n_used, x, w_gate, w_up, w_down)


def _first_argmax(val, idx, sentinel):
    m = jnp.max(val, axis=0, keepdims=True)
    i = jnp.min(jnp.where(val == m, idx, sentinel), axis=0, keepdims=True)
    return m, i


def _route_kernel(x_ref, wt_ref, b_ref, e_ref, g_ref, c_ref):
    x = x_ref[...]
    w = wt_ref[...]
    t = x.shape[0]
    xh = x.astype(BF16)
    xl = (x - xh.astype(F32)).astype(BF16)
    wh = w.astype(BF16)
    wl = (w - wh.astype(F32)).astype(BF16)
    logits = _dot_nt(wh, xh) + (_dot_nt(wl, xh) + _dot_nt(wh, xl))
    scores = _sigmoid(logits)
    biased = scores + b_ref[...]
    gs = N_EXPERTS // N_GROUPS
    neg = -jnp.inf
    eid = lax.broadcasted_iota(jnp.int32, (N_EXPERTS, t), 0)
    sub = lax.broadcasted_iota(jnp.int32, (gs, t), 0)
    grow = lax.broadcasted_iota(jnp.int32, (N_GROUPS, t), 0)
    grp = jnp.zeros((N_GROUPS, t), F32)
    for g in range(N_GROUPS):
        bg = biased[g * gs:(g + 1) * gs]
        m1, i1 = _first_argmax(bg, sub, gs)
        m2 = jnp.max(jnp.where(sub == i1, neg, bg), axis=0, keepdims=True)
        grp = jnp.where(grow == g, m1 + m2, grp)
    chosen = jnp.zeros((N_GROUPS, t), F32)
    for _ in range(TOPK_GROUPS):
        _, gi = _first_argmax(grp, grow, N_GROUPS)
        hit = grow == gi
        chosen = jnp.where(hit, 1.0, chosen)
        grp = jnp.where(hit, neg, grp)
    chosen_e = jnp.concatenate([jnp.broadcast_to(chosen[g:g + 1], (gs, t)) for g in range(N_GROUPS)], axis=0)
    masked = jnp.where(chosen_e > 0.0, biased, neg)
    krow = lax.broadcasted_iota(jnp.int32, (TOP_K, t), 0)
    e_out = jnp.zeros((TOP_K, t), jnp.int32)
    g_out = jnp.zeros((TOP_K, t), F32)
    member = jnp.zeros((N_EXPERTS, t), F32)
    for k in range(TOP_K):
        _, idx = _first_argmax(masked, eid, N_EXPERTS)
        hit = eid == idx
        gk = jnp.sum(jnp.where(hit, scores, 0.0), axis=0, keepdims=True)
        masked = jnp.where(hit, neg, masked)
        member = jnp.where(hit, 1.0, member)
        e_out = jnp.where(krow == k, idx, e_out)
        g_out = jnp.where(krow == k, gk, g_out)
    g_out = g_out / jnp.sum(g_out, axis=0, keepdims=True) * ROUTED_SCALE
    e_ref[...] = e_out
    g_ref[...] = g_out
    c_ref[...] = jnp.sum(member, axis=1, keepdims=True).astype(jnp.int32)


def _route(x, router_w, layer, router_b, tm):
    n, d = x.shape
    e = router_w.shape[2]
    wt = jnp.swapaxes(router_w, 1, 2)
    return pl.pallas_call(
        _route_kernel,
        grid=(n // tm,),
        in_specs=[pl.BlockSpec((tm, d), lambda i: (i, 0)),
                  pl.BlockSpec((None, e, d), lambda i: (layer, 0, 0)),
                  pl.BlockSpec((e, 1), lambda i: (0, 0))],
        out_specs=[pl.BlockSpec((TOP_K, tm), lambda i: (0, i)),
                   pl.BlockSpec((TOP_K, tm), lambda i: (0, i)),
                   pl.BlockSpec((None, e, 1), lambda i: (i, 0, 0))],
        out_shape=[jax.ShapeDtypeStruct((TOP_K, n), jnp.int32), jax.ShapeDtypeStruct((TOP_K, n), F32),
                   jax.ShapeDtypeStruct((n // tm, e, 1), jnp.int32)],
        compiler_params=_cparams(("parallel",), V7X_VMEM_LIMIT_BYTES),
        name="route",
    )(x, wt, router_b[layer].astype(F32).reshape(e, 1))


def _slots_kernel(e_ref, base_ref, o_ref):
    e = e_ref[...]
    t = e.shape[1]
    eid = lax.broadcasted_iota(jnp.int32, (N_EXPERTS, t), 0)
    member = jnp.zeros((N_EXPERTS, t), F32)
    for k in range(TOP_K):
        member = jnp.where(eid == e[k:k + 1], 1.0, member)
    before = (lax.broadcasted_iota(jnp.int32, (t, t), 0) < lax.broadcasted_iota(jnp.int32, (t, t), 1))
    rank = _dot(member.astype(BF16), jnp.where(before, 1.0, 0.0).astype(BF16))
    slot = rank + base_ref[...].astype(F32)
    krow = lax.broadcasted_iota(jnp.int32, (TOP_K, t), 0)
    out = jnp.zeros((TOP_K, t), F32)
    for k in range(TOP_K):
        sk = jnp.sum(jnp.where(eid == e[k:k + 1], slot, 0.0), axis=0, keepdims=True)
        out = jnp.where(krow == k, sk, out)
    o_ref[...] = out.astype(jnp.int32)


def _dispatch_plan(e_idx, counts, tm, tile):
    k, n = e_idx.shape
    n_blocks = -(-(n * k) // tm) + N_EXPERTS
    counts = counts.reshape(n // tile, N_EXPERTS)
    total = jnp.sum(counts, axis=0)
    padded = (total + tm - 1) // tm * tm
    pad_end = jnp.cumsum(padded)
    tile_base = (pad_end - padded)[None, :] + jnp.cumsum(counts, axis=0) - counts
    slot_of = pl.pallas_call(
        _slots_kernel,
        grid=(n // tile,),
        in_specs=[pl.BlockSpec((k, tile), lambda i: (0, i)),
                  pl.BlockSpec((None, N_EXPERTS, 1), lambda i: (i, 0, 0))],
        out_specs=pl.BlockSpec((k, tile), lambda i: (0, i)),
        out_shape=jax.ShapeDtypeStruct((k, n), jnp.int32),
        compiler_params=_cparams(("parallel",)),
        name="slots",
    )(e_idx, tile_base.astype(jnp.int32).reshape(n // tile, N_EXPERTS, 1))
    pos = jnp.arange(n_blocks, dtype=jnp.int32)
    blk_exp = jnp.minimum(jnp.sum((pad_end[None, :] <= (pos * tm)[:, None]).astype(jnp.int32), axis=1), N_EXPERTS - 1)
    is_exp = blk_exp[:, None] == jnp.arange(N_EXPERTS, dtype=blk_exp.dtype)[None, :]
    per_blk = lambda v: jnp.sum(jnp.where(is_exp, v[None, :], 0), axis=1)
    n_used = jnp.maximum(pad_end[-1:] // tm, 1).astype(jnp.int32)
    first, count = per_blk((pad_end - padded) // tm), jnp.maximum(per_blk(padded // tm), 1)
    blk_order = jnp.where(pos < n_used[0], first + (pos - first + count - 1) % count, pos)
    blk_valid = jnp.clip(per_blk(pad_end - padded + total) - blk_order * tm, 0, tm)
    return (slot_of, blk_exp.astype(jnp.int32), blk_valid.astype(jnp.int32), blk_order.astype(jnp.int32), n_used,
            n_blocks)


V7X_SC_CORES = 2
V7X_SC_SUBCORES = 16
SC_WORKERS = V7X_SC_CORES * V7X_SC_SUBCORES
SC_CHUNK = 32
SC_GATHER_CHUNK = 40


def _sc_mesh():
    return plsc.VectorSubcoreMesh(core_axis_name="c", subcore_axis_name="s",
                                  num_cores=V7X_SC_CORES, num_subcores=V7X_SC_SUBCORES)


def _sc_worker_id():
    return lax.axis_index("s") * V7X_SC_CORES + lax.axis_index("c")


def _sc_gather_rows(table, slot_of):
    k, n = slot_of.shape
    w = table.shape[1]
    n_rows = k * n
    ch = SC_GATHER_CHUNK
    per_worker = n_rows // (SC_WORKERS * ch)
    assert per_worker * SC_WORKERS * ch == n_rows and per_worker % 2 == 0
    idx = slot_of.reshape(SC_WORKERS, per_worker, ch)

    def body(table_hbm, idx_hbm, out_hbm, idx_v, rows_v, sems):
        wid = _sc_worker_id()
        pltpu.sync_copy(idx_hbm.at[wid], idx_v)

        def gather(c, slot):
            return pltpu.make_async_copy(table_hbm.at[idx_v.at[c]], rows_v.at[slot], sems.at[slot])

        gather(0, 0).start()

        @pl.loop(0, per_worker, step=2)
        def _(c0):
            for slot in range(2):
                c = c0 + slot
                gather(c, slot).wait()

                @pl.when(c + 1 < per_worker)
                def _():
                    gather(c + 1, 1 - slot).start()

                row0 = pl.multiple_of((wid * per_worker + c) * ch, 8)
                pltpu.sync_copy(rows_v.at[slot], out_hbm.at[pl.ds(row0, ch)])

    return pl.kernel(
        body, out_type=jax.ShapeDtypeStruct((n_rows, w), table.dtype), mesh=_sc_mesh(),
        scratch_types=[pltpu.VMEM((per_worker, ch), jnp.int32), pltpu.VMEM((2, ch, w), table.dtype),
                       pltpu.SemaphoreType.DMA((2,))],
        name="sc_gather_rows",
    )(table, idx)


def _sc_scatter_rows(x, slot_of, n_slots):
    k, n = slot_of.shape
    w = x.shape[1]
    n_chunks = n // SC_CHUNK
    assert n_chunks * SC_CHUNK == n
    idx = slot_of.reshape(k, n_chunks, SC_CHUNK).transpose(1, 0, 2)
    rounds = -(-n_chunks // SC_WORKERS)

    def body(x_hbm, idx_hbm, out_hbm, idx_v, rows_v, sem):
        wid = _sc_worker_id()

        @pl.loop(0, rounds)
        def _(j):
            q = j * SC_WORKERS + wid

            @pl.when(q < n_chunks)
            def _():
                pltpu.sync_copy(idx_hbm.at[q], idx_v)
                pltpu.sync_copy(x_hbm.at[pl.ds(pl.multiple_of(q * SC_CHUNK, SC_CHUNK), SC_CHUNK)], rows_v)
                copies = [pltpu.make_async_copy(rows_v, out_hbm.at[idx_v.at[kk]], sem) for kk in range(k)]
                for cp in copies:
                    cp.start()
                for cp in copies:
                    cp.wait()

    return pl.kernel(
        body, out_type=jax.ShapeDtypeStruct((n_slots, w), x.dtype), mesh=_sc_mesh(),
        scratch_types=[pltpu.VMEM((k, SC_CHUNK), jnp.int32), pltpu.VMEM((SC_CHUNK, w), x.dtype),
                       pltpu.SemaphoreType.DMA],
        name="sc_scatter_rows",
    )(x, idx)


MOE_BLOCK = 768
MOE_SUB = 384
FFN_COL_SPLIT = 2
ROW_TILE = 640


def _moe_postnorm_kernel(x_ref, y_ref, gate_ref, sh_ref, g_ref, b_ref, ofp_ref, ofs_ref, ob_ref, *, alpha, tiles_p):
    def part(which):
        of_ref = (ofp_ref, ofs_ref)[which]
        gate = gate_ref[...]
        lo, hi = _unpack_halves(sh_ref[...])
        for k in range(TOP_K):
            lo_k, hi_k = _unpack_halves(y_ref[k])
            lo = lo + gate[:, k:k + 1] * lo_k
            hi = hi + gate[:, k:k + 1] * hi_k
        z = alpha * x_ref[...] + jnp.concatenate([lo, hi], axis=1)
        out = _layer_norm(z, g_ref[...], b_ref[...])
        of_ref[...] = out
        ob_ref[...] = out.astype(BF16)

    _on_part(tiles_p, part)


def _moe_postnorm(x, n_p, y_tok, gate, shared, g, b, alpha):
    m, d = x.shape
    tm = PART_TILE
    row = pl.BlockSpec((tm, d), lambda i: (i, 0))
    vec = pl.BlockSpec((1, d), lambda i: (0, 0))
    part_p, part_s = _part_specs(n_p, (tm, d))
    return pl.pallas_call(
        functools.partial(_moe_postnorm_kernel, alpha=alpha, tiles_p=n_p // tm),
        grid=(m // tm,),
        in_specs=[row, pl.BlockSpec((TOP_K, tm, d // 2), lambda i: (0, i, 0)),
                  pl.BlockSpec((tm, TOP_K), lambda i: (i, 0)), pl.BlockSpec((tm, d // 2), lambda i: (i, 0)), vec, vec],
        out_specs=[part_p, part_s, row],
        out_shape=[jax.ShapeDtypeStruct((n_p, d), F32), jax.ShapeDtypeStruct((m - n_p, d), F32),
                   jax.ShapeDtypeStruct((m, d), BF16)],
        compiler_params=_cparams(("arbitrary",), V7X_VMEM_LIMIT_BYTES),
        name="postnorm_moe",
    )(x, y_tok, gate, shared, g.reshape(1, d), b.reshape(1, d))


def _moe(xf, n_p, xp, layer, router_w, router_b, w_gate, w_up, w_down, sw_gate, sw_up, sw_down, ln_g, ln_b, alpha):
    n, d = xf.shape
    e_idx, gate, counts = _route(xf, router_w, layer, router_b, ROW_TILE)
    slot_of, blk_exp, blk_valid, blk_order, n_used, n_blocks = _dispatch_plan(e_idx, counts, MOE_BLOCK, ROW_TILE)
    x_sorted = _sc_scatter_rows(xp, slot_of, n_blocks * MOE_BLOCK)
    y_sorted = _expert_ffn(x_sorted, blk_exp, blk_valid, blk_order, n_used, w_gate, w_up, w_down, layer,
                           MOE_BLOCK, MOE_SUB, "routed_ffn")
    y_tok = _sc_gather_rows(y_sorted, slot_of).reshape(TOP_K, n, d // 2)
    n_sh = n // ROW_TILE
    shared = _expert_ffn(xp, jnp.zeros((n_sh,), jnp.int32), jnp.full((n_sh,), ROW_TILE, jnp.int32),
                         jnp.arange(n_sh, dtype=jnp.int32), jnp.full((1,), n_sh, jnp.int32),
                         sw_gate[:, None], sw_up[:, None], sw_down[:, None], layer, ROW_TILE, ROW_TILE, "shared_ffn")
    return _moe_postnorm(xf, n_p, y_tok, gate.T, shared, ln_g, ln_b, alpha)


def kernel(x_prompt, x_sample, state_hgrn, cache_win_k, cache_win_v, cache_mem_k, cache_mem_v, mem_prompt,
           w_in_a, lb_logits, g_norm_a, w_out_a, w_in_b, w_out_b, w_kv_shared, w_mem_kv, ln_g, ln_b,
           router_w, router_b, exp_w_gate, exp_w_up, exp_w_down, sh_w_gate, sh_w_up, sh_w_down):
    bp, sp, d = x_prompt.shape
    ns = x_sample.shape[0]
    assert x_sample.shape[1] == 1
    depth = ln_g.shape[0]
    n_a = w_in_a.shape[0]
    alpha = (2 * depth) ** 0.25
    n_p = bp * sp
    mem_len = mem_prompt.shape[1]
    a_mix = 4 * A_HEADS * HEAD_DIM

    xf_p, xf_s = x_prompt.reshape(n_p, d), x_sample.reshape(ns, d)
    xb = jnp.concatenate([xf_p.astype(BF16), xf_s.astype(BF16)], axis=0)
    lower_bounds = jnp.cumsum(jax.nn.softmax(lb_logits.astype(F32), axis=0), axis=0)
    mem_flat = mem_prompt.reshape(bp * mem_len, d)
    pos_all = jnp.concatenate([jnp.tile(jnp.arange(sp, dtype=jnp.int32), bp),
                               jnp.full((ns,), PAST_LEN, jnp.int32)])
    tables = _rope_tables(pos_all)

    hgrn_p, hgrn_s, mem_k_p, mem_v_p = [], [], [], []
    for layer in range(depth):
        kvm = _proj([mem_flat], w_mem_kv, layer, F32, 2 * mem_len, 1024, "mem_kv")
        mem_k_p.append(kvm[:, :MEM_W].reshape(bp, mem_len, MEM_HEADS, HEAD_DIM))
        mem_v_p.append(kvm[:, MEM_W:].reshape(bp, mem_len, MEM_HEADS, HEAD_DIM))
        if layer < n_a:
            a = layer
            proj = _proj([xb], w_in_a, a, F32, ROW_TILE // 2, 1664, "proj_in_a")
            proj_s = proj[n_p:]
            o_x_s, st_s = _hgrn_step(proj_s.reshape(ns, 1, -1), state_hgrn, a, lower_bounds[a], g_norm_a[a])
            o_m_s = _mem_attn_decode(proj_s[:, a_mix:].reshape(ns, MEM_HEADS, HEAD_DIM), cache_mem_k, cache_mem_v, layer)
            o_x, st_p = _hgrn_prompt(proj, lower_bounds[a], g_norm_a[a], o_x_s.reshape(ns, -1), bp, sp)
            o_m = _mem_attn_prompt(proj, a_mix // MEM_W, kvm, o_m_s.reshape(ns, -1), bp, sp, 512)
            hgrn_p.append(st_p)
            hgrn_s.append(st_s)
            w_out, w_out_layer = w_out_a, a
        else:
            bl = layer - n_a
            if layer == n_a:
                kv = _proj([xb], w_kv_shared[None], 0, F32, ROW_TILE, 1024, "proj_kv")
                k_r, k_p, k_s, v_p, v_s = _shared_kv(kv, tables, n_p)
            proj = _proj([xb], w_in_b, bl, F32, ROW_TILE, 1024, "proj_in_b")
            q_r = _rope(proj, B_QHEADS, tables, F32, ROW_TILE, "rope_q")
            q_s = q_r[n_p:].reshape(ns, B_GROUPS, B_SLOTS, HEAD_DIM)
            o_x_s = _dil_attn_decode(q_s, k_s, v_s, cache_win_k, cache_win_v)
            o_m_s = _mem_attn_decode(proj[n_p:, B_QHEADS * HEAD_DIM:].reshape(ns, MEM_HEADS, HEAD_DIM),
                                     cache_mem_k, cache_mem_v, layer)
            o_x = _dil_attn_prompt(q_r, k_r, kv, o_x_s.reshape(ns, -1), bp, sp)
            o_m = _mem_attn_prompt(proj, B_QHEADS * HEAD_DIM // MEM_W, kvm, o_m_s.reshape(ns, -1), bp, sp, 512)
            w_out, w_out_layer = w_out_b, bl
        y = _proj([o_x, o_m], w_out, w_out_layer, F32, ROW_TILE, 1024, "proj_out")
        xf, xp = _postnorm(xf_p, xf_s, y, ln_g[layer, 0], ln_b[layer, 0], alpha, "postnorm_mix")
        xf_p, xf_s, xb = _moe(xf, n_p, xp, layer, router_w, router_b, exp_w_gate, exp_w_up, exp_w_down,
                              sh_w_gate, sh_w_up, sh_w_down, ln_g[layer, 1], ln_b[layer, 1], alpha)

    w_p = min(max(w for w, _ in B_PATTERNS), sp)
    k_p = k_p.reshape(bp, sp, B_SLOTS, HEAD_DIM)
    v_p = v_p.reshape(bp, sp, B_SLOTS, HEAD_DIM)
    return (xf_p.reshape(bp, sp, d), xf_s.reshape(ns, 1, d),
            jnp.stack(hgrn_p), jnp.stack(hgrn_s),
            k_p[:, sp - w_p:], v_p[:, sp - w_p:],
            k_s.reshape(ns, 1, B_SLOTS, HEAD_DIM), v_s.reshape(ns, 1, B_SLOTS, HEAD_DIM),
            jnp.stack(mem_k_p), jnp.stack(mem_v_p))
```

```python
import functools

import jax
import jax.numpy as jnp
from jax import lax
from jax.experimental import pallas as pl
from jax.experimental.pallas import tpu as pltpu
from jax.experimental.pallas import tpu_sc as plsc

F32 = jnp.float32
BF16 = jnp.bfloat16

HEAD_DIM = 128
A_HEADS = 12
A_CHUNK = 64
A_SUB = 16
B_PATTERNS = ((128, 1), (512, 4), (2048, 16))
B_SLOTS = 4
B_GROUPS = len(B_PATTERNS)
B_QHEADS = B_GROUPS * B_SLOTS
B_BLOCK = 128
MEM_HEADS = 4
MEM_W = MEM_HEADS * HEAD_DIM
KV_W = B_SLOTS * HEAD_DIM
ROPE_THETA = 500000.0
ROPE_DIM = HEAD_DIM // 4
N_EXPERTS = 64
N_GROUPS = 8
TOPK_GROUPS = 4
TOP_K = 8
ROUTED_SCALE = 2.5
LN_EPS = 1e-5
RMS_EPS = 1e-6
ATTN_SCALE = HEAD_DIM ** -0.5
PAST_LEN = 2048

V7X_VMEM_LIMIT_BYTES = 56 * 1024 * 1024
LANES = 128
NEG_BIG = -1e30


def _cparams(sem, vmem=None):
    return pltpu.CompilerParams(dimension_semantics=sem, vmem_limit_bytes=vmem)


def _dot(a, b):
    return jnp.dot(a, b, preferred_element_type=F32)


def _dot_nt(a, b):
    return lax.dot_general(a, b, (((1,), (1,)), ((), ())), preferred_element_type=F32)


def _sigmoid(x):
    return 1.0 / (1.0 + jnp.exp(-x))


def _proj_kernel(*refs, n_lhs):
    x_refs = refs[:n_lhs]
    w_refs = refs[n_lhs:2 * n_lhs]
    o_ref = refs[2 * n_lhs]
    wb_refs = refs[2 * n_lhs + 1:]

    @pl.when(pl.program_id(1) == 0)
    def _():
        for w_ref, wb_ref in zip(w_refs, wb_refs):
            wb_ref[...] = w_ref[...].astype(BF16)

    acc = None
    for x_ref, wb_ref in zip(x_refs, wb_refs):
        d = _dot(x_ref[...].astype(BF16), wb_ref[...])
        acc = d if acc is None else acc + d
    o_ref[...] = acc.astype(o_ref.dtype)


def _proj(lhs, w, layer, out_dtype, tm, tn, name):
    m = lhs[0].shape[0]
    n = w.shape[2]
    koff = 0
    in_specs, w_specs, scratch = [], [], []
    for x in lhs:
        k = x.shape[1]
        assert koff % k == 0 and m % tm == 0 and n % tn == 0
        in_specs.append(pl.BlockSpec((tm, k), lambda j, i: (i, 0)))
        w_specs.append(pl.BlockSpec((None, k, tn), functools.partial(lambda j, i, kb: (layer, kb, j), kb=koff // k)))
        scratch.append(pltpu.VMEM((k, tn), BF16))
        koff += k
    assert koff == w.shape[1]
    return pl.pallas_call(
        functools.partial(_proj_kernel, n_lhs=len(lhs)),
        grid=(n // tn, m // tm),
        in_specs=in_specs + w_specs,
        out_specs=pl.BlockSpec((tm, tn), lambda j, i: (i, j)),
        out_shape=jax.ShapeDtypeStruct((m, n), out_dtype),
        scratch_shapes=scratch,
        compiler_params=_cparams(("arbitrary", "arbitrary"), V7X_VMEM_LIMIT_BYTES),
        name=name,
    )(*lhs, *([w] * len(lhs)))


def _pack_halves(x):
    c = x.shape[1] // 2
    lo = pltpu.bitcast(x[:, :c].astype(BF16).astype(F32), jnp.int32)
    hi = pltpu.bitcast(x[:, c:].astype(BF16).astype(F32), jnp.int32)
    return hi | lax.shift_right_logical(lo, 16)


def _unpack_halves(w):
    lo = pltpu.bitcast(lax.shift_left(w, 16), F32)
    hi = pltpu.bitcast(w & jnp.int32(-65536), F32)
    return lo, hi


def _layer_norm(z, g, b):
    mu = jnp.mean(z, axis=-1, keepdims=True)
    zc = z - mu
    var = jnp.mean(zc * zc, axis=-1, keepdims=True)
    return zc * lax.rsqrt(var + LN_EPS) * g + b


PART_TILE = 128


def _part_specs(n_p, block):
    tiles_p = n_p // PART_TILE
    rest = (0,) * (len(block) - 1)
    return (pl.BlockSpec(block, lambda i: (jnp.minimum(i, tiles_p - 1),) + rest),
            pl.BlockSpec(block, lambda i: (jnp.maximum(i - tiles_p, 0),) + rest))


def _on_part(tiles_p, fn):
    i = pl.program_id(0)
    pl.when(i < tiles_p)(functools.partial(fn, 0))
    pl.when(i >= tiles_p)(functools.partial(fn, 1))


def _postnorm_kernel(xp_ref, xs_ref, y_ref, g_ref, b_ref, of_ref, op_ref, *, alpha, tiles_p):
    def part(which):
        x_ref = (xp_ref, xs_ref)[which]
        out = _layer_norm(alpha * x_ref[...] + y_ref[...], g_ref[...], b_ref[...])
        of_ref[...] = out
        op_ref[...] = _pack_halves(out)

    _on_part(tiles_p, part)


def _postnorm(x_p, x_s, y, g, b, alpha, name):
    n_p, d = x_p.shape
    m = n_p + x_s.shape[0]
    row = pl.BlockSpec((PART_TILE, d), lambda i: (i, 0))
    vec = pl.BlockSpec((1, d), lambda i: (0, 0))
    part_p, part_s = _part_specs(n_p, (PART_TILE, d))
    return pl.pallas_call(
        functools.partial(_postnorm_kernel, alpha=alpha, tiles_p=n_p // PART_TILE),
        grid=(m // PART_TILE,),
        in_specs=[part_p, part_s, row, vec, vec],
        out_specs=[row, pl.BlockSpec((PART_TILE, d // 2), lambda i: (i, 0))],
        out_shape=[jax.ShapeDtypeStruct((m, d), F32), jax.ShapeDtypeStruct((m, d // 2), jnp.int32)],
        compiler_params=_cparams(("arbitrary",)),
        name=name,
    )(x_p, x_s, y, g.reshape(1, d), b.reshape(1, d))


def _hgrn_prompt_kernel(q_ref, f_ref, v_ref, gate_ref, lb_ref, gn_ref, os_ref, o_ref, s_ref, st_ref,
                        *, seq, heads, n_batch):
    _prompt_then_sample(n_batch, os_ref, o_ref,
                        functools.partial(_hgrn_sequence, q_ref, f_ref, v_ref, gate_ref, lb_ref, gn_ref,
                                          o_ref, s_ref, st_ref, seq=seq, heads=heads), axis=1)


def _hgrn_sequence(q_ref, f_ref, v_ref, gate_ref, lb_ref, gn_ref, o_ref, s_ref, st_ref, *, seq, heads):
    c = A_CHUNK
    n_chunks = seq // c
    n_sub = c // A_SUB
    gn = gn_ref[...]
    row = lax.broadcasted_iota(jnp.int32, (c, c), 0)
    col = lax.broadcasted_iota(jnp.int32, (c, c), 1)
    tril = jnp.where(row >= col, 1.0, 0.0).astype(BF16)
    same_sub = (row // A_SUB) == (col // A_SUB)
    diag_dist = jnp.where(same_sub, row - col, -1)
    off_mask = col < (row // A_SUB) * A_SUB
    st_ref[...] = jnp.zeros_like(st_ref)

    def chunk(ci, carry):
        for hh in range(heads):
            head_chunk(ci, hh)
        return carry

    def head_chunk(ci, hh):
        r0 = pl.multiple_of(ci * c, c)
        hsl = slice(hh * HEAD_DIM, (hh + 1) * HEAD_DIM)
        lb = lb_ref[hh]
        qp = q_ref[pl.ds(r0, c), hsl]
        fp = f_ref[pl.ds(r0, c), hsl]
        v = v_ref[pl.ds(r0, c), hsl]
        gp = gate_ref[pl.ds(r0, c), hsl]
        st = st_ref[hh]
        q = qp * _sigmoid(qp)
        forget = lb + (1.0 - lb) * _sigmoid(fp)
        logf = jnp.log2(forget)
        k = 1.0 - forget
        hi = logf.astype(BF16)
        r1 = logf - hi.astype(F32)
        mid = r1.astype(BF16)
        lo = (r1 - mid.astype(F32)).astype(BF16)
        g = _dot(tril, hi) + _dot(tril, mid) + _dot(tril, lo)
        v_b = v.astype(BF16)
        o = _dot_nt((q * jnp.exp2(g)).astype(BF16), st.astype(BF16))
        rows = [jnp.zeros((A_SUB, c), F32)]
        for i in range(1, n_sub):
            gref = g[i * A_SUB:i * A_SUB + 1, :]
            qt = q[i * A_SUB:(i + 1) * A_SUB, :] * jnp.exp2(g[i * A_SUB:(i + 1) * A_SUB, :] - gref)
            kt = k * jnp.exp2(jnp.minimum(gref - g, 0.0))
            rows.append(_dot_nt(qt.astype(BF16), kt.astype(BF16)))
        a = jnp.where(off_mask, jnp.concatenate(rows, axis=0), 0.0)
        for d in range(A_SUB):
            kr = k if d == 0 else pltpu.roll(k, d, axis=0)
            gr = g if d == 0 else pltpu.roll(g, d, axis=0)
            x = q * kr * jnp.exp2(g - gr)
            a = jnp.where(diag_dist == d, jnp.sum(x, axis=-1, keepdims=True), a)
        o = o + _dot(a.astype(BF16), v_b)
        gend = g[c - 1:c, :]
        kt_end = k * jnp.exp2(gend - g)
        st_ref[hh] = jnp.exp2(gend) * st + _dot(v_b.T, kt_end.astype(BF16))
        o = o * lax.rsqrt(jnp.mean(o * o, axis=-1, keepdims=True) + RMS_EPS) * gn
        o_ref[pl.ds(r0, c), hsl] = (o * (gp * _sigmoid(gp))).astype(o_ref.dtype)

    lax.fori_loop(0, n_chunks, chunk, 0, unroll=A_CHUNK_UNROLL)
    for hh in range(heads):
        s_ref[hh] = st_ref[hh].T


A_HEADS_PER_STEP = 4
A_CHUNK_UNROLL = 4


def _hgrn_prompt(proj, lb, g_norm, o_sample, n_batch, seq):
    h = A_HEADS
    hp = A_HEADS_PER_STEP
    ns = o_sample.shape[0]
    assert h % hp == 0 and ns <= seq and proj.shape[0] == n_batch * seq + ns
    ng = h // hp
    w = hp * HEAD_DIM
    last = n_batch - 1
    blk = lambda off: pl.BlockSpec((seq, w), functools.partial(lambda hg, b, off: (jnp.minimum(b, last), off + hg), off=off))
    return pl.pallas_call(
        functools.partial(_hgrn_prompt_kernel, seq=seq, heads=hp, n_batch=n_batch),
        grid=(ng, n_batch + 1),
        in_specs=[blk(0), blk(ng), blk(2 * ng), blk(3 * ng),
                  pl.BlockSpec((hp, 1, HEAD_DIM), lambda hg, b: (hg, 0, 0)),
                  pl.BlockSpec((1, HEAD_DIM), lambda hg, b: (0, 0)),
                  pl.BlockSpec((ns, w), lambda hg, b: (0, hg))],
        out_specs=[pl.BlockSpec((seq, w), lambda hg, b: (b, hg)),
                   pl.BlockSpec((None, hp, HEAD_DIM, HEAD_DIM), lambda hg, b: (jnp.minimum(b, last), hg, 0, 0))],
        out_shape=[jax.ShapeDtypeStruct((proj.shape[0], h * HEAD_DIM), BF16),
                   jax.ShapeDtypeStruct((n_batch, h, HEAD_DIM, HEAD_DIM), F32)],
        scratch_shapes=[pltpu.VMEM((hp, HEAD_DIM, HEAD_DIM), F32)],
        compiler_params=_cparams(("arbitrary", "arbitrary"), V7X_VMEM_LIMIT_BYTES),
        name="hgrn_prompt",
    )(proj, proj, proj, proj, lb.reshape(h, 1, HEAD_DIM), g_norm.reshape(1, HEAD_DIM), o_sample)


def _softmax_av(s, v):
    m = jnp.max(s, axis=-1, keepdims=True)
    p = jnp.exp(s - m)
    l = jnp.sum(p, axis=-1, keepdims=True)
    return _dot(p.astype(BF16), v) / l


def _prompt_then_sample(n_prompt_steps, os_ref, o_ref, prompt_step, axis=0):
    i = pl.program_id(axis)
    pl.when(i < n_prompt_steps)(prompt_step)

    @pl.when(i >= n_prompt_steps)
    def _():
        o_ref[:os_ref.shape[0], :] = os_ref[...]


def _mem_attn_kernel(q_ref, k_ref, v_ref, os_ref, o_ref, *, n_steps):
    def step():
        for h in range(MEM_HEADS):
            sl = slice(h * HEAD_DIM, (h + 1) * HEAD_DIM)
            s = _dot_nt(q_ref[:, sl].astype(BF16), k_ref[:, sl].astype(BF16)) * ATTN_SCALE
            o_ref[:, sl] = _softmax_av(s, v_ref[:, sl].astype(BF16)).astype(o_ref.dtype)

    _prompt_then_sample(n_steps, os_ref, o_ref, step)


def _mem_attn_prompt(proj, q_col, kv, o_sample, n_batch, seq, tq):
    mem_len = kv.shape[0] // n_batch
    nq = seq // tq
    n_steps = n_batch * nq
    ns = o_sample.shape[0]
    assert ns <= tq and proj.shape[0] == n_batch * seq + ns
    last = n_steps - 1
    return pl.pallas_call(
        functools.partial(_mem_attn_kernel, n_steps=n_steps),
        grid=(n_steps + 1,),
        in_specs=[pl.BlockSpec((tq, MEM_W), lambda i: (jnp.minimum(i, last), q_col)),
                  pl.BlockSpec((mem_len, MEM_W), lambda i: (jnp.minimum(i, last) // nq, 0)),
                  pl.BlockSpec((mem_len, MEM_W), lambda i: (jnp.minimum(i, last) // nq, 1)),
                  pl.BlockSpec((ns, MEM_W), lambda i: (0, 0))],
        out_specs=pl.BlockSpec((tq, MEM_W), lambda i: (i, 0)),
        out_shape=jax.ShapeDtypeStruct((proj.shape[0], MEM_W), BF16),
        compiler_params=_cparams(("arbitrary",)),
        name="mem_attn_prompt",
    )(proj, kv, kv, o_sample)


def _rope_tables(pos):
    half = ROPE_DIM // 2
    inv_freq = ROPE_THETA ** (-jnp.arange(0, ROPE_DIM, 2, dtype=F32) / ROPE_DIM)
    ang = pos.astype(F32)[:, None] * inv_freq[None, :]
    cos, sin = jnp.cos(ang), jnp.sin(ang)
    n = pos.shape[0]
    one = jnp.ones((n, HEAD_DIM - ROPE_DIM), F32)
    zero = jnp.zeros((n, HEAD_DIM - half), F32)
    c = jnp.concatenate([cos, cos, one], axis=1)
    s_up = jnp.concatenate([-sin, zero], axis=1)
    s_dn = jnp.concatenate([jnp.zeros((n, half), F32), sin, zero[:, half:]], axis=1)
    return c, s_up, s_dn


def _rope_kernel(x_ref, c_ref, su_ref, sd_ref, o_ref, *, n_heads):
    c, su, sd = c_ref[...], su_ref[...], sd_ref[...]
    for h in range(n_heads):
        sl = slice(h * HEAD_DIM, (h + 1) * HEAD_DIM)
        o_ref[:, sl] = _rope_head(x_ref[:, sl], c, su, sd).astype(o_ref.dtype)


def _rope_head(x, c, su, sd):
    half = ROPE_DIM // 2
    up = pltpu.roll(x, HEAD_DIM - half, axis=1)
    dn = pltpu.roll(x, half, axis=1)
    return x * c + up * su + dn * sd


def _kv_kernel(kv_ref, c_ref, su_ref, sd_ref, k_ref, kp_ref, ks_ref, vp_ref, vs_ref, *, tiles_p):
    c, su, sd = c_ref[...], su_ref[...], sd_ref[...]

    def part(which):
        k4_ref, v4_ref = ((kp_ref, vp_ref), (ks_ref, vs_ref))[which]
        for h in range(B_SLOTS):
            sl = slice(h * HEAD_DIM, (h + 1) * HEAD_DIM)
            k = _rope_head(kv_ref[:, sl], c, su, sd)
            k_ref[:, sl] = k
            k4_ref[:, h, :] = k
            v4_ref[:, h, :] = kv_ref[:, KV_W + h * HEAD_DIM:KV_W + (h + 1) * HEAD_DIM]

    _on_part(tiles_p, part)


def _shared_kv(kv, tables, n_p):
    m = kv.shape[0]
    tm = PART_TILE
    tab = pl.BlockSpec((tm, HEAD_DIM), lambda i: (i, 0))
    part_p, part_s = _part_specs(n_p, (tm, B_SLOTS, HEAD_DIM))
    cache = lambda rows: jax.ShapeDtypeStruct((rows, B_SLOTS, HEAD_DIM), F32)
    return pl.pallas_call(
        functools.partial(_kv_kernel, tiles_p=n_p // tm),
        grid=(m // tm,),
        in_specs=[pl.BlockSpec((tm, 2 * KV_W), lambda i: (i, 0)), tab, tab, tab],
        out_specs=[pl.BlockSpec((tm, KV_W), lambda i: (i, 0)), part_p, part_s, part_p, part_s],
        out_shape=[jax.ShapeDtypeStruct((m, KV_W), F32), cache(n_p), cache(m - n_p), cache(n_p), cache(m - n_p)],
        compiler_params=_cparams(("arbitrary",)),
        name="shared_kv",
    )(kv, *tables)


def _rope(x, n_heads, tables, out_dtype, tm, name):
    m = x.shape[0]
    w = n_heads * HEAD_DIM
    tab = pl.BlockSpec((tm, HEAD_DIM), lambda i: (i, 0))
    return pl.pallas_call(
        functools.partial(_rope_kernel, n_heads=n_heads),
        grid=(m // tm,),
        in_specs=[pl.BlockSpec((tm, w), lambda i: (i, 0)), tab, tab, tab],
        out_specs=pl.BlockSpec((tm, w), lambda i: (i, 0)),
        out_shape=jax.ShapeDtypeStruct((m, w), out_dtype),
        compiler_params=_cparams(("parallel",)),
        name=name,
    )(x, *tables)


DIL_BATCH = 8


def _dil_prompt_kernel(*refs, seq, n_batch):
    os_ref, o_ref = refs[B_GROUPS + 2:B_GROUPS + 4]
    _prompt_then_sample(n_batch, os_ref, o_ref, functools.partial(_dil_sequence, *refs, seq=seq), axis=1)


def _dil_sequence(*refs, seq):
    q_refs = refs[:B_GROUPS]
    k_ref, v_ref, _, o_ref, m_ref, l_ref, acc_ref = refs[B_GROUPS:]
    blk = B_BLOCK
    base = (lax.broadcasted_iota(jnp.int32, (blk, blk), 0) - lax.broadcasted_iota(jnp.int32, (blk, blk), 1))
    for g, (win, dil) in enumerate(B_PATTERNS):
        span = win // dil
        n_blk = seq // dil // blk
        assert span <= blk

        def body(it, carry, g=g, dil=dil, span=span, n_blk=n_blk):
            own, prev, prev_ok = [], [], []
            for j in range(DIL_BATCH):
                t = it * DIL_BATCH + j
                r, n = t % dil, t // dil

                def rows(nn, r=r):
                    start = nn * (blk * dil) + r
                    return pl.ds(pl.multiple_of(start, blk), blk) if dil == 1 else pl.ds(start, blk, stride=dil)

                own.append(rows(n))
                prev.append(rows(jnp.maximum(n - 1, 0)))
                prev_ok.append(base <= jnp.where(n > 0, span - blk, -blk - 1))
            load = lambda ref, idx: jnp.stack([ref[i, :].astype(BF16) for i in idx])
            scores = lambda a, b: jnp.einsum('bqd,bkd->bqk', a, b, preferred_element_type=F32) * ATTN_SCALE
            weighted = lambda p, v: jnp.einsum('bqk,bkd->bqd', p.astype(BF16), v, preferred_element_type=F32)
            q = load(q_refs[g], own)
            s = jnp.where((base >= 0)[None], scores(q, load(k_ref, own)), NEG_BIG)
            m_b = jnp.max(s, axis=-1, keepdims=True)
            if n_blk > 1:
                s_prev = jnp.where(jnp.stack(prev_ok), scores(q, load(k_ref, prev)), NEG_BIG)
                m_b = jnp.maximum(m_b, jnp.max(s_prev, axis=-1, keepdims=True))
            p = jnp.exp(s - m_b)
            l_b = jnp.sum(p, axis=-1, keepdims=True)
            acc_b = weighted(p, load(v_ref, own))
            if n_blk > 1:
                p_prev = jnp.exp(s_prev - m_b)
                l_b = l_b + jnp.sum(p_prev, axis=-1, keepdims=True)
                acc_b = acc_b + weighted(p_prev, load(v_ref, prev))
            for j, idx in enumerate(own):
                if g == 0:
                    m_ref[idx, :] = jnp.broadcast_to(m_b[j], (blk, HEAD_DIM))
                    l_ref[idx, :] = jnp.broadcast_to(l_b[j], (blk, HEAD_DIM))
                    acc_ref[idx, :] = acc_b[j]
                else:
                    m_old = m_ref[idx, :]
                    m_new = jnp.maximum(m_old, m_b[j])
                    a_old = jnp.exp(m_old - m_new)
                    a_b = jnp.exp(m_b[j] - m_new)
                    m_ref[idx, :] = m_new
                    l_ref[idx, :] = l_ref[idx, :] * a_old + l_b[j] * a_b
                    acc_ref[idx, :] = acc_ref[idx, :] * a_old + acc_b[j] * a_b
            return carry

        assert (dil * n_blk) % DIL_BATCH == 0
        lax.fori_loop(0, dil * n_blk // DIL_BATCH, body, 0)
    o_ref[...] = (acc_ref[...] / l_ref[...]).astype(o_ref.dtype)


def _dil_attn_prompt(q, k, kv, o_sample, n_batch, seq):
    ns = o_sample.shape[0]
    assert ns <= seq and q.shape[0] == n_batch * seq + ns
    last = n_batch - 1
    col = lambda c0: pl.BlockSpec((seq, HEAD_DIM), functools.partial(lambda h, b, c0: (jnp.minimum(b, last), c0 + h), c0=c0))
    return pl.pallas_call(
        functools.partial(_dil_prompt_kernel, seq=seq, n_batch=n_batch),
        grid=(B_SLOTS, n_batch + 1),
        in_specs=[col(g * B_SLOTS) for g in range(B_GROUPS)] + [col(0), col(B_SLOTS),
                                                                 pl.BlockSpec((ns, HEAD_DIM), lambda h, b: (0, h))],
        out_specs=pl.BlockSpec((seq, HEAD_DIM), lambda h, b: (b, h)),
        out_shape=jax.ShapeDtypeStruct((q.shape[0], KV_W), BF16),
        scratch_shapes=[pltpu.VMEM((seq, HEAD_DIM), F32)] * 3,
        compiler_params=_cparams(("arbitrary", "arbitrary"), V7X_VMEM_LIMIT_BYTES),
        name="dil_attn_prompt",
    )(*([q] * B_GROUPS), k, kv, o_sample)


MEM_DECODE_SEQS = 4


def _mem_decode_kernel(q_ref, k_ref, v_ref, o_ref):
    n_seq, rows, _ = k_ref.shape
    nh = q_ref.shape[1]
    own = (lax.broadcasted_iota(jnp.int32, (nh, rows), 1) % nh) == lax.broadcasted_iota(jnp.int32, (nh, rows), 0)
    for i in range(n_seq):
        s = _dot_nt(q_ref[i].astype(BF16), k_ref[i].astype(BF16)) * ATTN_SCALE
        o_ref[i] = _softmax_av(jnp.where(own, s, NEG_BIG), v_ref[i].astype(BF16)).astype(o_ref.dtype)


def _mem_attn_decode(q, cache_k, cache_v, layer):
    ns, nh, hd = q.shape
    n_layers, _, mem_len = cache_k.shape[:3]
    g = MEM_DECODE_SEQS
    assert ns % g == 0
    rows = mem_len * nh
    cspec = pl.BlockSpec((None, g, rows, hd), lambda b: (layer, b, 0, 0))
    qspec = pl.BlockSpec((g, nh, hd), lambda b: (b, 0, 0))
    return pl.pallas_call(
        _mem_decode_kernel,
        grid=(ns // g,),
        in_specs=[qspec, cspec, cspec],
        out_specs=qspec,
        out_shape=jax.ShapeDtypeStruct((ns, nh, hd), BF16),
        compiler_params=_cparams(("parallel",)),
        name="mem_attn_decode",
    )(q, cache_k.reshape(n_layers, ns, rows, hd), cache_v.reshape(n_layers, ns, rows, hd))


DIL_DECODE_SEQS = 2
SUBLANES = 8


def _dil_decode_kernel(q_ref, kn_ref, vn_ref, *refs):
    o_ref = refs[-1]
    n_slots = kn_ref.shape[1]
    for i in range(q_ref.shape[0]):
        kn, vn = kn_ref[i], vn_ref[i]
        scores, values = [], []
        for g in range(B_GROUPS):
            k_ref, v_ref = refs[2 * g], refs[2 * g + 1]
            rows = k_ref.shape[1] * (k_ref.shape[2] if len(k_ref.shape) == 4 else 1)
            period = k_ref.shape[2] if len(k_ref.shape) == 4 else n_slots
            k = k_ref[i].reshape(rows, HEAD_DIM).astype(BF16)
            v = v_ref[i].reshape(rows, HEAD_DIM).astype(BF16)
            q = q_ref[i, g]
            own = (lax.broadcasted_iota(jnp.int32, (n_slots, rows), 1) % period
                   == lax.broadcasted_iota(jnp.int32, (n_slots, rows), 0))
            scores.append(jnp.where(own, _dot_nt(q.astype(BF16), k) * ATTN_SCALE, NEG_BIG))
            values.append(v)
            scores.append(jnp.sum(q * kn, axis=-1, keepdims=True) * ATTN_SCALE)
            values.append(None)
        m = functools.reduce(jnp.maximum, [jnp.max(s, axis=-1, keepdims=True) for s in scores])
        l = jnp.zeros_like(m)
        acc = jnp.zeros((n_slots, HEAD_DIM), F32)
        for s, v in zip(scores, values):
            p = jnp.exp(s - m)
            l = l + jnp.sum(p, axis=-1, keepdims=True)
            acc = acc + (p * vn if v is None else _dot(p.astype(BF16), v))
        o_ref[i] = (acc / l).astype(o_ref.dtype)


def _dil_attn_decode(q, k_new, v_new, cache_k, cache_v):
    ns, w_buf, ns_slots, hd = cache_k.shape
    n_seq = DIL_DECODE_SEQS
    assert ns % n_seq == 0
    in_specs = [pl.BlockSpec((n_seq, B_GROUPS, ns_slots, hd), lambda b: (b, 0, 0, 0)),
                pl.BlockSpec((n_seq, ns_slots, hd), lambda b: (b, 0, 0)),
                pl.BlockSpec((n_seq, ns_slots, hd), lambda b: (b, 0, 0))]
    args = [q, k_new, v_new]
    for win, dil in B_PATTERNS:
        span = win // dil
        assert w_buf % dil == 0 and (w_buf // dil) % span == 0
        last = w_buf // dil // span - 1
        if dil == 1:
            view = (ns, w_buf * ns_slots, hd)
            spec = pl.BlockSpec((n_seq, span * ns_slots, hd), functools.partial(lambda b, last: (b, last, 0), last=last))
        else:
            assert dil * ns_slots >= SUBLANES and ns_slots <= SUBLANES
            view = (ns, w_buf // dil, dil * ns_slots, hd)
            spec = pl.BlockSpec((n_seq, span, SUBLANES, hd), functools.partial(lambda b, last: (b, last, 0, 0), last=last))
        in_specs += [spec, spec]
        args += [cache_k.reshape(view), cache_v.reshape(view)]
    return pl.pallas_call(
        _dil_decode_kernel,
        grid=(ns // n_seq,),
        in_specs=in_specs,
        out_specs=pl.BlockSpec((n_seq, ns_slots, hd), lambda b: (b, 0, 0)),
        out_shape=jax.ShapeDtypeStruct((ns, ns_slots, hd), BF16),
        compiler_params=_cparams(("parallel",)),
        name="dil_attn_decode",
    )(*args)


HGRN_STEP_SEQS = 1


def _hgrn_step_kernel(p_ref, s_ref, lb_ref, gn_ref, o_ref, so_ref, rows_ref):
    h = A_HEADS
    hk = h * HEAD_DIM
    gn = gn_ref[...]
    first = lax.broadcasted_iota(jnp.int32, (8, HEAD_DIM), 0) == 0
    for b in range(p_ref.shape[0]):
        rows_ref[b] = jnp.zeros(rows_ref.shape[1:], F32)
        for i in range(h):
            fp = p_ref[b, :, hk + i * HEAD_DIM:hk + (i + 1) * HEAD_DIM]
            lb = lb_ref[i:i + 1, :]
            rows_ref[b, i:i + 1, :] = lb + (1.0 - lb) * _sigmoid(fp)
        cols = rows_ref[b].T
        for i in range(h):
            sl = slice(i * HEAD_DIM, (i + 1) * HEAD_DIM)
            qp = p_ref[b, :, sl]
            v = p_ref[b, :, 2 * hk + i * HEAD_DIM:2 * hk + (i + 1) * HEAD_DIM]
            gp = p_ref[b, :, 3 * hk + i * HEAD_DIM:3 * hk + (i + 1) * HEAD_DIM]
            k = 1.0 - rows_ref[b, i:i + 1, :]
            k8 = jnp.where(first, k, 0.0).astype(BF16)
            v8 = jnp.broadcast_to(v, (8, HEAD_DIM)).astype(BF16)
            kv = lax.dot_general(k8, v8, (((0,), (0,)), ((), ())), preferred_element_type=F32)
            s_new = cols[:, i:i + 1] * s_ref[b, i] + kv
            so_ref[b, i] = s_new
            q8 = jnp.broadcast_to(qp * _sigmoid(qp), (8, HEAD_DIM)).astype(BF16)
            o = _dot(q8, s_new.astype(BF16))[0:1]
            o = o * lax.rsqrt(jnp.mean(o * o, axis=-1, keepdims=True) + RMS_EPS) * gn
            o_ref[b, :, sl] = (o * (gp * _sigmoid(gp))).astype(o_ref.dtype)


def _hgrn_step(proj, state, layer, lb, g_norm):
    ns = proj.shape[0]
    h = A_HEADS
    g = HGRN_STEP_SEQS
    assert ns % g == 0
    sspec = pl.BlockSpec((g, h, HEAD_DIM, HEAD_DIM), lambda b: (b, 0, 0, 0))
    return pl.pallas_call(
        _hgrn_step_kernel,
        grid=(ns // g,),
        in_specs=[pl.BlockSpec((g, 1, proj.shape[2]), lambda b: (b, 0, 0)),
                  pl.BlockSpec((None, g, h, HEAD_DIM, HEAD_DIM), lambda b: (layer, b, 0, 0, 0)),
                  pl.BlockSpec((h, HEAD_DIM), lambda b: (0, 0)),
                  pl.BlockSpec((1, HEAD_DIM), lambda b: (0, 0))],
        out_specs=[pl.BlockSpec((g, 1, h * HEAD_DIM), lambda b: (b, 0, 0)), sspec],
        out_shape=[jax.ShapeDtypeStruct((ns, 1, h * HEAD_DIM), BF16),
                   jax.ShapeDtypeStruct((ns, h, HEAD_DIM, HEAD_DIM), F32)],
        scratch_shapes=[pltpu.VMEM((g, HEAD_DIM, HEAD_DIM), F32)],
        compiler_params=_cparams(("parallel",)),
        name="hgrn_step",
    )(proj, state, lb.reshape(h, HEAD_DIM), g_norm.reshape(1, HEAD_DIM))


def _ffn_kernel(be_ref, nv_ref, od_ref, nu_ref, x_ref, wg_ref, wu_ref, wd_ref, o_ref, wgb_ref, wub_ref, wdb_ref, *, sub):
    i = pl.program_id(0)
    n_valid = nv_ref[i]

    @pl.when((n_valid > 0) & ((i == 0) | (be_ref[i] != be_ref[jnp.maximum(i - 1, 0)])))
    def _():
        wgb_ref[...] = wg_ref[...].astype(BF16)
        wub_ref[...] = wu_ref[...].astype(BF16)
        wdb_ref[...] = wd_ref[...].astype(BF16)

    for j in range(x_ref.shape[0] // sub):
        rows = slice(j * sub, (j + 1) * sub)

        @pl.when(n_valid > j * sub)
        def _(j=j, rows=rows):
            lo, hi = _unpack_halves(x_ref[rows, :])
            c = lo.shape[1]
            keep = lax.broadcasted_iota(jnp.int32, lo.shape, 0) < n_valid - j * sub
            lo = jnp.where(keep, lo, 0.0).astype(BF16)
            hi = jnp.where(keep, hi, 0.0).astype(BF16)
            ff = wgb_ref.shape[1]
            y = None
            for part in range(FFN_COL_SPLIT):
                cs = slice(part * ff // FFN_COL_SPLIT, (part + 1) * ff // FFN_COL_SPLIT)
                hg = _dot(lo, wgb_ref[:c, cs]) + _dot(hi, wgb_ref[c:, cs])
                hu = _dot(lo, wub_ref[:c, cs]) + _dot(hi, wub_ref[c:, cs])
                hid = (hg * _sigmoid(hg) * hu).astype(BF16)
                yp = _dot(hid, wdb_ref[cs, :])
                y = yp if y is None else y + yp
            o_ref[rows, :] = _pack_halves(y)

        @pl.when(n_valid <= j * sub)
        def _(rows=rows):
            o_ref[rows, :] = jnp.zeros((sub, o_ref.shape[1]), o_ref.dtype)


def _expert_ffn(x, blk_exp, blk_valid, blk_order, n_used, w_gate, w_up, w_down, layer, tm, sub, name):
    r = x.shape[0]
    d, ff = w_gate.shape[2:]
    assert tm % sub == 0
    w_spec = lambda shape: pl.BlockSpec((None, None) + shape, lambda i, be, nv, od, nu: (layer, be[i], 0, 0))
    return pl.pallas_call(
        functools.partial(_ffn_kernel, sub=sub),
        grid_spec=pltpu.PrefetchScalarGridSpec(
            num_scalar_prefetch=4,
            grid=(r // tm,),
            in_specs=[pl.BlockSpec((tm, d // 2), lambda i, be, nv, od, nu: (od[jnp.minimum(i, nu[0] - 1)], 0)),
                      w_spec((d, ff)), w_spec((d, ff)), w_spec((ff, d))],
            out_specs=pl.BlockSpec((tm, d // 2), lambda i, be, nv, od, nu: (od[i], 0)),
            scratch_shapes=[pltpu.VMEM((d, ff), BF16), pltpu.VMEM((d, ff), BF16), pltpu.VMEM((ff, d), BF16)],
        ),
        out_shape=jax.ShapeDtypeStruct((r, d // 2), jnp.int32),
        compiler_params=_cparams(("arbitrary",), V7X_VMEM_LIMIT_BYTES),
        name=name,
    )(blk_exp, blk_valid, blk_order, n_used, x, w_gate, w_up, w_down)


def _first_argmax(val, idx, sentinel):
    m = jnp.max(val, axis=0, keepdims=True)
    i = jnp.min(jnp.where(val == m, idx, sentinel), axis=0, keepdims=True)
    return m, i


def _route_kernel(x_ref, wt_ref, b_ref, e_ref, g_ref, c_ref):
    x = x_ref[...]
    w = wt_ref[...]
    t = x.shape[0]
    xh = x.astype(BF16)
    xl = (x - xh.astype(F32)).astype(BF16)
    wh = w.astype(BF16)
    wl = (w - wh.astype(F32)).astype(BF16)
    logits = _dot_nt(wh, xh) + (_dot_nt(wl, xh) + _dot_nt(wh, xl))
    scores = _sigmoid(logits)
    biased = scores + b_ref[...]
    gs = N_EXPERTS // N_GROUPS
    neg = -jnp.inf
    eid = lax.broadcasted_iota(jnp.int32, (N_EXPERTS, t), 0)
    sub = lax.broadcasted_iota(jnp.int32, (gs, t), 0)
    grow = lax.broadcasted_iota(jnp.int32, (N_GROUPS, t), 0)
    grp = jnp.zeros((N_GROUPS, t), F32)
    for g in range(N_GROUPS):
        bg = biased[g * gs:(g + 1) * gs]
        m1, i1 = _first_argmax(bg, sub, gs)
        m2 = jnp.max(jnp.where(sub == i1, neg, bg), axis=0, keepdims=True)
        grp = jnp.where(grow == g, m1 + m2, grp)
    chosen = jnp.zeros((N_GROUPS, t), F32)
    for _ in range(TOPK_GROUPS):
        _, gi = _first_argmax(grp, grow, N_GROUPS)
        hit = grow == gi
        chosen = jnp.where(hit, 1.0, chosen)
        grp = jnp.where(hit, neg, grp)
    chosen_e = jnp.concatenate([jnp.broadcast_to(chosen[g:g + 1], (gs, t)) for g in range(N_GROUPS)], axis=0)
    masked = jnp.where(chosen_e > 0.0, biased, neg)
    krow = lax.broadcasted_iota(jnp.int32, (TOP_K, t), 0)
    e_out = jnp.zeros((TOP_K, t), jnp.int32)
    g_out = jnp.zeros((TOP_K, t), F32)
    member = jnp.zeros((N_EXPERTS, t), F32)
    for k in range(TOP_K):
        _, idx = _first_argmax(masked, eid, N_EXPERTS)
        hit = eid == idx
        gk = jnp.sum(jnp.where(hit, scores, 0.0), axis=0, keepdims=True)
        masked = jnp.where(hit, neg, masked)
        member = jnp.where(hit, 1.0, member)
        e_out = jnp.where(krow == k, idx, e_out)
        g_out = jnp.where(krow == k, gk, g_out)
    g_out = g_out / jnp.sum(g_out, axis=0, keepdims=True) * ROUTED_SCALE
    e_ref[...] = e_out
    g_ref[...] = g_out
    c_ref[...] = jnp.sum(member, axis=1, keepdims=True).astype(jnp.int32)


def _route(x, router_w, layer, router_b, tm):
    n, d = x.shape
    e = router_w.shape[2]
    wt = jnp.swapaxes(router_w, 1, 2)
    return pl.pallas_call(
        _route_kernel,
        grid=(n // tm,),
        in_specs=[pl.BlockSpec((tm, d), lambda i: (i, 0)),
                  pl.BlockSpec((None, e, d), lambda i: (layer, 0, 0)),
                  pl.BlockSpec((e, 1), lambda i: (0, 0))],
        out_specs=[pl.BlockSpec((TOP_K, tm), lambda i: (0, i)),
                   pl.BlockSpec((TOP_K, tm), lambda i: (0, i)),
                   pl.BlockSpec((None, e, 1), lambda i: (i, 0, 0))],
        out_shape=[jax.ShapeDtypeStruct((TOP_K, n), jnp.int32), jax.ShapeDtypeStruct((TOP_K, n), F32),
                   jax.ShapeDtypeStruct((n // tm, e, 1), jnp.int32)],
        compiler_params=_cparams(("parallel",), V7X_VMEM_LIMIT_BYTES),
        name="route",
    )(x, wt, router_b[layer].astype(F32).reshape(e, 1))


def _slots_kernel(e_ref, base_ref, o_ref):
    e = e_ref[...]
    t = e.shape[1]
    eid = lax.broadcasted_iota(jnp.int32, (N_EXPERTS, t), 0)
    member = jnp.zeros((N_EXPERTS, t), F32)
    for k in range(TOP_K):
        member = jnp.where(eid == e[k:k + 1], 1.0, member)
    before = (lax.broadcasted_iota(jnp.int32, (t, t), 0) < lax.broadcasted_iota(jnp.int32, (t, t), 1))
    rank = _dot(member.astype(BF16), jnp.where(before, 1.0, 0.0).astype(BF16))
    slot = rank + base_ref[...].astype(F32)
    krow = lax.broadcasted_iota(jnp.int32, (TOP_K, t), 0)
    out = jnp.zeros((TOP_K, t), F32)
    for k in range(TOP_K):
        sk = jnp.sum(jnp.where(eid == e[k:k + 1], slot, 0.0), axis=0, keepdims=True)
        out = jnp.where(krow == k, sk, out)
    o_ref[...] = out.astype(jnp.int32)


def _dispatch_plan(e_idx, counts, tm, tile):
    k, n = e_idx.shape
    n_blocks = -(-(n * k) // tm) + N_EXPERTS
    counts = counts.reshape(n // tile, N_EXPERTS)
    total = jnp.sum(counts, axis=0)
    padded = (total + tm - 1) // tm * tm
    pad_end = jnp.cumsum(padded)
    tile_base = (pad_end - padded)[None, :] + jnp.cumsum(counts, axis=0) - counts
    slot_of = pl.pallas_call(
        _slots_kernel,
        grid=(n // tile,),
        in_specs=[pl.BlockSpec((k, tile), lambda i: (0, i)),
                  pl.BlockSpec((None, N_EXPERTS, 1), lambda i: (i, 0, 0))],
        out_specs=pl.BlockSpec((k, tile), lambda i: (0, i)),
        out_shape=jax.ShapeDtypeStruct((k, n), jnp.int32),
        compiler_params=_cparams(("parallel",)),
        name="slots",
    )(e_idx, tile_base.astype(jnp.int32).reshape(n // tile, N_EXPERTS, 1))
    pos = jnp.arange(n_blocks, dtype=jnp.int32)
    blk_exp = jnp.minimum(jnp.sum((pad_end[None, :] <= (pos * tm)[:, None]).astype(jnp.int32), axis=1), N_EXPERTS - 1)
    is_exp = blk_exp[:, None] == jnp.arange(N_EXPERTS, dtype=blk_exp.dtype)[None, :]
    per_blk = lambda v: jnp.sum(jnp.where(is_exp, v[None, :], 0), axis=1)
    n_used = jnp.maximum(pad_end[-1:] // tm, 1).astype(jnp.int32)
    first, count = per_blk((pad_end - padded) // tm), jnp.maximum(per_blk(padded // tm), 1)
    blk_order = jnp.where(pos < n_used[0], first + (pos - first + count - 1) % count, pos)
    blk_valid = jnp.clip(per_blk(pad_end - padded + total) - blk_order * tm, 0, tm)
    return (slot_of, blk_exp.astype(jnp.int32), blk_valid.astype(jnp.int32), blk_order.astype(jnp.int32), n_used,
            n_blocks)


V7X_SC_CORES = 2
V7X_SC_SUBCORES = 16
SC_WORKERS = V7X_SC_CORES * V7X_SC_SUBCORES
SC_CHUNK = 32
SC_GATHER_CHUNK = 40


def _sc_mesh():
    return plsc.VectorSubcoreMesh(core_axis_name="c", subcore_axis_name="s",
                                  num_cores=V7X_SC_CORES, num_subcores=V7X_SC_SUBCORES)


def _sc_worker_id():
    return lax.axis_index("s") * V7X_SC_CORES + lax.axis_index("c")


def _sc_gather_rows(table, slot_of):
    k, n = slot_of.shape
    w = table.shape[1]
    n_rows = k * n
    ch = SC_GATHER_CHUNK
    per_worker = n_rows // (SC_WORKERS * ch)
    assert per_worker * SC_WORKERS * ch == n_rows and per_worker % 2 == 0
    idx = slot_of.reshape(SC_WORKERS, per_worker, ch)

    def body(table_hbm, idx_hbm, out_hbm, idx_v, rows_v, sems):
        wid = _sc_worker_id()
        pltpu.sync_copy(idx_hbm.at[wid], idx_v)

        def gather(c, slot):
            return pltpu.make_async_copy(table_hbm.at[idx_v.at[c]], rows_v.at[slot], sems.at[slot])

        gather(0, 0).start()

        @pl.loop(0, per_worker, step=2)
        def _(c0):
            for slot in range(2):
                c = c0 + slot
                gather(c, slot).wait()

                @pl.when(c + 1 < per_worker)
                def _():
                    gather(c + 1, 1 - slot).start()

                row0 = pl.multiple_of((wid * per_worker + c) * ch, 8)
                pltpu.sync_copy(rows_v.at[slot], out_hbm.at[pl.ds(row0, ch)])

    return pl.kernel(
        body, out_type=jax.ShapeDtypeStruct((n_rows, w), table.dtype), mesh=_sc_mesh(),
        scratch_types=[pltpu.VMEM((per_worker, ch), jnp.int32), pltpu.VMEM((2, ch, w), table.dtype),
                       pltpu.SemaphoreType.DMA((2,))],
        name="sc_gather_rows",
    )(table, idx)


def _sc_scatter_rows(x, slot_of, n_slots):
    k, n = slot_of.shape
    w = x.shape[1]
    n_chunks = n // SC_CHUNK
    assert n_chunks * SC_CHUNK == n
    idx = slot_of.reshape(k, n_chunks, SC_CHUNK).transpose(1, 0, 2)
    rounds = -(-n_chunks // SC_WORKERS)

    def body(x_hbm, idx_hbm, out_hbm, idx_v, rows_v, sem):
        wid = _sc_worker_id()

        @pl.loop(0, rounds)
        def _(j):
            q = j * SC_WORKERS + wid

            @pl.when(q < n_chunks)
            def _():
                pltpu.sync_copy(idx_hbm.at[q], idx_v)
                pltpu.sync_copy(x_hbm.at[pl.ds(pl.multiple_of(q * SC_CHUNK, SC_CHUNK), SC_CHUNK)], rows_v)
                copies = [pltpu.make_async_copy(rows_v, out_hbm.at[idx_v.at[kk]], sem) for kk in range(k)]
                for cp in copies:
                    cp.start()
                for cp in copies:
                    cp.wait()

    return pl.kernel(
        body, out_type=jax.ShapeDtypeStruct((n_slots, w), x.dtype), mesh=_sc_mesh(),
        scratch_types=[pltpu.VMEM((k, SC_CHUNK), jnp.int32), pltpu.VMEM((SC_CHUNK, w), x.dtype),
                       pltpu.SemaphoreType.DMA],
        name="sc_scatter_rows",
    )(x, idx)


MOE_BLOCK = 768
MOE_SUB = 384
FFN_COL_SPLIT = 2
ROW_TILE = 640


def _moe_postnorm_kernel(x_ref, y_ref, gate_ref, sh_ref, g_ref, b_ref, ofp_ref, ofs_ref, ob_ref, *, alpha, tiles_p):
    def part(which):
        of_ref = (ofp_ref, ofs_ref)[which]
        gate = gate_ref[...]
        lo, hi = _unpack_halves(sh_ref[...])
        for k in range(TOP_K):
            lo_k, hi_k = _unpack_halves(y_ref[k])
            lo = lo + gate[:, k:k + 1] * lo_k
            hi = hi + gate[:, k:k + 1] * hi_k
        z = alpha * x_ref[...] + jnp.concatenate([lo, hi], axis=1)
        out = _layer_norm(z, g_ref[...], b_ref[...])
        of_ref[...] = out
        ob_ref[...] = out.astype(BF16)

    _on_part(tiles_p, part)


def _moe_postnorm(x, n_p, y_tok, gate, shared, g, b, alpha):
    m, d = x.shape
    tm = PART_TILE
    row = pl.BlockSpec((tm, d), lambda i: (i, 0))
    vec = pl.BlockSpec((1, d), lambda i: (0, 0))
    part_p, part_s = _part_specs(n_p, (tm, d))
    return pl.pallas_call(
        functools.partial(_moe_postnorm_kernel, alpha=alpha, tiles_p=n_p // tm),
        grid=(m // tm,),
        in_specs=[row, pl.BlockSpec((TOP_K, tm, d // 2), lambda i: (0, i, 0)),
                  pl.BlockSpec((tm, TOP_K), lambda i: (i, 0)), pl.BlockSpec((tm, d // 2), lambda i: (i, 0)), vec, vec],
        out_specs=[part_p, part_s, row],
        out_shape=[jax.ShapeDtypeStruct((n_p, d), F32), jax.ShapeDtypeStruct((m - n_p, d), F32),
                   jax.ShapeDtypeStruct((m, d), BF16)],
        compiler_params=_cparams(("arbitrary",), V7X_VMEM_LIMIT_BYTES),
        name="postnorm_moe",
    )(x, y_tok, gate, shared, g.reshape(1, d), b.reshape(1, d))


def _moe(xf, n_p, xp, layer, router_w, router_b, w_gate, w_up, w_down, sw_gate, sw_up, sw_down, ln_g, ln_b, alpha):
    n, d = xf.shape
    e_idx, gate, counts = _route(xf, router_w, layer, router_b, ROW_TILE)
    slot_of, blk_exp, blk_valid, blk_order, n_used, n_blocks = _dispatch_plan(e_idx, counts, MOE_BLOCK, ROW_TILE)
    x_sorted = _sc_scatter_rows(xp, slot_of, n_blocks * MOE_BLOCK)
    y_sorted = _expert_ffn(x_sorted, blk_exp, blk_valid, blk_order, n_used, w_gate, w_up, w_down, layer,
                           MOE_BLOCK, MOE_SUB, "routed_ffn")
    y_tok = _sc_gather_rows(y_sorted, slot_of).reshape(TOP_K, n, d // 2)
    n_sh = n // ROW_TILE
    shared = _expert_ffn(xp, jnp.zeros((n_sh,), jnp.int32), jnp.full((n_sh,), ROW_TILE, jnp.int32),
                         jnp.arange(n_sh, dtype=jnp.int32), jnp.full((1,), n_sh, jnp.int32),
                         sw_gate[:, None], sw_up[:, None], sw_down[:, None], layer, ROW_TILE, ROW_TILE, "shared_ffn")
    return _moe_postnorm(xf, n_p, y_tok, gate.T, shared, ln_g, ln_b, alpha)


def kernel(x_prompt, x_sample, state_hgrn, cache_win_k, cache_win_v, cache_mem_k, cache_mem_v, mem_prompt,
           w_in_a, lb_logits, g_norm_a, w_out_a, w_in_b, w_out_b, w_kv_shared, w_mem_kv, ln_g, ln_b,
           router_w, router_b, exp_w_gate, exp_w_up, exp_w_down, sh_w_gate, sh_w_up, sh_w_down):
    bp, sp, d = x_prompt.shape
    ns = x_sample.shape[0]
    assert x_sample.shape[1] == 1
    depth = ln_g.shape[0]
    n_a = w_in_a.shape[0]
    alpha = (2 * depth) ** 0.25
    n_p = bp * sp
    mem_len = mem_prompt.shape[1]
    a_mix = 4 * A_HEADS * HEAD_DIM

    xf_p, xf_s = x_prompt.reshape(n_p, d), x_sample.reshape(ns, d)
    xb = jnp.concatenate([xf_p.astype(BF16), xf_s.astype(BF16)], axis=0)
    lower_bounds = jnp.cumsum(jax.nn.softmax(lb_logits.astype(F32), axis=0), axis=0)
    mem_flat = mem_prompt.reshape(bp * mem_len, d)
    pos_all = jnp.concatenate([jnp.tile(jnp.arange(sp, dtype=jnp.int32), bp),
                               jnp.full((ns,), PAST_LEN, jnp.int32)])
    tables = _rope_tables(pos_all)

    hgrn_p, hgrn_s, mem_k_p, mem_v_p = [], [], [], []
    for layer in range(depth):
        kvm = _proj([mem_flat], w_mem_kv, layer, F32, 2 * mem_len, 1024, "mem_kv")
        mem_k_p.append(kvm[:, :MEM_W].reshape(bp, mem_len, MEM_HEADS, HEAD_DIM))
        mem_v_p.append(kvm[:, MEM_W:].reshape(bp, mem_len, MEM_HEADS, HEAD_DIM))
        if layer < n_a:
            a = layer
            proj = _proj([xb], w_in_a, a, F32, ROW_TILE, 1664, "proj_in_a")
            proj_s = proj[n_p:]
            o_x_s, st_s = _hgrn_step(proj_s.reshape(ns, 1, -1), state_hgrn, a, lower_bounds[a], g_norm_a[a])
            o_m_s = _mem_attn_decode(proj_s[:, a_mix:].reshape(ns, MEM_HEADS, HEAD_DIM), cache_mem_k, cache_mem_v, layer)
            o_x, st_p = _hgrn_prompt(proj, lower_bounds[a], g_norm_a[a], o_x_s.reshape(ns, -1), bp, sp)
            o_m = _mem_attn_prompt(proj, a_mix // MEM_W, kvm, o_m_s.reshape(ns, -1), bp, sp, 512)
            hgrn_p.append(st_p)
            hgrn_s.append(st_s)
            w_out, w_out_layer = w_out_a, a
        else:
            bl = layer - n_a
            if layer == n_a:
                kv = _proj([xb], w_kv_shared[None], 0, F32, ROW_TILE, 1024, "proj_kv")
                k_r, k_p, k_s, v_p, v_s = _shared_kv(kv, tables, n_p)
            proj = _proj([xb], w_in_b, bl, F32, ROW_TILE, 1024, "proj_in_b")
            q_r = _rope(proj, B_QHEADS, tables, F32, ROW_TILE, "rope_q")
            q_s = q_r[n_p:].reshape(ns, B_GROUPS, B_SLOTS, HEAD_DIM)
            o_x_s = _dil_attn_decode(q_s, k_s, v_s, cache_win_k, cache_win_v)
            o_m_s = _mem_attn_decode(proj[n_p:, B_QHEADS * HEAD_DIM:].reshape(ns, MEM_HEADS, HEAD_DIM),
                                     cache_mem_k, cache_mem_v, layer)
            o_x = _dil_attn_prompt(q_r, k_r, kv, o_x_s.reshape(ns, -1), bp, sp)
            o_m = _mem_attn_prompt(proj, B_QHEADS * HEAD_DIM // MEM_W, kvm, o_m_s.reshape(ns, -1), bp, sp, 512)
            w_out, w_out_layer = w_out_b, bl
        y = _proj([o_x, o_m], w_out, w_out_layer, F32, ROW_TILE, 1024, "proj_out")
        xf, xp = _postnorm(xf_p, xf_s, y, ln_g[layer, 0], ln_b[layer, 0], alpha, "postnorm_mix")
        xf_p, xf_s, xb = _moe(xf, n_p, xp, layer, router_w, router_b, exp_w_gate, exp_w_up, exp_w_down,
                              sh_w_gate, sh_w_up, sh_w_down, ln_g[layer, 1], ln_b[layer, 1], alpha)

    w_p = min(max(w for w, _ in B_PATTERNS), sp)
    k_p = k_p.reshape(bp, sp, B_SLOTS, HEAD_DIM)
    v_p = v_p.reshape(bp, sp, B_SLOTS, HEAD_DIM)
    return (xf_p.reshape(bp, sp, d), xf_s.reshape(ns, 1, d),
            jnp.stack(hgrn_p), jnp.stack(hgrn_s),
            k_p[:, sp - w_p:], v_p[:, sp - w_p:],
            k_s.reshape(ns, 1, B_SLOTS, HEAD_DIM), v_s.reshape(ns, 1, B_SLOTS, HEAD_DIM),
            jnp.stack(mem_k_p), jnp.stack(mem_v_p))
```

```python
import functools

import jax
import jax.numpy as jnp
from jax import lax
from jax.experimental import pallas as pl
from jax.experimental.pallas import tpu as pltpu
from jax.experimental.pallas import tpu_sc as plsc

F32 = jnp.float32
BF16 = jnp.bfloat16

HEAD_DIM = 128
A_HEADS = 12
A_CHUNK = 64
A_SUB = 16
B_PATTERNS = ((128, 1), (512, 4), (2048, 16))
B_SLOTS = 4
B_GROUPS = len(B_PATTERNS)
B_QHEADS = B_GROUPS * B_SLOTS
B_BLOCK = 128
MEM_HEADS = 4
MEM_W = MEM_HEADS * HEAD_DIM
KV_W = B_SLOTS * HEAD_DIM
ROPE_THETA = 500000.0
ROPE_DIM = HEAD_DIM // 4
N_EXPERTS = 64
N_GROUPS = 8
TOPK_GROUPS = 4
TOP_K = 8
ROUTED_SCALE = 2.5
LN_EPS = 1e-5
RMS_EPS = 1e-6
ATTN_SCALE = HEAD_DIM ** -0.5
PAST_LEN = 2048

V7X_VMEM_LIMIT_BYTES = 56 * 1024 * 1024
LANES = 128
NEG_BIG = -1e30


def _cparams(sem, vmem=None):
    return pltpu.CompilerParams(dimension_semantics=sem, vmem_limit_bytes=vmem)


def _dot(a, b):
    return jnp.dot(a, b, preferred_element_type=F32)


def _dot_nt(a, b):
    return lax.dot_general(a, b, (((1,), (1,)), ((), ())), preferred_element_type=F32)


def _sigmoid(x):
    return 1.0 / (1.0 + jnp.exp(-x))


def _proj_kernel(*refs, n_lhs):
    x_refs = refs[:n_lhs]
    w_refs = refs[n_lhs:2 * n_lhs]
    o_ref = refs[2 * n_lhs]
    wb_refs = refs[2 * n_lhs + 1:]

    @pl.when(pl.program_id(1) == 0)
    def _():
        for w_ref, wb_ref in zip(w_refs, wb_refs):
            wb_ref[...] = w_ref[...].astype(BF16)

    acc = None
    for x_ref, wb_ref in zip(x_refs, wb_refs):
        d = _dot(x_ref[...].astype(BF16), wb_ref[...])
        acc = d if acc is None else acc + d
    o_ref[...] = acc.astype(o_ref.dtype)


def _proj(lhs, w, layer, out_dtype, tm, tn, name):
    m = lhs[0].shape[0]
    n = w.shape[2]
    koff = 0
    in_specs, w_specs, scratch = [], [], []
    for x in lhs:
        k = x.shape[1]
        assert koff % k == 0 and m % tm == 0 and n % tn == 0
        in_specs.append(pl.BlockSpec((tm, k), lambda j, i: (i, 0)))
        w_specs.append(pl.BlockSpec((None, k, tn), functools.partial(lambda j, i, kb: (layer, kb, j), kb=koff // k)))
        scratch.append(pltpu.VMEM((k, tn), BF16))
        koff += k
    assert koff == w.shape[1]
    return pl.pallas_call(
        functools.partial(_proj_kernel, n_lhs=len(lhs)),
        grid=(n // tn, m // tm),
        in_specs=in_specs + w_specs,
        out_specs=pl.BlockSpec((tm, tn), lambda j, i: (i, j)),
        out_shape=jax.ShapeDtypeStruct((m, n), out_dtype),
        scratch_shapes=scratch,
        compiler_params=_cparams(("arbitrary", "arbitrary"), V7X_VMEM_LIMIT_BYTES),
        name=name,
    )(*lhs, *([w] * len(lhs)))


def _pack_halves(x):
    c = x.shape[1] // 2
    lo = pltpu.bitcast(x[:, :c].astype(BF16).astype(F32), jnp.int32)
    hi = pltpu.bitcast(x[:, c:].astype(BF16).astype(F32), jnp.int32)
    return hi | lax.shift_right_logical(lo, 16)


def _unpack_halves(w):
    lo = pltpu.bitcast(lax.shift_left(w, 16), F32)
    hi = pltpu.bitcast(w & jnp.int32(-65536), F32)
    return lo, hi


def _layer_norm(z, g, b):
    mu = jnp.mean(z, axis=-1, keepdims=True)
    zc = z - mu
    var = jnp.mean(zc * zc, axis=-1, keepdims=True)
    return zc * lax.rsqrt(var + LN_EPS) * g + b


PART_TILE = 128


def _part_specs(n_p, block):
    tiles_p = n_p // PART_TILE
    rest = (0,) * (len(block) - 1)
    return (pl.BlockSpec(block, lambda i: (jnp.minimum(i, tiles_p - 1),) + rest),
            pl.BlockSpec(block, lambda i: (jnp.maximum(i - tiles_p, 0),) + rest))


def _on_part(tiles_p, fn):
    i = pl.program_id(0)
    pl.when(i < tiles_p)(functools.partial(fn, 0))
    pl.when(i >= tiles_p)(functools.partial(fn, 1))


def _postnorm_kernel(xp_ref, xs_ref, y_ref, g_ref, b_ref, wt_ref, rb_ref, of_ref, op_ref, e_ref, gate_ref, cnt_ref,
                     *, alpha, tiles_p):
    def part(which):
        x_ref = (xp_ref, xs_ref)[which]
        out = _layer_norm(alpha * x_ref[...] + y_ref[...], g_ref[...], b_ref[...])
        of_ref[...] = out
        op_ref[...] = _pack_halves(out)
        e_ref[...], gate_ref[...], cnt_ref[...] = _route_tile(out, wt_ref[...], rb_ref[...])

    _on_part(tiles_p, part)


def _postnorm_route(x_p, x_s, y, g, b, alpha, router_w, layer, router_b, name):
    n_p, d = x_p.shape
    m = n_p + x_s.shape[0]
    e = router_w.shape[2]
    t = PART_TILE
    row = pl.BlockSpec((t, d), lambda i: (i, 0))
    vec = pl.BlockSpec((1, d), lambda i: (0, 0))
    tok = pl.BlockSpec((TOP_K, t), lambda i: (0, i))
    part_p, part_s = _part_specs(n_p, (t, d))
    return pl.pallas_call(
        functools.partial(_postnorm_kernel, alpha=alpha, tiles_p=n_p // t),
        grid=(m // t,),
        in_specs=[part_p, part_s, row, vec, vec,
                  pl.BlockSpec((None, e, d), lambda i: (layer, 0, 0)), pl.BlockSpec((e, 1), lambda i: (0, 0))],
        out_specs=[row, pl.BlockSpec((t, d // 2), lambda i: (i, 0)), tok, tok,
                   pl.BlockSpec((None, e, 1), lambda i: (i, 0, 0))],
        out_shape=[jax.ShapeDtypeStruct((m, d), F32), jax.ShapeDtypeStruct((m, d // 2), jnp.int32),
                   jax.ShapeDtypeStruct((TOP_K, m), jnp.int32), jax.ShapeDtypeStruct((TOP_K, m), F32),
                   jax.ShapeDtypeStruct((m // t, e, 1), jnp.int32)],
        compiler_params=_cparams(("arbitrary",)),
        name=name,
    )(x_p, x_s, y, g.reshape(1, d), b.reshape(1, d), jnp.swapaxes(router_w, 1, 2),
      router_b[layer].astype(F32).reshape(e, 1))


def _hgrn_prompt_kernel(q_ref, f_ref, v_ref, gate_ref, lb_ref, gn_ref, os_ref, o_ref, s_ref, st_ref,
                        *, seq, heads, n_batch):
    _prompt_then_sample(n_batch, os_ref, o_ref,
                        functools.partial(_hgrn_sequence, q_ref, f_ref, v_ref, gate_ref, lb_ref, gn_ref,
                                          o_ref, s_ref, st_ref, seq=seq, heads=heads), axis=1)


def _hgrn_sequence(q_ref, f_ref, v_ref, gate_ref, lb_ref, gn_ref, o_ref, s_ref, st_ref, *, seq, heads):
    c = A_CHUNK
    n_chunks = seq // c
    n_sub = c // A_SUB
    gn = gn_ref[...]
    row = lax.broadcasted_iota(jnp.int32, (c, c), 0)
    col = lax.broadcasted_iota(jnp.int32, (c, c), 1)
    tril = jnp.where(row >= col, 1.0, 0.0).astype(BF16)
    same_sub = (row // A_SUB) == (col // A_SUB)
    diag_dist = jnp.where(same_sub, row - col, -1)
    off_mask = col < (row // A_SUB) * A_SUB
    st_ref[...] = jnp.zeros_like(st_ref)

    def chunk(ci, carry):
        for hh in range(heads):
            head_chunk(ci, hh)
        return carry

    def head_chunk(ci, hh):
        r0 = pl.multiple_of(ci * c, c)
        hsl = slice(hh * HEAD_DIM, (hh + 1) * HEAD_DIM)
        lb = lb_ref[hh]
        qp = q_ref[pl.ds(r0, c), hsl]
        fp = f_ref[pl.ds(r0, c), hsl]
        v = v_ref[pl.ds(r0, c), hsl]
        gp = gate_ref[pl.ds(r0, c), hsl]
        st = st_ref[hh]
        q = qp * _sigmoid(qp)
        forget = lb + (1.0 - lb) * _sigmoid(fp)
        logf = jnp.log2(forget)
        k = 1.0 - forget
        hi = logf.astype(BF16)
        r1 = logf - hi.astype(F32)
        mid = r1.astype(BF16)
        lo = (r1 - mid.astype(F32)).astype(BF16)
        g = _dot(tril, hi) + _dot(tril, mid) + _dot(tril, lo)
        v_b = v.astype(BF16)
        o = _dot_nt((q * jnp.exp2(g)).astype(BF16), st.astype(BF16))
        rows = [jnp.zeros((A_SUB, c), F32)]
        for i in range(1, n_sub):
            gref = g[i * A_SUB:i * A_SUB + 1, :]
            qt = q[i * A_SUB:(i + 1) * A_SUB, :] * jnp.exp2(g[i * A_SUB:(i + 1) * A_SUB, :] - gref)
            kt = k * jnp.exp2(jnp.minimum(gref - g, 0.0))
            rows.append(_dot_nt(qt.astype(BF16), kt.astype(BF16)))
        a = jnp.where(off_mask, jnp.concatenate(rows, axis=0), 0.0)
        for d in range(A_SUB):
            kr = k if d == 0 else pltpu.roll(k, d, axis=0)
            gr = g if d == 0 else pltpu.roll(g, d, axis=0)
            x = q * kr * jnp.exp2(g - gr)
            a = jnp.where(diag_dist == d, jnp.sum(x, axis=-1, keepdims=True), a)
        o = o + _dot(a.astype(BF16), v_b)
        gend = g[c - 1:c, :]
        kt_end = k * jnp.exp2(gend - g)
        st_ref[hh] = jnp.exp2(gend) * st + _dot(v_b.T, kt_end.astype(BF16))
        o = o * lax.rsqrt(jnp.mean(o * o, axis=-1, keepdims=True) + RMS_EPS) * gn
        o_ref[pl.ds(r0, c), hsl] = (o * (gp * _sigmoid(gp))).astype(o_ref.dtype)

    lax.fori_loop(0, n_chunks, chunk, 0, unroll=A_CHUNK_UNROLL)
    for hh in range(heads):
        s_ref[hh] = st_ref[hh].T


A_HEADS_PER_STEP = 4
A_CHUNK_UNROLL = 4


def _hgrn_prompt(proj, lb, g_norm, o_sample, n_batch, seq):
    h = A_HEADS
    hp = A_HEADS_PER_STEP
    ns = o_sample.shape[0]
    assert h % hp == 0 and ns <= seq and proj.shape[0] == n_batch * seq + ns
    ng = h // hp
    w = hp * HEAD_DIM
    last = n_batch - 1
    blk = lambda off: pl.BlockSpec((seq, w), functools.partial(lambda hg, b, off: (jnp.minimum(b, last), off + hg), off=off))
    return pl.pallas_call(
        functools.partial(_hgrn_prompt_kernel, seq=seq, heads=hp, n_batch=n_batch),
        grid=(ng, n_batch + 1),
        in_specs=[blk(0), blk(ng), blk(2 * ng), blk(3 * ng),
                  pl.BlockSpec((hp, 1, HEAD_DIM), lambda hg, b: (hg, 0, 0)),
                  pl.BlockSpec((1, HEAD_DIM), lambda hg, b: (0, 0)),
                  pl.BlockSpec((ns, w), lambda hg, b: (0, hg))],
        out_specs=[pl.BlockSpec((seq, w), lambda hg, b: (b, hg)),
                   pl.BlockSpec((None, hp, HEAD_DIM, HEAD_DIM), lambda hg, b: (jnp.minimum(b, last), hg, 0, 0))],
        out_shape=[jax.ShapeDtypeStruct((proj.shape[0], h * HEAD_DIM), BF16),
                   jax.ShapeDtypeStruct((n_batch, h, HEAD_DIM, HEAD_DIM), F32)],
        scratch_shapes=[pltpu.VMEM((hp, HEAD_DIM, HEAD_DIM), F32)],
        compiler_params=_cparams(("arbitrary", "arbitrary"), V7X_VMEM_LIMIT_BYTES),
        name="hgrn_prompt",
    )(proj, proj, proj, proj, lb.reshape(h, 1, HEAD_DIM), g_norm.reshape(1, HEAD_DIM), o_sample)


def _softmax_av(s, v):
    m = jnp.max(s, axis=-1, keepdims=True)
    p = jnp.exp(s - m)
    l = jnp.sum(p, axis=-1, keepdims=True)
    return _dot(p.astype(BF16), v) / l


def _prompt_then_sample(n_prompt_steps, os_ref, o_ref, prompt_step, axis=0):
    i = pl.program_id(axis)
    pl.when(i < n_prompt_steps)(prompt_step)

    @pl.when(i >= n_prompt_steps)
    def _():
        o_ref[:os_ref.shape[0], :] = os_ref[...]


def _mem_attn_kernel(q_ref, k_ref, v_ref, os_ref, o_ref, *, n_steps):
    def step():
        for h in range(MEM_HEADS):
            sl = slice(h * HEAD_DIM, (h + 1) * HEAD_DIM)
            s = _dot_nt(q_ref[:, sl].astype(BF16), k_ref[:, sl].astype(BF16)) * ATTN_SCALE
            o_ref[:, sl] = _softmax_av(s, v_ref[:, sl].astype(BF16)).astype(o_ref.dtype)

    _prompt_then_sample(n_steps, os_ref, o_ref, step)


def _mem_attn_prompt(proj, q_col, kv, o_sample, n_batch, seq, tq):
    mem_len = kv.shape[0] // n_batch
    nq = seq // tq
    n_steps = n_batch * nq
    ns = o_sample.shape[0]
    assert ns <= tq and proj.shape[0] == n_batch * seq + ns
    last = n_steps - 1
    return pl.pallas_call(
        functools.partial(_mem_attn_kernel, n_steps=n_steps),
        grid=(n_steps + 1,),
        in_specs=[pl.BlockSpec((tq, MEM_W), lambda i: (jnp.minimum(i, last), q_col)),
                  pl.BlockSpec((mem_len, MEM_W), lambda i: (jnp.minimum(i, last) // nq, 0)),
                  pl.BlockSpec((mem_len, MEM_W), lambda i: (jnp.minimum(i, last) // nq, 1)),
                  pl.BlockSpec((ns, MEM_W), lambda i: (0, 0))],
        out_specs=pl.BlockSpec((tq, MEM_W), lambda i: (i, 0)),
        out_shape=jax.ShapeDtypeStruct((proj.shape[0], MEM_W), BF16),
        compiler_params=_cparams(("arbitrary",)),
        name="mem_attn_prompt",
    )(proj, kv, kv, o_sample)


def _rope_tables(pos):
    half = ROPE_DIM // 2
    inv_freq = ROPE_THETA ** (-jnp.arange(0, ROPE_DIM, 2, dtype=F32) / ROPE_DIM)
    ang = pos.astype(F32)[:, None] * inv_freq[None, :]
    cos, sin = jnp.cos(ang), jnp.sin(ang)
    n = pos.shape[0]
    one = jnp.ones((n, HEAD_DIM - ROPE_DIM), F32)
    zero = jnp.zeros((n, HEAD_DIM - half), F32)
    c = jnp.concatenate([cos, cos, one], axis=1)
    s_up = jnp.concatenate([-sin, zero], axis=1)
    s_dn = jnp.concatenate([jnp.zeros((n, half), F32), sin, zero[:, half:]], axis=1)
    return c, s_up, s_dn


def _rope_kernel(x_ref, c_ref, su_ref, sd_ref, o_ref, *, n_heads):
    c, su, sd = c_ref[...], su_ref[...], sd_ref[...]
    for h in range(n_heads):
        sl = slice(h * HEAD_DIM, (h + 1) * HEAD_DIM)
        o_ref[:, sl] = _rope_head(x_ref[:, sl], c, su, sd).astype(o_ref.dtype)


def _rope_head(x, c, su, sd):
    half = ROPE_DIM // 2
    up = pltpu.roll(x, HEAD_DIM - half, axis=1)
    dn = pltpu.roll(x, half, axis=1)
    return x * c + up * su + dn * sd


def _kv_kernel(kv_ref, c_ref, su_ref, sd_ref, k_ref, kp_ref, ks_ref, vp_ref, vs_ref, *, tiles_p):
    c, su, sd = c_ref[...], su_ref[...], sd_ref[...]

    def part(which):
        k4_ref, v4_ref = ((kp_ref, vp_ref), (ks_ref, vs_ref))[which]
        for h in range(B_SLOTS):
            sl = slice(h * HEAD_DIM, (h + 1) * HEAD_DIM)
            k = _rope_head(kv_ref[:, sl], c, su, sd)
            k_ref[:, sl] = k
            k4_ref[:, h, :] = k
            v4_ref[:, h, :] = kv_ref[:, KV_W + h * HEAD_DIM:KV_W + (h + 1) * HEAD_DIM]

    _on_part(tiles_p, part)


def _shared_kv(kv, tables, n_p):
    m = kv.shape[0]
    tm = PART_TILE
    tab = pl.BlockSpec((tm, HEAD_DIM), lambda i: (i, 0))
    part_p, part_s = _part_specs(n_p, (tm, B_SLOTS, HEAD_DIM))
    cache = lambda rows: jax.ShapeDtypeStruct((rows, B_SLOTS, HEAD_DIM), F32)
    return pl.pallas_call(
        functools.partial(_kv_kernel, tiles_p=n_p // tm),
        grid=(m // tm,),
        in_specs=[pl.BlockSpec((tm, 2 * KV_W), lambda i: (i, 0)), tab, tab, tab],
        out_specs=[pl.BlockSpec((tm, KV_W), lambda i: (i, 0)), part_p, part_s, part_p, part_s],
        out_shape=[jax.ShapeDtypeStruct((m, KV_W), F32), cache(n_p), cache(m - n_p), cache(n_p), cache(m - n_p)],
        compiler_params=_cparams(("arbitrary",)),
        name="shared_kv",
    )(kv, *tables)


def _rope(x, n_heads, tables, out_dtype, tm, name):
    m = x.shape[0]
    w = n_heads * HEAD_DIM
    tab = pl.BlockSpec((tm, HEAD_DIM), lambda i: (i, 0))
    return pl.pallas_call(
        functools.partial(_rope_kernel, n_heads=n_heads),
        grid=(m // tm,),
        in_specs=[pl.BlockSpec((tm, w), lambda i: (i, 0)), tab, tab, tab],
        out_specs=pl.BlockSpec((tm, w), lambda i: (i, 0)),
        out_shape=jax.ShapeDtypeStruct((m, w), out_dtype),
        compiler_params=_cparams(("parallel",)),
        name=name,
    )(x, *tables)


DIL_BATCH = 8


def _dil_prompt_kernel(*refs, seq, n_batch):
    os_ref, o_ref = refs[B_GROUPS + 2:B_GROUPS + 4]
    _prompt_then_sample(n_batch, os_ref, o_ref, functools.partial(_dil_sequence, *refs, seq=seq), axis=1)


def _dil_sequence(*refs, seq):
    q_refs = refs[:B_GROUPS]
    k_ref, v_ref, _, o_ref, m_ref, l_ref, acc_ref = refs[B_GROUPS:]
    blk = B_BLOCK
    base = (lax.broadcasted_iota(jnp.int32, (blk, blk), 0) - lax.broadcasted_iota(jnp.int32, (blk, blk), 1))
    for g, (win, dil) in enumerate(B_PATTERNS):
        span = win // dil
        n_blk = seq // dil // blk
        assert span <= blk

        def body(it, carry, g=g, dil=dil, span=span, n_blk=n_blk):
            own, prev, prev_ok = [], [], []
            for j in range(DIL_BATCH):
                t = it * DIL_BATCH + j
                r, n = t % dil, t // dil

                def rows(nn, r=r):
                    start = nn * (blk * dil) + r
                    return pl.ds(pl.multiple_of(start, blk), blk) if dil == 1 else pl.ds(start, blk, stride=dil)

                own.append(rows(n))
                prev.append(rows(jnp.maximum(n - 1, 0)))
                prev_ok.append(base <= jnp.where(n > 0, span - blk, -blk - 1))
            load = lambda ref, idx: jnp.stack([ref[i, :].astype(BF16) for i in idx])
            scores = lambda a, b: jnp.einsum('bqd,bkd->bqk', a, b, preferred_element_type=F32) * ATTN_SCALE
            weighted = lambda p, v: jnp.einsum('bqk,bkd->bqd', p.astype(BF16), v, preferred_element_type=F32)
            q = load(q_refs[g], own)
            s = jnp.where((base >= 0)[None], scores(q, load(k_ref, own)), NEG_BIG)
            m_b = jnp.max(s, axis=-1, keepdims=True)
            if n_blk > 1:
                s_prev = jnp.where(jnp.stack(prev_ok), scores(q, load(k_ref, prev)), NEG_BIG)
                m_b = jnp.maximum(m_b, jnp.max(s_prev, axis=-1, keepdims=True))
            p = jnp.exp(s - m_b)
            l_b = jnp.sum(p, axis=-1, keepdims=True)
            acc_b = weighted(p, load(v_ref, own))
            if n_blk > 1:
                p_prev = jnp.exp(s_prev - m_b)
                l_b = l_b + jnp.sum(p_prev, axis=-1, keepdims=True)
                acc_b = acc_b + weighted(p_prev, load(v_ref, prev))
            for j, idx in enumerate(own):
                if g == 0:
                    m_ref[idx, :] = jnp.broadcast_to(m_b[j], (blk, HEAD_DIM))
                    l_ref[idx, :] = jnp.broadcast_to(l_b[j], (blk, HEAD_DIM))
                    acc_ref[idx, :] = acc_b[j]
                else:
                    m_old = m_ref[idx, :]
                    m_new = jnp.maximum(m_old, m_b[j])
                    a_old = jnp.exp(m_old - m_new)
                    a_b = jnp.exp(m_b[j] - m_new)
                    m_ref[idx, :] = m_new
                    l_ref[idx, :] = l_ref[idx, :] * a_old + l_b[j] * a_b
                    acc_ref[idx, :] = acc_ref[idx, :] * a_old + acc_b[j] * a_b
            return carry

        assert (dil * n_blk) % DIL_BATCH == 0
        lax.fori_loop(0, dil * n_blk // DIL_BATCH, body, 0)
    o_ref[...] = (acc_ref[...] / l_ref[...]).astype(o_ref.dtype)


def _dil_attn_prompt(q, k, kv, o_sample, n_batch, seq):
    ns = o_sample.shape[0]
    assert ns <= seq and q.shape[0] == n_batch * seq + ns
    last = n_batch - 1
    col = lambda c0: pl.BlockSpec((seq, HEAD_DIM), functools.partial(lambda h, b, c0: (jnp.minimum(b, last), c0 + h), c0=c0))
    return pl.pallas_call(
        functools.partial(_dil_prompt_kernel, seq=seq, n_batch=n_batch),
        grid=(B_SLOTS, n_batch + 1),
        in_specs=[col(g * B_SLOTS) for g in range(B_GROUPS)] + [col(0), col(B_SLOTS),
                                                                 pl.BlockSpec((ns, HEAD_DIM), lambda h, b: (0, h))],
        out_specs=pl.BlockSpec((seq, HEAD_DIM), lambda h, b: (b, h)),
        out_shape=jax.ShapeDtypeStruct((q.shape[0], KV_W), BF16),
        scratch_shapes=[pltpu.VMEM((seq, HEAD_DIM), F32)] * 3,
        compiler_params=_cparams(("arbitrary", "arbitrary"), V7X_VMEM_LIMIT_BYTES),
        name="dil_attn_prompt",
    )(*([q] * B_GROUPS), k, kv, o_sample)


MEM_DECODE_SEQS = 4


def _mem_decode_kernel(q_ref, k_ref, v_ref, o_ref):
    n_seq, rows, _ = k_ref.shape
    nh = q_ref.shape[1]
    own = (lax.broadcasted_iota(jnp.int32, (nh, rows), 1) % nh) == lax.broadcasted_iota(jnp.int32, (nh, rows), 0)
    for i in range(n_seq):
        s = _dot_nt(q_ref[i].astype(BF16), k_ref[i].astype(BF16)) * ATTN_SCALE
        o_ref[i] = _softmax_av(jnp.where(own, s, NEG_BIG), v_ref[i].astype(BF16)).astype(o_ref.dtype)


def _mem_attn_decode(q, cache_k, cache_v, layer):
    ns, nh, hd = q.shape
    n_layers, _, mem_len = cache_k.shape[:3]
    g = MEM_DECODE_SEQS
    assert ns % g == 0
    rows = mem_len * nh
    cspec = pl.BlockSpec((None, g, rows, hd), lambda b: (layer, b, 0, 0))
    qspec = pl.BlockSpec((g, nh, hd), lambda b: (b, 0, 0))
    return pl.pallas_call(
        _mem_decode_kernel,
        grid=(ns // g,),
        in_specs=[qspec, cspec, cspec],
        out_specs=qspec,
        out_shape=jax.ShapeDtypeStruct((ns, nh, hd), BF16),
        compiler_params=_cparams(("parallel",)),
        name="mem_attn_decode",
    )(q, cache_k.reshape(n_layers, ns, rows, hd), cache_v.reshape(n_layers, ns, rows, hd))


DIL_DECODE_SEQS = 2
SUBLANES = 8


def _dil_decode_kernel(q_ref, kn_ref, vn_ref, *refs):
    o_ref = refs[-1]
    n_slots = kn_ref.shape[1]
    for i in range(q_ref.shape[0]):
        kn, vn = kn_ref[i], vn_ref[i]
        scores, values = [], []
        for g in range(B_GROUPS):
            k_ref, v_ref = refs[2 * g], refs[2 * g + 1]
            rows = k_ref.shape[1] * (k_ref.shape[2] if len(k_ref.shape) == 4 else 1)
            period = k_ref.shape[2] if len(k_ref.shape) == 4 else n_slots
            k = k_ref[i].reshape(rows, HEAD_DIM).astype(BF16)
            v = v_ref[i].reshape(rows, HEAD_DIM).astype(BF16)
            q = q_ref[i, g]
            own = (lax.broadcasted_iota(jnp.int32, (n_slots, rows), 1) % period
                   == lax.broadcasted_iota(jnp.int32, (n_slots, rows), 0))
            scores.append(jnp.where(own, _dot_nt(q.astype(BF16), k) * ATTN_SCALE, NEG_BIG))
            values.append(v)
            scores.append(jnp.sum(q * kn, axis=-1, keepdims=True) * ATTN_SCALE)
            values.append(None)
        m = functools.reduce(jnp.maximum, [jnp.max(s, axis=-1, keepdims=True) for s in scores])
        l = jnp.zeros_like(m)
        acc = jnp.zeros((n_slots, HEAD_DIM), F32)
        for s, v in zip(scores, values):
            p = jnp.exp(s - m)
            l = l + jnp.sum(p, axis=-1, keepdims=True)
            acc = acc + (p * vn if v is None else _dot(p.astype(BF16), v))
        o_ref[i] = (acc / l).astype(o_ref.dtype)


def _dil_attn_decode(q, k_new, v_new, cache_k, cache_v):
    ns, w_buf, ns_slots, hd = cache_k.shape
    n_seq = DIL_DECODE_SEQS
    assert ns % n_seq == 0
    in_specs = [pl.BlockSpec((n_seq, B_GROUPS, ns_slots, hd), lambda b: (b, 0, 0, 0)),
                pl.BlockSpec((n_seq, ns_slots, hd), lambda b: (b, 0, 0)),
                pl.BlockSpec((n_seq, ns_slots, hd), lambda b: (b, 0, 0))]
    args = [q, k_new, v_new]
    for win, dil in B_PATTERNS:
        span = win // dil
        assert w_buf % dil == 0 and (w_buf // dil) % span == 0
        last = w_buf // dil // span - 1
        if dil == 1:
            view = (ns, w_buf * ns_slots, hd)
            spec = pl.BlockSpec((n_seq, span * ns_slots, hd), functools.partial(lambda b, last: (b, last, 0), last=last))
        else:
            assert dil * ns_slots >= SUBLANES and ns_slots <= SUBLANES
            view = (ns, w_buf // dil, dil * ns_slots, hd)
            spec = pl.BlockSpec((n_seq, span, SUBLANES, hd), functools.partial(lambda b, last: (b, last, 0, 0), last=last))
        in_specs += [spec, spec]
        args += [cache_k.reshape(view), cache_v.reshape(view)]
    return pl.pallas_call(
        _dil_decode_kernel,
        grid=(ns // n_seq,),
        in_specs=in_specs,
        out_specs=pl.BlockSpec((n_seq, ns_slots, hd), lambda b: (b, 0, 0)),
        out_shape=jax.ShapeDtypeStruct((ns, ns_slots, hd), BF16),
        compiler_params=_cparams(("parallel",)),
        name="dil_attn_decode",
    )(*args)


HGRN_STEP_SEQS = 1


def _hgrn_step_kernel(p_ref, s_ref, lb_ref, gn_ref, o_ref, so_ref, rows_ref):
    h = A_HEADS
    hk = h * HEAD_DIM
    gn = gn_ref[...]
    first = lax.broadcasted_iota(jnp.int32, (8, HEAD_DIM), 0) == 0
    for b in range(p_ref.shape[0]):
        rows_ref[b] = jnp.zeros(rows_ref.shape[1:], F32)
        for i in range(h):
            fp = p_ref[b, :, hk + i * HEAD_DIM:hk + (i + 1) * HEAD_DIM]
            lb = lb_ref[i:i + 1, :]
            rows_ref[b, i:i + 1, :] = lb + (1.0 - lb) * _sigmoid(fp)
        cols = rows_ref[b].T
        for i in range(h):
            sl = slice(i * HEAD_DIM, (i + 1) * HEAD_DIM)
            qp = p_ref[b, :, sl]
            v = p_ref[b, :, 2 * hk + i * HEAD_DIM:2 * hk + (i + 1) * HEAD_DIM]
            gp = p_ref[b, :, 3 * hk + i * HEAD_DIM:3 * hk + (i + 1) * HEAD_DIM]
            k = 1.0 - rows_ref[b, i:i + 1, :]
            k8 = jnp.where(first, k, 0.0).astype(BF16)
            v8 = jnp.broadcast_to(v, (8, HEAD_DIM)).astype(BF16)
            kv = lax.dot_general(k8, v8, (((0,), (0,)), ((), ())), preferred_element_type=F32)
            s_new = cols[:, i:i + 1] * s_ref[b, i] + kv
            so_ref[b, i] = s_new
            q8 = jnp.broadcast_to(qp * _sigmoid(qp), (8, HEAD_DIM)).astype(BF16)
            o = _dot(q8, s_new.astype(BF16))[0:1]
            o = o * lax.rsqrt(jnp.mean(o * o, axis=-1, keepdims=True) + RMS_EPS) * gn
            o_ref[b, :, sl] = (o * (gp * _sigmoid(gp))).astype(o_ref.dtype)


def _hgrn_step(proj, state, layer, lb, g_norm):
    ns = proj.shape[0]
    h = A_HEADS
    g = HGRN_STEP_SEQS
    assert ns % g == 0
    sspec = pl.BlockSpec((g, h, HEAD_DIM, HEAD_DIM), lambda b: (b, 0, 0, 0))
    return pl.pallas_call(
        _hgrn_step_kernel,
        grid=(ns // g,),
        in_specs=[pl.BlockSpec((g, 1, proj.shape[2]), lambda b: (b, 0, 0)),
                  pl.BlockSpec((None, g, h, HEAD_DIM, HEAD_DIM), lambda b: (layer, b, 0, 0, 0)),
                  pl.BlockSpec((h, HEAD_DIM), lambda b: (0, 0)),
                  pl.BlockSpec((1, HEAD_DIM), lambda b: (0, 0))],
        out_specs=[pl.BlockSpec((g, 1, h * HEAD_DIM), lambda b: (b, 0, 0)), sspec],
        out_shape=[jax.ShapeDtypeStruct((ns, 1, h * HEAD_DIM), BF16),
                   jax.ShapeDtypeStruct((ns, h, HEAD_DIM, HEAD_DIM), F32)],
        scratch_shapes=[pltpu.VMEM((g, HEAD_DIM, HEAD_DIM), F32)],
        compiler_params=_cparams(("parallel",)),
        name="hgrn_step",
    )(proj, state, lb.reshape(h, HEAD_DIM), g_norm.reshape(1, HEAD_DIM))


def _ffn_kernel(be_ref, nv_ref, od_ref, nu_ref, x_ref, wg_ref, wu_ref, wd_ref, o_ref, wgb_ref, wub_ref, wdb_ref, *, sub):
    i = pl.program_id(0)
    n_valid = nv_ref[i]

    @pl.when((n_valid > 0) & ((i == 0) | (be_ref[i] != be_ref[jnp.maximum(i - 1, 0)])))
    def _():
        wgb_ref[...] = wg_ref[...].astype(BF16)
        wub_ref[...] = wu_ref[...].astype(BF16)
        wdb_ref[...] = wd_ref[...].astype(BF16)

    for j in range(x_ref.shape[0] // sub):
        rows = slice(j * sub, (j + 1) * sub)

        @pl.when(n_valid > j * sub)
        def _(j=j, rows=rows):
            lo, hi = _unpack_halves(x_ref[rows, :])
            c = lo.shape[1]
            keep = lax.broadcasted_iota(jnp.int32, lo.shape, 0) < n_valid - j * sub
            lo = jnp.where(keep, lo, 0.0).astype(BF16)
            hi = jnp.where(keep, hi, 0.0).astype(BF16)
            ff = wgb_ref.shape[1]
            y = None
            for part in range(FFN_COL_SPLIT):
                cs = slice(part * ff // FFN_COL_SPLIT, (part + 1) * ff // FFN_COL_SPLIT)
                hg = _dot(lo, wgb_ref[:c, cs]) + _dot(hi, wgb_ref[c:, cs])
                hu = _dot(lo, wub_ref[:c, cs]) + _dot(hi, wub_ref[c:, cs])
                hid = (hg * _sigmoid(hg) * hu).astype(BF16)
                yp = _dot(hid, wdb_ref[cs, :])
                y = yp if y is None else y + yp
            o_ref[rows, :] = _pack_halves(y)

        @pl.when(n_valid <= j * sub)
        def _(rows=rows):
            o_ref[rows, :] = jnp.zeros((sub, o_ref.shape[1]), o_ref.dtype)


def _expert_ffn(x, blk_exp, blk_valid, blk_order, n_used, w_gate, w_up, w_down, layer, tm, sub, name):
    r = x.shape[0]
    d, ff = w_gate.shape[2:]
    assert tm % sub == 0
    w_spec = lambda shape: pl.BlockSpec((None, None) + shape, lambda i, be, nv, od, nu: (layer, be[i], 0, 0))
    return pl.pallas_call(
        functools.partial(_ffn_kernel, sub=sub),
        grid_spec=pltpu.PrefetchScalarGridSpec(
            num_scalar_prefetch=4,
            grid=(r // tm,),
            in_specs=[pl.BlockSpec((tm, d // 2), lambda i, be, nv, od, nu: (od[jnp.minimum(i, nu[0] - 1)], 0)),
                      w_spec((d, ff)), w_spec((d, ff)), w_spec((ff, d))],
            out_specs=pl.BlockSpec((tm, d // 2), lambda i, be, nv, od, nu: (od[i], 0)),
            scratch_shapes=[pltpu.VMEM((d, ff), BF16), pltpu.VMEM((d, ff), BF16), pltpu.VMEM((ff, d), BF16)],
        ),
        out_shape=jax.ShapeDtypeStruct((r, d // 2), jnp.int32),
        compiler_params=_cparams(("arbitrary",), V7X_VMEM_LIMIT_BYTES),
        name=name,
    )(blk_exp, blk_valid, blk_order, n_used, x, w_gate, w_up, w_down)


def _first_argmax(val, idx, sentinel):
    m = jnp.max(val, axis=0, keepdims=True)
    i = jnp.min(jnp.where(val == m, idx, sentinel), axis=0, keepdims=True)
    return m, i


def _route_tile(x, w, b):
    t = x.shape[0]
    xh = x.astype(BF16)
    xl = (x - xh.astype(F32)).astype(BF16)
    wh = w.astype(BF16)
    wl = (w - wh.astype(F32)).astype(BF16)
    logits = _dot_nt(wh, xh) + (_dot_nt(wl, xh) + _dot_nt(wh, xl))
    scores = _sigmoid(logits)
    biased = scores + b
    gs = N_EXPERTS // N_GROUPS
    neg = -jnp.inf
    eid = lax.broadcasted_iota(jnp.int32, (N_EXPERTS, t), 0)
    sub = lax.broadcasted_iota(jnp.int32, (gs, t), 0)
    grow = lax.broadcasted_iota(jnp.int32, (N_GROUPS, t), 0)
    grp = jnp.zeros((N_GROUPS, t), F32)
    for g in range(N_GROUPS):
        bg = biased[g * gs:(g + 1) * gs]
        m1, i1 = _first_argmax(bg, sub, gs)
        m2 = jnp.max(jnp.where(sub == i1, neg, bg), axis=0, keepdims=True)
        grp = jnp.where(grow == g, m1 + m2, grp)
    chosen = jnp.zeros((N_GROUPS, t), F32)
    for _ in range(TOPK_GROUPS):
        _, gi = _first_argmax(grp, grow, N_GROUPS)
        hit = grow == gi
        chosen = jnp.where(hit, 1.0, chosen)
        grp = jnp.where(hit, neg, grp)
    chosen_e = jnp.concatenate([jnp.broadcast_to(chosen[g:g + 1], (gs, t)) for g in range(N_GROUPS)], axis=0)
    masked = jnp.where(chosen_e > 0.0, biased, neg)
    krow = lax.broadcasted_iota(jnp.int32, (TOP_K, t), 0)
    e_out = jnp.zeros((TOP_K, t), jnp.int32)
    g_out = jnp.zeros((TOP_K, t), F32)
    member = jnp.zeros((N_EXPERTS, t), F32)
    for k in range(TOP_K):
        _, idx = _first_argmax(masked, eid, N_EXPERTS)
        hit = eid == idx
        gk = jnp.sum(jnp.where(hit, scores, 0.0), axis=0, keepdims=True)
        masked = jnp.where(hit, neg, masked)
        member = jnp.where(hit, 1.0, member)
        e_out = jnp.where(krow == k, idx, e_out)
        g_out = jnp.where(krow == k, gk, g_out)
    g_out = g_out / jnp.sum(g_out, axis=0, keepdims=True) * ROUTED_SCALE
    return e_out, g_out, jnp.sum(member, axis=1, keepdims=True).astype(jnp.int32)


def _slots_kernel(e_ref, base_ref, o_ref):
    e = e_ref[...]
    t = e.shape[1]
    eid = lax.broadcasted_iota(jnp.int32, (N_EXPERTS, t), 0)
    member = jnp.zeros((N_EXPERTS, t), F32)
    for k in range(TOP_K):
        member = jnp.where(eid == e[k:k + 1], 1.0, member)
    before = (lax.broadcasted_iota(jnp.int32, (t, t), 0) < lax.broadcasted_iota(jnp.int32, (t, t), 1))
    rank = _dot(member.astype(BF16), jnp.where(before, 1.0, 0.0).astype(BF16))
    slot = rank + base_ref[...].astype(F32)
    krow = lax.broadcasted_iota(jnp.int32, (TOP_K, t), 0)
    out = jnp.zeros((TOP_K, t), F32)
    for k in range(TOP_K):
        sk = jnp.sum(jnp.where(eid == e[k:k + 1], slot, 0.0), axis=0, keepdims=True)
        out = jnp.where(krow == k, sk, out)
    o_ref[...] = out.astype(jnp.int32)


def _dispatch_plan(e_idx, counts, tm, tile):
    k, n = e_idx.shape
    n_blocks = -(-(n * k) // tm) + N_EXPERTS
    counts = counts.reshape(n // tile, N_EXPERTS)
    total = jnp.sum(counts, axis=0)
    padded = (total + tm - 1) // tm * tm
    pad_end = jnp.cumsum(padded)
    tile_base = (pad_end - padded)[None, :] + jnp.cumsum(counts, axis=0) - counts
    slot_of = pl.pallas_call(
        _slots_kernel,
        grid=(n // tile,),
        in_specs=[pl.BlockSpec((k, tile), lambda i: (0, i)),
                  pl.BlockSpec((None, N_EXPERTS, 1), lambda i: (i, 0, 0))],
        out_specs=pl.BlockSpec((k, tile), lambda i: (0, i)),
        out_shape=jax.ShapeDtypeStruct((k, n), jnp.int32),
        compiler_params=_cparams(("parallel",)),
        name="slots",
    )(e_idx, tile_base.astype(jnp.int32).reshape(n // tile, N_EXPERTS, 1))
    pos = jnp.arange(n_blocks, dtype=jnp.int32)
    blk_exp = jnp.minimum(jnp.sum((pad_end[None, :] <= (pos * tm)[:, None]).astype(jnp.int32), axis=1), N_EXPERTS - 1)
    is_exp = blk_exp[:, None] == jnp.arange(N_EXPERTS, dtype=blk_exp.dtype)[None, :]
    per_blk = lambda v: jnp.sum(jnp.where(is_exp, v[None, :], 0), axis=1)
    n_used = jnp.maximum(pad_end[-1:] // tm, 1).astype(jnp.int32)
    first, count = per_blk((pad_end - padded) // tm), jnp.maximum(per_blk(padded // tm), 1)
    blk_order = jnp.where(pos < n_used[0], first + (pos - first + count - 1) % count, pos)
    blk_valid = jnp.clip(per_blk(pad_end - padded + total) - blk_order * tm, 0, tm)
    return (slot_of, blk_exp.astype(jnp.int32), blk_valid.astype(jnp.int32), blk_order.astype(jnp.int32), n_used,
            n_blocks)


V7X_SC_CORES = 2
V7X_SC_SUBCORES = 16
SC_WORKERS = V7X_SC_CORES * V7X_SC_SUBCORES
SC_CHUNK = 32
SC_GATHER_CHUNK = 40


def _sc_mesh():
    return plsc.VectorSubcoreMesh(core_axis_name="c", subcore_axis_name="s",
                                  num_cores=V7X_SC_CORES, num_subcores=V7X_SC_SUBCORES)


def _sc_worker_id():
    return lax.axis_index("s") * V7X_SC_CORES + lax.axis_index("c")


def _sc_gather_rows(table, slot_of):
    k, n = slot_of.shape
    w = table.shape[1]
    n_rows = k * n
    ch = SC_GATHER_CHUNK
    per_worker = n_rows // (SC_WORKERS * ch)
    assert per_worker * SC_WORKERS * ch == n_rows and per_worker % 2 == 0
    idx = slot_of.reshape(SC_WORKERS, per_worker, ch)

    def body(table_hbm, idx_hbm, out_hbm, idx_v, rows_v, sems):
        wid = _sc_worker_id()
        pltpu.sync_copy(idx_hbm.at[wid], idx_v)

        def gather(c, slot):
            return pltpu.make_async_copy(table_hbm.at[idx_v.at[c]], rows_v.at[slot], sems.at[slot])

        gather(0, 0).start()

        @pl.loop(0, per_worker, step=2)
        def _(c0):
            for slot in range(2):
                c = c0 + slot
                gather(c, slot).wait()

                @pl.when(c + 1 < per_worker)
                def _():
                    gather(c + 1, 1 - slot).start()

                row0 = pl.multiple_of((wid * per_worker + c) * ch, 8)
                pltpu.sync_copy(rows_v.at[slot], out_hbm.at[pl.ds(row0, ch)])

    return pl.kernel(
        body, out_type=jax.ShapeDtypeStruct((n_rows, w), table.dtype), mesh=_sc_mesh(),
        scratch_types=[pltpu.VMEM((per_worker, ch), jnp.int32), pltpu.VMEM((2, ch, w), table.dtype),
                       pltpu.SemaphoreType.DMA((2,))],
        name="sc_gather_rows",
    )(table, idx)


def _sc_scatter_rows(x, slot_of, n_slots):
    k, n = slot_of.shape
    w = x.shape[1]
    n_chunks = n // SC_CHUNK
    assert n_chunks * SC_CHUNK == n
    idx = slot_of.reshape(k, n_chunks, SC_CHUNK).transpose(1, 0, 2)
    rounds = -(-n_chunks // SC_WORKERS)

    def body(x_hbm, idx_hbm, out_hbm, idx_v, rows_v, sem):
        wid = _sc_worker_id()

        @pl.loop(0, rounds)
        def _(j):
            q = j * SC_WORKERS + wid

            @pl.when(q < n_chunks)
            def _():
                pltpu.sync_copy(idx_hbm.at[q], idx_v)
                pltpu.sync_copy(x_hbm.at[pl.ds(pl.multiple_of(q * SC_CHUNK, SC_CHUNK), SC_CHUNK)], rows_v)
                copies = [pltpu.make_async_copy(rows_v, out_hbm.at[idx_v.at[kk]], sem) for kk in range(k)]
                for cp in copies:
                    cp.start()
                for cp in copies:
                    cp.wait()

    return pl.kernel(
        body, out_type=jax.ShapeDtypeStruct((n_slots, w), x.dtype), mesh=_sc_mesh(),
        scratch_types=[pltpu.VMEM((k, SC_CHUNK), jnp.int32), pltpu.VMEM((SC_CHUNK, w), x.dtype),
                       pltpu.SemaphoreType.DMA],
        name="sc_scatter_rows",
    )(x, idx)


MOE_BLOCK = 768
MOE_SUB = 384
FFN_COL_SPLIT = 2
ROW_TILE = 640


def _moe_postnorm_kernel(x_ref, y_ref, gate_ref, sh_ref, g_ref, b_ref, ofp_ref, ofs_ref, ob_ref, *, alpha, tiles_p):
    def part(which):
        of_ref = (ofp_ref, ofs_ref)[which]
        gate = gate_ref[...]
        lo, hi = _unpack_halves(sh_ref[...])
        for k in range(TOP_K):
            lo_k, hi_k = _unpack_halves(y_ref[k])
            lo = lo + gate[:, k:k + 1] * lo_k
            hi = hi + gate[:, k:k + 1] * hi_k
        z = alpha * x_ref[...] + jnp.concatenate([lo, hi], axis=1)
        out = _layer_norm(z, g_ref[...], b_ref[...])
        of_ref[...] = out
        ob_ref[...] = out.astype(BF16)

    _on_part(tiles_p, part)


def _moe_postnorm(x, n_p, y_tok, gate, shared, g, b, alpha):
    m, d = x.shape
    tm = PART_TILE
    row = pl.BlockSpec((tm, d), lambda i: (i, 0))
    vec = pl.BlockSpec((1, d), lambda i: (0, 0))
    part_p, part_s = _part_specs(n_p, (tm, d))
    return pl.pallas_call(
        functools.partial(_moe_postnorm_kernel, alpha=alpha, tiles_p=n_p // tm),
        grid=(m // tm,),
        in_specs=[row, pl.BlockSpec((TOP_K, tm, d // 2), lambda i: (0, i, 0)),
                  pl.BlockSpec((tm, TOP_K), lambda i: (i, 0)), pl.BlockSpec((tm, d // 2), lambda i: (i, 0)), vec, vec],
        out_specs=[part_p, part_s, row],
        out_shape=[jax.ShapeDtypeStruct((n_p, d), F32), jax.ShapeDtypeStruct((m - n_p, d), F32),
                   jax.ShapeDtypeStruct((m, d), BF16)],
        compiler_params=_cparams(("arbitrary",), V7X_VMEM_LIMIT_BYTES),
        name="postnorm_moe",
    )(x, y_tok, gate, shared, g.reshape(1, d), b.reshape(1, d))


def _moe(xf, n_p, xp, routing, layer, w_gate, w_up, w_down, sw_gate, sw_up, sw_down, ln_g, ln_b, alpha):
    n, d = xf.shape
    e_idx, gate, counts = routing
    slot_of, blk_exp, blk_valid, blk_order, n_used, n_blocks = _dispatch_plan(e_idx, counts, MOE_BLOCK, PART_TILE)
    x_sorted = _sc_scatter_rows(xp, slot_of, n_blocks * MOE_BLOCK)
    y_sorted = _expert_ffn(x_sorted, blk_exp, blk_valid, blk_order, n_used, w_gate, w_up, w_down, layer,
                           MOE_BLOCK, MOE_SUB, "routed_ffn")
    y_tok = _sc_gather_rows(y_sorted, slot_of).reshape(TOP_K, n, d // 2)
    n_sh = n // ROW_TILE
    shared = _expert_ffn(xp, jnp.zeros((n_sh,), jnp.int32), jnp.full((n_sh,), ROW_TILE, jnp.int32),
                         jnp.arange(n_sh, dtype=jnp.int32), jnp.full((1,), n_sh, jnp.int32),
                         sw_gate[:, None], sw_up[:, None], sw_down[:, None], layer, ROW_TILE, ROW_TILE, "shared_ffn")
    return _moe_postnorm(xf, n_p, y_tok, gate.T, shared, ln_g, ln_b, alpha)


def kernel(x_prompt, x_sample, state_hgrn, cache_win_k, cache_win_v, cache_mem_k, cache_mem_v, mem_prompt,
           w_in_a, lb_logits, g_norm_a, w_out_a, w_in_b, w_out_b, w_kv_shared, w_mem_kv, ln_g, ln_b,
           router_w, router_b, exp_w_gate, exp_w_up, exp_w_down, sh_w_gate, sh_w_up, sh_w_down):
    bp, sp, d = x_prompt.shape
    ns = x_sample.shape[0]
    assert x_sample.shape[1] == 1
    depth = ln_g.shape[0]
    n_a = w_in_a.shape[0]
    alpha = (2 * depth) ** 0.25
    n_p = bp * sp
    mem_len = mem_prompt.shape[1]
    a_mix = 4 * A_HEADS * HEAD_DIM

    xf_p, xf_s = x_prompt.reshape(n_p, d), x_sample.reshape(ns, d)
    xb = jnp.concatenate([xf_p.astype(BF16), xf_s.astype(BF16)], axis=0)
    lower_bounds = jnp.cumsum(jax.nn.softmax(lb_logits.astype(F32), axis=0), axis=0)
    mem_flat = mem_prompt.reshape(bp * mem_len, d)
    pos_all = jnp.concatenate([jnp.tile(jnp.arange(sp, dtype=jnp.int32), bp),
                               jnp.full((ns,), PAST_LEN, jnp.int32)])
    tables = _rope_tables(pos_all)

    hgrn_p, hgrn_s, mem_k_p, mem_v_p = [], [], [], []
    for layer in range(depth):
        kvm = _proj([mem_flat], w_mem_kv, layer, F32, 2 * mem_len, 1024, "mem_kv")
        mem_k_p.append(kvm[:, :MEM_W].reshape(bp, mem_len, MEM_HEADS, HEAD_DIM))
        mem_v_p.append(kvm[:, MEM_W:].reshape(bp, mem_len, MEM_HEADS, HEAD_DIM))
        if layer < n_a:
            a = layer
            proj = _proj([xb], w_in_a, a, F32, ROW_TILE, 1664, "proj_in_a")
            proj_s = proj[n_p:]
            o_x_s, st_s = _hgrn_step(proj_s.reshape(ns, 1, -1), state_hgrn, a, lower_bounds[a], g_norm_a[a])
            o_m_s = _mem_attn_decode(proj_s[:, a_mix:].reshape(ns, MEM_HEADS, HEAD_DIM), cache_mem_k, cache_mem_v, layer)
            o_x, st_p = _hgrn_prompt(proj, lower_bounds[a], g_norm_a[a], o_x_s.reshape(ns, -1), bp, sp)
            o_m = _mem_attn_prompt(proj, a_mix // MEM_W, kvm, o_m_s.reshape(ns, -1), bp, sp, 512)
            hgrn_p.append(st_p)
            hgrn_s.append(st_s)
            w_out, w_out_layer = w_out_a, a
        else:
            bl = layer - n_a
            if layer == n_a:
                kv = _proj([xb], w_kv_shared[None], 0, F32, ROW_TILE, 1024, "proj_kv")
                k_r, k_p, k_s, v_p, v_s = _shared_kv(kv, tables, n_p)
            proj = _proj([xb], w_in_b, bl, F32, ROW_TILE, 1024, "proj_in_b")
            q_r = _rope(proj, B_QHEADS, tables, F32, ROW_TILE, "rope_q")
            q_s = q_r[n_p:].reshape(ns, B_GROUPS, B_SLOTS, HEAD_DIM)
            o_x_s = _dil_attn_decode(q_s, k_s, v_s, cache_win_k, cache_win_v)
            o_m_s = _mem_attn_decode(proj[n_p:, B_QHEADS * HEAD_DIM:].reshape(ns, MEM_HEADS, HEAD_DIM),
                                     cache_mem_k, cache_mem_v, layer)
            o_x = _dil_attn_prompt(q_r, k_r, kv, o_x_s.reshape(ns, -1), bp, sp)
            o_m = _mem_attn_prompt(proj, B_QHEADS * HEAD_DIM // MEM_W, kvm, o_m_s.reshape(ns, -1), bp, sp, 512)
            w_out, w_out_layer = w_out_b, bl
        y = _proj([o_x, o_m], w_out, w_out_layer, F32, ROW_TILE, 1024, "proj_out")
        xf, xp, *routing = _postnorm_route(xf_p, xf_s, y, ln_g[layer, 0], ln_b[layer, 0], alpha,
                                           router_w, layer, router_b, "postnorm_route")
        xf_p, xf_s, xb = _moe(xf, n_p, xp, routing, layer, exp_w_gate, exp_w_up, exp_w_down,
                              sh_w_gate, sh_w_up, sh_w_down, ln_g[layer, 1], ln_b[layer, 1], alpha)

    w_p = min(max(w for w, _ in B_PATTERNS), sp)
    k_p = k_p.reshape(bp, sp, B_SLOTS, HEAD_DIM)
    v_p = v_p.reshape(bp, sp, B_SLOTS, HEAD_DIM)
    return (xf_p.reshape(bp, sp, d), xf_s.reshape(ns, 1, d),
            jnp.stack(hgrn_p), jnp.stack(hgrn_s),
            k_p[:, sp - w_p:], v_p[:, sp - w_p:],
            k_s.reshape(ns, 1, B_SLOTS, HEAD_DIM), v_s.reshape(ns, 1, B_SLOTS, HEAD_DIM),
            jnp.stack(mem_k_p), jnp.stack(mem_v_p))
```

```python
import functools

import jax
import jax.numpy as jnp
from jax import lax
from jax.experimental import pallas as pl
from jax.experimental.pallas import tpu as pltpu
from jax.experimental.pallas import tpu_sc as plsc

F32 = jnp.float32
BF16 = jnp.bfloat16

HEAD_DIM = 128
A_HEADS = 12
A_CHUNK = 64
A_SUB = 16
B_PATTERNS = ((128, 1), (512, 4), (2048, 16))
B_SLOTS = 4
B_GROUPS = len(B_PATTERNS)
B_QHEADS = B_GROUPS * B_SLOTS
B_BLOCK = 128
MEM_HEADS = 4
MEM_W = MEM_HEADS * HEAD_DIM
KV_W = B_SLOTS * HEAD_DIM
ROPE_THETA = 500000.0
ROPE_DIM = HEAD_DIM // 4
N_EXPERTS = 64
N_GROUPS = 8
TOPK_GROUPS = 4
TOP_K = 8
ROUTED_SCALE = 2.5
LN_EPS = 1e-5
RMS_EPS = 1e-6
ATTN_SCALE = HEAD_DIM ** -0.5
PAST_LEN = 2048

V7X_VMEM_LIMIT_BYTES = 56 * 1024 * 1024
LANES = 128
NEG_BIG = -1e30


def _cparams(sem, vmem=None):
    return pltpu.CompilerParams(dimension_semantics=sem, vmem_limit_bytes=vmem)


def _dot(a, b):
    return jnp.dot(a, b, preferred_element_type=F32)


def _dot_nt(a, b):
    return lax.dot_general(a, b, (((1,), (1,)), ((), ())), preferred_element_type=F32)


def _sigmoid(x):
    return 1.0 / (1.0 + jnp.exp(-x))


def _proj_kernel(*refs, n_lhs, rope_heads):
    x_refs = refs[:n_lhs]
    w_refs = refs[n_lhs:2 * n_lhs]
    n_tab = 3 if rope_heads else 0
    tab_refs = refs[2 * n_lhs:2 * n_lhs + n_tab]
    o_ref = refs[2 * n_lhs + n_tab]
    wb_refs = refs[2 * n_lhs + n_tab + 1:]

    @pl.when(pl.program_id(1) == 0)
    def _():
        for w_ref, wb_ref in zip(w_refs, wb_refs):
            wb_ref[...] = w_ref[...].astype(BF16)

    acc = None
    for x_ref, wb_ref in zip(x_refs, wb_refs):
        d = _dot(x_ref[...].astype(BF16), wb_ref[...])
        acc = d if acc is None else acc + d
    if not rope_heads:
        o_ref[...] = acc.astype(o_ref.dtype)
    else:
        c, su, sd = (t[...] for t in tab_refs)
        heads_per_tile = o_ref.shape[1] // HEAD_DIM
        for h in range(heads_per_tile):
            sl = slice(h * HEAD_DIM, (h + 1) * HEAD_DIM)
            plain = acc[:, sl]
            roped = _rope_head(plain, c, su, sd)
            is_roped = pl.program_id(0) * heads_per_tile + h < rope_heads
            o_ref[:, sl] = jnp.where(is_roped, roped, plain).astype(o_ref.dtype)


def _proj(lhs, w, layer, out_dtype, tm, tn, name, rope_tables=None, rope_heads=0):
    m = lhs[0].shape[0]
    n = w.shape[2]
    koff = 0
    in_specs, w_specs, scratch = [], [], []
    for x in lhs:
        k = x.shape[1]
        assert koff % k == 0 and m % tm == 0 and n % tn == 0
        in_specs.append(pl.BlockSpec((tm, k), lambda j, i: (i, 0)))
        w_specs.append(pl.BlockSpec((None, k, tn), functools.partial(lambda j, i, kb: (layer, kb, j), kb=koff // k)))
        scratch.append(pltpu.VMEM((k, tn), BF16))
        koff += k
    assert koff == w.shape[1]
    tables = list(rope_tables) if rope_heads else []
    tab_specs = [pl.BlockSpec((tm, HEAD_DIM), lambda j, i: (i, 0))] * len(tables)
    return pl.pallas_call(
        functools.partial(_proj_kernel, n_lhs=len(lhs), rope_heads=rope_heads),
        grid=(n // tn, m // tm),
        in_specs=in_specs + w_specs + tab_specs,
        out_specs=pl.BlockSpec((tm, tn), lambda j, i: (i, j)),
        out_shape=jax.ShapeDtypeStruct((m, n), out_dtype),
        scratch_shapes=scratch,
        compiler_params=_cparams(("arbitrary", "arbitrary"), V7X_VMEM_LIMIT_BYTES),
        name=name,
    )(*lhs, *([w] * len(lhs)), *tables)


def _pack_halves(x):
    c = x.shape[1] // 2
    lo = pltpu.bitcast(x[:, :c].astype(BF16).astype(F32), jnp.int32)
    hi = pltpu.bitcast(x[:, c:].astype(BF16).astype(F32), jnp.int32)
    return hi | lax.shift_right_logical(lo, 16)


def _unpack_halves(w):
    lo = pltpu.bitcast(lax.shift_left(w, 16), F32)
    hi = pltpu.bitcast(w & jnp.int32(-65536), F32)
    return lo, hi


def _layer_norm(z, g, b):
    mu = jnp.mean(z, axis=-1, keepdims=True)
    zc = z - mu
    var = jnp.mean(zc * zc, axis=-1, keepdims=True)
    return zc * lax.rsqrt(var + LN_EPS) * g + b


PART_TILE = 128


def _part_specs(n_p, block):
    tiles_p = n_p // PART_TILE
    rest = (0,) * (len(block) - 1)
    return (pl.BlockSpec(block, lambda i: (jnp.minimum(i, tiles_p - 1),) + rest),
            pl.BlockSpec(block, lambda i: (jnp.maximum(i - tiles_p, 0),) + rest))


def _on_part(tiles_p, fn):
    i = pl.program_id(0)
    pl.when(i < tiles_p)(functools.partial(fn, 0))
    pl.when(i >= tiles_p)(functools.partial(fn, 1))


def _postnorm_kernel(xp_ref, xs_ref, y_ref, g_ref, b_ref, wt_ref, rb_ref, of_ref, op_ref, e_ref, gate_ref, cnt_ref,
                     *, alpha, tiles_p):
    def part(which):
        x_ref = (xp_ref, xs_ref)[which]
        out = _layer_norm(alpha * x_ref[...] + y_ref[...], g_ref[...], b_ref[...])
        of_ref[...] = out
        op_ref[...] = _pack_halves(out)
        e_ref[...], gate_ref[...], cnt_ref[...] = _route_tile(out, wt_ref[...], rb_ref[...])

    _on_part(tiles_p, part)


def _postnorm_route(x_p, x_s, y, g, b, alpha, router_w, layer, router_b, name):
    n_p, d = x_p.shape
    m = n_p + x_s.shape[0]
    e = router_w.shape[2]
    t = PART_TILE
    row = pl.BlockSpec((t, d), lambda i: (i, 0))
    vec = pl.BlockSpec((1, d), lambda i: (0, 0))
    tok = pl.BlockSpec((TOP_K, t), lambda i: (0, i))
    part_p, part_s = _part_specs(n_p, (t, d))
    return pl.pallas_call(
        functools.partial(_postnorm_kernel, alpha=alpha, tiles_p=n_p // t),
        grid=(m // t,),
        in_specs=[part_p, part_s, row, vec, vec,
                  pl.BlockSpec((None, e, d), lambda i: (layer, 0, 0)), pl.BlockSpec((e, 1), lambda i: (0, 0))],
        out_specs=[row, pl.BlockSpec((t, d // 2), lambda i: (i, 0)), tok, tok,
                   pl.BlockSpec((None, e, 1), lambda i: (i, 0, 0))],
        out_shape=[jax.ShapeDtypeStruct((m, d), F32), jax.ShapeDtypeStruct((m, d // 2), jnp.int32),
                   jax.ShapeDtypeStruct((TOP_K, m), jnp.int32), jax.ShapeDtypeStruct((TOP_K, m), F32),
                   jax.ShapeDtypeStruct((m // t, e, 1), jnp.int32)],
        compiler_params=_cparams(("arbitrary",)),
        name=name,
    )(x_p, x_s, y, g.reshape(1, d), b.reshape(1, d), jnp.swapaxes(router_w, 1, 2),
      router_b[layer].astype(F32).reshape(e, 1))


def _hgrn_prompt_kernel(q_ref, f_ref, v_ref, gate_ref, lb_ref, gn_ref, os_ref, o_ref, s_ref, st_ref,
                        *, seq, heads, n_batch):
    _prompt_then_sample(n_batch, os_ref, o_ref,
                        functools.partial(_hgrn_sequence, q_ref, f_ref, v_ref, gate_ref, lb_ref, gn_ref,
                                          o_ref, s_ref, st_ref, seq=seq, heads=heads), axis=1)


def _hgrn_sequence(q_ref, f_ref, v_ref, gate_ref, lb_ref, gn_ref, o_ref, s_ref, st_ref, *, seq, heads):
    c = A_CHUNK
    n_chunks = seq // c
    n_sub = c // A_SUB
    gn = gn_ref[...]
    row = lax.broadcasted_iota(jnp.int32, (c, c), 0)
    col = lax.broadcasted_iota(jnp.int32, (c, c), 1)
    tril = jnp.where(row >= col, 1.0, 0.0).astype(BF16)
    same_sub = (row // A_SUB) == (col // A_SUB)
    diag_dist = jnp.where(same_sub, row - col, -1)
    off_mask = col < (row // A_SUB) * A_SUB
    st_ref[...] = jnp.zeros_like(st_ref)

    def chunk(ci, carry):
        for hh in range(heads):
            head_chunk(ci, hh)
        return carry

    def head_chunk(ci, hh):
        r0 = pl.multiple_of(ci * c, c)
        hsl = slice(hh * HEAD_DIM, (hh + 1) * HEAD_DIM)
        lb = lb_ref[hh]
        qp = q_ref[pl.ds(r0, c), hsl]
        fp = f_ref[pl.ds(r0, c), hsl]
        v = v_ref[pl.ds(r0, c), hsl]
        gp = gate_ref[pl.ds(r0, c), hsl]
        st = st_ref[hh]
        q = qp * _sigmoid(qp)
        forget = lb + (1.0 - lb) * _sigmoid(fp)
        logf = jnp.log2(forget)
        k = 1.0 - forget
        hi = logf.astype(BF16)
        r1 = logf - hi.astype(F32)
        mid = r1.astype(BF16)
        lo = (r1 - mid.astype(F32)).astype(BF16)
        g = _dot(tril, hi) + _dot(tril, mid) + _dot(tril, lo)
        v_b = v.astype(BF16)
        o = _dot_nt((q * jnp.exp2(g)).astype(BF16), st.astype(BF16))
        rows = [jnp.zeros((A_SUB, c), F32)]
        for i in range(1, n_sub):
            gref = g[i * A_SUB:i * A_SUB + 1, :]
            qt = q[i * A_SUB:(i + 1) * A_SUB, :] * jnp.exp2(g[i * A_SUB:(i + 1) * A_SUB, :] - gref)
            kt = k * jnp.exp2(jnp.minimum(gref - g, 0.0))
            rows.append(_dot_nt(qt.astype(BF16), kt.astype(BF16)))
        a = jnp.where(off_mask, jnp.concatenate(rows, axis=0), 0.0)
        for d in range(A_SUB):
            kr = k if d == 0 else pltpu.roll(k, d, axis=0)
            gr = g if d == 0 else pltpu.roll(g, d, axis=0)
            x = q * kr * jnp.exp2(g - gr)
            a = jnp.where(diag_dist == d, jnp.sum(x, axis=-1, keepdims=True), a)
        o = o + _dot(a.astype(BF16), v_b)
        gend = g[c - 1:c, :]
        kt_end = k * jnp.exp2(gend - g)
        st_ref[hh] = jnp.exp2(gend) * st + _dot(v_b.T, kt_end.astype(BF16))
        o = o * lax.rsqrt(jnp.mean(o * o, axis=-1, keepdims=True) + RMS_EPS) * gn
        o_ref[pl.ds(r0, c), hsl] = (o * (gp * _sigmoid(gp))).astype(o_ref.dtype)

    lax.fori_loop(0, n_chunks, chunk, 0, unroll=A_CHUNK_UNROLL)
    for hh in range(heads):
        s_ref[hh] = st_ref[hh].T


A_HEADS_PER_STEP = 4
A_CHUNK_UNROLL = 4


def _hgrn_prompt(proj, lb, g_norm, o_sample, n_batch, seq):
    h = A_HEADS
    hp = A_HEADS_PER_STEP
    ns = o_sample.shape[0]
    assert h % hp == 0 and ns <= seq and proj.shape[0] == n_batch * seq + ns
    ng = h // hp
    w = hp * HEAD_DIM
    last = n_batch - 1
    blk = lambda off: pl.BlockSpec((seq, w), functools.partial(lambda hg, b, off: (jnp.minimum(b, last), off + hg), off=off))
    return pl.pallas_call(
        functools.partial(_hgrn_prompt_kernel, seq=seq, heads=hp, n_batch=n_batch),
        grid=(ng, n_batch + 1),
        in_specs=[blk(0), blk(ng), blk(2 * ng), blk(3 * ng),
                  pl.BlockSpec((hp, 1, HEAD_DIM), lambda hg, b: (hg, 0, 0)),
                  pl.BlockSpec((1, HEAD_DIM), lambda hg, b: (0, 0)),
                  pl.BlockSpec((ns, w), lambda hg, b: (0, hg))],
        out_specs=[pl.BlockSpec((seq, w), lambda hg, b: (b, hg)),
                   pl.BlockSpec((None, hp, HEAD_DIM, HEAD_DIM), lambda hg, b: (jnp.minimum(b, last), hg, 0, 0))],
        out_shape=[jax.ShapeDtypeStruct((proj.shape[0], h * HEAD_DIM), BF16),
                   jax.ShapeDtypeStruct((n_batch, h, HEAD_DIM, HEAD_DIM), F32)],
        scratch_shapes=[pltpu.VMEM((hp, HEAD_DIM, HEAD_DIM), F32)],
        compiler_params=_cparams(("arbitrary", "arbitrary"), V7X_VMEM_LIMIT_BYTES),
        name="hgrn_prompt",
    )(proj, proj, proj, proj, lb.reshape(h, 1, HEAD_DIM), g_norm.reshape(1, HEAD_DIM), o_sample)


def _softmax_av(s, v):
    m = jnp.max(s, axis=-1, keepdims=True)
    p = jnp.exp(s - m)
    l = jnp.sum(p, axis=-1, keepdims=True)
    return _dot(p.astype(BF16), v) / l


def _prompt_then_sample(n_prompt_steps, os_ref, o_ref, prompt_step, axis=0):
    i = pl.program_id(axis)
    pl.when(i < n_prompt_steps)(prompt_step)

    @pl.when(i >= n_prompt_steps)
    def _():
        o_ref[:os_ref.shape[0], :] = os_ref[...]


def _mem_attn_kernel(q_ref, k_ref, v_ref, os_ref, o_ref, *, n_steps):
    def step():
        for h in range(MEM_HEADS):
            sl = slice(h * HEAD_DIM, (h + 1) * HEAD_DIM)
            s = _dot_nt(q_ref[:, sl].astype(BF16), k_ref[:, sl].astype(BF16)) * ATTN_SCALE
            o_ref[:, sl] = _softmax_av(s, v_ref[:, sl].astype(BF16)).astype(o_ref.dtype)

    _prompt_then_sample(n_steps, os_ref, o_ref, step)


def _mem_attn_prompt(proj, q_col, kv, o_sample, n_batch, seq, tq):
    mem_len = kv.shape[0] // n_batch
    nq = seq // tq
    n_steps = n_batch * nq
    ns = o_sample.shape[0]
    assert ns <= tq and proj.shape[0] == n_batch * seq + ns
    last = n_steps - 1
    return pl.pallas_call(
        functools.partial(_mem_attn_kernel, n_steps=n_steps),
        grid=(n_steps + 1,),
        in_specs=[pl.BlockSpec((tq, MEM_W), lambda i: (jnp.minimum(i, last), q_col)),
                  pl.BlockSpec((mem_len, MEM_W), lambda i: (jnp.minimum(i, last) // nq, 0)),
                  pl.BlockSpec((mem_len, MEM_W), lambda i: (jnp.minimum(i, last) // nq, 1)),
                  pl.BlockSpec((ns, MEM_W), lambda i: (0, 0))],
        out_specs=pl.BlockSpec((tq, MEM_W), lambda i: (i, 0)),
        out_shape=jax.ShapeDtypeStruct((proj.shape[0], MEM_W), BF16),
        compiler_params=_cparams(("arbitrary",)),
        name="mem_attn_prompt",
    )(proj, kv, kv, o_sample)


def _rope_tables(pos):
    half = ROPE_DIM // 2
    inv_freq = ROPE_THETA ** (-jnp.arange(0, ROPE_DIM, 2, dtype=F32) / ROPE_DIM)
    ang = pos.astype(F32)[:, None] * inv_freq[None, :]
    cos, sin = jnp.cos(ang), jnp.sin(ang)
    n = pos.shape[0]
    one = jnp.ones((n, HEAD_DIM - ROPE_DIM), F32)
    zero = jnp.zeros((n, HEAD_DIM - half), F32)
    c = jnp.concatenate([cos, cos, one], axis=1)
    s_up = jnp.concatenate([-sin, zero], axis=1)
    s_dn = jnp.concatenate([jnp.zeros((n, half), F32), sin, zero[:, half:]], axis=1)
    return c, s_up, s_dn


def _rope_head(x, c, su, sd):
    half = ROPE_DIM // 2
    up = pltpu.roll(x, HEAD_DIM - half, axis=1)
    dn = pltpu.roll(x, half, axis=1)
    return x * c + up * su + dn * sd


def _kv_kernel(kv_ref, c_ref, su_ref, sd_ref, k_ref, kp_ref, ks_ref, vp_ref, vs_ref, *, tiles_p):
    c, su, sd = c_ref[...], su_ref[...], sd_ref[...]

    def part(which):
        k4_ref, v4_ref = ((kp_ref, vp_ref), (ks_ref, vs_ref))[which]
        for h in range(B_SLOTS):
            sl = slice(h * HEAD_DIM, (h + 1) * HEAD_DIM)
            k = _rope_head(kv_ref[:, sl], c, su, sd)
            k_ref[:, sl] = k
            k4_ref[:, h, :] = k
            v4_ref[:, h, :] = kv_ref[:, KV_W + h * HEAD_DIM:KV_W + (h + 1) * HEAD_DIM]

    _on_part(tiles_p, part)


def _shared_kv(kv, tables, n_p):
    m = kv.shape[0]
    tm = PART_TILE
    tab = pl.BlockSpec((tm, HEAD_DIM), lambda i: (i, 0))
    part_p, part_s = _part_specs(n_p, (tm, B_SLOTS, HEAD_DIM))
    cache = lambda rows: jax.ShapeDtypeStruct((rows, B_SLOTS, HEAD_DIM), F32)
    return pl.pallas_call(
        functools.partial(_kv_kernel, tiles_p=n_p // tm),
        grid=(m // tm,),
        in_specs=[pl.BlockSpec((tm, 2 * KV_W), lambda i: (i, 0)), tab, tab, tab],
        out_specs=[pl.BlockSpec((tm, KV_W), lambda i: (i, 0)), part_p, part_s, part_p, part_s],
        out_shape=[jax.ShapeDtypeStruct((m, KV_W), F32), cache(n_p), cache(m - n_p), cache(n_p), cache(m - n_p)],
        compiler_params=_cparams(("arbitrary",)),
        name="shared_kv",
    )(kv, *tables)


DIL_BATCH = 8


def _dil_prompt_kernel(*refs, seq, n_batch):
    os_ref, o_ref = refs[B_GROUPS + 2:B_GROUPS + 4]
    _prompt_then_sample(n_batch, os_ref, o_ref, functools.partial(_dil_sequence, *refs, seq=seq), axis=1)


def _dil_sequence(*refs, seq):
    q_refs = refs[:B_GROUPS]
    k_ref, v_ref, _, o_ref, m_ref, l_ref, acc_ref = refs[B_GROUPS:]
    blk = B_BLOCK
    base = (lax.broadcasted_iota(jnp.int32, (blk, blk), 0) - lax.broadcasted_iota(jnp.int32, (blk, blk), 1))
    for g, (win, dil) in enumerate(B_PATTERNS):
        span = win // dil
        n_blk = seq // dil // blk
        assert span <= blk

        def body(it, carry, g=g, dil=dil, span=span, n_blk=n_blk):
            own, prev, prev_ok = [], [], []
            for j in range(DIL_BATCH):
                t = it * DIL_BATCH + j
                r, n = t % dil, t // dil

                def rows(nn, r=r):
                    start = nn * (blk * dil) + r
                    return pl.ds(pl.multiple_of(start, blk), blk) if dil == 1 else pl.ds(start, blk, stride=dil)

                own.append(rows(n))
                prev.append(rows(jnp.maximum(n - 1, 0)))
                prev_ok.append(base <= jnp.where(n > 0, span - blk, -blk - 1))
            load = lambda ref, idx: jnp.stack([ref[i, :].astype(BF16) for i in idx])
            scores = lambda a, b: jnp.einsum('bqd,bkd->bqk', a, b, preferred_element_type=F32) * ATTN_SCALE
            weighted = lambda p, v: jnp.einsum('bqk,bkd->bqd', p.astype(BF16), v, preferred_element_type=F32)
            q = load(q_refs[g], own)
            s = jnp.where((base >= 0)[None], scores(q, load(k_ref, own)), NEG_BIG)
            m_b = jnp.max(s, axis=-1, keepdims=True)
            if n_blk > 1:
                s_prev = jnp.where(jnp.stack(prev_ok), scores(q, load(k_ref, prev)), NEG_BIG)
                m_b = jnp.maximum(m_b, jnp.max(s_prev, axis=-1, keepdims=True))
            p = jnp.exp(s - m_b)
            l_b = jnp.sum(p, axis=-1, keepdims=True)
            acc_b = weighted(p, load(v_ref, own))
            if n_blk > 1:
                p_prev = jnp.exp(s_prev - m_b)
                l_b = l_b + jnp.sum(p_prev, axis=-1, keepdims=True)
                acc_b = acc_b + weighted(p_prev, load(v_ref, prev))
            for j, idx in enumerate(own):
                if g == 0:
                    m_ref[idx, :] = jnp.broadcast_to(m_b[j], (blk, HEAD_DIM))
                    l_ref[idx, :] = jnp.broadcast_to(l_b[j], (blk, HEAD_DIM))
                    acc_ref[idx, :] = acc_b[j]
                else:
                    m_old = m_ref[idx, :]
                    m_new = jnp.maximum(m_old, m_b[j])
                    a_old = jnp.exp(m_old - m_new)
                    a_b = jnp.exp(m_b[j] - m_new)
                    m_ref[idx, :] = m_new
                    l_ref[idx, :] = l_ref[idx, :] * a_old + l_b[j] * a_b
                    acc_ref[idx, :] = acc_ref[idx, :] * a_old + acc_b[j] * a_b
            return carry

        assert (dil * n_blk) % DIL_BATCH == 0
        lax.fori_loop(0, dil * n_blk // DIL_BATCH, body, 0)
    o_ref[...] = (acc_ref[...] / l_ref[...]).astype(o_ref.dtype)


def _dil_attn_prompt(q, k, kv, o_sample, n_batch, seq):
    ns = o_sample.shape[0]
    assert ns <= seq and q.shape[0] == n_batch * seq + ns
    last = n_batch - 1
    col = lambda c0: pl.BlockSpec((seq, HEAD_DIM), functools.partial(lambda h, b, c0: (jnp.minimum(b, last), c0 + h), c0=c0))
    return pl.pallas_call(
        functools.partial(_dil_prompt_kernel, seq=seq, n_batch=n_batch),
        grid=(B_SLOTS, n_batch + 1),
        in_specs=[col(g * B_SLOTS) for g in range(B_GROUPS)] + [col(0), col(B_SLOTS),
                                                                 pl.BlockSpec((ns, HEAD_DIM), lambda h, b: (0, h))],
        out_specs=pl.BlockSpec((seq, HEAD_DIM), lambda h, b: (b, h)),
        out_shape=jax.ShapeDtypeStruct((q.shape[0], KV_W), BF16),
        scratch_shapes=[pltpu.VMEM((seq, HEAD_DIM), F32)] * 3,
        compiler_params=_cparams(("arbitrary", "arbitrary"), V7X_VMEM_LIMIT_BYTES),
        name="dil_attn_prompt",
    )(*([q] * B_GROUPS), k, kv, o_sample)


MEM_DECODE_SEQS = 4


def _mem_decode_kernel(q_ref, k_ref, v_ref, o_ref):
    n_seq, rows, _ = k_ref.shape
    nh = q_ref.shape[1]
    own = (lax.broadcasted_iota(jnp.int32, (nh, rows), 1) % nh) == lax.broadcasted_iota(jnp.int32, (nh, rows), 0)
    for i in range(n_seq):
        s = _dot_nt(q_ref[i].astype(BF16), k_ref[i].astype(BF16)) * ATTN_SCALE
        o_ref[i] = _softmax_av(jnp.where(own, s, NEG_BIG), v_ref[i].astype(BF16)).astype(o_ref.dtype)


def _mem_attn_decode(q, cache_k, cache_v, layer):
    ns, nh, hd = q.shape
    n_layers, _, mem_len = cache_k.shape[:3]
    g = MEM_DECODE_SEQS
    assert ns % g == 0
    rows = mem_len * nh
    cspec = pl.BlockSpec((None, g, rows, hd), lambda b: (layer, b, 0, 0))
    qspec = pl.BlockSpec((g, nh, hd), lambda b: (b, 0, 0))
    return pl.pallas_call(
        _mem_decode_kernel,
        grid=(ns // g,),
        in_specs=[qspec, cspec, cspec],
        out_specs=qspec,
        out_shape=jax.ShapeDtypeStruct((ns, nh, hd), BF16),
        compiler_params=_cparams(("parallel",)),
        name="mem_attn_decode",
    )(q, cache_k.reshape(n_layers, ns, rows, hd), cache_v.reshape(n_layers, ns, rows, hd))


DIL_DECODE_SEQS = 2
SUBLANES = 8


def _dil_decode_kernel(q_ref, kn_ref, vn_ref, *refs):
    o_ref = refs[-1]
    n_slots = kn_ref.shape[1]
    for i in range(q_ref.shape[0]):
        kn, vn = kn_ref[i], vn_ref[i]
        scores, values = [], []
        for g in range(B_GROUPS):
            k_ref, v_ref = refs[2 * g], refs[2 * g + 1]
            rows = k_ref.shape[1] * (k_ref.shape[2] if len(k_ref.shape) == 4 else 1)
            period = k_ref.shape[2] if len(k_ref.shape) == 4 else n_slots
            k = k_ref[i].reshape(rows, HEAD_DIM).astype(BF16)
            v = v_ref[i].reshape(rows, HEAD_DIM).astype(BF16)
            q = q_ref[i, g]
            own = (lax.broadcasted_iota(jnp.int32, (n_slots, rows), 1) % period
                   == lax.broadcasted_iota(jnp.int32, (n_slots, rows), 0))
            scores.append(jnp.where(own, _dot_nt(q.astype(BF16), k) * ATTN_SCALE, NEG_BIG))
            values.append(v)
            scores.append(jnp.sum(q * kn, axis=-1, keepdims=True) * ATTN_SCALE)
            values.append(None)
        m = functools.reduce(jnp.maximum, [jnp.max(s, axis=-1, keepdims=True) for s in scores])
        l = jnp.zeros_like(m)
        acc = jnp.zeros((n_slots, HEAD_DIM), F32)
        for s, v in zip(scores, values):
            p = jnp.exp(s - m)
            l = l + jnp.sum(p, axis=-1, keepdims=True)
            acc = acc + (p * vn if v is None else _dot(p.astype(BF16), v))
        o_ref[i] = (acc / l).astype(o_ref.dtype)


def _dil_attn_decode(q, k_new, v_new, cache_k, cache_v):
    ns, w_buf, ns_slots, hd = cache_k.shape
    n_seq = DIL_DECODE_SEQS
    assert ns % n_seq == 0
    in_specs = [pl.BlockSpec((n_seq, B_GROUPS, ns_slots, hd), lambda b: (b, 0, 0, 0)),
                pl.BlockSpec((n_seq, ns_slots, hd), lambda b: (b, 0, 0)),
                pl.BlockSpec((n_seq, ns_slots, hd), lambda b: (b, 0, 0))]
    args = [q, k_new, v_new]
    for win, dil in B_PATTERNS:
        span = win // dil
        assert w_buf % dil == 0 and (w_buf // dil) % span == 0
        last = w_buf // dil // span - 1
        if dil == 1:
            view = (ns, w_buf * ns_slots, hd)
            spec = pl.BlockSpec((n_seq, span * ns_slots, hd), functools.partial(lambda b, last: (b, last, 0), last=last))
        else:
            assert dil * ns_slots >= SUBLANES and ns_slots <= SUBLANES
            view = (ns, w_buf // dil, dil * ns_slots, hd)
            spec = pl.BlockSpec((n_seq, span, SUBLANES, hd), functools.partial(lambda b, last: (b, last, 0, 0), last=last))
        in_specs += [spec, spec]
        args += [cache_k.reshape(view), cache_v.reshape(view)]
    return pl.pallas_call(
        _dil_decode_kernel,
        grid=(ns // n_seq,),
        in_specs=in_specs,
        out_specs=pl.BlockSpec((n_seq, ns_slots, hd), lambda b: (b, 0, 0)),
        out_shape=jax.ShapeDtypeStruct((ns, ns_slots, hd), BF16),
        compiler_params=_cparams(("parallel",)),
        name="dil_attn_decode",
    )(*args)


HGRN_STEP_SEQS = 1


def _hgrn_step_kernel(p_ref, s_ref, lb_ref, gn_ref, o_ref, so_ref, rows_ref):
    h = A_HEADS
    hk = h * HEAD_DIM
    gn = gn_ref[...]
    first = lax.broadcasted_iota(jnp.int32, (8, HEAD_DIM), 0) == 0
    for b in range(p_ref.shape[0]):
        rows_ref[b] = jnp.zeros(rows_ref.shape[1:], F32)
        for i in range(h):
            fp = p_ref[b, :, hk + i * HEAD_DIM:hk + (i + 1) * HEAD_DIM]
            lb = lb_ref[i:i + 1, :]
            rows_ref[b, i:i + 1, :] = lb + (1.0 - lb) * _sigmoid(fp)
        cols = rows_ref[b].T
        for i in range(h):
            sl = slice(i * HEAD_DIM, (i + 1) * HEAD_DIM)
            qp = p_ref[b, :, sl]
            v = p_ref[b, :, 2 * hk + i * HEAD_DIM:2 * hk + (i + 1) * HEAD_DIM]
            gp = p_ref[b, :, 3 * hk + i * HEAD_DIM:3 * hk + (i + 1) * HEAD_DIM]
            k = 1.0 - rows_ref[b, i:i + 1, :]
            k8 = jnp.where(first, k, 0.0).astype(BF16)
            v8 = jnp.broadcast_to(v, (8, HEAD_DIM)).astype(BF16)
            kv = lax.dot_general(k8, v8, (((0,), (0,)), ((), ())), preferred_element_type=F32)
            s_new = cols[:, i:i + 1] * s_ref[b, i] + kv
            so_ref[b, i] = s_new
            q8 = jnp.broadcast_to(qp * _sigmoid(qp), (8, HEAD_DIM)).astype(BF16)
            o = _dot(q8, s_new.astype(BF16))[0:1]
            o = o * lax.rsqrt(jnp.mean(o * o, axis=-1, keepdims=True) + RMS_EPS) * gn
            o_ref[b, :, sl] = (o * (gp * _sigmoid(gp))).astype(o_ref.dtype)


def _hgrn_step(proj, state, layer, lb, g_norm):
    ns = proj.shape[0]
    h = A_HEADS
    g = HGRN_STEP_SEQS
    assert ns % g == 0
    sspec = pl.BlockSpec((g, h, HEAD_DIM, HEAD_DIM), lambda b: (b, 0, 0, 0))
    return pl.pallas_call(
        _hgrn_step_kernel,
        grid=(ns // g,),
        in_specs=[pl.BlockSpec((g, 1, proj.shape[2]), lambda b: (b, 0, 0)),
                  pl.BlockSpec((None, g, h, HEAD_DIM, HEAD_DIM), lambda b: (layer, b, 0, 0, 0)),
                  pl.BlockSpec((h, HEAD_DIM), lambda b: (0, 0)),
                  pl.BlockSpec((1, HEAD_DIM), lambda b: (0, 0))],
        out_specs=[pl.BlockSpec((g, 1, h * HEAD_DIM), lambda b: (b, 0, 0)), sspec],
        out_shape=[jax.ShapeDtypeStruct((ns, 1, h * HEAD_DIM), BF16),
                   jax.ShapeDtypeStruct((ns, h, HEAD_DIM, HEAD_DIM), F32)],
        scratch_shapes=[pltpu.VMEM((g, HEAD_DIM, HEAD_DIM), F32)],
        compiler_params=_cparams(("parallel",)),
        name="hgrn_step",
    )(proj, state, lb.reshape(h, HEAD_DIM), g_norm.reshape(1, HEAD_DIM))


def _ffn_kernel(be_ref, nv_ref, od_ref, nu_ref, x_ref, wg_ref, wu_ref, wd_ref, o_ref, wgb_ref, wub_ref, wdb_ref, *, sub):
    i = pl.program_id(0)
    n_valid = nv_ref[i]

    @pl.when((n_valid > 0) & ((i == 0) | (be_ref[i] != be_ref[jnp.maximum(i - 1, 0)])))
    def _():
        wgb_ref[...] = wg_ref[...].astype(BF16)
        wub_ref[...] = wu_ref[...].astype(BF16)
        wdb_ref[...] = wd_ref[...].astype(BF16)

    for j in range(x_ref.shape[0] // sub):
        rows = slice(j * sub, (j + 1) * sub)

        @pl.when(n_valid > j * sub)
        def _(j=j, rows=rows):
            lo, hi = _unpack_halves(x_ref[rows, :])
            c = lo.shape[1]
            keep = lax.broadcasted_iota(jnp.int32, lo.shape, 0) < n_valid - j * sub
            lo = jnp.where(keep, lo, 0.0).astype(BF16)
            hi = jnp.where(keep, hi, 0.0).astype(BF16)
            ff = wgb_ref.shape[1]
            y = None
            for part in range(FFN_COL_SPLIT):
                cs = slice(part * ff // FFN_COL_SPLIT, (part + 1) * ff // FFN_COL_SPLIT)
                hg = _dot(lo, wgb_ref[:c, cs]) + _dot(hi, wgb_ref[c:, cs])
                hu = _dot(lo, wub_ref[:c, cs]) + _dot(hi, wub_ref[c:, cs])
                hid = (hg * _sigmoid(hg) * hu).astype(BF16)
                yp = _dot(hid, wdb_ref[cs, :])
                y = yp if y is None else y + yp
            o_ref[rows, :] = _pack_halves(y)

        @pl.when(n_valid <= j * sub)
        def _(rows=rows):
            o_ref[rows, :] = jnp.zeros((sub, o_ref.shape[1]), o_ref.dtype)


def _expert_ffn(x, blk_exp, blk_valid, blk_order, n_used, w_gate, w_up, w_down, layer, tm, sub, name):
    r = x.shape[0]
    d, ff = w_gate.shape[2:]
    assert tm % sub == 0
    w_spec = lambda shape: pl.BlockSpec((None, None) + shape, lambda i, be, nv, od, nu: (layer, be[i], 0, 0))
    return pl.pallas_call(
        functools.partial(_ffn_kernel, sub=sub),
        grid_spec=pltpu.PrefetchScalarGridSpec(
            num_scalar_prefetch=4,
            grid=(r // tm,),
            in_specs=[pl.BlockSpec((tm, d // 2), lambda i, be, nv, od, nu: (od[jnp.minimum(i, nu[0] - 1)], 0)),
                      w_spec((d, ff)), w_spec((d, ff)), w_spec((ff, d))],
            out_specs=pl.BlockSpec((tm, d // 2), lambda i, be, nv, od, nu: (od[i], 0)),
            scratch_shapes=[pltpu.VMEM((d, ff), BF16), pltpu.VMEM((d, ff), BF16), pltpu.VMEM((ff, d), BF16)],
        ),
        out_shape=jax.ShapeDtypeStruct((r, d // 2), jnp.int32),
        compiler_params=_cparams(("arbitrary",), V7X_VMEM_LIMIT_BYTES),
        name=name,
    )(blk_exp, blk_valid, blk_order, n_used, x, w_gate, w_up, w_down)


def _first_argmax(val, idx, sentinel):
    m = jnp.max(val, axis=0, keepdims=True)
    i = jnp.min(jnp.where(val == m, idx, sentinel), axis=0, keepdims=True)
    return m, i


def _route_tile(x, w, b):
    t = x.shape[0]
    xh = x.astype(BF16)
    xl = (x - xh.astype(F32)).astype(BF16)
    wh = w.astype(BF16)
    wl = (w - wh.astype(F32)).astype(BF16)
    logits = _dot_nt(wh, xh) + (_dot_nt(wl, xh) + _dot_nt(wh, xl))
    scores = _sigmoid(logits)
    biased = scores + b
    gs = N_EXPERTS // N_GROUPS
    neg = -jnp.inf
    eid = lax.broadcasted_iota(jnp.int32, (N_EXPERTS, t), 0)
    sub = lax.broadcasted_iota(jnp.int32, (gs, t), 0)
    grow = lax.broadcasted_iota(jnp.int32, (N_GROUPS, t), 0)
    grp = jnp.zeros((N_GROUPS, t), F32)
    for g in range(N_GROUPS):
        bg = biased[g * gs:(g + 1) * gs]
        m1, i1 = _first_argmax(bg, sub, gs)
        m2 = jnp.max(jnp.where(sub == i1, neg, bg), axis=0, keepdims=True)
        grp = jnp.where(grow == g, m1 + m2, grp)
    chosen = jnp.zeros((N_GROUPS, t), F32)
    for _ in range(TOPK_GROUPS):
        _, gi = _first_argmax(grp, grow, N_GROUPS)
        hit = grow == gi
        chosen = jnp.where(hit, 1.0, chosen)
        grp = jnp.where(hit, neg, grp)
    chosen_e = jnp.concatenate([jnp.broadcast_to(chosen[g:g + 1], (gs, t)) for g in range(N_GROUPS)], axis=0)
    masked = jnp.where(chosen_e > 0.0, biased, neg)
    krow = lax.broadcasted_iota(jnp.int32, (TOP_K, t), 0)
    e_out = jnp.zeros((TOP_K, t), jnp.int32)
    g_out = jnp.zeros((TOP_K, t), F32)
    member = jnp.zeros((N_EXPERTS, t), F32)
    for k in range(TOP_K):
        _, idx = _first_argmax(masked, eid, N_EXPERTS)
        hit = eid == idx
        gk = jnp.sum(jnp.where(hit, scores, 0.0), axis=0, keepdims=True)
        masked = jnp.where(hit, neg, masked)
        member = jnp.where(hit, 1.0, member)
        e_out = jnp.where(krow == k, idx, e_out)
        g_out = jnp.where(krow == k, gk, g_out)
    g_out = g_out / jnp.sum(g_out, axis=0, keepdims=True) * ROUTED_SCALE
    return e_out, g_out, jnp.sum(member, axis=1, keepdims=True).astype(jnp.int32)


def _slots_kernel(e_ref, base_ref, o_ref):
    e = e_ref[...]
    t = e.shape[1]
    eid = lax.broadcasted_iota(jnp.int32, (N_EXPERTS, t), 0)
    member = jnp.zeros((N_EXPERTS, t), F32)
    for k in range(TOP_K):
        member = jnp.where(eid == e[k:k + 1], 1.0, member)
    before = (lax.broadcasted_iota(jnp.int32, (t, t), 0) < lax.broadcasted_iota(jnp.int32, (t, t), 1))
    rank = _dot(member.astype(BF16), jnp.where(before, 1.0, 0.0).astype(BF16))
    slot = rank + base_ref[...].astype(F32)
    krow = lax.broadcasted_iota(jnp.int32, (TOP_K, t), 0)
    out = jnp.zeros((TOP_K, t), F32)
    for k in range(TOP_K):
        sk = jnp.sum(jnp.where(eid == e[k:k + 1], slot, 0.0), axis=0, keepdims=True)
        out = jnp.where(krow == k, sk, out)
    o_ref[...] = out.astype(jnp.int32)


def _dispatch_plan(e_idx, counts, tm, tile):
    k, n = e_idx.shape
    n_blocks = -(-(n * k) // tm) + N_EXPERTS
    counts = counts.reshape(n // tile, N_EXPERTS)
    total = jnp.sum(counts, axis=0)
    padded = (total + tm - 1) // tm * tm
    pad_end = jnp.cumsum(padded)
    tile_base = (pad_end - padded)[None, :] + jnp.cumsum(counts, axis=0) - counts
    slot_of = pl.pallas_call(
        _slots_kernel,
        grid=(n // tile,),
        in_specs=[pl.BlockSpec((k, tile), lambda i: (0, i)),
                  pl.BlockSpec((None, N_EXPERTS, 1), lambda i: (i, 0, 0))],
        out_specs=pl.BlockSpec((k, tile), lambda i: (0, i)),
        out_shape=jax.ShapeDtypeStruct((k, n), jnp.int32),
        compiler_params=_cparams(("parallel",)),
        name="slots",
    )(e_idx, tile_base.astype(jnp.int32).reshape(n // tile, N_EXPERTS, 1))
    pos = jnp.arange(n_blocks, dtype=jnp.int32)
    blk_exp = jnp.minimum(jnp.sum((pad_end[None, :] <= (pos * tm)[:, None]).astype(jnp.int32), axis=1), N_EXPERTS - 1)
    is_exp = blk_exp[:, None] == jnp.arange(N_EXPERTS, dtype=blk_exp.dtype)[None, :]
    per_blk = lambda v: jnp.sum(jnp.where(is_exp, v[None, :], 0), axis=1)
    n_used = jnp.maximum(pad_end[-1:] // tm, 1).astype(jnp.int32)
    first, count = per_blk((pad_end - padded) // tm), jnp.maximum(per_blk(padded // tm), 1)
    blk_order = jnp.where(pos < n_used[0], first + (pos - first + count - 1) % count, pos)
    blk_valid = jnp.clip(per_blk(pad_end - padded + total) - blk_order * tm, 0, tm)
    return (slot_of, blk_exp.astype(jnp.int32), blk_valid.astype(jnp.int32), blk_order.astype(jnp.int32), n_used,
            n_blocks)


V7X_SC_CORES = 2
V7X_SC_SUBCORES = 16
SC_WORKERS = V7X_SC_CORES * V7X_SC_SUBCORES
SC_CHUNK = 32
SC_GATHER_CHUNK = 40


def _sc_mesh():
    return plsc.VectorSubcoreMesh(core_axis_name="c", subcore_axis_name="s",
                                  num_cores=V7X_SC_CORES, num_subcores=V7X_SC_SUBCORES)


def _sc_worker_id():
    return lax.axis_index("s") * V7X_SC_CORES + lax.axis_index("c")


def _sc_gather_rows(table, slot_of):
    k, n = slot_of.shape
    w = table.shape[1]
    n_rows = k * n
    ch = SC_GATHER_CHUNK
    per_worker = n_rows // (SC_WORKERS * ch)
    assert per_worker * SC_WORKERS * ch == n_rows and per_worker % 2 == 0
    idx = slot_of.reshape(SC_WORKERS, per_worker, ch)

    def body(table_hbm, idx_hbm, out_hbm, idx_v, rows_v, sems):
        wid = _sc_worker_id()
        pltpu.sync_copy(idx_hbm.at[wid], idx_v)

        def gather(c, slot):
            return pltpu.make_async_copy(table_hbm.at[idx_v.at[c]], rows_v.at[slot], sems.at[slot])

        gather(0, 0).start()

        @pl.loop(0, per_worker, step=2)
        def _(c0):
            for slot in range(2):
                c = c0 + slot
                gather(c, slot).wait()

                @pl.when(c + 1 < per_worker)
                def _():
                    gather(c + 1, 1 - slot).start()

                row0 = pl.multiple_of((wid * per_worker + c) * ch, 8)
                pltpu.sync_copy(rows_v.at[slot], out_hbm.at[pl.ds(row0, ch)])

    return pl.kernel(
        body, out_type=jax.ShapeDtypeStruct((n_rows, w), table.dtype), mesh=_sc_mesh(),
        scratch_types=[pltpu.VMEM((per_worker, ch), jnp.int32), pltpu.VMEM((2, ch, w), table.dtype),
                       pltpu.SemaphoreType.DMA((2,))],
        name="sc_gather_rows",
    )(table, idx)


def _sc_scatter_rows(x, slot_of, n_slots):
    k, n = slot_of.shape
    w = x.shape[1]
    n_chunks = n // SC_CHUNK
    assert n_chunks * SC_CHUNK == n
    idx = slot_of.reshape(k, n_chunks, SC_CHUNK).transpose(1, 0, 2)
    rounds = -(-n_chunks // SC_WORKERS)

    def body(x_hbm, idx_hbm, out_hbm, idx_v, rows_v, sem):
        wid = _sc_worker_id()

        @pl.loop(0, rounds)
        def _(j):
            q = j * SC_WORKERS + wid

            @pl.when(q < n_chunks)
            def _():
                pltpu.sync_copy(idx_hbm.at[q], idx_v)
                pltpu.sync_copy(x_hbm.at[pl.ds(pl.multiple_of(q * SC_CHUNK, SC_CHUNK), SC_CHUNK)], rows_v)
                copies = [pltpu.make_async_copy(rows_v, out_hbm.at[idx_v.at[kk]], sem) for kk in range(k)]
                for cp in copies:
                    cp.start()
                for cp in copies:
                    cp.wait()

    return pl.kernel(
        body, out_type=jax.ShapeDtypeStruct((n_slots, w), x.dtype), mesh=_sc_mesh(),
        scratch_types=[pltpu.VMEM((k, SC_CHUNK), jnp.int32), pltpu.VMEM((SC_CHUNK, w), x.dtype),
                       pltpu.SemaphoreType.DMA],
        name="sc_scatter_rows",
    )(x, idx)


MOE_BLOCK = 768
MOE_SUB = 384
FFN_COL_SPLIT = 2
ROW_TILE = 640


def _moe_postnorm_kernel(x_ref, y_ref, gate_ref, sh_ref, g_ref, b_ref, ofp_ref, ofs_ref, ob_ref, *, alpha, tiles_p):
    def part(which):
        of_ref = (ofp_ref, ofs_ref)[which]
        gate = gate_ref[...]
        lo, hi = _unpack_halves(sh_ref[...])
        for k in range(TOP_K):
            lo_k, hi_k = _unpack_halves(y_ref[k])
            lo = lo + gate[:, k:k + 1] * lo_k
            hi = hi + gate[:, k:k + 1] * hi_k
        z = alpha * x_ref[...] + jnp.concatenate([lo, hi], axis=1)
        out = _layer_norm(z, g_ref[...], b_ref[...])
        of_ref[...] = out
        ob_ref[...] = out.astype(BF16)

    _on_part(tiles_p, part)


def _moe_postnorm(x, n_p, y_tok, gate, shared, g, b, alpha):
    m, d = x.shape
    tm = PART_TILE
    row = pl.BlockSpec((tm, d), lambda i: (i, 0))
    vec = pl.BlockSpec((1, d), lambda i: (0, 0))
    part_p, part_s = _part_specs(n_p, (tm, d))
    return pl.pallas_call(
        functools.partial(_moe_postnorm_kernel, alpha=alpha, tiles_p=n_p // tm),
        grid=(m // tm,),
        in_specs=[row, pl.BlockSpec((TOP_K, tm, d // 2), lambda i: (0, i, 0)),
                  pl.BlockSpec((tm, TOP_K), lambda i: (i, 0)), pl.BlockSpec((tm, d // 2), lambda i: (i, 0)), vec, vec],
        out_specs=[part_p, part_s, row],
        out_shape=[jax.ShapeDtypeStruct((n_p, d), F32), jax.ShapeDtypeStruct((m - n_p, d), F32),
                   jax.ShapeDtypeStruct((m, d), BF16)],
        compiler_params=_cparams(("arbitrary",), V7X_VMEM_LIMIT_BYTES),
        name="postnorm_moe",
    )(x, y_tok, gate, shared, g.reshape(1, d), b.reshape(1, d))


def _moe(xf, n_p, xp, routing, layer, w_gate, w_up, w_down, sw_gate, sw_up, sw_down, ln_g, ln_b, alpha):
    n, d = xf.shape
    e_idx, gate, counts = routing
    counts = jnp.sum(counts.reshape(n // ROW_TILE, ROW_TILE // PART_TILE, N_EXPERTS), axis=1)
    slot_of, blk_exp, blk_valid, blk_order, n_used, n_blocks = _dispatch_plan(e_idx, counts, MOE_BLOCK, ROW_TILE)
    x_sorted = _sc_scatter_rows(xp, slot_of, n_blocks * MOE_BLOCK)
    y_sorted = _expert_ffn(x_sorted, blk_exp, blk_valid, blk_order, n_used, w_gate, w_up, w_down, layer,
                           MOE_BLOCK, MOE_SUB, "routed_ffn")
    y_tok = _sc_gather_rows(y_sorted, slot_of).reshape(TOP_K, n, d // 2)
    n_sh = n // ROW_TILE
    shared = _expert_ffn(xp, jnp.zeros((n_sh,), jnp.int32), jnp.full((n_sh,), ROW_TILE, jnp.int32),
                         jnp.arange(n_sh, dtype=jnp.int32), jnp.full((1,), n_sh, jnp.int32),
                         sw_gate[:, None], sw_up[:, None], sw_down[:, None], layer, ROW_TILE, ROW_TILE, "shared_ffn")
    return _moe_postnorm(xf, n_p, y_tok, gate.T, shared, ln_g, ln_b, alpha)


def kernel(x_prompt, x_sample, state_hgrn, cache_win_k, cache_win_v, cache_mem_k, cache_mem_v, mem_prompt,
           w_in_a, lb_logits, g_norm_a, w_out_a, w_in_b, w_out_b, w_kv_shared, w_mem_kv, ln_g, ln_b,
           router_w, router_b, exp_w_gate, exp_w_up, exp_w_down, sh_w_gate, sh_w_up, sh_w_down):
    bp, sp, d = x_prompt.shape
    ns = x_sample.shape[0]
    assert x_sample.shape[1] == 1
    depth = ln_g.shape[0]
    n_a = w_in_a.shape[0]
    alpha = (2 * depth) ** 0.25
    n_p = bp * sp
    mem_len = mem_prompt.shape[1]
    a_mix = 4 * A_HEADS * HEAD_DIM

    xf_p, xf_s = x_prompt.reshape(n_p, d), x_sample.reshape(ns, d)
    xb = jnp.concatenate([xf_p.astype(BF16), xf_s.astype(BF16)], axis=0)
    lower_bounds = jnp.cumsum(jax.nn.softmax(lb_logits.astype(F32), axis=0), axis=0)
    mem_flat = mem_prompt.reshape(bp * mem_len, d)
    pos_all = jnp.concatenate([jnp.tile(jnp.arange(sp, dtype=jnp.int32), bp),
                               jnp.full((ns,), PAST_LEN, jnp.int32)])
    tables = _rope_tables(pos_all)

    hgrn_p, hgrn_s, mem_k_p, mem_v_p = [], [], [], []
    for layer in range(depth):
        kvm = _proj([mem_flat], w_mem_kv, layer, F32, 2 * mem_len, 1024, "mem_kv")
        mem_k_p.append(kvm[:, :MEM_W].reshape(bp, mem_len, MEM_HEADS, HEAD_DIM))
        mem_v_p.append(kvm[:, MEM_W:].reshape(bp, mem_len, MEM_HEADS, HEAD_DIM))
        if layer < n_a:
            a = layer
            proj = _proj([xb], w_in_a, a, F32, ROW_TILE, 1664, "proj_in_a")
            proj_s = proj[n_p:]
            o_x_s, st_s = _hgrn_step(proj_s.reshape(ns, 1, -1), state_hgrn, a, lower_bounds[a], g_norm_a[a])
            o_m_s = _mem_attn_decode(proj_s[:, a_mix:].reshape(ns, MEM_HEADS, HEAD_DIM), cache_mem_k, cache_mem_v, layer)
            o_x, st_p = _hgrn_prompt(proj, lower_bounds[a], g_norm_a[a], o_x_s.reshape(ns, -1), bp, sp)
            o_m = _mem_attn_prompt(proj, a_mix // MEM_W, kvm, o_m_s.reshape(ns, -1), bp, sp, 512)
            hgrn_p.append(st_p)
            hgrn_s.append(st_s)
            w_out, w_out_layer = w_out_a, a
        else:
            bl = layer - n_a
            if layer == n_a:
                kv = _proj([xb], w_kv_shared[None], 0, F32, ROW_TILE, 1024, "proj_kv")
                k_r, k_p, k_s, v_p, v_s = _shared_kv(kv, tables, n_p)
            proj = _proj([xb], w_in_b, bl, F32, ROW_TILE, 1024, "proj_in_b", tables, B_QHEADS)
            q_r = proj
            q_s = proj[n_p:, :B_QHEADS * HEAD_DIM].reshape(ns, B_GROUPS, B_SLOTS, HEAD_DIM)
            o_x_s = _dil_attn_decode(q_s, k_s, v_s, cache_win_k, cache_win_v)
            o_m_s = _mem_attn_decode(proj[n_p:, B_QHEADS * HEAD_DIM:].reshape(ns, MEM_HEADS, HEAD_DIM),
                                     cache_mem_k, cache_mem_v, layer)
            o_x = _dil_attn_prompt(q_r, k_r, kv, o_x_s.reshape(ns, -1), bp, sp)
            o_m = _mem_attn_prompt(proj, B_QHEADS * HEAD_DIM // MEM_W, kvm, o_m_s.reshape(ns, -1), bp, sp, 512)
            w_out, w_out_layer = w_out_b, bl
        y = _proj([o_x, o_m], w_out, w_out_layer, F32, ROW_TILE, 1024, "proj_out")
        xf, xp, *routing = _postnorm_route(xf_p, xf_s, y, ln_g[layer, 0], ln_b[layer, 0], alpha,
                                           router_w, layer, router_b, "postnorm_route")
        xf_p, xf_s, xb = _moe(xf, n_p, xp, routing, layer, exp_w_gate, exp_w_up, exp_w_down,
                              sh_w_gate, sh_w_up, sh_w_down, ln_g[layer, 1], ln_b[layer, 1], alpha)

    w_p = min(max(w for w, _ in B_PATTERNS), sp)
    k_p = k_p.reshape(bp, sp, B_SLOTS, HEAD_DIM)
    v_p = v_p.reshape(bp, sp, B_SLOTS, HEAD_DIM)
    return (xf_p.reshape(bp, sp, d), xf_s.reshape(ns, 1, d),
            jnp.stack(hgrn_p), jnp.stack(hgrn_s),
            k_p[:, sp - w_p:], v_p[:, sp - w_p:],
            k_s.reshape(ns, 1, B_SLOTS, HEAD_DIM), v_s.reshape(ns, 1, B_SLOTS, HEAD_DIM),
            jnp.stack(mem_k_p), jnp.stack(mem_v_p))
```

```python
import functools

import jax
import jax.numpy as jnp
from jax import lax
from jax.experimental import pallas as pl
from jax.experimental.pallas import tpu as pltpu
from jax.experimental.pallas import tpu_sc as plsc

F32 = jnp.float32
BF16 = jnp.bfloat16

HEAD_DIM = 128
A_HEADS = 12
A_CHUNK = 64
A_SUB = 16
B_PATTERNS = ((128, 1), (512, 4), (2048, 16))
B_SLOTS = 4
B_GROUPS = len(B_PATTERNS)
B_QHEADS = B_GROUPS * B_SLOTS
B_BLOCK = 128
MEM_HEADS = 4
MEM_W = MEM_HEADS * HEAD_DIM
KV_W = B_SLOTS * HEAD_DIM
ROPE_THETA = 500000.0
ROPE_DIM = HEAD_DIM // 4
N_EXPERTS = 64
N_GROUPS = 8
TOPK_GROUPS = 4
TOP_K = 8
ROUTED_SCALE = 2.5
LN_EPS = 1e-5
RMS_EPS = 1e-6
ATTN_SCALE = HEAD_DIM ** -0.5
PAST_LEN = 2048

V7X_VMEM_LIMIT_BYTES = 56 * 1024 * 1024
V7X_SC_CORES = 2
V7X_SC_SUBCORES = 16
SUBLANES = 8

ROW_TILE = 640
PART_TILE = 128
A_HEADS_PER_STEP = 4
A_CHUNK_UNROLL = 4
DIL_BATCH = 8
MEM_DECODE_SEQS = 4
DIL_DECODE_SEQS = 2
MOE_BLOCK = 768
MOE_SUB = 384
SC_WORKERS = V7X_SC_CORES * V7X_SC_SUBCORES
SC_CHUNK = 32
SC_GATHER_CHUNK = 40

NEG_BIG = -1e30


def _cparams(sem, vmem=None):
    return pltpu.CompilerParams(dimension_semantics=sem, vmem_limit_bytes=vmem)


def _dot(a, b):
    return jnp.dot(a, b, preferred_element_type=F32)


def _dot_nt(a, b):
    return lax.dot_general(a, b, (((1,), (1,)), ((), ())), preferred_element_type=F32)


def _sigmoid(x):
    return 1.0 / (1.0 + jnp.exp(-x))


def _proj_kernel(*refs, n_lhs, rope_heads):
    x_refs = refs[:n_lhs]
    w_refs = refs[n_lhs:2 * n_lhs]
    n_tab = 3 if rope_heads else 0
    tab_refs = refs[2 * n_lhs:2 * n_lhs + n_tab]
    o_ref = refs[2 * n_lhs + n_tab]
    wb_refs = refs[2 * n_lhs + n_tab + 1:]

    @pl.when(pl.program_id(1) == 0)
    def _():
        for w_ref, wb_ref in zip(w_refs, wb_refs):
            wb_ref[...] = w_ref[...].astype(BF16)

    acc = None
    for x_ref, wb_ref in zip(x_refs, wb_refs):
        d = _dot(x_ref[...].astype(BF16), wb_ref[...])
        acc = d if acc is None else acc + d
    if not rope_heads:
        o_ref[...] = acc.astype(o_ref.dtype)
    else:
        c, su, sd = (t[...] for t in tab_refs)
        heads_per_tile = o_ref.shape[1] // HEAD_DIM
        for h in range(heads_per_tile):
            sl = slice(h * HEAD_DIM, (h + 1) * HEAD_DIM)
            plain = acc[:, sl]
            roped = _rope_head(plain, c, su, sd)
            is_roped = pl.program_id(0) * heads_per_tile + h < rope_heads
            o_ref[:, sl] = jnp.where(is_roped, roped, plain).astype(o_ref.dtype)


def _proj(lhs, w, layer, out_dtype, tm, tn, name, rope_tables=None, rope_heads=0):
    m = lhs[0].shape[0]
    n = w.shape[2]
    koff = 0
    in_specs, w_specs, scratch = [], [], []
    for x in lhs:
        k = x.shape[1]
        assert koff % k == 0 and m % tm == 0 and n % tn == 0
        in_specs.append(pl.BlockSpec((tm, k), lambda j, i: (i, 0)))
        w_specs.append(pl.BlockSpec((None, k, tn), functools.partial(lambda j, i, kb: (layer, kb, j), kb=koff // k)))
        scratch.append(pltpu.VMEM((k, tn), BF16))
        koff += k
    assert koff == w.shape[1]
    tables = list(rope_tables) if rope_heads else []
    tab_specs = [pl.BlockSpec((tm, HEAD_DIM), lambda j, i: (i, 0))] * len(tables)
    return pl.pallas_call(
        functools.partial(_proj_kernel, n_lhs=len(lhs), rope_heads=rope_heads),
        grid=(n // tn, m // tm),
        in_specs=in_specs + w_specs + tab_specs,
        out_specs=pl.BlockSpec((tm, tn), lambda j, i: (i, j)),
        out_shape=jax.ShapeDtypeStruct((m, n), out_dtype),
        scratch_shapes=scratch,
        compiler_params=_cparams(("arbitrary", "arbitrary"), V7X_VMEM_LIMIT_BYTES),
        name=name,
    )(*lhs, *([w] * len(lhs)), *tables)


def _pack_halves(x):
    c = x.shape[1] // 2
    lo = pltpu.bitcast(x[:, :c].astype(BF16).astype(F32), jnp.int32)
    hi = pltpu.bitcast(x[:, c:].astype(BF16).astype(F32), jnp.int32)
    return hi | lax.shift_right_logical(lo, 16)


def _unpack_halves(w):
    lo = pltpu.bitcast(lax.shift_left(w, 16), F32)
    hi = pltpu.bitcast(w & jnp.int32(-65536), F32)
    return lo, hi


def _layer_norm(z, g, b):
    mu = jnp.mean(z, axis=-1, keepdims=True)
    zc = z - mu
    var = jnp.mean(zc * zc, axis=-1, keepdims=True)
    return zc * lax.rsqrt(var + LN_EPS) * g + b


def _part_specs(n_p, block):
    tiles_p = n_p // PART_TILE
    rest = (0,) * (len(block) - 1)
    return (pl.BlockSpec(block, lambda i: (jnp.minimum(i, tiles_p - 1),) + rest),
            pl.BlockSpec(block, lambda i: (jnp.maximum(i - tiles_p, 0),) + rest))


def _on_part(tiles_p, fn):
    i = pl.program_id(0)
    pl.when(i < tiles_p)(functools.partial(fn, 0))
    pl.when(i >= tiles_p)(functools.partial(fn, 1))


def _postnorm_kernel(xp_ref, xs_ref, y_ref, g_ref, b_ref, wt_ref, rb_ref, of_ref, op_ref, e_ref, gate_ref, cnt_ref,
                     *, alpha, tiles_p):
    def part(which):
        x_ref = (xp_ref, xs_ref)[which]
        out = _layer_norm(alpha * x_ref[...] + y_ref[...], g_ref[...], b_ref[...])
        of_ref[...] = out
        op_ref[...] = _pack_halves(out)
        e_ref[...], gate_ref[...], cnt_ref[...] = _route_tile(out, wt_ref[...], rb_ref[...])

    _on_part(tiles_p, part)


def _postnorm_route(x_p, x_s, y, g, b, alpha, router_w, layer, router_b, name):
    n_p, d = x_p.shape
    m = n_p + x_s.shape[0]
    e = router_w.shape[2]
    t = PART_TILE
    row = pl.BlockSpec((t, d), lambda i: (i, 0))
    vec = pl.BlockSpec((1, d), lambda i: (0, 0))
    tok = pl.BlockSpec((TOP_K, t), lambda i: (0, i))
    part_p, part_s = _part_specs(n_p, (t, d))
    return pl.pallas_call(
        functools.partial(_postnorm_kernel, alpha=alpha, tiles_p=n_p // t),
        grid=(m // t,),
        in_specs=[part_p, part_s, row, vec, vec,
                  pl.BlockSpec((None, e, d), lambda i: (layer, 0, 0)), pl.BlockSpec((e, 1), lambda i: (0, 0))],
        out_specs=[row, pl.BlockSpec((t, d // 2), lambda i: (i, 0)), tok, tok,
                   pl.BlockSpec((None, e, 1), lambda i: (i, 0, 0))],
        out_shape=[jax.ShapeDtypeStruct((m, d), F32), jax.ShapeDtypeStruct((m, d // 2), jnp.int32),
                   jax.ShapeDtypeStruct((TOP_K, m), jnp.int32), jax.ShapeDtypeStruct((TOP_K, m), F32),
                   jax.ShapeDtypeStruct((m // t, e, 1), jnp.int32)],
        compiler_params=_cparams(("arbitrary",)),
        name=name,
    )(x_p, x_s, y, g.reshape(1, d), b.reshape(1, d), jnp.swapaxes(router_w, 1, 2),
      router_b[layer].astype(F32).reshape(e, 1))


def _hgrn_prompt_kernel(q_ref, f_ref, v_ref, gate_ref, lb_ref, gn_ref, os_ref, o_ref, s_ref, st_ref,
                        *, seq, heads, n_batch):
    _prompt_then_sample(n_batch, os_ref, o_ref,
                        functools.partial(_hgrn_sequence, q_ref, f_ref, v_ref, gate_ref, lb_ref, gn_ref,
                                          o_ref, s_ref, st_ref, seq=seq, heads=heads), axis=1)


def _hgrn_sequence(q_ref, f_ref, v_ref, gate_ref, lb_ref, gn_ref, o_ref, s_ref, st_ref, *, seq, heads):
    c = A_CHUNK
    n_chunks = seq // c
    n_sub = c // A_SUB
    gn = gn_ref[...]
    row = lax.broadcasted_iota(jnp.int32, (c, c), 0)
    col = lax.broadcasted_iota(jnp.int32, (c, c), 1)
    tril = jnp.where(row >= col, 1.0, 0.0).astype(BF16)
    same_sub = (row // A_SUB) == (col // A_SUB)
    diag_dist = jnp.where(same_sub, row - col, -1)
    off_mask = col < (row // A_SUB) * A_SUB
    st_ref[...] = jnp.zeros_like(st_ref)

    def chunk(ci, carry):
        for hh in range(heads):
            head_chunk(ci, hh)
        return carry

    def head_chunk(ci, hh):
        r0 = pl.multiple_of(ci * c, c)
        hsl = slice(hh * HEAD_DIM, (hh + 1) * HEAD_DIM)
        lb = lb_ref[hh]
        qp = q_ref[pl.ds(r0, c), hsl]
        fp = f_ref[pl.ds(r0, c), hsl]
        v = v_ref[pl.ds(r0, c), hsl]
        gp = gate_ref[pl.ds(r0, c), hsl]
        st = st_ref[hh]
        q = qp * _sigmoid(qp)
        forget = lb + (1.0 - lb) * _sigmoid(fp)
        logf = jnp.log2(forget)
        k = 1.0 - forget
        hi = logf.astype(BF16)
        r1 = logf - hi.astype(F32)
        mid = r1.astype(BF16)
        lo = (r1 - mid.astype(F32)).astype(BF16)
        g = _dot(tril, hi) + _dot(tril, mid) + _dot(tril, lo)
        v_b = v.astype(BF16)
        o = _dot_nt((q * jnp.exp2(g)).astype(BF16), st.astype(BF16))
        rows = [jnp.zeros((A_SUB, c), F32)]
        for i in range(1, n_sub):
            gref = g[i * A_SUB:i * A_SUB + 1, :]
            qt = q[i * A_SUB:(i + 1) * A_SUB, :] * jnp.exp2(g[i * A_SUB:(i + 1) * A_SUB, :] - gref)
            kt = k * jnp.exp2(jnp.minimum(gref - g, 0.0))
            rows.append(_dot_nt(qt.astype(BF16), kt.astype(BF16)))
        a = jnp.where(off_mask, jnp.concatenate(rows, axis=0), 0.0)
        for d in range(A_SUB):
            kr = k if d == 0 else pltpu.roll(k, d, axis=0)
            gr = g if d == 0 else pltpu.roll(g, d, axis=0)
            x = q * kr * jnp.exp2(g - gr)
            a = jnp.where(diag_dist == d, jnp.sum(x, axis=-1, keepdims=True), a)
        o = o + _dot(a.astype(BF16), v_b)
        gend = g[c - 1:c, :]
        kt_end = k * jnp.exp2(gend - g)
        st_ref[hh] = jnp.exp2(gend) * st + _dot(v_b.T, kt_end.astype(BF16))
        o = o * lax.rsqrt(jnp.mean(o * o, axis=-1, keepdims=True) + RMS_EPS) * gn
        o_ref[pl.ds(r0, c), hsl] = (o * (gp * _sigmoid(gp))).astype(o_ref.dtype)

    lax.fori_loop(0, n_chunks, chunk, 0, unroll=A_CHUNK_UNROLL)
    for hh in range(heads):
        s_ref[hh] = st_ref[hh].T


def _hgrn_prompt(proj, lb, g_norm, o_sample, n_batch, seq):
    h = A_HEADS
    hp = A_HEADS_PER_STEP
    ns = o_sample.shape[0]
    assert h % hp == 0 and ns <= seq and proj.shape[0] == n_batch * seq + ns
    ng = h // hp
    w = hp * HEAD_DIM
    last = n_batch - 1
    blk = lambda off: pl.BlockSpec((seq, w), functools.partial(lambda hg, b, off: (jnp.minimum(b, last), off + hg), off=off))
    return pl.pallas_call(
        functools.partial(_hgrn_prompt_kernel, seq=seq, heads=hp, n_batch=n_batch),
        grid=(ng, n_batch + 1),
        in_specs=[blk(0), blk(ng), blk(2 * ng), blk(3 * ng),
                  pl.BlockSpec((hp, 1, HEAD_DIM), lambda hg, b: (hg, 0, 0)),
                  pl.BlockSpec((1, HEAD_DIM), lambda hg, b: (0, 0)),
                  pl.BlockSpec((ns, w), lambda hg, b: (0, hg))],
        out_specs=[pl.BlockSpec((seq, w), lambda hg, b: (b, hg)),
                   pl.BlockSpec((None, hp, HEAD_DIM, HEAD_DIM), lambda hg, b: (jnp.minimum(b, last), hg, 0, 0))],
        out_shape=[jax.ShapeDtypeStruct((proj.shape[0], h * HEAD_DIM), BF16),
                   jax.ShapeDtypeStruct((n_batch, h, HEAD_DIM, HEAD_DIM), F32)],
        scratch_shapes=[pltpu.VMEM((hp, HEAD_DIM, HEAD_DIM), F32)],
        compiler_params=_cparams(("arbitrary", "arbitrary"), V7X_VMEM_LIMIT_BYTES),
        name="hgrn_prompt",
    )(proj, proj, proj, proj, lb.reshape(h, 1, HEAD_DIM), g_norm.reshape(1, HEAD_DIM), o_sample)


def _softmax_av(s, v):
    m = jnp.max(s, axis=-1, keepdims=True)
    p = jnp.exp(s - m)
    l = jnp.sum(p, axis=-1, keepdims=True)
    return _dot(p.astype(BF16), v) / l


def _prompt_then_sample(n_prompt_steps, os_ref, o_ref, prompt_step, axis=0):
    i = pl.program_id(axis)
    pl.when(i < n_prompt_steps)(prompt_step)

    @pl.when(i >= n_prompt_steps)
    def _():
        o_ref[:os_ref.shape[0], :] = os_ref[...]


def _mem_attn_kernel(q_ref, k_ref, v_ref, os_ref, o_ref, *, n_steps):
    def step():
        for h in range(MEM_HEADS):
            sl = slice(h * HEAD_DIM, (h + 1) * HEAD_DIM)
            s = _dot_nt(q_ref[:, sl].astype(BF16), k_ref[:, sl].astype(BF16)) * ATTN_SCALE
            o_ref[:, sl] = _softmax_av(s, v_ref[:, sl].astype(BF16)).astype(o_ref.dtype)

    _prompt_then_sample(n_steps, os_ref, o_ref, step)


def _mem_attn_prompt(proj, q_col, kv, o_sample, n_batch, seq, tq):
    mem_len = kv.shape[0] // n_batch
    nq = seq // tq
    n_steps = n_batch * nq
    ns = o_sample.shape[0]
    assert ns <= tq and proj.shape[0] == n_batch * seq + ns
    last = n_steps - 1
    return pl.pallas_call(
        functools.partial(_mem_attn_kernel, n_steps=n_steps),
        grid=(n_steps + 1,),
        in_specs=[pl.BlockSpec((tq, MEM_W), lambda i: (jnp.minimum(i, last), q_col)),
                  pl.BlockSpec((mem_len, MEM_W), lambda i: (jnp.minimum(i, last) // nq, 0)),
                  pl.BlockSpec((mem_len, MEM_W), lambda i: (jnp.minimum(i, last) // nq, 1)),
                  pl.BlockSpec((ns, MEM_W), lambda i: (0, 0))],
        out_specs=pl.BlockSpec((tq, MEM_W), lambda i: (i, 0)),
        out_shape=jax.ShapeDtypeStruct((proj.shape[0], MEM_W), BF16),
        compiler_params=_cparams(("arbitrary",)),
        name="mem_attn_prompt",
    )(proj, kv, kv, o_sample)


def _rope_tables(pos):
    half = ROPE_DIM // 2
    inv_freq = ROPE_THETA ** (-jnp.arange(0, ROPE_DIM, 2, dtype=F32) / ROPE_DIM)
    ang = pos.astype(F32)[:, None] * inv_freq[None, :]
    cos, sin = jnp.cos(ang), jnp.sin(ang)
    n = pos.shape[0]
    one = jnp.ones((n, HEAD_DIM - ROPE_DIM), F32)
    zero = jnp.zeros((n, HEAD_DIM - half), F32)
    c = jnp.concatenate([cos, cos, one], axis=1)
    s_up = jnp.concatenate([-sin, zero], axis=1)
    s_dn = jnp.concatenate([jnp.zeros((n, half), F32), sin, zero[:, half:]], axis=1)
    return c, s_up, s_dn


def _rope_head(x, c, su, sd):
    half = ROPE_DIM // 2
    up = pltpu.roll(x, HEAD_DIM - half, axis=1)
    dn = pltpu.roll(x, half, axis=1)
    return x * c + up * su + dn * sd


def _kv_kernel(kv_ref, c_ref, su_ref, sd_ref, k_ref, kp_ref, ks_ref, vp_ref, vs_ref, *, tiles_p):
    c, su, sd = c_ref[...], su_ref[...], sd_ref[...]

    def part(which):
        k4_ref, v4_ref = ((kp_ref, vp_ref), (ks_ref, vs_ref))[which]
        for h in range(B_SLOTS):
            sl = slice(h * HEAD_DIM, (h + 1) * HEAD_DIM)
            k = _rope_head(kv_ref[:, sl], c, su, sd)
            k_ref[:, sl] = k
            k4_ref[:, h, :] = k
            v4_ref[:, h, :] = kv_ref[:, KV_W + h * HEAD_DIM:KV_W + (h + 1) * HEAD_DIM]

    _on_part(tiles_p, part)


def _shared_kv(kv, tables, n_p):
    m = kv.shape[0]
    tm = PART_TILE
    tab = pl.BlockSpec((tm, HEAD_DIM), lambda i: (i, 0))
    part_p, part_s = _part_specs(n_p, (tm, B_SLOTS, HEAD_DIM))
    cache = lambda rows: jax.ShapeDtypeStruct((rows, B_SLOTS, HEAD_DIM), F32)
    return pl.pallas_call(
        functools.partial(_kv_kernel, tiles_p=n_p // tm),
        grid=(m // tm,),
        in_specs=[pl.BlockSpec((tm, 2 * KV_W), lambda i: (i, 0)), tab, tab, tab],
        out_specs=[pl.BlockSpec((tm, KV_W), lambda i: (i, 0)), part_p, part_s, part_p, part_s],
        out_shape=[jax.ShapeDtypeStruct((m, KV_W), F32), cache(n_p), cache(m - n_p), cache(n_p), cache(m - n_p)],
        compiler_params=_cparams(("arbitrary",)),
        name="shared_kv",
    )(kv, *tables)


def _dil_prompt_kernel(*refs, seq, n_batch):
    os_ref, o_ref = refs[B_GROUPS + 2:B_GROUPS + 4]
    _prompt_then_sample(n_batch, os_ref, o_ref, functools.partial(_dil_sequence, *refs, seq=seq), axis=1)


def _dil_sequence(*refs, seq):
    q_refs = refs[:B_GROUPS]
    k_ref, v_ref, _, o_ref, m_ref, l_ref, acc_ref = refs[B_GROUPS:]
    blk = B_BLOCK
    base = (lax.broadcasted_iota(jnp.int32, (blk, blk), 0) - lax.broadcasted_iota(jnp.int32, (blk, blk), 1))
    for g, (win, dil) in enumerate(B_PATTERNS):
        span = win // dil
        n_blk = seq // dil // blk
        assert span <= blk

        def body(it, carry, g=g, dil=dil, span=span, n_blk=n_blk):
            own, prev, prev_ok = [], [], []
            for j in range(DIL_BATCH):
                t = it * DIL_BATCH + j
                r, n = t % dil, t // dil

                def rows(nn, r=r):
                    start = nn * (blk * dil) + r
                    return pl.ds(pl.multiple_of(start, blk), blk) if dil == 1 else pl.ds(start, blk, stride=dil)

                own.append(rows(n))
                prev.append(rows(jnp.maximum(n - 1, 0)))
                prev_ok.append(base <= jnp.where(n > 0, span - blk, -blk - 1))
            load = lambda ref, idx: jnp.stack([ref[i, :].astype(BF16) for i in idx])
            scores = lambda a, b: jnp.einsum('bqd,bkd->bqk', a, b, preferred_element_type=F32) * ATTN_SCALE
            weighted = lambda p, v: jnp.einsum('bqk,bkd->bqd', p.astype(BF16), v, preferred_element_type=F32)
            q = load(q_refs[g], own)
            s = jnp.where((base >= 0)[None], scores(q, load(k_ref, own)), NEG_BIG)
            m_b = jnp.max(s, axis=-1, keepdims=True)
            if n_blk > 1:
                s_prev = jnp.where(jnp.stack(prev_ok), scores(q, load(k_ref, prev)), NEG_BIG)
                m_b = jnp.maximum(m_b, jnp.max(s_prev, axis=-1, keepdims=True))
            p = jnp.exp(s - m_b)
            l_b = jnp.sum(p, axis=-1, keepdims=True)
            acc_b = weighted(p, load(v_ref, own))
            if n_blk > 1:
                p_prev = jnp.exp(s_prev - m_b)
                l_b = l_b + jnp.sum(p_prev, axis=-1, keepdims=True)
                acc_b = acc_b + weighted(p_prev, load(v_ref, prev))
            for j, idx in enumerate(own):
                if g == 0:
                    m_ref[idx, :] = jnp.broadcast_to(m_b[j], (blk, HEAD_DIM))
                    l_ref[idx, :] = jnp.broadcast_to(l_b[j], (blk, HEAD_DIM))
                    acc_ref[idx, :] = acc_b[j]
                else:
                    m_old = m_ref[idx, :]
                    m_new = jnp.maximum(m_old, m_b[j])
                    a_old = jnp.exp(m_old - m_new)
                    a_b = jnp.exp(m_b[j] - m_new)
                    m_ref[idx, :] = m_new
                    l_ref[idx, :] = l_ref[idx, :] * a_old + l_b[j] * a_b
                    acc_ref[idx, :] = acc_ref[idx, :] * a_old + acc_b[j] * a_b
            return carry

        assert (dil * n_blk) % DIL_BATCH == 0
        lax.fori_loop(0, dil * n_blk // DIL_BATCH, body, 0)
    o_ref[...] = (acc_ref[...] / l_ref[...]).astype(o_ref.dtype)


def _dil_attn_prompt(q, k, kv, o_sample, n_batch, seq):
    ns = o_sample.shape[0]
    assert ns <= seq and q.shape[0] == n_batch * seq + ns
    last = n_batch - 1
    col = lambda c0: pl.BlockSpec((seq, HEAD_DIM), functools.partial(lambda h, b, c0: (jnp.minimum(b, last), c0 + h), c0=c0))
    return pl.pallas_call(
        functools.partial(_dil_prompt_kernel, seq=seq, n_batch=n_batch),
        grid=(B_SLOTS, n_batch + 1),
        in_specs=[col(g * B_SLOTS) for g in range(B_GROUPS)] + [col(0), col(B_SLOTS),
                                                                 pl.BlockSpec((ns, HEAD_DIM), lambda h, b: (0, h))],
        out_specs=pl.BlockSpec((seq, HEAD_DIM), lambda h, b: (b, h)),
        out_shape=jax.ShapeDtypeStruct((q.shape[0], KV_W), BF16),
        scratch_shapes=[pltpu.VMEM((seq, HEAD_DIM), F32)] * 3,
        compiler_params=_cparams(("arbitrary", "arbitrary"), V7X_VMEM_LIMIT_BYTES),
        name="dil_attn_prompt",
    )(*([q] * B_GROUPS), k, kv, o_sample)


def _mem_decode_kernel(q_ref, k_ref, v_ref, o_ref):
    n_seq, rows, _ = k_ref.shape
    nh = q_ref.shape[1]
    own = (lax.broadcasted_iota(jnp.int32, (nh, rows), 1) % nh) == lax.broadcasted_iota(jnp.int32, (nh, rows), 0)
    for i in range(n_seq):
        s = _dot_nt(q_ref[i].astype(BF16), k_ref[i].astype(BF16)) * ATTN_SCALE
        o_ref[i] = _softmax_av(jnp.where(own, s, NEG_BIG), v_ref[i].astype(BF16)).astype(o_ref.dtype)


def _mem_attn_decode(q, cache_k, cache_v, layer):
    ns, nh, hd = q.shape
    n_layers, _, mem_len = cache_k.shape[:3]
    g = MEM_DECODE_SEQS
    assert ns % g == 0
    rows = mem_len * nh
    cspec = pl.BlockSpec((None, g, rows, hd), lambda b: (layer, b, 0, 0))
    qspec = pl.BlockSpec((g, nh, hd), lambda b: (b, 0, 0))
    return pl.pallas_call(
        _mem_decode_kernel,
        grid=(ns // g,),
        in_specs=[qspec, cspec, cspec],
        out_specs=qspec,
        out_shape=jax.ShapeDtypeStruct((ns, nh, hd), BF16),
        compiler_params=_cparams(("parallel",)),
        name="mem_attn_decode",
    )(q, cache_k.reshape(n_layers, ns, rows, hd), cache_v.reshape(n_layers, ns, rows, hd))


def _dil_decode_kernel(q_ref, kn_ref, vn_ref, *refs):
    o_ref = refs[-1]
    n_slots = kn_ref.shape[1]
    for i in range(q_ref.shape[0]):
        kn, vn = kn_ref[i], vn_ref[i]
        scores, values = [], []
        for g in range(B_GROUPS):
            k_ref, v_ref = refs[2 * g], refs[2 * g + 1]
            rows = k_ref.shape[1] * (k_ref.shape[2] if len(k_ref.shape) == 4 else 1)
            period = k_ref.shape[2] if len(k_ref.shape) == 4 else n_slots
            k = k_ref[i].reshape(rows, HEAD_DIM).astype(BF16)
            v = v_ref[i].reshape(rows, HEAD_DIM).astype(BF16)
            q = q_ref[i, g]
            own = (lax.broadcasted_iota(jnp.int32, (n_slots, rows), 1) % period
                   == lax.broadcasted_iota(jnp.int32, (n_slots, rows), 0))
            scores.append(jnp.where(own, _dot_nt(q.astype(BF16), k) * ATTN_SCALE, NEG_BIG))
            values.append(v)
            scores.append(jnp.sum(q * kn, axis=-1, keepdims=True) * ATTN_SCALE)
            values.append(None)
        m = functools.reduce(jnp.maximum, [jnp.max(s, axis=-1, keepdims=True) for s in scores])
        l = jnp.zeros_like(m)
        acc = jnp.zeros((n_slots, HEAD_DIM), F32)
        for s, v in zip(scores, values):
            p = jnp.exp(s - m)
            l = l + jnp.sum(p, axis=-1, keepdims=True)
            acc = acc + (p * vn if v is None else _dot(p.astype(BF16), v))
        o_ref[i] = (acc / l).astype(o_ref.dtype)


def _dil_attn_decode(q, k_new, v_new, cache_k, cache_v):
    ns, w_buf, ns_slots, hd = cache_k.shape
    n_seq = DIL_DECODE_SEQS
    assert ns % n_seq == 0
    in_specs = [pl.BlockSpec((n_seq, B_GROUPS, ns_slots, hd), lambda b: (b, 0, 0, 0)),
                pl.BlockSpec((n_seq, ns_slots, hd), lambda b: (b, 0, 0)),
                pl.BlockSpec((n_seq, ns_slots, hd), lambda b: (b, 0, 0))]
    args = [q, k_new, v_new]
    for win, dil in B_PATTERNS:
        span = win // dil
        assert w_buf % dil == 0 and (w_buf // dil) % span == 0
        last = w_buf // dil // span - 1
        if dil == 1:
            view = (ns, w_buf * ns_slots, hd)
            spec = pl.BlockSpec((n_seq, span * ns_slots, hd), functools.partial(lambda b, last: (b, last, 0), last=last))
        else:
            assert dil * ns_slots >= SUBLANES and ns_slots <= SUBLANES
            view = (ns, w_buf // dil, dil * ns_slots, hd)
            spec = pl.BlockSpec((n_seq, span, SUBLANES, hd), functools.partial(lambda b, last: (b, last, 0, 0), last=last))
        in_specs += [spec, spec]
        args += [cache_k.reshape(view), cache_v.reshape(view)]
    return pl.pallas_call(
        _dil_decode_kernel,
        grid=(ns // n_seq,),
        in_specs=in_specs,
        out_specs=pl.BlockSpec((n_seq, ns_slots, hd), lambda b: (b, 0, 0)),
        out_shape=jax.ShapeDtypeStruct((ns, ns_slots, hd), BF16),
        compiler_params=_cparams(("parallel",)),
        name="dil_attn_decode",
    )(*args)


def _hgrn_step_kernel(p_ref, s_ref, lb_ref, gn_ref, o_ref, so_ref, rows_ref):
    h = A_HEADS
    hk = h * HEAD_DIM
    gn = gn_ref[...]
    rows_ref[...] = jnp.zeros_like(rows_ref)
    for i in range(h):
        fp = p_ref[:, hk + i * HEAD_DIM:hk + (i + 1) * HEAD_DIM]
        lb = lb_ref[i:i + 1, :]
        rows_ref[i:i + 1, :] = lb + (1.0 - lb) * _sigmoid(fp)
    cols = rows_ref[...].T
    first = lax.broadcasted_iota(jnp.int32, (SUBLANES, HEAD_DIM), 0) == 0
    for i in range(h):
        sl = slice(i * HEAD_DIM, (i + 1) * HEAD_DIM)
        qp = p_ref[:, sl]
        v = p_ref[:, 2 * hk + i * HEAD_DIM:2 * hk + (i + 1) * HEAD_DIM]
        gp = p_ref[:, 3 * hk + i * HEAD_DIM:3 * hk + (i + 1) * HEAD_DIM]
        k = 1.0 - rows_ref[i:i + 1, :]
        k8 = jnp.where(first, k, 0.0).astype(BF16)
        v8 = jnp.broadcast_to(v, (SUBLANES, HEAD_DIM)).astype(BF16)
        kv = lax.dot_general(k8, v8, (((0,), (0,)), ((), ())), preferred_element_type=F32)
        s_new = cols[:, i:i + 1] * s_ref[i] + kv
        so_ref[i] = s_new
        q8 = jnp.broadcast_to(qp * _sigmoid(qp), (SUBLANES, HEAD_DIM)).astype(BF16)
        o = _dot(q8, s_new.astype(BF16))[0:1]
        o = o * lax.rsqrt(jnp.mean(o * o, axis=-1, keepdims=True) + RMS_EPS) * gn
        o_ref[:, sl] = (o * (gp * _sigmoid(gp))).astype(o_ref.dtype)


def _hgrn_step(proj, state, layer, lb, g_norm):
    ns = proj.shape[0]
    h = A_HEADS
    sspec = pl.BlockSpec((None, h, HEAD_DIM, HEAD_DIM), lambda b: (b, 0, 0, 0))
    return pl.pallas_call(
        _hgrn_step_kernel,
        grid=(ns,),
        in_specs=[pl.BlockSpec((None, 1, proj.shape[2]), lambda b: (b, 0, 0)),
                  pl.BlockSpec((None, None, h, HEAD_DIM, HEAD_DIM), lambda b: (layer, b, 0, 0, 0)),
                  pl.BlockSpec((h, HEAD_DIM), lambda b: (0, 0)),
                  pl.BlockSpec((1, HEAD_DIM), lambda b: (0, 0))],
        out_specs=[pl.BlockSpec((None, 1, h * HEAD_DIM), lambda b: (b, 0, 0)), sspec],
        out_shape=[jax.ShapeDtypeStruct((ns, 1, h * HEAD_DIM), BF16),
                   jax.ShapeDtypeStruct((ns, h, HEAD_DIM, HEAD_DIM), F32)],
        scratch_shapes=[pltpu.VMEM((HEAD_DIM, HEAD_DIM), F32)],
        compiler_params=_cparams(("parallel",)),
        name="hgrn_step",
    )(proj, state, lb.reshape(h, HEAD_DIM), g_norm.reshape(1, HEAD_DIM))


def _ffn_kernel(be_ref, nv_ref, od_ref, nu_ref, x_ref, wg_ref, wu_ref, wd_ref, o_ref, wgb_ref, wub_ref, wdb_ref, *, sub):
    i = pl.program_id(0)
    n_valid = nv_ref[i]

    @pl.when((n_valid > 0) & ((i == 0) | (be_ref[i] != be_ref[jnp.maximum(i - 1, 0)])))
    def _():
        wgb_ref[...] = wg_ref[...].astype(BF16)
        wub_ref[...] = wu_ref[...].astype(BF16)
        wdb_ref[...] = wd_ref[...].astype(BF16)

    for j in range(x_ref.shape[0] // sub):
        rows = slice(j * sub, (j + 1) * sub)

        @pl.when(n_valid > j * sub)
        def _(j=j, rows=rows):
            lo, hi = _unpack_halves(x_ref[rows, :])
            c = lo.shape[1]
            keep = lax.broadcasted_iota(jnp.int32, lo.shape, 0) < n_valid - j * sub
            lo = jnp.where(keep, lo, 0.0).astype(BF16)
            hi = jnp.where(keep, hi, 0.0).astype(BF16)
            hg = _dot(lo, wgb_ref[:c, :]) + _dot(hi, wgb_ref[c:, :])
            hu = _dot(lo, wub_ref[:c, :]) + _dot(hi, wub_ref[c:, :])
            hid = hg * _sigmoid(hg) * hu
            o_ref[rows, :] = _pack_halves(_dot(hid.astype(BF16), wdb_ref[...]))

        @pl.when(n_valid <= j * sub)
        def _(rows=rows):
            o_ref[rows, :] = jnp.zeros((sub, o_ref.shape[1]), o_ref.dtype)


def _expert_ffn(x, blk_exp, blk_valid, blk_order, n_used, w_gate, w_up, w_down, layer, tm, sub, name):
    r = x.shape[0]
    d, ff = w_gate.shape[2:]
    assert tm % sub == 0
    w_spec = lambda shape: pl.BlockSpec((None, None) + shape, lambda i, be, nv, od, nu: (layer, be[i], 0, 0))
    return pl.pallas_call(
        functools.partial(_ffn_kernel, sub=sub),
        grid_spec=pltpu.PrefetchScalarGridSpec(
            num_scalar_prefetch=4,
            grid=(r // tm,),
            in_specs=[pl.BlockSpec((tm, d // 2), lambda i, be, nv, od, nu: (od[jnp.minimum(i, nu[0] - 1)], 0)),
                      w_spec((d, ff)), w_spec((d, ff)), w_spec((ff, d))],
            out_specs=pl.BlockSpec((tm, d // 2), lambda i, be, nv, od, nu: (od[i], 0)),
            scratch_shapes=[pltpu.VMEM((d, ff), BF16), pltpu.VMEM((d, ff), BF16), pltpu.VMEM((ff, d), BF16)],
        ),
        out_shape=jax.ShapeDtypeStruct((r, d // 2), jnp.int32),
        compiler_params=_cparams(("arbitrary",), V7X_VMEM_LIMIT_BYTES),
        name=name,
    )(blk_exp, blk_valid, blk_order, n_used, x, w_gate, w_up, w_down)


def _first_argmax(val, idx, sentinel):
    m = jnp.max(val, axis=0, keepdims=True)
    i = jnp.min(jnp.where(val == m, idx, sentinel), axis=0, keepdims=True)
    return m, i


def _route_tile(x, w, b):
    t = x.shape[0]
    xh = x.astype(BF16)
    xl = (x - xh.astype(F32)).astype(BF16)
    wh = w.astype(BF16)
    wl = (w - wh.astype(F32)).astype(BF16)
    logits = _dot_nt(wh, xh) + (_dot_nt(wl, xh) + _dot_nt(wh, xl))
    scores = _sigmoid(logits)
    biased = scores + b
    gs = N_EXPERTS // N_GROUPS
    neg = -jnp.inf
    eid = lax.broadcasted_iota(jnp.int32, (N_EXPERTS, t), 0)
    sub = lax.broadcasted_iota(jnp.int32, (gs, t), 0)
    grow = lax.broadcasted_iota(jnp.int32, (N_GROUPS, t), 0)
    grp = jnp.zeros((N_GROUPS, t), F32)
    for g in range(N_GROUPS):
        bg = biased[g * gs:(g + 1) * gs]
        m1, i1 = _first_argmax(bg, sub, gs)
        m2 = jnp.max(jnp.where(sub == i1, neg, bg), axis=0, keepdims=True)
        grp = jnp.where(grow == g, m1 + m2, grp)
    chosen = jnp.zeros((N_GROUPS, t), F32)
    for _ in range(TOPK_GROUPS):
        _, gi = _first_argmax(grp, grow, N_GROUPS)
        hit = grow == gi
        chosen = jnp.where(hit, 1.0, chosen)
        grp = jnp.where(hit, neg, grp)
    chosen_e = jnp.concatenate([jnp.broadcast_to(chosen[g:g + 1], (gs, t)) for g in range(N_GROUPS)], axis=0)
    masked = jnp.where(chosen_e > 0.0, biased, neg)
    krow = lax.broadcasted_iota(jnp.int32, (TOP_K, t), 0)
    e_out = jnp.zeros((TOP_K, t), jnp.int32)
    g_out = jnp.zeros((TOP_K, t), F32)
    member = jnp.zeros((N_EXPERTS, t), F32)
    for k in range(TOP_K):
        _, idx = _first_argmax(masked, eid, N_EXPERTS)
        hit = eid == idx
        gk = jnp.sum(jnp.where(hit, scores, 0.0), axis=0, keepdims=True)
        masked = jnp.where(hit, neg, masked)
        member = jnp.where(hit, 1.0, member)
        e_out = jnp.where(krow == k, idx, e_out)
        g_out = jnp.where(krow == k, gk, g_out)
    g_out = g_out / jnp.sum(g_out, axis=0, keepdims=True) * ROUTED_SCALE
    return e_out, g_out, jnp.sum(member, axis=1, keepdims=True).astype(jnp.int32)


def _slots_kernel(e_ref, base_ref, o_ref):
    e = e_ref[...]
    t = e.shape[1]
    eid = lax.broadcasted_iota(jnp.int32, (N_EXPERTS, t), 0)
    member = jnp.zeros((N_EXPERTS, t), F32)
    for k in range(TOP_K):
        member = jnp.where(eid == e[k:k + 1], 1.0, member)
    before = (lax.broadcasted_iota(jnp.int32, (t, t), 0) < lax.broadcasted_iota(jnp.int32, (t, t), 1))
    rank = _dot(member.astype(BF16), jnp.where(before, 1.0, 0.0).astype(BF16))
    slot = rank + base_ref[...].astype(F32)
    krow = lax.broadcasted_iota(jnp.int32, (TOP_K, t), 0)
    out = jnp.zeros((TOP_K, t), F32)
    for k in range(TOP_K):
        sk = jnp.sum(jnp.where(eid == e[k:k + 1], slot, 0.0), axis=0, keepdims=True)
        out = jnp.where(krow == k, sk, out)
    o_ref[...] = out.astype(jnp.int32)


def _dispatch_plan(e_idx, counts, tm, tile):
    k, n = e_idx.shape
    n_blocks = -(-(n * k) // tm) + N_EXPERTS
    counts = counts.reshape(n // tile, N_EXPERTS)
    total = jnp.sum(counts, axis=0)
    padded = (total + tm - 1) // tm * tm
    pad_end = jnp.cumsum(padded)
    tile_base = (pad_end - padded)[None, :] + jnp.cumsum(counts, axis=0) - counts
    slot_of = pl.pallas_call(
        _slots_kernel,
        grid=(n // tile,),
        in_specs=[pl.BlockSpec((k, tile), lambda i: (0, i)),
                  pl.BlockSpec((None, N_EXPERTS, 1), lambda i: (i, 0, 0))],
        out_specs=pl.BlockSpec((k, tile), lambda i: (0, i)),
        out_shape=jax.ShapeDtypeStruct((k, n), jnp.int32),
        compiler_params=_cparams(("parallel",)),
        name="slots",
    )(e_idx, tile_base.astype(jnp.int32).reshape(n // tile, N_EXPERTS, 1))
    pos = jnp.arange(n_blocks, dtype=jnp.int32)
    blk_exp = jnp.minimum(jnp.sum((pad_end[None, :] <= (pos * tm)[:, None]).astype(jnp.int32), axis=1), N_EXPERTS - 1)
    is_exp = blk_exp[:, None] == jnp.arange(N_EXPERTS, dtype=blk_exp.dtype)[None, :]
    per_blk = lambda v: jnp.sum(jnp.where(is_exp, v[None, :], 0), axis=1)
    n_used = jnp.maximum(pad_end[-1:] // tm, 1).astype(jnp.int32)
    first, count = per_blk((pad_end - padded) // tm), jnp.maximum(per_blk(padded // tm), 1)
    blk_order = jnp.where(pos < n_used[0], first + (pos - first + count - 1) % count, pos)
    blk_valid = jnp.clip(per_blk(pad_end - padded + total) - blk_order * tm, 0, tm)
    return (slot_of, blk_exp.astype(jnp.int32), blk_valid.astype(jnp.int32), blk_order.astype(jnp.int32), n_used,
            n_blocks)


def _sc_mesh():
    return plsc.VectorSubcoreMesh(core_axis_name="c", subcore_axis_name="s",
                                  num_cores=V7X_SC_CORES, num_subcores=V7X_SC_SUBCORES)


def _sc_worker_id():
    return lax.axis_index("s") * V7X_SC_CORES + lax.axis_index("c")


def _sc_gather_rows(table, slot_of):
    k, n = slot_of.shape
    w = table.shape[1]
    n_rows = k * n
    ch = SC_GATHER_CHUNK
    per_worker = n_rows // (SC_WORKERS * ch)
    assert per_worker * SC_WORKERS * ch == n_rows and per_worker % 2 == 0
    idx = slot_of.reshape(SC_WORKERS, per_worker, ch)

    def body(table_hbm, idx_hbm, out_hbm, idx_v, rows_v, sems):
        wid = _sc_worker_id()
        pltpu.sync_copy(idx_hbm.at[wid], idx_v)

        def gather(c, slot):
            return pltpu.make_async_copy(table_hbm.at[idx_v.at[c]], rows_v.at[slot], sems.at[slot])

        gather(0, 0).start()

        @pl.loop(0, per_worker, step=2)
        def _(c0):
            for slot in range(2):
                c = c0 + slot
                gather(c, slot).wait()

                @pl.when(c + 1 < per_worker)
                def _():
                    gather(c + 1, 1 - slot).start()

                row0 = pl.multiple_of((wid * per_worker + c) * ch, 8)
                pltpu.sync_copy(rows_v.at[slot], out_hbm.at[pl.ds(row0, ch)])

    return pl.kernel(
        body, out_type=jax.ShapeDtypeStruct((n_rows, w), table.dtype), mesh=_sc_mesh(),
        scratch_types=[pltpu.VMEM((per_worker, ch), jnp.int32), pltpu.VMEM((2, ch, w), table.dtype),
                       pltpu.SemaphoreType.DMA((2,))],
        name="sc_gather_rows",
    )(table, idx)


def _sc_scatter_rows(x, slot_of, n_slots):
    k, n = slot_of.shape
    w = x.shape[1]
    n_chunks = n // SC_CHUNK
    assert n_chunks * SC_CHUNK == n
    idx = slot_of.reshape(k, n_chunks, SC_CHUNK).transpose(1, 0, 2)
    rounds = -(-n_chunks // SC_WORKERS)

    def body(x_hbm, idx_hbm, out_hbm, idx_v, rows_v, sem):
        wid = _sc_worker_id()

        @pl.loop(0, rounds)
        def _(j):
            q = j * SC_WORKERS + wid

            @pl.when(q < n_chunks)
            def _():
                pltpu.sync_copy(idx_hbm.at[q], idx_v)
                pltpu.sync_copy(x_hbm.at[pl.ds(pl.multiple_of(q * SC_CHUNK, SC_CHUNK), SC_CHUNK)], rows_v)
                copies = [pltpu.make_async_copy(rows_v, out_hbm.at[idx_v.at[kk]], sem) for kk in range(k)]
                for cp in copies:
                    cp.start()
                for cp in copies:
                    cp.wait()

    return pl.kernel(
        body, out_type=jax.ShapeDtypeStruct((n_slots, w), x.dtype), mesh=_sc_mesh(),
        scratch_types=[pltpu.VMEM((k, SC_CHUNK), jnp.int32), pltpu.VMEM((SC_CHUNK, w), x.dtype),
                       pltpu.SemaphoreType.DMA],
        name="sc_scatter_rows",
    )(x, idx)


def _moe_postnorm_kernel(x_ref, y_ref, gate_ref, sh_ref, g_ref, b_ref, ofp_ref, ofs_ref, ob_ref, *, alpha, tiles_p):
    def part(which):
        of_ref = (ofp_ref, ofs_ref)[which]
        gate = gate_ref[...]
        lo, hi = _unpack_halves(sh_ref[...])
        for k in range(TOP_K):
            lo_k, hi_k = _unpack_halves(y_ref[k])
            lo = lo + gate[:, k:k + 1] * lo_k
            hi = hi + gate[:, k:k + 1] * hi_k
        z = alpha * x_ref[...] + jnp.concatenate([lo, hi], axis=1)
        out = _layer_norm(z, g_ref[...], b_ref[...])
        of_ref[...] = out
        ob_ref[...] = out.astype(BF16)

    _on_part(tiles_p, part)


def _moe_postnorm(x, n_p, y_tok, gate, shared, g, b, alpha):
    m, d = x.shape
    tm = PART_TILE
    row = pl.BlockSpec((tm, d), lambda i: (i, 0))
    vec = pl.BlockSpec((1, d), lambda i: (0, 0))
    part_p, part_s = _part_specs(n_p, (tm, d))
    return pl.pallas_call(
        functools.partial(_moe_postnorm_kernel, alpha=alpha, tiles_p=n_p // tm),
        grid=(m // tm,),
        in_specs=[row, pl.BlockSpec((TOP_K, tm, d // 2), lambda i: (0, i, 0)),
                  pl.BlockSpec((tm, TOP_K), lambda i: (i, 0)), pl.BlockSpec((tm, d // 2), lambda i: (i, 0)), vec, vec],
        out_specs=[part_p, part_s, row],
        out_shape=[jax.ShapeDtypeStruct((n_p, d), F32), jax.ShapeDtypeStruct((m - n_p, d), F32),
                   jax.ShapeDtypeStruct((m, d), BF16)],
        compiler_params=_cparams(("arbitrary",), V7X_VMEM_LIMIT_BYTES),
        name="postnorm_moe",
    )(x, y_tok, gate, shared, g.reshape(1, d), b.reshape(1, d))


def _moe(xf, n_p, xp, routing, layer, w_gate, w_up, w_down, sw_gate, sw_up, sw_down, ln_g, ln_b, alpha):
    n, d = xf.shape
    e_idx, gate, counts = routing
    counts = jnp.sum(counts.reshape(n // ROW_TILE, ROW_TILE // PART_TILE, N_EXPERTS), axis=1)
    slot_of, blk_exp, blk_valid, blk_order, n_used, n_blocks = _dispatch_plan(e_idx, counts, MOE_BLOCK, ROW_TILE)
    x_sorted = _sc_scatter_rows(xp, slot_of, n_blocks * MOE_BLOCK)
    y_sorted = _expert_ffn(x_sorted, blk_exp, blk_valid, blk_order, n_used, w_gate, w_up, w_down, layer,
                           MOE_BLOCK, MOE_SUB, "routed_ffn")
    y_tok = _sc_gather_rows(y_sorted, slot_of).reshape(TOP_K, n, d // 2)
    n_sh = n // ROW_TILE
    shared = _expert_ffn(xp, jnp.zeros((n_sh,), jnp.int32), jnp.full((n_sh,), ROW_TILE, jnp.int32),
                         jnp.arange(n_sh, dtype=jnp.int32), jnp.full((1,), n_sh, jnp.int32),
                         sw_gate[:, None], sw_up[:, None], sw_down[:, None], layer, ROW_TILE, ROW_TILE, "shared_ffn")
    return _moe_postnorm(xf, n_p, y_tok, gate.T, shared, ln_g, ln_b, alpha)


def kernel(x_prompt, x_sample, state_hgrn, cache_win_k, cache_win_v, cache_mem_k, cache_mem_v, mem_prompt,
           w_in_a, lb_logits, g_norm_a, w_out_a, w_in_b, w_out_b, w_kv_shared, w_mem_kv, ln_g, ln_b,
           router_w, router_b, exp_w_gate, exp_w_up, exp_w_down, sh_w_gate, sh_w_up, sh_w_down):
    bp, sp, d = x_prompt.shape
    ns = x_sample.shape[0]
    assert x_sample.shape[1] == 1
    depth = ln_g.shape[0]
    n_a = w_in_a.shape[0]
    alpha = (2 * depth) ** 0.25
    n_p = bp * sp
    mem_len = mem_prompt.shape[1]
    a_mix = 4 * A_HEADS * HEAD_DIM

    xf_p, xf_s = x_prompt.reshape(n_p, d), x_sample.reshape(ns, d)
    xb = jnp.concatenate([xf_p.astype(BF16), xf_s.astype(BF16)], axis=0)
    lower_bounds = jnp.cumsum(jax.nn.softmax(lb_logits.astype(F32), axis=0), axis=0)
    mem_flat = mem_prompt.reshape(bp * mem_len, d)
    pos_all = jnp.concatenate([jnp.tile(jnp.arange(sp, dtype=jnp.int32), bp),
                               jnp.full((ns,), PAST_LEN, jnp.int32)])
    tables = _rope_tables(pos_all)

    hgrn_p, hgrn_s, mem_k_p, mem_v_p = [], [], [], []
    for layer in range(depth):
        kvm = _proj([mem_flat], w_mem_kv, layer, F32, 2 * mem_len, 1024, "mem_kv")
        mem_k_p.append(kvm[:, :MEM_W].reshape(bp, mem_len, MEM_HEADS, HEAD_DIM))
        mem_v_p.append(kvm[:, MEM_W:].reshape(bp, mem_len, MEM_HEADS, HEAD_DIM))
        if layer < n_a:
            a = layer
            proj = _proj([xb], w_in_a, a, F32, ROW_TILE, 1664, "proj_in_a")
            proj_s = proj[n_p:]
            o_x_s, st_s = _hgrn_step(proj_s.reshape(ns, 1, -1), state_hgrn, a, lower_bounds[a], g_norm_a[a])
            o_m_s = _mem_attn_decode(proj_s[:, a_mix:].reshape(ns, MEM_HEADS, HEAD_DIM), cache_mem_k, cache_mem_v, layer)
            o_x, st_p = _hgrn_prompt(proj, lower_bounds[a], g_norm_a[a], o_x_s.reshape(ns, -1), bp, sp)
            o_m = _mem_attn_prompt(proj, a_mix // MEM_W, kvm, o_m_s.reshape(ns, -1), bp, sp, 512)
            hgrn_p.append(st_p)
            hgrn_s.append(st_s)
            w_out, w_out_layer = w_out_a, a
        else:
            bl = layer - n_a
            if layer == n_a:
                kv = _proj([xb], w_kv_shared[None], 0, F32, ROW_TILE, 1024, "proj_kv")
                k_r, k_p, k_s, v_p, v_s = _shared_kv(kv, tables, n_p)
            proj = _proj([xb], w_in_b, bl, F32, ROW_TILE, 1024, "proj_in_b", tables, B_QHEADS)
            q_s = proj[n_p:, :B_QHEADS * HEAD_DIM].reshape(ns, B_GROUPS, B_SLOTS, HEAD_DIM)
            o_x_s = _dil_attn_decode(q_s, k_s, v_s, cache_win_k, cache_win_v)
            o_m_s = _mem_attn_decode(proj[n_p:, B_QHEADS * HEAD_DIM:].reshape(ns, MEM_HEADS, HEAD_DIM),
                                     cache_mem_k, cache_mem_v, layer)
            o_x = _dil_attn_prompt(proj, k_r, kv, o_x_s.reshape(ns, -1), bp, sp)
            o_m = _mem_attn_prompt(proj, B_QHEADS * HEAD_DIM // MEM_W, kvm, o_m_s.reshape(ns, -1), bp, sp, 512)
            w_out, w_out_layer = w_out_b, bl
        y = _proj([o_x, o_m], w_out, w_out_layer, F32, ROW_TILE, 1024, "proj_out")
        xf, xp, *routing = _postnorm_route(xf_p, xf_s, y, ln_g[layer, 0], ln_b[layer, 0], alpha,
                                           router_w, layer, router_b, "postnorm_route")
        xf_p, xf_s, xb = _moe(xf, n_p, xp, routing, layer, exp_w_gate, exp_w_up, exp_w_down,
                              sh_w_gate, sh_w_up, sh_w_down, ln_g[layer, 1], ln_b[layer, 1], alpha)

    w_p = min(max(w for w, _ in B_PATTERNS), sp)
    k_p = k_p.reshape(bp, sp, B_SLOTS, HEAD_DIM)
    v_p = v_p.reshape(bp, sp, B_SLOTS, HEAD_DIM)
    return (xf_p.reshape(bp, sp, d), xf_s.reshape(ns, 1, d),
            jnp.stack(hgrn_p), jnp.stack(hgrn_s),
            k_p[:, sp - w_p:], v_p[:, sp - w_p:],
            k_s.reshape(ns, 1, B_SLOTS, HEAD_DIM), v_s.reshape(ns, 1, B_SLOTS, HEAD_DIM),
            jnp.stack(mem_k_p), jnp.stack(mem_v_p))
```

```python
import functools

import jax
import jax.numpy as jnp
from jax import lax
from jax.experimental import pallas as pl
from jax.experimental.pallas import tpu as pltpu
from jax.experimental.pallas import tpu_sc as plsc

F32 = jnp.float32
BF16 = jnp.bfloat16

HEAD_DIM = 128
A_HEADS = 12
A_CHUNK = 64
A_SUB = 16
B_PATTERNS = ((128, 1), (512, 4), (2048, 16))
B_SLOTS = 4
B_GROUPS = len(B_PATTERNS)
B_QHEADS = B_GROUPS * B_SLOTS
B_BLOCK = 128
MEM_HEADS = 4
MEM_W = MEM_HEADS * HEAD_DIM
KV_W = B_SLOTS * HEAD_DIM
ROPE_THETA = 500000.0
ROPE_DIM = HEAD_DIM // 4
N_EXPERTS = 64
N_GROUPS = 8
TOPK_GROUPS = 4
TOP_K = 8
ROUTED_SCALE = 2.5
LN_EPS = 1e-5
RMS_EPS = 1e-6
ATTN_SCALE = HEAD_DIM ** -0.5
PAST_LEN = 2048

V7X_VMEM_LIMIT_BYTES = 56 * 1024 * 1024
V7X_SC_CORES = 2
V7X_SC_SUBCORES = 16
SUBLANES = 8

ROW_TILE = 640
PART_TILE = 128
A_HEADS_PER_STEP = 4
A_CHUNK_UNROLL = 4
DIL_BATCH = 8
MEM_DECODE_SEQS = 4
DIL_DECODE_SEQS = 2
MOE_BLOCK = 768
MOE_SUB = 384
FFN_COL_SPLIT = 2
SC_WORKERS = V7X_SC_CORES * V7X_SC_SUBCORES
SC_CHUNK = 32
SC_GATHER_CHUNK = 40

NEG_BIG = -1e30


def _cparams(sem, vmem=None):
    return pltpu.CompilerParams(dimension_semantics=sem, vmem_limit_bytes=vmem)


def _dot(a, b):
    return jnp.dot(a, b, preferred_element_type=F32)


def _dot_nt(a, b):
    return lax.dot_general(a, b, (((1,), (1,)), ((), ())), preferred_element_type=F32)


def _sigmoid(x):
    return 1.0 / (1.0 + jnp.exp(-x))


def _proj_kernel(*refs, n_lhs, rope_heads):
    x_refs = refs[:n_lhs]
    w_refs = refs[n_lhs:2 * n_lhs]
    n_tab = 3 if rope_heads else 0
    tab_refs = refs[2 * n_lhs:2 * n_lhs + n_tab]
    o_ref = refs[2 * n_lhs + n_tab]
    wb_refs = refs[2 * n_lhs + n_tab + 1:]

    @pl.when(pl.program_id(1) == 0)
    def _():
        for w_ref, wb_ref in zip(w_refs, wb_refs):
            wb_ref[...] = w_ref[...].astype(BF16)

    acc = None
    for x_ref, wb_ref in zip(x_refs, wb_refs):
        d = _dot(x_ref[...].astype(BF16), wb_ref[...])
        acc = d if acc is None else acc + d
    if not rope_heads:
        o_ref[...] = acc.astype(o_ref.dtype)
    else:
        c, su, sd = (t[...] for t in tab_refs)
        heads_per_tile = o_ref.shape[1] // HEAD_DIM
        for h in range(heads_per_tile):
            sl = slice(h * HEAD_DIM, (h + 1) * HEAD_DIM)
            plain = acc[:, sl]
            roped = _rope_head(plain, c, su, sd)
            is_roped = pl.program_id(0) * heads_per_tile + h < rope_heads
            o_ref[:, sl] = jnp.where(is_roped, roped, plain).astype(o_ref.dtype)


def _proj(lhs, w, layer, out_dtype, tm, tn, name, rope_tables=None, rope_heads=0):
    m = lhs[0].shape[0]
    n = w.shape[2]
    koff = 0
    in_specs, w_specs, scratch = [], [], []
    for x in lhs:
        k = x.shape[1]
        assert koff % k == 0 and m % tm == 0 and n % tn == 0
        in_specs.append(pl.BlockSpec((tm, k), lambda j, i: (i, 0)))
        w_specs.append(pl.BlockSpec((None, k, tn), functools.partial(lambda j, i, kb: (layer, kb, j), kb=koff // k)))
        scratch.append(pltpu.VMEM((k, tn), BF16))
        koff += k
    assert koff == w.shape[1]
    tables = list(rope_tables) if rope_heads else []
    tab_specs = [pl.BlockSpec((tm, HEAD_DIM), lambda j, i: (i, 0))] * len(tables)
    return pl.pallas_call(
        functools.partial(_proj_kernel, n_lhs=len(lhs), rope_heads=rope_heads),
        grid=(n // tn, m // tm),
        in_specs=in_specs + w_specs + tab_specs,
        out_specs=pl.BlockSpec((tm, tn), lambda j, i: (i, j)),
        out_shape=jax.ShapeDtypeStruct((m, n), out_dtype),
        scratch_shapes=scratch,
        compiler_params=_cparams(("arbitrary", "arbitrary"), V7X_VMEM_LIMIT_BYTES),
        name=name,
    )(*lhs, *([w] * len(lhs)), *tables)


def _pack_halves(x):
    c = x.shape[1] // 2
    lo = pltpu.bitcast(x[:, :c].astype(BF16).astype(F32), jnp.int32)
    hi = pltpu.bitcast(x[:, c:].astype(BF16).astype(F32), jnp.int32)
    return hi | lax.shift_right_logical(lo, 16)


def _unpack_halves(w):
    lo = pltpu.bitcast(lax.shift_left(w, 16), F32)
    hi = pltpu.bitcast(w & jnp.int32(-65536), F32)
    return lo, hi


def _layer_norm(z, g, b):
    mu = jnp.mean(z, axis=-1, keepdims=True)
    zc = z - mu
    var = jnp.mean(zc * zc, axis=-1, keepdims=True)
    return zc * lax.rsqrt(var + LN_EPS) * g + b


def _part_specs(n_p, block):
    tiles_p = n_p // PART_TILE
    rest = (0,) * (len(block) - 1)
    return (pl.BlockSpec(block, lambda i: (jnp.minimum(i, tiles_p - 1),) + rest),
            pl.BlockSpec(block, lambda i: (jnp.maximum(i - tiles_p, 0),) + rest))


def _on_part(tiles_p, fn):
    i = pl.program_id(0)
    pl.when(i < tiles_p)(functools.partial(fn, 0))
    pl.when(i >= tiles_p)(functools.partial(fn, 1))


def _postnorm_kernel(xp_ref, xs_ref, y_ref, g_ref, b_ref, wt_ref, rb_ref, of_ref, op_ref, e_ref, gate_ref, cnt_ref,
                     *, alpha, tiles_p):
    def part(which):
        x_ref = (xp_ref, xs_ref)[which]
        out = _layer_norm(alpha * x_ref[...] + y_ref[...], g_ref[...], b_ref[...])
        of_ref[...] = out
        op_ref[...] = _pack_halves(out)
        e_ref[...], gate_ref[...], cnt_ref[...] = _route_tile(out, wt_ref[...], rb_ref[...])

    _on_part(tiles_p, part)


def _postnorm_route(x_p, x_s, y, g, b, alpha, router_w, layer, router_b, name):
    n_p, d = x_p.shape
    m = n_p + x_s.shape[0]
    e = router_w.shape[2]
    t = PART_TILE
    row = pl.BlockSpec((t, d), lambda i: (i, 0))
    vec = pl.BlockSpec((1, d), lambda i: (0, 0))
    tok = pl.BlockSpec((TOP_K, t), lambda i: (0, i))
    part_p, part_s = _part_specs(n_p, (t, d))
    return pl.pallas_call(
        functools.partial(_postnorm_kernel, alpha=alpha, tiles_p=n_p // t),
        grid=(m // t,),
        in_specs=[part_p, part_s, row, vec, vec,
                  pl.BlockSpec((None, e, d), lambda i: (layer, 0, 0)), pl.BlockSpec((e, 1), lambda i: (0, 0))],
        out_specs=[row, pl.BlockSpec((t, d // 2), lambda i: (i, 0)), tok, tok,
                   pl.BlockSpec((None, e, 1), lambda i: (i, 0, 0))],
        out_shape=[jax.ShapeDtypeStruct((m, d), F32), jax.ShapeDtypeStruct((m, d // 2), jnp.int32),
                   jax.ShapeDtypeStruct((TOP_K, m), jnp.int32), jax.ShapeDtypeStruct((TOP_K, m), F32),
                   jax.ShapeDtypeStruct((m // t, e, 1), jnp.int32)],
        compiler_params=_cparams(("arbitrary",)),
        name=name,
    )(x_p, x_s, y, g.reshape(1, d), b.reshape(1, d), jnp.swapaxes(router_w, 1, 2),
      router_b[layer].astype(F32).reshape(e, 1))


def _hgrn_prompt_kernel(q_ref, f_ref, v_ref, gate_ref, lb_ref, gn_ref, os_ref, o_ref, s_ref, st_ref,
                        *, seq, heads, n_batch):
    _prompt_then_sample(n_batch, os_ref, o_ref,
                        functools.partial(_hgrn_sequence, q_ref, f_ref, v_ref, gate_ref, lb_ref, gn_ref,
                                          o_ref, s_ref, st_ref, seq=seq, heads=heads), axis=1)


def _hgrn_sequence(q_ref, f_ref, v_ref, gate_ref, lb_ref, gn_ref, o_ref, s_ref, st_ref, *, seq, heads):
    c = A_CHUNK
    n_chunks = seq // c
    n_sub = c // A_SUB
    gn = gn_ref[...]
    row = lax.broadcasted_iota(jnp.int32, (c, c), 0)
    col = lax.broadcasted_iota(jnp.int32, (c, c), 1)
    tril = jnp.where(row >= col, 1.0, 0.0).astype(BF16)
    same_sub = (row // A_SUB) == (col // A_SUB)
    diag_dist = jnp.where(same_sub, row - col, -1)
    off_mask = col < (row // A_SUB) * A_SUB
    st_ref[...] = jnp.zeros_like(st_ref)

    def chunk(ci, carry):
        for hh in range(heads):
            head_chunk(ci, hh)
        return carry

    def head_chunk(ci, hh):
        r0 = pl.multiple_of(ci * c, c)
        hsl = slice(hh * HEAD_DIM, (hh + 1) * HEAD_DIM)
        lb = lb_ref[hh]
        qp = q_ref[pl.ds(r0, c), hsl]
        fp = f_ref[pl.ds(r0, c), hsl]
        v = v_ref[pl.ds(r0, c), hsl]
        gp = gate_ref[pl.ds(r0, c), hsl]
        st = st_ref[hh]
        q = qp * _sigmoid(qp)
        forget = lb + (1.0 - lb) * _sigmoid(fp)
        logf = jnp.log2(forget)
        k = 1.0 - forget
        hi = logf.astype(BF16)
        r1 = logf - hi.astype(F32)
        mid = r1.astype(BF16)
        lo = (r1 - mid.astype(F32)).astype(BF16)
        g = _dot(tril, hi) + _dot(tril, mid) + _dot(tril, lo)
        v_b = v.astype(BF16)
        o = _dot_nt((q * jnp.exp2(g)).astype(BF16), st.astype(BF16))
        rows = [jnp.zeros((A_SUB, c), F32)]
        for i in range(1, n_sub):
            gref = g[i * A_SUB:i * A_SUB + 1, :]
            qt = q[i * A_SUB:(i + 1) * A_SUB, :] * jnp.exp2(g[i * A_SUB:(i + 1) * A_SUB, :] - gref)
            kt = k * jnp.exp2(jnp.minimum(gref - g, 0.0))
            rows.append(_dot_nt(qt.astype(BF16), kt.astype(BF16)))
        a = jnp.where(off_mask, jnp.concatenate(rows, axis=0), 0.0)
        for d in range(A_SUB):
            kr = k if d == 0 else pltpu.roll(k, d, axis=0)
            gr = g if d == 0 else pltpu.roll(g, d, axis=0)
            x = q * kr * jnp.exp2(g - gr)
            a = jnp.where(diag_dist == d, jnp.sum(x, axis=-1, keepdims=True), a)
        o = o + _dot(a.astype(BF16), v_b)
        gend = g[c - 1:c, :]
        kt_end = k * jnp.exp2(gend - g)
        st_ref[hh] = jnp.exp2(gend) * st + _dot(v_b.T, kt_end.astype(BF16))
        o = o * lax.rsqrt(jnp.mean(o * o, axis=-1, keepdims=True) + RMS_EPS) * gn
        o_ref[pl.ds(r0, c), hsl] = (o * (gp * _sigmoid(gp))).astype(o_ref.dtype)

    lax.fori_loop(0, n_chunks, chunk, 0, unroll=A_CHUNK_UNROLL)
    for hh in range(heads):
        s_ref[hh] = st_ref[hh].T


def _hgrn_prompt(proj, lb, g_norm, o_sample, n_batch, seq):
    h = A_HEADS
    hp = A_HEADS_PER_STEP
    ns = o_sample.shape[0]
    assert h % hp == 0 and ns <= seq and proj.shape[0] == n_batch * seq + ns
    ng = h // hp
    w = hp * HEAD_DIM
    last = n_batch - 1
    blk = lambda off: pl.BlockSpec((seq, w), functools.partial(lambda hg, b, off: (jnp.minimum(b, last), off + hg), off=off))
    return pl.pallas_call(
        functools.partial(_hgrn_prompt_kernel, seq=seq, heads=hp, n_batch=n_batch),
        grid=(ng, n_batch + 1),
        in_specs=[blk(0), blk(ng), blk(2 * ng), blk(3 * ng),
                  pl.BlockSpec((hp, 1, HEAD_DIM), lambda hg, b: (hg, 0, 0)),
                  pl.BlockSpec((1, HEAD_DIM), lambda hg, b: (0, 0)),
                  pl.BlockSpec((ns, w), lambda hg, b: (0, hg))],
        out_specs=[pl.BlockSpec((seq, w), lambda hg, b: (b, hg)),
                   pl.BlockSpec((None, hp, HEAD_DIM, HEAD_DIM), lambda hg, b: (jnp.minimum(b, last), hg, 0, 0))],
        out_shape=[jax.ShapeDtypeStruct((proj.shape[0], h * HEAD_DIM), BF16),
                   jax.ShapeDtypeStruct((n_batch, h, HEAD_DIM, HEAD_DIM), F32)],
        scratch_shapes=[pltpu.VMEM((hp, HEAD_DIM, HEAD_DIM), F32)],
        compiler_params=_cparams(("arbitrary", "arbitrary"), V7X_VMEM_LIMIT_BYTES),
        name="hgrn_prompt",
    )(proj, proj, proj, proj, lb.reshape(h, 1, HEAD_DIM), g_norm.reshape(1, HEAD_DIM), o_sample)


def _softmax_av(s, v):
    m = jnp.max(s, axis=-1, keepdims=True)
    p = jnp.exp(s - m)
    l = jnp.sum(p, axis=-1, keepdims=True)
    return _dot(p.astype(BF16), v) / l


def _prompt_then_sample(n_prompt_steps, os_ref, o_ref, prompt_step, axis=0):
    i = pl.program_id(axis)
    pl.when(i < n_prompt_steps)(prompt_step)

    @pl.when(i >= n_prompt_steps)
    def _():
        o_ref[:os_ref.shape[0], :] = os_ref[...]


def _mem_attn_kernel(q_ref, k_ref, v_ref, os_ref, o_ref, *, n_steps):
    def step():
        for h in range(MEM_HEADS):
            sl = slice(h * HEAD_DIM, (h + 1) * HEAD_DIM)
            s = _dot_nt(q_ref[:, sl].astype(BF16), k_ref[:, sl].astype(BF16)) * ATTN_SCALE
            o_ref[:, sl] = _softmax_av(s, v_ref[:, sl].astype(BF16)).astype(o_ref.dtype)

    _prompt_then_sample(n_steps, os_ref, o_ref, step)


def _mem_attn_prompt(proj, q_col, kv, o_sample, n_batch, seq, tq):
    mem_len = kv.shape[0] // n_batch
    nq = seq // tq
    n_steps = n_batch * nq
    ns = o_sample.shape[0]
    assert ns <= tq and proj.shape[0] == n_batch * seq + ns
    last = n_steps - 1
    return pl.pallas_call(
        functools.partial(_mem_attn_kernel, n_steps=n_steps),
        grid=(n_steps + 1,),
        in_specs=[pl.BlockSpec((tq, MEM_W), lambda i: (jnp.minimum(i, last), q_col)),
                  pl.BlockSpec((mem_len, MEM_W), lambda i: (jnp.minimum(i, last) // nq, 0)),
                  pl.BlockSpec((mem_len, MEM_W), lambda i: (jnp.minimum(i, last) // nq, 1)),
                  pl.BlockSpec((ns, MEM_W), lambda i: (0, 0))],
        out_specs=pl.BlockSpec((tq, MEM_W), lambda i: (i, 0)),
        out_shape=jax.ShapeDtypeStruct((proj.shape[0], MEM_W), BF16),
        compiler_params=_cparams(("arbitrary",)),
        name="mem_attn_prompt",
    )(proj, kv, kv, o_sample)


def _rope_tables(pos):
    half = ROPE_DIM // 2
    inv_freq = ROPE_THETA ** (-jnp.arange(0, ROPE_DIM, 2, dtype=F32) / ROPE_DIM)
    ang = pos.astype(F32)[:, None] * inv_freq[None, :]
    cos, sin = jnp.cos(ang), jnp.sin(ang)
    n = pos.shape[0]
    one = jnp.ones((n, HEAD_DIM - ROPE_DIM), F32)
    zero = jnp.zeros((n, HEAD_DIM - half), F32)
    c = jnp.concatenate([cos, cos, one], axis=1)
    s_up = jnp.concatenate([-sin, zero], axis=1)
    s_dn = jnp.concatenate([jnp.zeros((n, half), F32), sin, zero[:, half:]], axis=1)
    return c, s_up, s_dn


def _rope_head(x, c, su, sd):
    half = ROPE_DIM // 2
    up = pltpu.roll(x, HEAD_DIM - half, axis=1)
    dn = pltpu.roll(x, half, axis=1)
    return x * c + up * su + dn * sd


def _kv_kernel(kv_ref, c_ref, su_ref, sd_ref, k_ref, kp_ref, ks_ref, vp_ref, vs_ref, *, tiles_p):
    c, su, sd = c_ref[...], su_ref[...], sd_ref[...]

    def part(which):
        k4_ref, v4_ref = ((kp_ref, vp_ref), (ks_ref, vs_ref))[which]
        for h in range(B_SLOTS):
            sl = slice(h * HEAD_DIM, (h + 1) * HEAD_DIM)
            k = _rope_head(kv_ref[:, sl], c, su, sd)
            k_ref[:, sl] = k
            k4_ref[:, h, :] = k
            v4_ref[:, h, :] = kv_ref[:, KV_W + h * HEAD_DIM:KV_W + (h + 1) * HEAD_DIM]

    _on_part(tiles_p, part)


def _shared_kv(kv, tables, n_p):
    m = kv.shape[0]
    tm = PART_TILE
    tab = pl.BlockSpec((tm, HEAD_DIM), lambda i: (i, 0))
    part_p, part_s = _part_specs(n_p, (tm, B_SLOTS, HEAD_DIM))
    cache = lambda rows: jax.ShapeDtypeStruct((rows, B_SLOTS, HEAD_DIM), F32)
    return pl.pallas_call(
        functools.partial(_kv_kernel, tiles_p=n_p // tm),
        grid=(m // tm,),
        in_specs=[pl.BlockSpec((tm, 2 * KV_W), lambda i: (i, 0)), tab, tab, tab],
        out_specs=[pl.BlockSpec((tm, KV_W), lambda i: (i, 0)), part_p, part_s, part_p, part_s],
        out_shape=[jax.ShapeDtypeStruct((m, KV_W), F32), cache(n_p), cache(m - n_p), cache(n_p), cache(m - n_p)],
        compiler_params=_cparams(("arbitrary",)),
        name="shared_kv",
    )(kv, *tables)


def _dil_prompt_kernel(*refs, seq, n_batch):
    os_ref, o_ref = refs[B_GROUPS + 2:B_GROUPS + 4]
    _prompt_then_sample(n_batch, os_ref, o_ref, functools.partial(_dil_sequence, *refs, seq=seq), axis=1)


def _dil_sequence(*refs, seq):
    q_refs = refs[:B_GROUPS]
    k_ref, v_ref, _, o_ref, m_ref, l_ref, acc_ref = refs[B_GROUPS:]
    blk = B_BLOCK
    base = (lax.broadcasted_iota(jnp.int32, (blk, blk), 0) - lax.broadcasted_iota(jnp.int32, (blk, blk), 1))
    for g, (win, dil) in enumerate(B_PATTERNS):
        span = win // dil
        n_blk = seq // dil // blk
        assert span <= blk

        def body(it, carry, g=g, dil=dil, span=span, n_blk=n_blk):
            own, prev, prev_ok = [], [], []
            for j in range(DIL_BATCH):
                t = it * DIL_BATCH + j
                r, n = t % dil, t // dil

                def rows(nn, r=r):
                    start = nn * (blk * dil) + r
                    return pl.ds(pl.multiple_of(start, blk), blk) if dil == 1 else pl.ds(start, blk, stride=dil)

                own.append(rows(n))
                prev.append(rows(jnp.maximum(n - 1, 0)))
                prev_ok.append(base <= jnp.where(n > 0, span - blk, -blk - 1))
            load = lambda ref, idx: jnp.stack([ref[i, :].astype(BF16) for i in idx])
            scores = lambda a, b: jnp.einsum('bqd,bkd->bqk', a, b, preferred_element_type=F32) * ATTN_SCALE
            weighted = lambda p, v: jnp.einsum('bqk,bkd->bqd', p.astype(BF16), v, preferred_element_type=F32)
            q = load(q_refs[g], own)
            s = jnp.where((base >= 0)[None], scores(q, load(k_ref, own)), NEG_BIG)
            m_b = jnp.max(s, axis=-1, keepdims=True)
            if n_blk > 1:
                s_prev = jnp.where(jnp.stack(prev_ok), scores(q, load(k_ref, prev)), NEG_BIG)
                m_b = jnp.maximum(m_b, jnp.max(s_prev, axis=-1, keepdims=True))
            p = jnp.exp(s - m_b)
            l_b = jnp.sum(p, axis=-1, keepdims=True)
            acc_b = weighted(p, load(v_ref, own))
            if n_blk > 1:
                p_prev = jnp.exp(s_prev - m_b)
                l_b = l_b + jnp.sum(p_prev, axis=-1, keepdims=True)
                acc_b = acc_b + weighted(p_prev, load(v_ref, prev))
            for j, idx in enumerate(own):
                if g == 0:
                    m_ref[idx, :] = jnp.broadcast_to(m_b[j], (blk, HEAD_DIM))
                    l_ref[idx, :] = jnp.broadcast_to(l_b[j], (blk, HEAD_DIM))
                    acc_ref[idx, :] = acc_b[j]
                else:
                    m_old = m_ref[idx, :]
                    m_new = jnp.maximum(m_old, m_b[j])
                    a_old = jnp.exp(m_old - m_new)
                    a_b = jnp.exp(m_b[j] - m_new)
                    m_ref[idx, :] = m_new
                    l_ref[idx, :] = l_ref[idx, :] * a_old + l_b[j] * a_b
                    acc_ref[idx, :] = acc_ref[idx, :] * a_old + acc_b[j] * a_b
            return carry

        assert (dil * n_blk) % DIL_BATCH == 0
        lax.fori_loop(0, dil * n_blk // DIL_BATCH, body, 0)
    o_ref[...] = (acc_ref[...] / l_ref[...]).astype(o_ref.dtype)


def _dil_attn_prompt(q, k, kv, o_sample, n_batch, seq):
    ns = o_sample.shape[0]
    assert ns <= seq and q.shape[0] == n_batch * seq + ns
    last = n_batch - 1
    col = lambda c0: pl.BlockSpec((seq, HEAD_DIM), functools.partial(lambda h, b, c0: (jnp.minimum(b, last), c0 + h), c0=c0))
    return pl.pallas_call(
        functools.partial(_dil_prompt_kernel, seq=seq, n_batch=n_batch),
        grid=(B_SLOTS, n_batch + 1),
        in_specs=[col(g * B_SLOTS) for g in range(B_GROUPS)] + [col(0), col(B_SLOTS),
                                                                 pl.BlockSpec((ns, HEAD_DIM), lambda h, b: (0, h))],
        out_specs=pl.BlockSpec((seq, HEAD_DIM), lambda h, b: (b, h)),
        out_shape=jax.ShapeDtypeStruct((q.shape[0], KV_W), BF16),
        scratch_shapes=[pltpu.VMEM((seq, HEAD_DIM), F32)] * 3,
        compiler_params=_cparams(("arbitrary", "arbitrary"), V7X_VMEM_LIMIT_BYTES),
        name="dil_attn_prompt",
    )(*([q] * B_GROUPS), k, kv, o_sample)


def _mem_decode_kernel(q_ref, k_ref, v_ref, o_ref):
    n_seq, rows, _ = k_ref.shape
    nh = q_ref.shape[1]
    own = (lax.broadcasted_iota(jnp.int32, (nh, rows), 1) % nh) == lax.broadcasted_iota(jnp.int32, (nh, rows), 0)
    for i in range(n_seq):
        s = _dot_nt(q_ref[i].astype(BF16), k_ref[i].astype(BF16)) * ATTN_SCALE
        o_ref[i] = _softmax_av(jnp.where(own, s, NEG_BIG), v_ref[i].astype(BF16)).astype(o_ref.dtype)


def _mem_attn_decode(q, cache_k, cache_v, layer):
    ns, nh, hd = q.shape
    n_layers, _, mem_len = cache_k.shape[:3]
    g = MEM_DECODE_SEQS
    assert ns % g == 0
    rows = mem_len * nh
    cspec = pl.BlockSpec((None, g, rows, hd), lambda b: (layer, b, 0, 0))
    qspec = pl.BlockSpec((g, nh, hd), lambda b: (b, 0, 0))
    return pl.pallas_call(
        _mem_decode_kernel,
        grid=(ns // g,),
        in_specs=[qspec, cspec, cspec],
        out_specs=qspec,
        out_shape=jax.ShapeDtypeStruct((ns, nh, hd), BF16),
        compiler_params=_cparams(("parallel",)),
        name="mem_attn_decode",
    )(q, cache_k.reshape(n_layers, ns, rows, hd), cache_v.reshape(n_layers, ns, rows, hd))


def _dil_decode_kernel(q_ref, kn_ref, vn_ref, *refs):
    o_ref = refs[-1]
    n_slots = kn_ref.shape[1]
    for i in range(q_ref.shape[0]):
        kn, vn = kn_ref[i], vn_ref[i]
        scores, values = [], []
        for g in range(B_GROUPS):
            k_ref, v_ref = refs[2 * g], refs[2 * g + 1]
            rows = k_ref.shape[1] * (k_ref.shape[2] if len(k_ref.shape) == 4 else 1)
            period = k_ref.shape[2] if len(k_ref.shape) == 4 else n_slots
            k = k_ref[i].reshape(rows, HEAD_DIM).astype(BF16)
            v = v_ref[i].reshape(rows, HEAD_DIM).astype(BF16)
            q = q_ref[i, g]
            own = (lax.broadcasted_iota(jnp.int32, (n_slots, rows), 1) % period
                   == lax.broadcasted_iota(jnp.int32, (n_slots, rows), 0))
            scores.append(jnp.where(own, _dot_nt(q.astype(BF16), k) * ATTN_SCALE, NEG_BIG))
            values.append(v)
            scores.append(jnp.sum(q * kn, axis=-1, keepdims=True) * ATTN_SCALE)
            values.append(None)
        m = functools.reduce(jnp.maximum, [jnp.max(s, axis=-1, keepdims=True) for s in scores])
        l = jnp.zeros_like(m)
        acc = jnp.zeros((n_slots, HEAD_DIM), F32)
        for s, v in zip(scores, values):
            p = jnp.exp(s - m)
            l = l + jnp.sum(p, axis=-1, keepdims=True)
            acc = acc + (p * vn if v is None else _dot(p.astype(BF16), v))
        o_ref[i] = (acc / l).astype(o_ref.dtype)


def _dil_attn_decode(q, k_new, v_new, cache_k, cache_v):
    ns, w_buf, ns_slots, hd = cache_k.shape
    n_seq = DIL_DECODE_SEQS
    assert ns % n_seq == 0
    in_specs = [pl.BlockSpec((n_seq, B_GROUPS, ns_slots, hd), lambda b: (b, 0, 0, 0)),
                pl.BlockSpec((n_seq, ns_slots, hd), lambda b: (b, 0, 0)),
                pl.BlockSpec((n_seq, ns_slots, hd), lambda b: (b, 0, 0))]
    args = [q, k_new, v_new]
    for win, dil in B_PATTERNS:
        span = win // dil
        assert w_buf % dil == 0 and (w_buf // dil) % span == 0
        last = w_buf // dil // span - 1
        if dil == 1:
            view = (ns, w_buf * ns_slots, hd)
            spec = pl.BlockSpec((n_seq, span * ns_slots, hd), functools.partial(lambda b, last: (b, last, 0), last=last))
        else:
            assert dil * ns_slots >= SUBLANES and ns_slots <= SUBLANES
            view = (ns, w_buf // dil, dil * ns_slots, hd)
            spec = pl.BlockSpec((n_seq, span, SUBLANES, hd), functools.partial(lambda b, last: (b, last, 0, 0), last=last))
        in_specs += [spec, spec]
        args += [cache_k.reshape(view), cache_v.reshape(view)]
    return pl.pallas_call(
        _dil_decode_kernel,
        grid=(ns // n_seq,),
        in_specs=in_specs,
        out_specs=pl.BlockSpec((n_seq, ns_slots, hd), lambda b: (b, 0, 0)),
        out_shape=jax.ShapeDtypeStruct((ns, ns_slots, hd), BF16),
        compiler_params=_cparams(("parallel",)),
        name="dil_attn_decode",
    )(*args)


def _hgrn_step_kernel(p_ref, s_ref, lb_ref, gn_ref, o_ref, so_ref, rows_ref):
    h = A_HEADS
    hk = h * HEAD_DIM
    gn = gn_ref[...]
    rows_ref[...] = jnp.zeros_like(rows_ref)
    for i in range(h):
        fp = p_ref[:, hk + i * HEAD_DIM:hk + (i + 1) * HEAD_DIM]
        lb = lb_ref[i:i + 1, :]
        rows_ref[i:i + 1, :] = lb + (1.0 - lb) * _sigmoid(fp)
    cols = rows_ref[...].T
    first = lax.broadcasted_iota(jnp.int32, (SUBLANES, HEAD_DIM), 0) == 0
    for i in range(h):
        sl = slice(i * HEAD_DIM, (i + 1) * HEAD_DIM)
        qp = p_ref[:, sl]
        v = p_ref[:, 2 * hk + i * HEAD_DIM:2 * hk + (i + 1) * HEAD_DIM]
        gp = p_ref[:, 3 * hk + i * HEAD_DIM:3 * hk + (i + 1) * HEAD_DIM]
        k = 1.0 - rows_ref[i:i + 1, :]
        k8 = jnp.where(first, k, 0.0).astype(BF16)
        v8 = jnp.broadcast_to(v, (SUBLANES, HEAD_DIM)).astype(BF16)
        kv = lax.dot_general(k8, v8, (((0,), (0,)), ((), ())), preferred_element_type=F32)
        s_new = cols[:, i:i + 1] * s_ref[i] + kv
        so_ref[i] = s_new
        q8 = jnp.broadcast_to(qp * _sigmoid(qp), (SUBLANES, HEAD_DIM)).astype(BF16)
        o = _dot(q8, s_new.astype(BF16))[0:1]
        o = o * lax.rsqrt(jnp.mean(o * o, axis=-1, keepdims=True) + RMS_EPS) * gn
        o_ref[:, sl] = (o * (gp * _sigmoid(gp))).astype(o_ref.dtype)


def _hgrn_step(proj, state, layer, lb, g_norm):
    ns = proj.shape[0]
    h = A_HEADS
    sspec = pl.BlockSpec((None, h, HEAD_DIM, HEAD_DIM), lambda b: (b, 0, 0, 0))
    return pl.pallas_call(
        _hgrn_step_kernel,
        grid=(ns,),
        in_specs=[pl.BlockSpec((None, 1, proj.shape[2]), lambda b: (b, 0, 0)),
                  pl.BlockSpec((None, None, h, HEAD_DIM, HEAD_DIM), lambda b: (layer, b, 0, 0, 0)),
                  pl.BlockSpec((h, HEAD_DIM), lambda b: (0, 0)),
                  pl.BlockSpec((1, HEAD_DIM), lambda b: (0, 0))],
        out_specs=[pl.BlockSpec((None, 1, h * HEAD_DIM), lambda b: (b, 0, 0)), sspec],
        out_shape=[jax.ShapeDtypeStruct((ns, 1, h * HEAD_DIM), BF16),
                   jax.ShapeDtypeStruct((ns, h, HEAD_DIM, HEAD_DIM), F32)],
        scratch_shapes=[pltpu.VMEM((HEAD_DIM, HEAD_DIM), F32)],
        compiler_params=_cparams(("parallel",)),
        name="hgrn_step",
    )(proj, state, lb.reshape(h, HEAD_DIM), g_norm.reshape(1, HEAD_DIM))


def _ffn_kernel(be_ref, nv_ref, od_ref, nu_ref, x_ref, wg_ref, wu_ref, wd_ref, o_ref, wgb_ref, wub_ref, wdb_ref, *, sub):
    i = pl.program_id(0)
    n_valid = nv_ref[i]

    @pl.when((n_valid > 0) & ((i == 0) | (be_ref[i] != be_ref[jnp.maximum(i - 1, 0)])))
    def _():
        wgb_ref[...] = wg_ref[...].astype(BF16)
        wub_ref[...] = wu_ref[...].astype(BF16)
        wdb_ref[...] = wd_ref[...].astype(BF16)

    for j in range(x_ref.shape[0] // sub):
        rows = slice(j * sub, (j + 1) * sub)

        @pl.when(n_valid > j * sub)
        def _(j=j, rows=rows):
            lo, hi = _unpack_halves(x_ref[rows, :])
            c = lo.shape[1]
            keep = lax.broadcasted_iota(jnp.int32, lo.shape, 0) < n_valid - j * sub
            lo = jnp.where(keep, lo, 0.0).astype(BF16)
            hi = jnp.where(keep, hi, 0.0).astype(BF16)
            ff = wgb_ref.shape[1]
            y = None
            for part in range(FFN_COL_SPLIT):
                cs = slice(part * ff // FFN_COL_SPLIT, (part + 1) * ff // FFN_COL_SPLIT)
                hg = _dot(lo, wgb_ref[:c, cs]) + _dot(hi, wgb_ref[c:, cs])
                hu = _dot(lo, wub_ref[:c, cs]) + _dot(hi, wub_ref[c:, cs])
                hid = (hg * _sigmoid(hg) * hu).astype(BF16)
                yp = _dot(hid, wdb_ref[cs, :])
                y = yp if y is None else y + yp
            o_ref[rows, :] = _pack_halves(y)

        @pl.when(n_valid <= j * sub)
        def _(rows=rows):
            o_ref[rows, :] = jnp.zeros((sub, o_ref.shape[1]), o_ref.dtype)


def _expert_ffn(x, blk_exp, blk_valid, blk_order, n_used, w_gate, w_up, w_down, layer, tm, sub, name):
    r = x.shape[0]
    d, ff = w_gate.shape[2:]
    assert tm % sub == 0
    w_spec = lambda shape: pl.BlockSpec((None, None) + shape, lambda i, be, nv, od, nu: (layer, be[i], 0, 0))
    return pl.pallas_call(
        functools.partial(_ffn_kernel, sub=sub),
        grid_spec=pltpu.PrefetchScalarGridSpec(
            num_scalar_prefetch=4,
            grid=(r // tm,),
            in_specs=[pl.BlockSpec((tm, d // 2), lambda i, be, nv, od, nu: (od[jnp.minimum(i, nu[0] - 1)], 0)),
                      w_spec((d, ff)), w_spec((d, ff)), w_spec((ff, d))],
            out_specs=pl.BlockSpec((tm, d // 2), lambda i, be, nv, od, nu: (od[i], 0)),
            scratch_shapes=[pltpu.VMEM((d, ff), BF16), pltpu.VMEM((d, ff), BF16), pltpu.VMEM((ff, d), BF16)],
        ),
        out_shape=jax.ShapeDtypeStruct((r, d // 2), jnp.int32),
        compiler_params=_cparams(("arbitrary",), V7X_VMEM_LIMIT_BYTES),
        name=name,
    )(blk_exp, blk_valid, blk_order, n_used, x, w_gate, w_up, w_down)


def _first_argmax(val, idx, sentinel):
    m = jnp.max(val, axis=0, keepdims=True)
    i = jnp.min(jnp.where(val == m, idx, sentinel), axis=0, keepdims=True)
    return m, i


def _route_tile(x, w, b):
    t = x.shape[0]
    xh = x.astype(BF16)
    xl = (x - xh.astype(F32)).astype(BF16)
    wh = w.astype(BF16)
    wl = (w - wh.astype(F32)).astype(BF16)
    logits = _dot_nt(wh, xh) + (_dot_nt(wl, xh) + _dot_nt(wh, xl))
    scores = _sigmoid(logits)
    biased = scores + b
    gs = N_EXPERTS // N_GROUPS
    neg = -jnp.inf
    eid = lax.broadcasted_iota(jnp.int32, (N_EXPERTS, t), 0)
    sub = lax.broadcasted_iota(jnp.int32, (gs, t), 0)
    grow = lax.broadcasted_iota(jnp.int32, (N_GROUPS, t), 0)
    grp = jnp.zeros((N_GROUPS, t), F32)
    for g in range(N_GROUPS):
        bg = biased[g * gs:(g + 1) * gs]
        m1, i1 = _first_argmax(bg, sub, gs)
        m2 = jnp.max(jnp.where(sub == i1, neg, bg), axis=0, keepdims=True)
        grp = jnp.where(grow == g, m1 + m2, grp)
    chosen = jnp.zeros((N_GROUPS, t), F32)
    for _ in range(TOPK_GROUPS):
        _, gi = _first_argmax(grp, grow, N_GROUPS)
        hit = grow == gi
        chosen = jnp.where(hit, 1.0, chosen)
        grp = jnp.where(hit, neg, grp)
    chosen_e = jnp.concatenate([jnp.broadcast_to(chosen[g:g + 1], (gs, t)) for g in range(N_GROUPS)], axis=0)
    masked = jnp.where(chosen_e > 0.0, biased, neg)
    krow = lax.broadcasted_iota(jnp.int32, (TOP_K, t), 0)
    e_out = jnp.zeros((TOP_K, t), jnp.int32)
    g_out = jnp.zeros((TOP_K, t), F32)
    member = jnp.zeros((N_EXPERTS, t), F32)
    for k in range(TOP_K):
        _, idx = _first_argmax(masked, eid, N_EXPERTS)
        hit = eid == idx
        gk = jnp.sum(jnp.where(hit, scores, 0.0), axis=0, keepdims=True)
        masked = jnp.where(hit, neg, masked)
        member = jnp.where(hit, 1.0, member)
        e_out = jnp.where(krow == k, idx, e_out)
        g_out = jnp.where(krow == k, gk, g_out)
    g_out = g_out / jnp.sum(g_out, axis=0, keepdims=True) * ROUTED_SCALE
    return e_out, g_out, jnp.sum(member, axis=1, keepdims=True).astype(jnp.int32)


def _slots_kernel(e_ref, base_ref, o_ref):
    e = e_ref[...]
    t = e.shape[1]
    eid = lax.broadcasted_iota(jnp.int32, (N_EXPERTS, t), 0)
    member = jnp.zeros((N_EXPERTS, t), F32)
    for k in range(TOP_K):
        member = jnp.where(eid == e[k:k + 1], 1.0, member)
    before = (lax.broadcasted_iota(jnp.int32, (t, t), 0) < lax.broadcasted_iota(jnp.int32, (t, t), 1))
    rank = _dot(member.astype(BF16), jnp.where(before, 1.0, 0.0).astype(BF16))
    slot = rank + base_ref[...].astype(F32)
    krow = lax.broadcasted_iota(jnp.int32, (TOP_K, t), 0)
    out = jnp.zeros((TOP_K, t), F32)
    for k in range(TOP_K):
        sk = jnp.sum(jnp.where(eid == e[k:k + 1], slot, 0.0), axis=0, keepdims=True)
        out = jnp.where(krow == k, sk, out)
    o_ref[...] = out.astype(jnp.int32)


def _dispatch_plan(e_idx, counts, tm, tile):
    k, n = e_idx.shape
    n_blocks = -(-(n * k) // tm) + N_EXPERTS
    counts = counts.reshape(n // tile, N_EXPERTS)
    total = jnp.sum(counts, axis=0)
    padded = (total + tm - 1) // tm * tm
    pad_end = jnp.cumsum(padded)
    tile_base = (pad_end - padded)[None, :] + jnp.cumsum(counts, axis=0) - counts
    slot_of = pl.pallas_call(
        _slots_kernel,
        grid=(n // tile,),
        in_specs=[pl.BlockSpec((k, tile), lambda i: (0, i)),
                  pl.BlockSpec((None, N_EXPERTS, 1), lambda i: (i, 0, 0))],
        out_specs=pl.BlockSpec((k, tile), lambda i: (0, i)),
        out_shape=jax.ShapeDtypeStruct((k, n), jnp.int32),
        compiler_params=_cparams(("parallel",)),
        name="slots",
    )(e_idx, tile_base.astype(jnp.int32).reshape(n // tile, N_EXPERTS, 1))
    pos = jnp.arange(n_blocks, dtype=jnp.int32)
    blk_exp = jnp.minimum(jnp.sum((pad_end[None, :] <= (pos * tm)[:, None]).astype(jnp.int32), axis=1), N_EXPERTS - 1)
    is_exp = blk_exp[:, None] == jnp.arange(N_EXPERTS, dtype=blk_exp.dtype)[None, :]
    per_blk = lambda v: jnp.sum(jnp.where(is_exp, v[None, :], 0), axis=1)
    n_used = jnp.maximum(pad_end[-1:] // tm, 1).astype(jnp.int32)
    first, count = per_blk((pad_end - padded) // tm), jnp.maximum(per_blk(padded // tm), 1)
    blk_order = jnp.where(pos < n_used[0], first + (pos - first + count - 1) % count, pos)
    blk_valid = jnp.clip(per_blk(pad_end - padded + total) - blk_order * tm, 0, tm)
    return (slot_of, blk_exp.astype(jnp.int32), blk_valid.astype(jnp.int32), blk_order.astype(jnp.int32), n_used,
            n_blocks)


def _sc_mesh():
    return plsc.VectorSubcoreMesh(core_axis_name="c", subcore_axis_name="s",
                                  num_cores=V7X_SC_CORES, num_subcores=V7X_SC_SUBCORES)


def _sc_worker_id():
    return lax.axis_index("s") * V7X_SC_CORES + lax.axis_index("c")


def _sc_gather_rows(table, slot_of):
    k, n = slot_of.shape
    w = table.shape[1]
    n_rows = k * n
    ch = SC_GATHER_CHUNK
    per_worker = n_rows // (SC_WORKERS * ch)
    assert per_worker * SC_WORKERS * ch == n_rows and per_worker % 2 == 0
    idx = slot_of.reshape(SC_WORKERS, per_worker, ch)

    def body(table_hbm, idx_hbm, out_hbm, idx_v, rows_v, sems):
        wid = _sc_worker_id()
        pltpu.sync_copy(idx_hbm.at[wid], idx_v)

        def gather(c, slot):
            return pltpu.make_async_copy(table_hbm.at[idx_v.at[c]], rows_v.at[slot], sems.at[slot])

        gather(0, 0).start()

        @pl.loop(0, per_worker, step=2)
        def _(c0):
            for slot in range(2):
                c = c0 + slot
                gather(c, slot).wait()

                @pl.when(c + 1 < per_worker)
                def _():
                    gather(c + 1, 1 - slot).start()

                row0 = pl.multiple_of((wid * per_worker + c) * ch, 8)
                pltpu.sync_copy(rows_v.at[slot], out_hbm.at[pl.ds(row0, ch)])

    return pl.kernel(
        body, out_type=jax.ShapeDtypeStruct((n_rows, w), table.dtype), mesh=_sc_mesh(),
        scratch_types=[pltpu.VMEM((per_worker, ch), jnp.int32), pltpu.VMEM((2, ch, w), table.dtype),
                       pltpu.SemaphoreType.DMA((2,))],
        name="sc_gather_rows",
    )(table, idx)


def _sc_scatter_rows(x, slot_of, n_slots):
    k, n = slot_of.shape
    w = x.shape[1]
    n_chunks = n // SC_CHUNK
    assert n_chunks * SC_CHUNK == n
    idx = slot_of.reshape(k, n_chunks, SC_CHUNK).transpose(1, 0, 2)
    rounds = -(-n_chunks // SC_WORKERS)

    def body(x_hbm, idx_hbm, out_hbm, idx_v, rows_v, sem):
        wid = _sc_worker_id()

        @pl.loop(0, rounds)
        def _(j):
            q = j * SC_WORKERS + wid

            @pl.when(q < n_chunks)
            def _():
                pltpu.sync_copy(idx_hbm.at[q], idx_v)
                pltpu.sync_copy(x_hbm.at[pl.ds(pl.multiple_of(q * SC_CHUNK, SC_CHUNK), SC_CHUNK)], rows_v)
                copies = [pltpu.make_async_copy(rows_v, out_hbm.at[idx_v.at[kk]], sem) for kk in range(k)]
                for cp in copies:
                    cp.start()
                for cp in copies:
                    cp.wait()

    return pl.kernel(
        body, out_type=jax.ShapeDtypeStruct((n_slots, w), x.dtype), mesh=_sc_mesh(),
        scratch_types=[pltpu.VMEM((k, SC_CHUNK), jnp.int32), pltpu.VMEM((SC_CHUNK, w), x.dtype),
                       pltpu.SemaphoreType.DMA],
        name="sc_scatter_rows",
    )(x, idx)


def _moe_postnorm_kernel(x_ref, y_ref, gate_ref, sh_ref, g_ref, b_ref, ofp_ref, ofs_ref, ob_ref, *, alpha, tiles_p):
    def part(which):
        of_ref = (ofp_ref, ofs_ref)[which]
        gate = gate_ref[...]
        lo, hi = _unpack_halves(sh_ref[...])
        for k in range(TOP_K):
            lo_k, hi_k = _unpack_halves(y_ref[k])
            lo = lo + gate[:, k:k + 1] * lo_k
            hi = hi + gate[:, k:k + 1] * hi_k
        z = alpha * x_ref[...] + jnp.concatenate([lo, hi], axis=1)
        out = _layer_norm(z, g_ref[...], b_ref[...])
        of_ref[...] = out
        ob_ref[...] = out.astype(BF16)

    _on_part(tiles_p, part)


def _moe_postnorm(x, n_p, y_tok, gate, shared, g, b, alpha):
    m, d = x.shape
    tm = PART_TILE
    row = pl.BlockSpec((tm, d), lambda i: (i, 0))
    vec = pl.BlockSpec((1, d), lambda i: (0, 0))
    part_p, part_s = _part_specs(n_p, (tm, d))
    return pl.pallas_call(
        functools.partial(_moe_postnorm_kernel, alpha=alpha, tiles_p=n_p // tm),
        grid=(m // tm,),
        in_specs=[row, pl.BlockSpec((TOP_K, tm, d // 2), lambda i: (0, i, 0)),
                  pl.BlockSpec((tm, TOP_K), lambda i: (i, 0)), pl.BlockSpec((tm, d // 2), lambda i: (i, 0)), vec, vec],
        out_specs=[part_p, part_s, row],
        out_shape=[jax.ShapeDtypeStruct((n_p, d), F32), jax.ShapeDtypeStruct((m - n_p, d), F32),
                   jax.ShapeDtypeStruct((m, d), BF16)],
        compiler_params=_cparams(("arbitrary",), V7X_VMEM_LIMIT_BYTES),
        name="postnorm_moe",
    )(x, y_tok, gate, shared, g.reshape(1, d), b.reshape(1, d))


def _moe(xf, n_p, xp, routing, layer, w_gate, w_up, w_down, sw_gate, sw_up, sw_down, ln_g, ln_b, alpha):
    n, d = xf.shape
    e_idx, gate, counts = routing
    counts = jnp.sum(counts.reshape(n // ROW_TILE, ROW_TILE // PART_TILE, N_EXPERTS), axis=1)
    slot_of, blk_exp, blk_valid, blk_order, n_used, n_blocks = _dispatch_plan(e_idx, counts, MOE_BLOCK, ROW_TILE)
    x_sorted = _sc_scatter_rows(xp, slot_of, n_blocks * MOE_BLOCK)
    y_sorted = _expert_ffn(x_sorted, blk_exp, blk_valid, blk_order, n_used, w_gate, w_up, w_down, layer,
                           MOE_BLOCK, MOE_SUB, "routed_ffn")
    y_tok = _sc_gather_rows(y_sorted, slot_of).reshape(TOP_K, n, d // 2)
    n_sh = n // ROW_TILE
    shared = _expert_ffn(xp, jnp.zeros((n_sh,), jnp.int32), jnp.full((n_sh,), ROW_TILE, jnp.int32),
                         jnp.arange(n_sh, dtype=jnp.int32), jnp.full((1,), n_sh, jnp.int32),
                         sw_gate[:, None], sw_up[:, None], sw_down[:, None], layer, ROW_TILE, ROW_TILE, "shared_ffn")
    return _moe_postnorm(xf, n_p, y_tok, gate.T, shared, ln_g, ln_b, alpha)


def kernel(x_prompt, x_sample, state_hgrn, cache_win_k, cache_win_v, cache_mem_k, cache_mem_v, mem_prompt,
           w_in_a, lb_logits, g_norm_a, w_out_a, w_in_b, w_out_b, w_kv_shared, w_mem_kv, ln_g, ln_b,
           router_w, router_b, exp_w_gate, exp_w_up, exp_w_down, sh_w_gate, sh_w_up, sh_w_down):
    bp, sp, d = x_prompt.shape
    ns = x_sample.shape[0]
    assert x_sample.shape[1] == 1
    depth = ln_g.shape[0]
    n_a = w_in_a.shape[0]
    alpha = (2 * depth) ** 0.25
    n_p = bp * sp
    mem_len = mem_prompt.shape[1]
    a_mix = 4 * A_HEADS * HEAD_DIM

    xf_p, xf_s = x_prompt.reshape(n_p, d), x_sample.reshape(ns, d)
    xb = jnp.concatenate([xf_p, xf_s], axis=0).astype(BF16)
    lower_bounds = jnp.cumsum(jax.nn.softmax(lb_logits.astype(F32), axis=0), axis=0)
    mem_flat = mem_prompt.reshape(bp * mem_len, d)
    pos_all = jnp.concatenate([jnp.tile(jnp.arange(sp, dtype=jnp.int32), bp),
                               jnp.full((ns,), PAST_LEN, jnp.int32)])
    tables = _rope_tables(pos_all)

    hgrn_p, hgrn_s, mem_k_p, mem_v_p = [], [], [], []
    for layer in range(depth):
        kvm = _proj([mem_flat], w_mem_kv, layer, F32, 2 * mem_len, 1024, "mem_kv")
        mem_k_p.append(kvm[:, :MEM_W].reshape(bp, mem_len, MEM_HEADS, HEAD_DIM))
        mem_v_p.append(kvm[:, MEM_W:].reshape(bp, mem_len, MEM_HEADS, HEAD_DIM))
        if layer < n_a:
            a = layer
            proj = _proj([xb], w_in_a, a, F32, ROW_TILE, 1664, "proj_in_a")
            proj_s = proj[n_p:]
            o_x_s, st_s = _hgrn_step(proj_s.reshape(ns, 1, -1), state_hgrn, a, lower_bounds[a], g_norm_a[a])
            o_m_s = _mem_attn_decode(proj_s[:, a_mix:].reshape(ns, MEM_HEADS, HEAD_DIM), cache_mem_k, cache_mem_v, layer)
            o_x, st_p = _hgrn_prompt(proj, lower_bounds[a], g_norm_a[a], o_x_s.reshape(ns, -1), bp, sp)
            o_m = _mem_attn_prompt(proj, a_mix // MEM_W, kvm, o_m_s.reshape(ns, -1), bp, sp, 512)
            hgrn_p.append(st_p)
            hgrn_s.append(st_s)
            w_out, w_out_layer = w_out_a, a
        else:
            bl = layer - n_a
            if layer == n_a:
                kv = _proj([xb], w_kv_shared[None], 0, F32, ROW_TILE, 1024, "proj_kv")
                k_r, k_p, k_s, v_p, v_s = _shared_kv(kv, tables, n_p)
            proj = _proj([xb], w_in_b, bl, F32, ROW_TILE, 1024, "proj_in_b", tables, B_QHEADS)
            q_s = proj[n_p:, :B_QHEADS * HEAD_DIM].reshape(ns, B_GROUPS, B_SLOTS, HEAD_DIM)
            o_x_s = _dil_attn_decode(q_s, k_s, v_s, cache_win_k, cache_win_v)
            o_m_s = _mem_attn_decode(proj[n_p:, B_QHEADS * HEAD_DIM:].reshape(ns, MEM_HEADS, HEAD_DIM),
                                     cache_mem_k, cache_mem_v, layer)
            o_x = _dil_attn_prompt(proj, k_r, kv, o_x_s.reshape(ns, -1), bp, sp)
            o_m = _mem_attn_prompt(proj, B_QHEADS * HEAD_DIM // MEM_W, kvm, o_m_s.reshape(ns, -1), bp, sp, 512)
            w_out, w_out_layer = w_out_b, bl
        y = _proj([o_x, o_m], w_out, w_out_layer, F32, ROW_TILE, 1024, "proj_out")
        xf, xp, *routing = _postnorm_route(xf_p, xf_s, y, ln_g[layer, 0], ln_b[layer, 0], alpha,
                                           router_w, layer, router_b, "postnorm_route")
        xf_p, xf_s, xb = _moe(xf, n_p, xp, routing, layer, exp_w_gate, exp_w_up, exp_w_down,
                              sh_w_gate, sh_w_up, sh_w_down, ln_g[layer, 1], ln_b[layer, 1], alpha)

    w_p = min(max(w for w, _ in B_PATTERNS), sp)
    k_p = k_p.reshape(bp, sp, B_SLOTS, HEAD_DIM)
    v_p = v_p.reshape(bp, sp, B_SLOTS, HEAD_DIM)
    return (xf_p.reshape(bp, sp, d), xf_s.reshape(ns, 1, d),
            jnp.stack(hgrn_p), jnp.stack(hgrn_s),
            k_p[:, sp - w_p:], v_p[:, sp - w_p:],
            k_s.reshape(ns, 1, B_SLOTS, HEAD_DIM), v_s.reshape(ns, 1, B_SLOTS, HEAD_DIM),
            jnp.stack(mem_k_p), jnp.stack(mem_v_p))
```

```python
import functools

import jax
import jax.numpy as jnp
from jax import lax
from jax.experimental import pallas as pl
from jax.experimental.pallas import tpu as pltpu
from jax.experimental.pallas import tpu_sc as plsc

F32 = jnp.float32
BF16 = jnp.bfloat16

HEAD_DIM = 128
A_HEADS = 12
A_CHUNK = 64
A_SUB = 16
B_PATTERNS = ((128, 1), (512, 4), (2048, 16))
B_SLOTS = 4
B_GROUPS = len(B_PATTERNS)
B_QHEADS = B_GROUPS * B_SLOTS
B_BLOCK = 128
MEM_HEADS = 4
MEM_W = MEM_HEADS * HEAD_DIM
KV_W = B_SLOTS * HEAD_DIM
ROPE_THETA = 500000.0
ROPE_DIM = HEAD_DIM // 4
N_EXPERTS = 64
N_GROUPS = 8
TOPK_GROUPS = 4
TOP_K = 8
ROUTED_SCALE = 2.5
LN_EPS = 1e-5
RMS_EPS = 1e-6
ATTN_SCALE = HEAD_DIM ** -0.5
PAST_LEN = 2048

V7X_VMEM_LIMIT_BYTES = 56 * 1024 * 1024
V7X_SC_CORES = 2
V7X_SC_SUBCORES = 16
SUBLANES = 8

ROW_TILE = 640
PART_TILE = 128
A_HEADS_PER_STEP = 4
A_CHUNK_UNROLL = 4
DIL_BATCH = 8
MEM_DECODE_SEQS = 4
DIL_DECODE_SEQS = 2
MOE_BLOCK = 1152
MOE_SUB = 384
FFN_COL_SPLIT = 2
SC_WORKERS = V7X_SC_CORES * V7X_SC_SUBCORES
SC_CHUNK = 32
SC_GATHER_CHUNK = 40

NEG_BIG = -1e30


def _cparams(sem, vmem=None):
    return pltpu.CompilerParams(dimension_semantics=sem, vmem_limit_bytes=vmem)


def _dot(a, b):
    return jnp.dot(a, b, preferred_element_type=F32)


def _dot_nt(a, b):
    return lax.dot_general(a, b, (((1,), (1,)), ((), ())), preferred_element_type=F32)


def _sigmoid(x):
    return 1.0 / (1.0 + jnp.exp(-x))


def _proj_kernel(*refs, n_lhs, rope_heads):
    x_refs = refs[:n_lhs]
    w_refs = refs[n_lhs:2 * n_lhs]
    n_tab = 3 if rope_heads else 0
    tab_refs = refs[2 * n_lhs:2 * n_lhs + n_tab]
    o_ref = refs[2 * n_lhs + n_tab]
    wb_refs = refs[2 * n_lhs + n_tab + 1:]

    @pl.when(pl.program_id(1) == 0)
    def _():
        for w_ref, wb_ref in zip(w_refs, wb_refs):
            wb_ref[...] = w_ref[...].astype(BF16)

    acc = None
    for x_ref, wb_ref in zip(x_refs, wb_refs):
        d = _dot(x_ref[...].astype(BF16), wb_ref[...])
        acc = d if acc is None else acc + d
    if not rope_heads:
        o_ref[...] = acc.astype(o_ref.dtype)
    else:
        c, su, sd = (t[...] for t in tab_refs)
        heads_per_tile = o_ref.shape[1] // HEAD_DIM
        for h in range(heads_per_tile):
            sl = slice(h * HEAD_DIM, (h + 1) * HEAD_DIM)
            plain = acc[:, sl]
            roped = _rope_head(plain, c, su, sd)
            is_roped = pl.program_id(0) * heads_per_tile + h < rope_heads
            o_ref[:, sl] = jnp.where(is_roped, roped, plain).astype(o_ref.dtype)


def _proj(lhs, w, layer, out_dtype, tm, tn, name, rope_tables=None, rope_heads=0):
    m = lhs[0].shape[0]
    n = w.shape[2]
    koff = 0
    in_specs, w_specs, scratch = [], [], []
    for x in lhs:
        k = x.shape[1]
        assert koff % k == 0 and m % tm == 0 and n % tn == 0
        in_specs.append(pl.BlockSpec((tm, k), lambda j, i: (i, 0)))
        w_specs.append(pl.BlockSpec((None, k, tn), functools.partial(lambda j, i, kb: (layer, kb, j), kb=koff // k)))
        scratch.append(pltpu.VMEM((k, tn), BF16))
        koff += k
    assert koff == w.shape[1]
    tables = list(rope_tables) if rope_heads else []
    tab_specs = [pl.BlockSpec((tm, HEAD_DIM), lambda j, i: (i, 0))] * len(tables)
    return pl.pallas_call(
        functools.partial(_proj_kernel, n_lhs=len(lhs), rope_heads=rope_heads),
        grid=(n // tn, m // tm),
        in_specs=in_specs + w_specs + tab_specs,
        out_specs=pl.BlockSpec((tm, tn), lambda j, i: (i, j)),
        out_shape=jax.ShapeDtypeStruct((m, n), out_dtype),
        scratch_shapes=scratch,
        compiler_params=_cparams(("arbitrary", "arbitrary"), V7X_VMEM_LIMIT_BYTES),
        name=name,
    )(*lhs, *([w] * len(lhs)), *tables)


def _pack_halves(x):
    c = x.shape[1] // 2
    lo = pltpu.bitcast(x[:, :c].astype(BF16).astype(F32), jnp.int32)
    hi = pltpu.bitcast(x[:, c:].astype(BF16).astype(F32), jnp.int32)
    return hi | lax.shift_right_logical(lo, 16)


def _unpack_halves(w):
    lo = pltpu.bitcast(lax.shift_left(w, 16), F32)
    hi = pltpu.bitcast(w & jnp.int32(-65536), F32)
    return lo, hi


def _layer_norm(z, g, b):
    mu = jnp.mean(z, axis=-1, keepdims=True)
    zc = z - mu
    var = jnp.mean(zc * zc, axis=-1, keepdims=True)
    return zc * lax.rsqrt(var + LN_EPS) * g + b


def _part_specs(n_p, block):
    tiles_p = n_p // PART_TILE
    rest = (0,) * (len(block) - 1)
    return (pl.BlockSpec(block, lambda i: (jnp.minimum(i, tiles_p - 1),) + rest),
            pl.BlockSpec(block, lambda i: (jnp.maximum(i - tiles_p, 0),) + rest))


def _on_part(tiles_p, fn):
    i = pl.program_id(0)
    pl.when(i < tiles_p)(functools.partial(fn, 0))
    pl.when(i >= tiles_p)(functools.partial(fn, 1))


def _postnorm_kernel(xp_ref, xs_ref, y_ref, g_ref, b_ref, wt_ref, rb_ref, of_ref, op_ref, e_ref, gate_ref, cnt_ref,
                     *, alpha, tiles_p):
    def part(which):
        x_ref = (xp_ref, xs_ref)[which]
        out = _layer_norm(alpha * x_ref[...] + y_ref[...], g_ref[...], b_ref[...])
        of_ref[...] = out
        op_ref[...] = _pack_halves(out)
        e_ref[...], gate_ref[...], cnt_ref[...] = _route_tile(out, wt_ref[...], rb_ref[...])

    _on_part(tiles_p, part)


def _postnorm_route(x_p, x_s, y, g, b, alpha, router_w, layer, router_b, name):
    n_p, d = x_p.shape
    m = n_p + x_s.shape[0]
    e = router_w.shape[2]
    t = PART_TILE
    row = pl.BlockSpec((t, d), lambda i: (i, 0))
    vec = pl.BlockSpec((1, d), lambda i: (0, 0))
    tok = pl.BlockSpec((TOP_K, t), lambda i: (0, i))
    part_p, part_s = _part_specs(n_p, (t, d))
    return pl.pallas_call(
        functools.partial(_postnorm_kernel, alpha=alpha, tiles_p=n_p // t),
        grid=(m // t,),
        in_specs=[part_p, part_s, row, vec, vec,
                  pl.BlockSpec((None, e, d), lambda i: (layer, 0, 0)), pl.BlockSpec((e, 1), lambda i: (0, 0))],
        out_specs=[row, pl.BlockSpec((t, d // 2), lambda i: (i, 0)), tok, tok,
                   pl.BlockSpec((None, e, 1), lambda i: (i, 0, 0))],
        out_shape=[jax.ShapeDtypeStruct((m, d), F32), jax.ShapeDtypeStruct((m, d // 2), jnp.int32),
                   jax.ShapeDtypeStruct((TOP_K, m), jnp.int32), jax.ShapeDtypeStruct((TOP_K, m), F32),
                   jax.ShapeDtypeStruct((m // t, e, 1), jnp.int32)],
        compiler_params=_cparams(("arbitrary",)),
        name=name,
    )(x_p, x_s, y, g.reshape(1, d), b.reshape(1, d), jnp.swapaxes(router_w, 1, 2),
      router_b[layer].astype(F32).reshape(e, 1))


def _hgrn_prompt_kernel(q_ref, f_ref, v_ref, gate_ref, lb_ref, gn_ref, os_ref, o_ref, s_ref, st_ref,
                        *, seq, heads, n_batch):
    _prompt_then_sample(n_batch, os_ref, o_ref,
                        functools.partial(_hgrn_sequence, q_ref, f_ref, v_ref, gate_ref, lb_ref, gn_ref,
                                          o_ref, s_ref, st_ref, seq=seq, heads=heads), axis=1)


def _hgrn_sequence(q_ref, f_ref, v_ref, gate_ref, lb_ref, gn_ref, o_ref, s_ref, st_ref, *, seq, heads):
    c = A_CHUNK
    n_chunks = seq // c
    n_sub = c // A_SUB
    gn = gn_ref[...]
    row = lax.broadcasted_iota(jnp.int32, (c, c), 0)
    col = lax.broadcasted_iota(jnp.int32, (c, c), 1)
    tril = jnp.where(row >= col, 1.0, 0.0).astype(BF16)
    same_sub = (row // A_SUB) == (col // A_SUB)
    diag_dist = jnp.where(same_sub, row - col, -1)
    off_mask = col < (row // A_SUB) * A_SUB
    st_ref[...] = jnp.zeros_like(st_ref)

    def chunk(ci, carry):
        for hh in range(heads):
            head_chunk(ci, hh)
        return carry

    def head_chunk(ci, hh):
        r0 = pl.multiple_of(ci * c, c)
        hsl = slice(hh * HEAD_DIM, (hh + 1) * HEAD_DIM)
        lb = lb_ref[hh]
        qp = q_ref[pl.ds(r0, c), hsl]
        fp = f_ref[pl.ds(r0, c), hsl]
        v = v_ref[pl.ds(r0, c), hsl]
        gp = gate_ref[pl.ds(r0, c), hsl]
        st = st_ref[hh]
        q = qp * _sigmoid(qp)
        forget = lb + (1.0 - lb) * _sigmoid(fp)
        logf = jnp.log2(forget)
        k = 1.0 - forget
        hi = logf.astype(BF16)
        r1 = logf - hi.astype(F32)
        mid = r1.astype(BF16)
        lo = (r1 - mid.astype(F32)).astype(BF16)
        g = _dot(tril, hi) + _dot(tril, mid) + _dot(tril, lo)
        v_b = v.astype(BF16)
        o = _dot_nt((q * jnp.exp2(g)).astype(BF16), st.astype(BF16))
        rows = [jnp.zeros((A_SUB, c), F32)]
        for i in range(1, n_sub):
            gref = g[i * A_SUB:i * A_SUB + 1, :]
            qt = q[i * A_SUB:(i + 1) * A_SUB, :] * jnp.exp2(g[i * A_SUB:(i + 1) * A_SUB, :] - gref)
            kt = k * jnp.exp2(jnp.minimum(gref - g, 0.0))
            rows.append(_dot_nt(qt.astype(BF16), kt.astype(BF16)))
        a = jnp.where(off_mask, jnp.concatenate(rows, axis=0), 0.0)
        for d in range(A_SUB):
            kr = k if d == 0 else pltpu.roll(k, d, axis=0)
            gr = g if d == 0 else pltpu.roll(g, d, axis=0)
            x = q * kr * jnp.exp2(g - gr)
            a = jnp.where(diag_dist == d, jnp.sum(x, axis=-1, keepdims=True), a)
        o = o + _dot(a.astype(BF16), v_b)
        gend = g[c - 1:c, :]
        kt_end = k * jnp.exp2(gend - g)
        st_ref[hh] = jnp.exp2(gend) * st + _dot(v_b.T, kt_end.astype(BF16))
        o = o * lax.rsqrt(jnp.mean(o * o, axis=-1, keepdims=True) + RMS_EPS) * gn
        o_ref[pl.ds(r0, c), hsl] = (o * (gp * _sigmoid(gp))).astype(o_ref.dtype)

    lax.fori_loop(0, n_chunks, chunk, 0, unroll=A_CHUNK_UNROLL)
    for hh in range(heads):
        s_ref[hh] = st_ref[hh].T


def _hgrn_prompt(proj, lb, g_norm, o_sample, n_batch, seq):
    h = A_HEADS
    hp = A_HEADS_PER_STEP
    ns = o_sample.shape[0]
    assert h % hp == 0 and ns <= seq and proj.shape[0] == n_batch * seq + ns
    ng = h // hp
    w = hp * HEAD_DIM
    last = n_batch - 1
    blk = lambda off: pl.BlockSpec((seq, w), functools.partial(lambda hg, b, off: (jnp.minimum(b, last), off + hg), off=off))
    return pl.pallas_call(
        functools.partial(_hgrn_prompt_kernel, seq=seq, heads=hp, n_batch=n_batch),
        grid=(ng, n_batch + 1),
        in_specs=[blk(0), blk(ng), blk(2 * ng), blk(3 * ng),
                  pl.BlockSpec((hp, 1, HEAD_DIM), lambda hg, b: (hg, 0, 0)),
                  pl.BlockSpec((1, HEAD_DIM), lambda hg, b: (0, 0)),
                  pl.BlockSpec((ns, w), lambda hg, b: (0, hg))],
        out_specs=[pl.BlockSpec((seq, w), lambda hg, b: (b, hg)),
                   pl.BlockSpec((None, hp, HEAD_DIM, HEAD_DIM), lambda hg, b: (jnp.minimum(b, last), hg, 0, 0))],
        out_shape=[jax.ShapeDtypeStruct((proj.shape[0], h * HEAD_DIM), BF16),
                   jax.ShapeDtypeStruct((n_batch, h, HEAD_DIM, HEAD_DIM), F32)],
        scratch_shapes=[pltpu.VMEM((hp, HEAD_DIM, HEAD_DIM), F32)],
        compiler_params=_cparams(("arbitrary", "arbitrary"), V7X_VMEM_LIMIT_BYTES),
        name="hgrn_prompt",
    )(proj, proj, proj, proj, lb.reshape(h, 1, HEAD_DIM), g_norm.reshape(1, HEAD_DIM), o_sample)


def _softmax_av(s, v):
    m = jnp.max(s, axis=-1, keepdims=True)
    p = jnp.exp(s - m)
    l = jnp.sum(p, axis=-1, keepdims=True)
    return _dot(p.astype(BF16), v) / l


def _prompt_then_sample(n_prompt_steps, os_ref, o_ref, prompt_step, axis=0):
    i = pl.program_id(axis)
    pl.when(i < n_prompt_steps)(prompt_step)

    @pl.when(i >= n_prompt_steps)
    def _():
        o_ref[:os_ref.shape[0], :] = os_ref[...]


def _mem_attn_kernel(q_ref, k_ref, v_ref, os_ref, o_ref, *, n_steps):
    def step():
        for h in range(MEM_HEADS):
            sl = slice(h * HEAD_DIM, (h + 1) * HEAD_DIM)
            s = _dot_nt(q_ref[:, sl].astype(BF16), k_ref[:, sl].astype(BF16)) * ATTN_SCALE
            o_ref[:, sl] = _softmax_av(s, v_ref[:, sl].astype(BF16)).astype(o_ref.dtype)

    _prompt_then_sample(n_steps, os_ref, o_ref, step)


def _mem_attn_prompt(proj, q_col, kv, o_sample, n_batch, seq, tq):
    mem_len = kv.shape[0] // n_batch
    nq = seq // tq
    n_steps = n_batch * nq
    ns = o_sample.shape[0]
    assert ns <= tq and proj.shape[0] == n_batch * seq + ns
    last = n_steps - 1
    return pl.pallas_call(
        functools.partial(_mem_attn_kernel, n_steps=n_steps),
        grid=(n_steps + 1,),
        in_specs=[pl.BlockSpec((tq, MEM_W), lambda i: (jnp.minimum(i, last), q_col)),
                  pl.BlockSpec((mem_len, MEM_W), lambda i: (jnp.minimum(i, last) // nq, 0)),
                  pl.BlockSpec((mem_len, MEM_W), lambda i: (jnp.minimum(i, last) // nq, 1)),
                  pl.BlockSpec((ns, MEM_W), lambda i: (0, 0))],
        out_specs=pl.BlockSpec((tq, MEM_W), lambda i: (i, 0)),
        out_shape=jax.ShapeDtypeStruct((proj.shape[0], MEM_W), BF16),
        compiler_params=_cparams(("arbitrary",)),
        name="mem_attn_prompt",
    )(proj, kv, kv, o_sample)


def _rope_tables(pos):
    half = ROPE_DIM // 2
    inv_freq = ROPE_THETA ** (-jnp.arange(0, ROPE_DIM, 2, dtype=F32) / ROPE_DIM)
    ang = pos.astype(F32)[:, None] * inv_freq[None, :]
    cos, sin = jnp.cos(ang), jnp.sin(ang)
    n = pos.shape[0]
    one = jnp.ones((n, HEAD_DIM - ROPE_DIM), F32)
    zero = jnp.zeros((n, HEAD_DIM - half), F32)
    c = jnp.concatenate([cos, cos, one], axis=1)
    s_up = jnp.concatenate([-sin, zero], axis=1)
    s_dn = jnp.concatenate([jnp.zeros((n, half), F32), sin, zero[:, half:]], axis=1)
    return c, s_up, s_dn


def _rope_head(x, c, su, sd):
    half = ROPE_DIM // 2
    up = pltpu.roll(x, HEAD_DIM - half, axis=1)
    dn = pltpu.roll(x, half, axis=1)
    return x * c + up * su + dn * sd


def _kv_kernel(kv_ref, c_ref, su_ref, sd_ref, k_ref, kp_ref, ks_ref, vp_ref, vs_ref, *, tiles_p):
    c, su, sd = c_ref[...], su_ref[...], sd_ref[...]

    def part(which):
        k4_ref, v4_ref = ((kp_ref, vp_ref), (ks_ref, vs_ref))[which]
        for h in range(B_SLOTS):
            sl = slice(h * HEAD_DIM, (h + 1) * HEAD_DIM)
            k = _rope_head(kv_ref[:, sl], c, su, sd)
            k_ref[:, sl] = k
            k4_ref[:, h, :] = k
            v4_ref[:, h, :] = kv_ref[:, KV_W + h * HEAD_DIM:KV_W + (h + 1) * HEAD_DIM]

    _on_part(tiles_p, part)


def _shared_kv(kv, tables, n_p):
    m = kv.shape[0]
    tm = PART_TILE
    tab = pl.BlockSpec((tm, HEAD_DIM), lambda i: (i, 0))
    part_p, part_s = _part_specs(n_p, (tm, B_SLOTS, HEAD_DIM))
    cache = lambda rows: jax.ShapeDtypeStruct((rows, B_SLOTS, HEAD_DIM), F32)
    return pl.pallas_call(
        functools.partial(_kv_kernel, tiles_p=n_p // tm),
        grid=(m // tm,),
        in_specs=[pl.BlockSpec((tm, 2 * KV_W), lambda i: (i, 0)), tab, tab, tab],
        out_specs=[pl.BlockSpec((tm, KV_W), lambda i: (i, 0)), part_p, part_s, part_p, part_s],
        out_shape=[jax.ShapeDtypeStruct((m, KV_W), F32), cache(n_p), cache(m - n_p), cache(n_p), cache(m - n_p)],
        compiler_params=_cparams(("arbitrary",)),
        name="shared_kv",
    )(kv, *tables)


def _dil_prompt_kernel(*refs, seq, n_batch):
    os_ref, o_ref = refs[B_GROUPS + 2:B_GROUPS + 4]
    _prompt_then_sample(n_batch, os_ref, o_ref, functools.partial(_dil_sequence, *refs, seq=seq), axis=1)


def _dil_sequence(*refs, seq):
    q_refs = refs[:B_GROUPS]
    k_ref, v_ref, _, o_ref, m_ref, l_ref, acc_ref = refs[B_GROUPS:]
    blk = B_BLOCK
    base = (lax.broadcasted_iota(jnp.int32, (blk, blk), 0) - lax.broadcasted_iota(jnp.int32, (blk, blk), 1))
    for g, (win, dil) in enumerate(B_PATTERNS):
        span = win // dil
        n_blk = seq // dil // blk
        assert span <= blk

        def body(it, carry, g=g, dil=dil, span=span, n_blk=n_blk):
            own, prev, prev_ok = [], [], []
            for j in range(DIL_BATCH):
                t = it * DIL_BATCH + j
                r, n = t % dil, t // dil

                def rows(nn, r=r):
                    start = nn * (blk * dil) + r
                    return pl.ds(pl.multiple_of(start, blk), blk) if dil == 1 else pl.ds(start, blk, stride=dil)

                own.append(rows(n))
                prev.append(rows(jnp.maximum(n - 1, 0)))
                prev_ok.append(base <= jnp.where(n > 0, span - blk, -blk - 1))
            load = lambda ref, idx: jnp.stack([ref[i, :].astype(BF16) for i in idx])
            scores = lambda a, b: jnp.einsum('bqd,bkd->bqk', a, b, preferred_element_type=F32) * ATTN_SCALE
            weighted = lambda p, v: jnp.einsum('bqk,bkd->bqd', p.astype(BF16), v, preferred_element_type=F32)
            q = load(q_refs[g], own)
            s = jnp.where((base >= 0)[None], scores(q, load(k_ref, own)), NEG_BIG)
            m_b = jnp.max(s, axis=-1, keepdims=True)
            if n_blk > 1:
                s_prev = jnp.where(jnp.stack(prev_ok), scores(q, load(k_ref, prev)), NEG_BIG)
                m_b = jnp.maximum(m_b, jnp.max(s_prev, axis=-1, keepdims=True))
            p = jnp.exp(s - m_b)
            l_b = jnp.sum(p, axis=-1, keepdims=True)
            acc_b = weighted(p, load(v_ref, own))
            if n_blk > 1:
                p_prev = jnp.exp(s_prev - m_b)
                l_b = l_b + jnp.sum(p_prev, axis=-1, keepdims=True)
                acc_b = acc_b + weighted(p_prev, load(v_ref, prev))
            for j, idx in enumerate(own):
                if g == 0:
                    m_ref[idx, :] = jnp.broadcast_to(m_b[j], (blk, HEAD_DIM))
                    l_ref[idx, :] = jnp.broadcast_to(l_b[j], (blk, HEAD_DIM))
                    acc_ref[idx, :] = acc_b[j]
                else:
                    m_old = m_ref[idx, :]
                    m_new = jnp.maximum(m_old, m_b[j])
                    a_old = jnp.exp(m_old - m_new)
                    a_b = jnp.exp(m_b[j] - m_new)
                    m_ref[idx, :] = m_new
                    l_ref[idx, :] = l_ref[idx, :] * a_old + l_b[j] * a_b
                    acc_ref[idx, :] = acc_ref[idx, :] * a_old + acc_b[j] * a_b
            return carry

        assert (dil * n_blk) % DIL_BATCH == 0
        lax.fori_loop(0, dil * n_blk // DIL_BATCH, body, 0)
    o_ref[...] = (acc_ref[...] / l_ref[...]).astype(o_ref.dtype)


def _dil_attn_prompt(q, k, kv, o_sample, n_batch, seq):
    ns = o_sample.shape[0]
    assert ns <= seq and q.shape[0] == n_batch * seq + ns
    last = n_batch - 1
    col = lambda c0: pl.BlockSpec((seq, HEAD_DIM), functools.partial(lambda h, b, c0: (jnp.minimum(b, last), c0 + h), c0=c0))
    return pl.pallas_call(
        functools.partial(_dil_prompt_kernel, seq=seq, n_batch=n_batch),
        grid=(B_SLOTS, n_batch + 1),
        in_specs=[col(g * B_SLOTS) for g in range(B_GROUPS)] + [col(0), col(B_SLOTS),
                                                                 pl.BlockSpec((ns, HEAD_DIM), lambda h, b: (0, h))],
        out_specs=pl.BlockSpec((seq, HEAD_DIM), lambda h, b: (b, h)),
        out_shape=jax.ShapeDtypeStruct((q.shape[0], KV_W), BF16),
        scratch_shapes=[pltpu.VMEM((seq, HEAD_DIM), F32)] * 3,
        compiler_params=_cparams(("arbitrary", "arbitrary"), V7X_VMEM_LIMIT_BYTES),
        name="dil_attn_prompt",
    )(*([q] * B_GROUPS), k, kv, o_sample)


def _mem_decode_kernel(q_ref, k_ref, v_ref, o_ref):
    n_seq, rows, _ = k_ref.shape
    nh = q_ref.shape[1]
    own = (lax.broadcasted_iota(jnp.int32, (nh, rows), 1) % nh) == lax.broadcasted_iota(jnp.int32, (nh, rows), 0)
    for i in range(n_seq):
        s = _dot_nt(q_ref[i].astype(BF16), k_ref[i].astype(BF16)) * ATTN_SCALE
        o_ref[i] = _softmax_av(jnp.where(own, s, NEG_BIG), v_ref[i].astype(BF16)).astype(o_ref.dtype)


def _mem_attn_decode(q, cache_k, cache_v, layer):
    ns, nh, hd = q.shape
    n_layers, _, mem_len = cache_k.shape[:3]
    g = MEM_DECODE_SEQS
    assert ns % g == 0
    rows = mem_len * nh
    cspec = pl.BlockSpec((None, g, rows, hd), lambda b: (layer, b, 0, 0))
    qspec = pl.BlockSpec((g, nh, hd), lambda b: (b, 0, 0))
    return pl.pallas_call(
        _mem_decode_kernel,
        grid=(ns // g,),
        in_specs=[qspec, cspec, cspec],
        out_specs=qspec,
        out_shape=jax.ShapeDtypeStruct((ns, nh, hd), BF16),
        compiler_params=_cparams(("parallel",)),
        name="mem_attn_decode",
    )(q, cache_k.reshape(n_layers, ns, rows, hd), cache_v.reshape(n_layers, ns, rows, hd))


def _dil_decode_kernel(q_ref, kn_ref, vn_ref, *refs):
    o_ref = refs[-1]
    n_slots = kn_ref.shape[1]
    for i in range(q_ref.shape[0]):
        kn, vn = kn_ref[i], vn_ref[i]
        scores, values = [], []
        for g in range(B_GROUPS):
            k_ref, v_ref = refs[2 * g], refs[2 * g + 1]
            rows = k_ref.shape[1] * (k_ref.shape[2] if len(k_ref.shape) == 4 else 1)
            period = k_ref.shape[2] if len(k_ref.shape) == 4 else n_slots
            k = k_ref[i].reshape(rows, HEAD_DIM).astype(BF16)
            v = v_ref[i].reshape(rows, HEAD_DIM).astype(BF16)
            q = q_ref[i, g]
            own = (lax.broadcasted_iota(jnp.int32, (n_slots, rows), 1) % period
                   == lax.broadcasted_iota(jnp.int32, (n_slots, rows), 0))
            scores.append(jnp.where(own, _dot_nt(q.astype(BF16), k) * ATTN_SCALE, NEG_BIG))
            values.append(v)
            scores.append(jnp.sum(q * kn, axis=-1, keepdims=True) * ATTN_SCALE)
            values.append(None)
        m = functools.reduce(jnp.maximum, [jnp.max(s, axis=-1, keepdims=True) for s in scores])
        l = jnp.zeros_like(m)
        acc = jnp.zeros((n_slots, HEAD_DIM), F32)
        for s, v in zip(scores, values):
            p = jnp.exp(s - m)
            l = l + jnp.sum(p, axis=-1, keepdims=True)
            acc = acc + (p * vn if v is None else _dot(p.astype(BF16), v))
        o_ref[i] = (acc / l).astype(o_ref.dtype)


def _dil_attn_decode(q, k_new, v_new, cache_k, cache_v):
    ns, w_buf, ns_slots, hd = cache_k.shape
    n_seq = DIL_DECODE_SEQS
    assert ns % n_seq == 0
    in_specs = [pl.BlockSpec((n_seq, B_GROUPS, ns_slots, hd), lambda b: (b, 0, 0, 0)),
                pl.BlockSpec((n_seq, ns_slots, hd), lambda b: (b, 0, 0)),
                pl.BlockSpec((n_seq, ns_slots, hd), lambda b: (b, 0, 0))]
    args = [q, k_new, v_new]
    for win, dil in B_PATTERNS:
        span = win // dil
        assert w_buf % dil == 0 and (w_buf // dil) % span == 0
        last = w_buf // dil // span - 1
        if dil == 1:
            view = (ns, w_buf * ns_slots, hd)
            spec = pl.BlockSpec((n_seq, span * ns_slots, hd), functools.partial(lambda b, last: (b, last, 0), last=last))
        else:
            assert dil * ns_slots >= SUBLANES and ns_slots <= SUBLANES
            view = (ns, w_buf // dil, dil * ns_slots, hd)
            spec = pl.BlockSpec((n_seq, span, SUBLANES, hd), functools.partial(lambda b, last: (b, last, 0, 0), last=last))
        in_specs += [spec, spec]
        args += [cache_k.reshape(view), cache_v.reshape(view)]
    return pl.pallas_call(
        _dil_decode_kernel,
        grid=(ns // n_seq,),
        in_specs=in_specs,
        out_specs=pl.BlockSpec((n_seq, ns_slots, hd), lambda b: (b, 0, 0)),
        out_shape=jax.ShapeDtypeStruct((ns, ns_slots, hd), BF16),
        compiler_params=_cparams(("parallel",)),
        name="dil_attn_decode",
    )(*args)


def _hgrn_step_kernel(p_ref, s_ref, lb_ref, gn_ref, o_ref, so_ref, rows_ref):
    h = A_HEADS
    hk = h * HEAD_DIM
    gn = gn_ref[...]
    rows_ref[...] = jnp.zeros_like(rows_ref)
    for i in range(h):
        fp = p_ref[:, hk + i * HEAD_DIM:hk + (i + 1) * HEAD_DIM]
        lb = lb_ref[i:i + 1, :]
        rows_ref[i:i + 1, :] = lb + (1.0 - lb) * _sigmoid(fp)
    cols = rows_ref[...].T
    first = lax.broadcasted_iota(jnp.int32, (SUBLANES, HEAD_DIM), 0) == 0
    for i in range(h):
        sl = slice(i * HEAD_DIM, (i + 1) * HEAD_DIM)
        qp = p_ref[:, sl]
        v = p_ref[:, 2 * hk + i * HEAD_DIM:2 * hk + (i + 1) * HEAD_DIM]
        gp = p_ref[:, 3 * hk + i * HEAD_DIM:3 * hk + (i + 1) * HEAD_DIM]
        k = 1.0 - rows_ref[i:i + 1, :]
        k8 = jnp.where(first, k, 0.0).astype(BF16)
        v8 = jnp.broadcast_to(v, (SUBLANES, HEAD_DIM)).astype(BF16)
        kv = lax.dot_general(k8, v8, (((0,), (0,)), ((), ())), preferred_element_type=F32)
        s_new = cols[:, i:i + 1] * s_ref[i] + kv
        so_ref[i] = s_new
        q8 = jnp.broadcast_to(qp * _sigmoid(qp), (SUBLANES, HEAD_DIM)).astype(BF16)
        o = _dot(q8, s_new.astype(BF16))[0:1]
        o = o * lax.rsqrt(jnp.mean(o * o, axis=-1, keepdims=True) + RMS_EPS) * gn
        o_ref[:, sl] = (o * (gp * _sigmoid(gp))).astype(o_ref.dtype)


def _hgrn_step(proj, state, layer, lb, g_norm):
    ns = proj.shape[0]
    h = A_HEADS
    sspec = pl.BlockSpec((None, h, HEAD_DIM, HEAD_DIM), lambda b: (b, 0, 0, 0))
    return pl.pallas_call(
        _hgrn_step_kernel,
        grid=(ns,),
        in_specs=[pl.BlockSpec((None, 1, proj.shape[2]), lambda b: (b, 0, 0)),
                  pl.BlockSpec((None, None, h, HEAD_DIM, HEAD_DIM), lambda b: (layer, b, 0, 0, 0)),
                  pl.BlockSpec((h, HEAD_DIM), lambda b: (0, 0)),
                  pl.BlockSpec((1, HEAD_DIM), lambda b: (0, 0))],
        out_specs=[pl.BlockSpec((None, 1, h * HEAD_DIM), lambda b: (b, 0, 0)), sspec],
        out_shape=[jax.ShapeDtypeStruct((ns, 1, h * HEAD_DIM), BF16),
                   jax.ShapeDtypeStruct((ns, h, HEAD_DIM, HEAD_DIM), F32)],
        scratch_shapes=[pltpu.VMEM((HEAD_DIM, HEAD_DIM), F32)],
        compiler_params=_cparams(("parallel",)),
        name="hgrn_step",
    )(proj, state, lb.reshape(h, HEAD_DIM), g_norm.reshape(1, HEAD_DIM))


def _ffn_kernel(be_ref, nv_ref, od_ref, nu_ref, x_ref, wg_ref, wu_ref, wd_ref, o_ref, wgb_ref, wub_ref, wdb_ref, *, sub):
    i = pl.program_id(0)
    n_valid = nv_ref[i]

    @pl.when((n_valid > 0) & ((i == 0) | (be_ref[i] != be_ref[jnp.maximum(i - 1, 0)])))
    def _():
        wgb_ref[...] = wg_ref[...].astype(BF16)
        wub_ref[...] = wu_ref[...].astype(BF16)
        wdb_ref[...] = wd_ref[...].astype(BF16)

    for j in range(x_ref.shape[0] // sub):
        rows = slice(j * sub, (j + 1) * sub)

        @pl.when(n_valid > j * sub)
        def _(j=j, rows=rows):
            lo, hi = _unpack_halves(x_ref[rows, :])
            c = lo.shape[1]
            keep = lax.broadcasted_iota(jnp.int32, lo.shape, 0) < n_valid - j * sub
            lo = jnp.where(keep, lo, 0.0).astype(BF16)
            hi = jnp.where(keep, hi, 0.0).astype(BF16)
            ff = wgb_ref.shape[1]
            y = None
            for part in range(FFN_COL_SPLIT):
                cs = slice(part * ff // FFN_COL_SPLIT, (part + 1) * ff // FFN_COL_SPLIT)
                hg = _dot(lo, wgb_ref[:c, cs]) + _dot(hi, wgb_ref[c:, cs])
                hu = _dot(lo, wub_ref[:c, cs]) + _dot(hi, wub_ref[c:, cs])
                hid = (hg * _sigmoid(hg) * hu).astype(BF16)
                yp = _dot(hid, wdb_ref[cs, :])
                y = yp if y is None else y + yp
            o_ref[rows, :] = _pack_halves(y)

        @pl.when(n_valid <= j * sub)
        def _(rows=rows):
            o_ref[rows, :] = jnp.zeros((sub, o_ref.shape[1]), o_ref.dtype)


def _expert_ffn(x, blk_exp, blk_valid, blk_order, n_used, w_gate, w_up, w_down, layer, tm, sub, name):
    r = x.shape[0]
    d, ff = w_gate.shape[2:]
    assert tm % sub == 0
    w_spec = lambda shape: pl.BlockSpec((None, None) + shape, lambda i, be, nv, od, nu: (layer, be[i], 0, 0))
    return pl.pallas_call(
        functools.partial(_ffn_kernel, sub=sub),
        grid_spec=pltpu.PrefetchScalarGridSpec(
            num_scalar_prefetch=4,
            grid=(r // tm,),
            in_specs=[pl.BlockSpec((tm, d // 2), lambda i, be, nv, od, nu: (od[jnp.minimum(i, nu[0] - 1)], 0)),
                      w_spec((d, ff)), w_spec((d, ff)), w_spec((ff, d))],
            out_specs=pl.BlockSpec((tm, d // 2), lambda i, be, nv, od, nu: (od[i], 0)),
            scratch_shapes=[pltpu.VMEM((d, ff), BF16), pltpu.VMEM((d, ff), BF16), pltpu.VMEM((ff, d), BF16)],
        ),
        out_shape=jax.ShapeDtypeStruct((r, d // 2), jnp.int32),
        compiler_params=_cparams(("arbitrary",), V7X_VMEM_LIMIT_BYTES),
        name=name,
    )(blk_exp, blk_valid, blk_order, n_used, x, w_gate, w_up, w_down)


def _first_argmax(val, idx, sentinel):
    m = jnp.max(val, axis=0, keepdims=True)
    i = jnp.min(jnp.where(val == m, idx, sentinel), axis=0, keepdims=True)
    return m, i


def _route_tile(x, w, b):
    t = x.shape[0]
    xh = x.astype(BF16)
    xl = (x - xh.astype(F32)).astype(BF16)
    wh = w.astype(BF16)
    wl = (w - wh.astype(F32)).astype(BF16)
    logits = _dot_nt(wh, xh) + (_dot_nt(wl, xh) + _dot_nt(wh, xl))
    scores = _sigmoid(logits)
    biased = scores + b
    gs = N_EXPERTS // N_GROUPS
    neg = -jnp.inf
    eid = lax.broadcasted_iota(jnp.int32, (N_EXPERTS, t), 0)
    sub = lax.broadcasted_iota(jnp.int32, (gs, t), 0)
    grow = lax.broadcasted_iota(jnp.int32, (N_GROUPS, t), 0)
    grp = jnp.zeros((N_GROUPS, t), F32)
    for g in range(N_GROUPS):
        bg = biased[g * gs:(g + 1) * gs]
        m1, i1 = _first_argmax(bg, sub, gs)
        m2 = jnp.max(jnp.where(sub == i1, neg, bg), axis=0, keepdims=True)
        grp = jnp.where(grow == g, m1 + m2, grp)
    chosen = jnp.zeros((N_GROUPS, t), F32)
    for _ in range(TOPK_GROUPS):
        _, gi = _first_argmax(grp, grow, N_GROUPS)
        hit = grow == gi
        chosen = jnp.where(hit, 1.0, chosen)
        grp = jnp.where(hit, neg, grp)
    chosen_e = jnp.concatenate([jnp.broadcast_to(chosen[g:g + 1], (gs, t)) for g in range(N_GROUPS)], axis=0)
    masked = jnp.where(chosen_e > 0.0, biased, neg)
    krow = lax.broadcasted_iota(jnp.int32, (TOP_K, t), 0)
    e_out = jnp.zeros((TOP_K, t), jnp.int32)
    g_out = jnp.zeros((TOP_K, t), F32)
    member = jnp.zeros((N_EXPERTS, t), F32)
    for k in range(TOP_K):
        _, idx = _first_argmax(masked, eid, N_EXPERTS)
        hit = eid == idx
        gk = jnp.sum(jnp.where(hit, scores, 0.0), axis=0, keepdims=True)
        masked = jnp.where(hit, neg, masked)
        member = jnp.where(hit, 1.0, member)
        e_out = jnp.where(krow == k, idx, e_out)
        g_out = jnp.where(krow == k, gk, g_out)
    g_out = g_out / jnp.sum(g_out, axis=0, keepdims=True) * ROUTED_SCALE
    return e_out, g_out, jnp.sum(member, axis=1, keepdims=True).astype(jnp.int32)


def _slots_kernel(e_ref, base_ref, o_ref):
    e = e_ref[...]
    t = e.shape[1]
    eid = lax.broadcasted_iota(jnp.int32, (N_EXPERTS, t), 0)
    member = jnp.zeros((N_EXPERTS, t), F32)
    for k in range(TOP_K):
        member = jnp.where(eid == e[k:k + 1], 1.0, member)
    before = (lax.broadcasted_iota(jnp.int32, (t, t), 0) < lax.broadcasted_iota(jnp.int32, (t, t), 1))
    rank = _dot(member.astype(BF16), jnp.where(before, 1.0, 0.0).astype(BF16))
    slot = rank + base_ref[...].astype(F32)
    krow = lax.broadcasted_iota(jnp.int32, (TOP_K, t), 0)
    out = jnp.zeros((TOP_K, t), F32)
    for k in range(TOP_K):
        sk = jnp.sum(jnp.where(eid == e[k:k + 1], slot, 0.0), axis=0, keepdims=True)
        out = jnp.where(krow == k, sk, out)
    o_ref[...] = out.astype(jnp.int32)


def _dispatch_plan(e_idx, counts, tm, tile):
    k, n = e_idx.shape
    n_blocks = -(-(n * k) // tm) + N_EXPERTS
    counts = counts.reshape(n // tile, N_EXPERTS)
    total = jnp.sum(counts, axis=0)
    padded = (total + tm - 1) // tm * tm
    pad_end = jnp.cumsum(padded)
    tile_base = (pad_end - padded)[None, :] + jnp.cumsum(counts, axis=0) - counts
    slot_of = pl.pallas_call(
        _slots_kernel,
        grid=(n // tile,),
        in_specs=[pl.BlockSpec((k, tile), lambda i: (0, i)),
                  pl.BlockSpec((None, N_EXPERTS, 1), lambda i: (i, 0, 0))],
        out_specs=pl.BlockSpec((k, tile), lambda i: (0, i)),
        out_shape=jax.ShapeDtypeStruct((k, n), jnp.int32),
        compiler_params=_cparams(("parallel",)),
        name="slots",
    )(e_idx, tile_base.astype(jnp.int32).reshape(n // tile, N_EXPERTS, 1))
    pos = jnp.arange(n_blocks, dtype=jnp.int32)
    blk_exp = jnp.minimum(jnp.sum((pad_end[None, :] <= (pos * tm)[:, None]).astype(jnp.int32), axis=1), N_EXPERTS - 1)
    is_exp = blk_exp[:, None] == jnp.arange(N_EXPERTS, dtype=blk_exp.dtype)[None, :]
    per_blk = lambda v: jnp.sum(jnp.where(is_exp, v[None, :], 0), axis=1)
    n_used = jnp.maximum(pad_end[-1:] // tm, 1).astype(jnp.int32)
    first, count = per_blk((pad_end - padded) // tm), jnp.maximum(per_blk(padded // tm), 1)
    blk_order = jnp.where(pos < n_used[0], first + (pos - first + count - 1) % count, pos)
    blk_valid = jnp.clip(per_blk(pad_end - padded + total) - blk_order * tm, 0, tm)
    return (slot_of, blk_exp.astype(jnp.int32), blk_valid.astype(jnp.int32), blk_order.astype(jnp.int32), n_used,
            n_blocks)


def _sc_mesh():
    return plsc.VectorSubcoreMesh(core_axis_name="c", subcore_axis_name="s",
                                  num_cores=V7X_SC_CORES, num_subcores=V7X_SC_SUBCORES)


def _sc_worker_id():
    return lax.axis_index("s") * V7X_SC_CORES + lax.axis_index("c")


def _sc_gather_rows(table, slot_of):
    k, n = slot_of.shape
    w = table.shape[1]
    n_rows = k * n
    ch = SC_GATHER_CHUNK
    per_worker = n_rows // (SC_WORKERS * ch)
    assert per_worker * SC_WORKERS * ch == n_rows and per_worker % 2 == 0
    idx = slot_of.reshape(SC_WORKERS, per_worker, ch)

    def body(table_hbm, idx_hbm, out_hbm, idx_v, rows_v, sems):
        wid = _sc_worker_id()
        pltpu.sync_copy(idx_hbm.at[wid], idx_v)

        def gather(c, slot):
            return pltpu.make_async_copy(table_hbm.at[idx_v.at[c]], rows_v.at[slot], sems.at[slot])

        gather(0, 0).start()

        @pl.loop(0, per_worker, step=2)
        def _(c0):
            for slot in range(2):
                c = c0 + slot
                gather(c, slot).wait()

                @pl.when(c + 1 < per_worker)
                def _():
                    gather(c + 1, 1 - slot).start()

                row0 = pl.multiple_of((wid * per_worker + c) * ch, 8)
                pltpu.sync_copy(rows_v.at[slot], out_hbm.at[pl.ds(row0, ch)])

    return pl.kernel(
        body, out_type=jax.ShapeDtypeStruct((n_rows, w), table.dtype), mesh=_sc_mesh(),
        scratch_types=[pltpu.VMEM((per_worker, ch), jnp.int32), pltpu.VMEM((2, ch, w), table.dtype),
                       pltpu.SemaphoreType.DMA((2,))],
        name="sc_gather_rows",
    )(table, idx)


def _sc_scatter_rows(x, slot_of, n_slots):
    k, n = slot_of.shape
    w = x.shape[1]
    n_chunks = n // SC_CHUNK
    assert n_chunks * SC_CHUNK == n
    idx = slot_of.reshape(k, n_chunks, SC_CHUNK).transpose(1, 0, 2)
    rounds = -(-n_chunks // SC_WORKERS)

    def body(x_hbm, idx_hbm, out_hbm, idx_v, rows_v, sem):
        wid = _sc_worker_id()

        @pl.loop(0, rounds)
        def _(j):
            q = j * SC_WORKERS + wid

            @pl.when(q < n_chunks)
            def _():
                pltpu.sync_copy(idx_hbm.at[q], idx_v)
                pltpu.sync_copy(x_hbm.at[pl.ds(pl.multiple_of(q * SC_CHUNK, SC_CHUNK), SC_CHUNK)], rows_v)
                copies = [pltpu.make_async_copy(rows_v, out_hbm.at[idx_v.at[kk]], sem) for kk in range(k)]
                for cp in copies:
                    cp.start()
                for cp in copies:
                    cp.wait()

    return pl.kernel(
        body, out_type=jax.ShapeDtypeStruct((n_slots, w), x.dtype), mesh=_sc_mesh(),
        scratch_types=[pltpu.VMEM((k, SC_CHUNK), jnp.int32), pltpu.VMEM((SC_CHUNK, w), x.dtype),
                       pltpu.SemaphoreType.DMA],
        name="sc_scatter_rows",
    )(x, idx)


def _moe_postnorm_kernel(x_ref, y_ref, gate_ref, sh_ref, g_ref, b_ref, ofp_ref, ofs_ref, ob_ref, *, alpha, tiles_p):
    def part(which):
        of_ref = (ofp_ref, ofs_ref)[which]
        gate = gate_ref[...]
        lo, hi = _unpack_halves(sh_ref[...])
        for k in range(TOP_K):
            lo_k, hi_k = _unpack_halves(y_ref[k])
            lo = lo + gate[:, k:k + 1] * lo_k
            hi = hi + gate[:, k:k + 1] * hi_k
        z = alpha * x_ref[...] + jnp.concatenate([lo, hi], axis=1)
        out = _layer_norm(z, g_ref[...], b_ref[...])
        of_ref[...] = out
        ob_ref[...] = out.astype(BF16)

    _on_part(tiles_p, part)


def _moe_postnorm(x, n_p, y_tok, gate, shared, g, b, alpha):
    m, d = x.shape
    tm = PART_TILE
    row = pl.BlockSpec((tm, d), lambda i: (i, 0))
    vec = pl.BlockSpec((1, d), lambda i: (0, 0))
    part_p, part_s = _part_specs(n_p, (tm, d))
    return pl.pallas_call(
        functools.partial(_moe_postnorm_kernel, alpha=alpha, tiles_p=n_p // tm),
        grid=(m // tm,),
        in_specs=[row, pl.BlockSpec((TOP_K, tm, d // 2), lambda i: (0, i, 0)),
                  pl.BlockSpec((tm, TOP_K), lambda i: (i, 0)), pl.BlockSpec((tm, d // 2), lambda i: (i, 0)), vec, vec],
        out_specs=[part_p, part_s, row],
        out_shape=[jax.ShapeDtypeStruct((n_p, d), F32), jax.ShapeDtypeStruct((m - n_p, d), F32),
                   jax.ShapeDtypeStruct((m, d), BF16)],
        compiler_params=_cparams(("arbitrary",), V7X_VMEM_LIMIT_BYTES),
        name="postnorm_moe",
    )(x, y_tok, gate, shared, g.reshape(1, d), b.reshape(1, d))


def _moe(xf, n_p, xp, routing, layer, w_gate, w_up, w_down, sw_gate, sw_up, sw_down, ln_g, ln_b, alpha):
    n, d = xf.shape
    e_idx, gate, counts = routing
    counts = jnp.sum(counts.reshape(n // ROW_TILE, ROW_TILE // PART_TILE, N_EXPERTS), axis=1)
    slot_of, blk_exp, blk_valid, blk_order, n_used, n_blocks = _dispatch_plan(e_idx, counts, MOE_BLOCK, ROW_TILE)
    x_sorted = _sc_scatter_rows(xp, slot_of, n_blocks * MOE_BLOCK)
    y_sorted = _expert_ffn(x_sorted, blk_exp, blk_valid, blk_order, n_used, w_gate, w_up, w_down, layer,
                           MOE_BLOCK, MOE_SUB, "routed_ffn")
    y_tok = _sc_gather_rows(y_sorted, slot_of).reshape(TOP_K, n, d // 2)
    n_sh = n // ROW_TILE
    shared = _expert_ffn(xp, jnp.zeros((n_sh,), jnp.int32), jnp.full((n_sh,), ROW_TILE, jnp.int32),
                         jnp.arange(n_sh, dtype=jnp.int32), jnp.full((1,), n_sh, jnp.int32),
                         sw_gate[:, None], sw_up[:, None], sw_down[:, None], layer, ROW_TILE, ROW_TILE, "shared_ffn")
    return _moe_postnorm(xf, n_p, y_tok, gate.T, shared, ln_g, ln_b, alpha)


def kernel(x_prompt, x_sample, state_hgrn, cache_win_k, cache_win_v, cache_mem_k, cache_mem_v, mem_prompt,
           w_in_a, lb_logits, g_norm_a, w_out_a, w_in_b, w_out_b, w_kv_shared, w_mem_kv, ln_g, ln_b,
           router_w, router_b, exp_w_gate, exp_w_up, exp_w_down, sh_w_gate, sh_w_up, sh_w_down):
    bp, sp, d = x_prompt.shape
    ns = x_sample.shape[0]
    assert x_sample.shape[1] == 1
    depth = ln_g.shape[0]
    n_a = w_in_a.shape[0]
    alpha = (2 * depth) ** 0.25
    n_p = bp * sp
    mem_len = mem_prompt.shape[1]
    a_mix = 4 * A_HEADS * HEAD_DIM

    xf_p, xf_s = x_prompt.reshape(n_p, d), x_sample.reshape(ns, d)
    xb = jnp.concatenate([xf_p, xf_s], axis=0).astype(BF16)
    lower_bounds = jnp.cumsum(jax.nn.softmax(lb_logits.astype(F32), axis=0), axis=0)
    mem_flat = mem_prompt.reshape(bp * mem_len, d)
    pos_all = jnp.concatenate([jnp.tile(jnp.arange(sp, dtype=jnp.int32), bp),
                               jnp.full((ns,), PAST_LEN, jnp.int32)])
    tables = _rope_tables(pos_all)

    hgrn_p, hgrn_s, mem_k_p, mem_v_p = [], [], [], []
    for layer in range(depth):
        kvm = _proj([mem_flat], w_mem_kv, layer, F32, 2 * mem_len, 1024, "mem_kv")
        mem_k_p.append(kvm[:, :MEM_W].reshape(bp, mem_len, MEM_HEADS, HEAD_DIM))
        mem_v_p.append(kvm[:, MEM_W:].reshape(bp, mem_len, MEM_HEADS, HEAD_DIM))
        if layer < n_a:
            a = layer
            proj = _proj([xb], w_in_a, a, F32, ROW_TILE, 1664, "proj_in_a")
            proj_s = proj[n_p:]
            o_x_s, st_s = _hgrn_step(proj_s.reshape(ns, 1, -1), state_hgrn, a, lower_bounds[a], g_norm_a[a])
            o_m_s = _mem_attn_decode(proj_s[:, a_mix:].reshape(ns, MEM_HEADS, HEAD_DIM), cache_mem_k, cache_mem_v, layer)
            o_x, st_p = _hgrn_prompt(proj, lower_bounds[a], g_norm_a[a], o_x_s.reshape(ns, -1), bp, sp)
            o_m = _mem_attn_prompt(proj, a_mix // MEM_W, kvm, o_m_s.reshape(ns, -1), bp, sp, 512)
            hgrn_p.append(st_p)
            hgrn_s.append(st_s)
            w_out, w_out_layer = w_out_a, a
        else:
            bl = layer - n_a
            if layer == n_a:
                kv = _proj([xb], w_kv_shared[None], 0, F32, ROW_TILE, 1024, "proj_kv")
                k_r, k_p, k_s, v_p, v_s = _shared_kv(kv, tables, n_p)
            proj = _proj([xb], w_in_b, bl, F32, ROW_TILE, 1024, "proj_in_b", tables, B_QHEADS)
            q_s = proj[n_p:, :B_QHEADS * HEAD_DIM].reshape(ns, B_GROUPS, B_SLOTS, HEAD_DIM)
            o_x_s = _dil_attn_decode(q_s, k_s, v_s, cache_win_k, cache_win_v)
            o_m_s = _mem_attn_decode(proj[n_p:, B_QHEADS * HEAD_DIM:].reshape(ns, MEM_HEADS, HEAD_DIM),
                                     cache_mem_k, cache_mem_v, layer)
            o_x = _dil_attn_prompt(proj, k_r, kv, o_x_s.reshape(ns, -1), bp, sp)
            o_m = _mem_attn_prompt(proj, B_QHEADS * HEAD_DIM // MEM_W, kvm, o_m_s.reshape(ns, -1), bp, sp, 512)
            w_out, w_out_layer = w_out_b, bl
        y = _proj([o_x, o_m], w_out, w_out_layer, F32, ROW_TILE, 1024, "proj_out")
        xf, xp, *routing = _postnorm_route(xf_p, xf_s, y, ln_g[layer, 0], ln_b[layer, 0], alpha,
                                           router_w, layer, router_b, "postnorm_route")
        xf_p, xf_s, xb = _moe(xf, n_p, xp, routing, layer, exp_w_gate, exp_w_up, exp_w_down,
                              sh_w_gate, sh_w_up, sh_w_down, ln_g[layer, 1], ln_b[layer, 1], alpha)

    w_p = min(max(w for w, _ in B_PATTERNS), sp)
    k_p = k_p.reshape(bp, sp, B_SLOTS, HEAD_DIM)
    v_p = v_p.reshape(bp, sp, B_SLOTS, HEAD_DIM)
    return (xf_p.reshape(bp, sp, d), xf_s.reshape(ns, 1, d),
            jnp.stack(hgrn_p), jnp.stack(hgrn_s),
            k_p[:, sp - w_p:], v_p[:, sp - w_p:],
            k_s.reshape(ns, 1, B_SLOTS, HEAD_DIM), v_s.reshape(ns, 1, B_SLOTS, HEAD_DIM),
            jnp.stack(mem_k_p), jnp.stack(mem_v_p))
```

```python
import functools

import jax
import jax.numpy as jnp
from jax import lax
from jax.experimental import pallas as pl
from jax.experimental.pallas import tpu as pltpu
from jax.experimental.pallas import tpu_sc as plsc

F32 = jnp.float32
BF16 = jnp.bfloat16

HEAD_DIM = 128
A_HEADS = 12
A_CHUNK = 64
A_SUB = 16
B_PATTERNS = ((128, 1), (512, 4), (2048, 16))
B_SLOTS = 4
B_GROUPS = len(B_PATTERNS)
B_QHEADS = B_GROUPS * B_SLOTS
B_BLOCK = 128
MEM_HEADS = 4
MEM_W = MEM_HEADS * HEAD_DIM
KV_W = B_SLOTS * HEAD_DIM
ROPE_THETA = 500000.0
ROPE_DIM = HEAD_DIM // 4
N_EXPERTS = 64
N_GROUPS = 8
TOPK_GROUPS = 4
TOP_K = 8
ROUTED_SCALE = 2.5
LN_EPS = 1e-5
RMS_EPS = 1e-6
ATTN_SCALE = HEAD_DIM ** -0.5
PAST_LEN = 2048

V7X_VMEM_LIMIT_BYTES = 56 * 1024 * 1024
V7X_SC_CORES = 2
V7X_SC_SUBCORES = 16
SUBLANES = 8

ROW_TILE = 640
PART_TILE = 128
A_HEADS_PER_STEP = 4
A_CHUNK_UNROLL = 4
DIL_BATCH = 8
MEM_DECODE_SEQS = 8
DIL_DECODE_SEQS = 4
MOE_BLOCK = 1152
MOE_SUB = 384
FFN_COL_SPLIT = 2
SC_WORKERS = V7X_SC_CORES * V7X_SC_SUBCORES
SC_CHUNK = 32
SC_GATHER_CHUNK = 40

NEG_BIG = -1e30


def _cparams(sem, vmem=None):
    return pltpu.CompilerParams(dimension_semantics=sem, vmem_limit_bytes=vmem)


def _dot(a, b):
    return jnp.dot(a, b, preferred_element_type=F32)


def _dot_nt(a, b):
    return lax.dot_general(a, b, (((1,), (1,)), ((), ())), preferred_element_type=F32)


def _sigmoid(x):
    return 1.0 / (1.0 + jnp.exp(-x))


def _proj_kernel(*refs, n_lhs, rope_heads):
    x_refs = refs[:n_lhs]
    w_refs = refs[n_lhs:2 * n_lhs]
    n_tab = 3 if rope_heads else 0
    tab_refs = refs[2 * n_lhs:2 * n_lhs + n_tab]
    o_ref = refs[2 * n_lhs + n_tab]
    wb_refs = refs[2 * n_lhs + n_tab + 1:]

    @pl.when(pl.program_id(1) == 0)
    def _():
        for w_ref, wb_ref in zip(w_refs, wb_refs):
            wb_ref[...] = w_ref[...].astype(BF16)

    acc = None
    for x_ref, wb_ref in zip(x_refs, wb_refs):
        d = _dot(x_ref[...].astype(BF16), wb_ref[...])
        acc = d if acc is None else acc + d
    if not rope_heads:
        o_ref[...] = acc.astype(o_ref.dtype)
    else:
        c, su, sd = (t[...] for t in tab_refs)
        heads_per_tile = o_ref.shape[1] // HEAD_DIM
        for h in range(heads_per_tile):
            sl = slice(h * HEAD_DIM, (h + 1) * HEAD_DIM)
            plain = acc[:, sl]
            roped = _rope_head(plain, c, su, sd)
            is_roped = pl.program_id(0) * heads_per_tile + h < rope_heads
            o_ref[:, sl] = jnp.where(is_roped, roped, plain).astype(o_ref.dtype)


def _proj(lhs, w, layer, out_dtype, tm, tn, name, rope_tables=None, rope_heads=0):
    m = lhs[0].shape[0]
    n = w.shape[2]
    koff = 0
    in_specs, w_specs, scratch = [], [], []
    for x in lhs:
        k = x.shape[1]
        assert koff % k == 0 and m % tm == 0 and n % tn == 0
        in_specs.append(pl.BlockSpec((tm, k), lambda j, i: (i, 0)))
        w_specs.append(pl.BlockSpec((None, k, tn), functools.partial(lambda j, i, kb: (layer, kb, j), kb=koff // k)))
        scratch.append(pltpu.VMEM((k, tn), BF16))
        koff += k
    assert koff == w.shape[1]
    tables = list(rope_tables) if rope_heads else []
    tab_specs = [pl.BlockSpec((tm, HEAD_DIM), lambda j, i: (i, 0))] * len(tables)
    return pl.pallas_call(
        functools.partial(_proj_kernel, n_lhs=len(lhs), rope_heads=rope_heads),
        grid=(n // tn, m // tm),
        in_specs=in_specs + w_specs + tab_specs,
        out_specs=pl.BlockSpec((tm, tn), lambda j, i: (i, j)),
        out_shape=jax.ShapeDtypeStruct((m, n), out_dtype),
        scratch_shapes=scratch,
        compiler_params=_cparams(("arbitrary", "arbitrary"), V7X_VMEM_LIMIT_BYTES),
        name=name,
    )(*lhs, *([w] * len(lhs)), *tables)


def _pack_halves(x):
    c = x.shape[1] // 2
    lo = pltpu.bitcast(x[:, :c].astype(BF16).astype(F32), jnp.int32)
    hi = pltpu.bitcast(x[:, c:].astype(BF16).astype(F32), jnp.int32)
    return hi | lax.shift_right_logical(lo, 16)


def _unpack_halves(w):
    lo = pltpu.bitcast(lax.shift_left(w, 16), F32)
    hi = pltpu.bitcast(w & jnp.int32(-65536), F32)
    return lo, hi


def _layer_norm(z, g, b):
    mu = jnp.mean(z, axis=-1, keepdims=True)
    zc = z - mu
    var = jnp.mean(zc * zc, axis=-1, keepdims=True)
    return zc * lax.rsqrt(var + LN_EPS) * g + b


def _part_specs(n_p, block):
    tiles_p = n_p // PART_TILE
    rest = (0,) * (len(block) - 1)
    return (pl.BlockSpec(block, lambda i: (jnp.minimum(i, tiles_p - 1),) + rest),
            pl.BlockSpec(block, lambda i: (jnp.maximum(i - tiles_p, 0),) + rest))


def _on_part(tiles_p, fn):
    i = pl.program_id(0)
    pl.when(i < tiles_p)(functools.partial(fn, 0))
    pl.when(i >= tiles_p)(functools.partial(fn, 1))


def _postnorm_kernel(xp_ref, xs_ref, y_ref, g_ref, b_ref, wt_ref, rb_ref, of_ref, op_ref, e_ref, gate_ref, cnt_ref,
                     *, alpha, tiles_p):
    def part(which):
        x_ref = (xp_ref, xs_ref)[which]
        out = _layer_norm(alpha * x_ref[...] + y_ref[...], g_ref[...], b_ref[...])
        of_ref[...] = out
        op_ref[...] = _pack_halves(out)
        e_ref[...], gate_ref[...], cnt_ref[...] = _route_tile(out, wt_ref[...], rb_ref[...])

    _on_part(tiles_p, part)


def _postnorm_route(x_p, x_s, y, g, b, alpha, router_w, layer, router_b, name):
    n_p, d = x_p.shape
    m = n_p + x_s.shape[0]
    e = router_w.shape[2]
    t = PART_TILE
    row = pl.BlockSpec((t, d), lambda i: (i, 0))
    vec = pl.BlockSpec((1, d), lambda i: (0, 0))
    tok = pl.BlockSpec((TOP_K, t), lambda i: (0, i))
    part_p, part_s = _part_specs(n_p, (t, d))
    return pl.pallas_call(
        functools.partial(_postnorm_kernel, alpha=alpha, tiles_p=n_p // t),
        grid=(m // t,),
        in_specs=[part_p, part_s, row, vec, vec,
                  pl.BlockSpec((None, e, d), lambda i: (layer, 0, 0)), pl.BlockSpec((e, 1), lambda i: (0, 0))],
        out_specs=[row, pl.BlockSpec((t, d // 2), lambda i: (i, 0)), tok, tok,
                   pl.BlockSpec((None, e, 1), lambda i: (i, 0, 0))],
        out_shape=[jax.ShapeDtypeStruct((m, d), F32), jax.ShapeDtypeStruct((m, d // 2), jnp.int32),
                   jax.ShapeDtypeStruct((TOP_K, m), jnp.int32), jax.ShapeDtypeStruct((TOP_K, m), F32),
                   jax.ShapeDtypeStruct((m // t, e, 1), jnp.int32)],
        compiler_params=_cparams(("arbitrary",)),
        name=name,
    )(x_p, x_s, y, g.reshape(1, d), b.reshape(1, d), jnp.swapaxes(router_w, 1, 2),
      router_b[layer].astype(F32).reshape(e, 1))


def _hgrn_prompt_kernel(q_ref, f_ref, v_ref, gate_ref, lb_ref, gn_ref, os_ref, o_ref, s_ref, st_ref,
                        *, seq, heads, n_batch):
    _prompt_then_sample(n_batch, os_ref, o_ref,
                        functools.partial(_hgrn_sequence, q_ref, f_ref, v_ref, gate_ref, lb_ref, gn_ref,
                                          o_ref, s_ref, st_ref, seq=seq, heads=heads), axis=1)


def _hgrn_sequence(q_ref, f_ref, v_ref, gate_ref, lb_ref, gn_ref, o_ref, s_ref, st_ref, *, seq, heads):
    c = A_CHUNK
    n_chunks = seq // c
    n_sub = c // A_SUB
    gn = gn_ref[...]
    row = lax.broadcasted_iota(jnp.int32, (c, c), 0)
    col = lax.broadcasted_iota(jnp.int32, (c, c), 1)
    tril = jnp.where(row >= col, 1.0, 0.0).astype(BF16)
    same_sub = (row // A_SUB) == (col // A_SUB)
    diag_dist = jnp.where(same_sub, row - col, -1)
    off_mask = col < (row // A_SUB) * A_SUB
    st_ref[...] = jnp.zeros_like(st_ref)

    def chunk(ci, carry):
        for hh in range(heads):
            head_chunk(ci, hh)
        return carry

    def head_chunk(ci, hh):
        r0 = pl.multiple_of(ci * c, c)
        hsl = slice(hh * HEAD_DIM, (hh + 1) * HEAD_DIM)
        lb = lb_ref[hh]
        qp = q_ref[pl.ds(r0, c), hsl]
        fp = f_ref[pl.ds(r0, c), hsl]
        v = v_ref[pl.ds(r0, c), hsl]
        gp = gate_ref[pl.ds(r0, c), hsl]
        st = st_ref[hh]
        q = qp * _sigmoid(qp)
        forget = lb + (1.0 - lb) * _sigmoid(fp)
        logf = jnp.log2(forget)
        k = 1.0 - forget
        hi = logf.astype(BF16)
        r1 = logf - hi.astype(F32)
        mid = r1.astype(BF16)
        lo = (r1 - mid.astype(F32)).astype(BF16)
        g = _dot(tril, hi) + _dot(tril, mid) + _dot(tril, lo)
        v_b = v.astype(BF16)
        o = _dot_nt((q * jnp.exp2(g)).astype(BF16), st.astype(BF16))
        rows = [jnp.zeros((A_SUB, c), F32)]
        for i in range(1, n_sub):
            gref = g[i * A_SUB:i * A_SUB + 1, :]
            qt = q[i * A_SUB:(i + 1) * A_SUB, :] * jnp.exp2(g[i * A_SUB:(i + 1) * A_SUB, :] - gref)
            kt = k * jnp.exp2(jnp.minimum(gref - g, 0.0))
            rows.append(_dot_nt(qt.astype(BF16), kt.astype(BF16)))
        a = jnp.where(off_mask, jnp.concatenate(rows, axis=0), 0.0)
        for d in range(A_SUB):
            kr = k if d == 0 else pltpu.roll(k, d, axis=0)
            gr = g if d == 0 else pltpu.roll(g, d, axis=0)
            x = q * kr * jnp.exp2(g - gr)
            a = jnp.where(diag_dist == d, jnp.sum(x, axis=-1, keepdims=True), a)
        o = o + _dot(a.astype(BF16), v_b)
        gend = g[c - 1:c, :]
        kt_end = k * jnp.exp2(gend - g)
        st_ref[hh] = jnp.exp2(gend) * st + _dot(v_b.T, kt_end.astype(BF16))
        o = o * lax.rsqrt(jnp.mean(o * o, axis=-1, keepdims=True) + RMS_EPS) * gn
        o_ref[pl.ds(r0, c), hsl] = (o * (gp * _sigmoid(gp))).astype(o_ref.dtype)

    lax.fori_loop(0, n_chunks, chunk, 0, unroll=A_CHUNK_UNROLL)
    for hh in range(heads):
        s_ref[hh] = st_ref[hh].T


def _hgrn_prompt(proj, lb, g_norm, o_sample, n_batch, seq):
    h = A_HEADS
    hp = A_HEADS_PER_STEP
    ns = o_sample.shape[0]
    assert h % hp == 0 and ns <= seq and proj.shape[0] == n_batch * seq + ns
    ng = h // hp
    w = hp * HEAD_DIM
    last = n_batch - 1
    blk = lambda off: pl.BlockSpec((seq, w), functools.partial(lambda hg, b, off: (jnp.minimum(b, last), off + hg), off=off))
    return pl.pallas_call(
        functools.partial(_hgrn_prompt_kernel, seq=seq, heads=hp, n_batch=n_batch),
        grid=(ng, n_batch + 1),
        in_specs=[blk(0), blk(ng), blk(2 * ng), blk(3 * ng),
                  pl.BlockSpec((hp, 1, HEAD_DIM), lambda hg, b: (hg, 0, 0)),
                  pl.BlockSpec((1, HEAD_DIM), lambda hg, b: (0, 0)),
                  pl.BlockSpec((ns, w), lambda hg, b: (0, hg))],
        out_specs=[pl.BlockSpec((seq, w), lambda hg, b: (b, hg)),
                   pl.BlockSpec((None, hp, HEAD_DIM, HEAD_DIM), lambda hg, b: (jnp.minimum(b, last), hg, 0, 0))],
        out_shape=[jax.ShapeDtypeStruct((proj.shape[0], h * HEAD_DIM), BF16),
                   jax.ShapeDtypeStruct((n_batch, h, HEAD_DIM, HEAD_DIM), F32)],
        scratch_shapes=[pltpu.VMEM((hp, HEAD_DIM, HEAD_DIM), F32)],
        compiler_params=_cparams(("arbitrary", "arbitrary"), V7X_VMEM_LIMIT_BYTES),
        name="hgrn_prompt",
    )(proj, proj, proj, proj, lb.reshape(h, 1, HEAD_DIM), g_norm.reshape(1, HEAD_DIM), o_sample)


def _softmax_av(s, v):
    m = jnp.max(s, axis=-1, keepdims=True)
    p = jnp.exp(s - m)
    l = jnp.sum(p, axis=-1, keepdims=True)
    return _dot(p.astype(BF16), v) / l


def _prompt_then_sample(n_prompt_steps, os_ref, o_ref, prompt_step, axis=0):
    i = pl.program_id(axis)
    pl.when(i < n_prompt_steps)(prompt_step)

    @pl.when(i >= n_prompt_steps)
    def _():
        o_ref[:os_ref.shape[0], :] = os_ref[...]


def _mem_attn_kernel(q_ref, k_ref, v_ref, os_ref, o_ref, *, n_steps):
    def step():
        for h in range(MEM_HEADS):
            sl = slice(h * HEAD_DIM, (h + 1) * HEAD_DIM)
            s = _dot_nt(q_ref[:, sl].astype(BF16), k_ref[:, sl].astype(BF16)) * ATTN_SCALE
            o_ref[:, sl] = _softmax_av(s, v_ref[:, sl].astype(BF16)).astype(o_ref.dtype)

    _prompt_then_sample(n_steps, os_ref, o_ref, step)


def _mem_attn_prompt(proj, q_col, kv, o_sample, n_batch, seq, tq):
    mem_len = kv.shape[0] // n_batch
    nq = seq // tq
    n_steps = n_batch * nq
    ns = o_sample.shape[0]
    assert ns <= tq and proj.shape[0] == n_batch * seq + ns
    last = n_steps - 1
    return pl.pallas_call(
        functools.partial(_mem_attn_kernel, n_steps=n_steps),
        grid=(n_steps + 1,),
        in_specs=[pl.BlockSpec((tq, MEM_W), lambda i: (jnp.minimum(i, last), q_col)),
                  pl.BlockSpec((mem_len, MEM_W), lambda i: (jnp.minimum(i, last) // nq, 0)),
                  pl.BlockSpec((mem_len, MEM_W), lambda i: (jnp.minimum(i, last) // nq, 1)),
                  pl.BlockSpec((ns, MEM_W), lambda i: (0, 0))],
        out_specs=pl.BlockSpec((tq, MEM_W), lambda i: (i, 0)),
        out_shape=jax.ShapeDtypeStruct((proj.shape[0], MEM_W), BF16),
        compiler_params=_cparams(("arbitrary",)),
        name="mem_attn_prompt",
    )(proj, kv, kv, o_sample)


def _rope_tables(pos):
    half = ROPE_DIM // 2
    inv_freq = ROPE_THETA ** (-jnp.arange(0, ROPE_DIM, 2, dtype=F32) / ROPE_DIM)
    ang = pos.astype(F32)[:, None] * inv_freq[None, :]
    cos, sin = jnp.cos(ang), jnp.sin(ang)
    n = pos.shape[0]
    one = jnp.ones((n, HEAD_DIM - ROPE_DIM), F32)
    zero = jnp.zeros((n, HEAD_DIM - half), F32)
    c = jnp.concatenate([cos, cos, one], axis=1)
    s_up = jnp.concatenate([-sin, zero], axis=1)
    s_dn = jnp.concatenate([jnp.zeros((n, half), F32), sin, zero[:, half:]], axis=1)
    return c, s_up, s_dn


def _rope_head(x, c, su, sd):
    half = ROPE_DIM // 2
    up = pltpu.roll(x, HEAD_DIM - half, axis=1)
    dn = pltpu.roll(x, half, axis=1)
    return x * c + up * su + dn * sd


def _kv_kernel(kv_ref, c_ref, su_ref, sd_ref, k_ref, kp_ref, ks_ref, vp_ref, vs_ref, *, tiles_p):
    c, su, sd = c_ref[...], su_ref[...], sd_ref[...]

    def part(which):
        k4_ref, v4_ref = ((kp_ref, vp_ref), (ks_ref, vs_ref))[which]
        for h in range(B_SLOTS):
            sl = slice(h * HEAD_DIM, (h + 1) * HEAD_DIM)
            k = _rope_head(kv_ref[:, sl], c, su, sd)
            k_ref[:, sl] = k
            k4_ref[:, h, :] = k
            v4_ref[:, h, :] = kv_ref[:, KV_W + h * HEAD_DIM:KV_W + (h + 1) * HEAD_DIM]

    _on_part(tiles_p, part)


def _shared_kv(kv, tables, n_p):
    m = kv.shape[0]
    tm = PART_TILE
    tab = pl.BlockSpec((tm, HEAD_DIM), lambda i: (i, 0))
    part_p, part_s = _part_specs(n_p, (tm, B_SLOTS, HEAD_DIM))
    cache = lambda rows: jax.ShapeDtypeStruct((rows, B_SLOTS, HEAD_DIM), F32)
    return pl.pallas_call(
        functools.partial(_kv_kernel, tiles_p=n_p // tm),
        grid=(m // tm,),
        in_specs=[pl.BlockSpec((tm, 2 * KV_W), lambda i: (i, 0)), tab, tab, tab],
        out_specs=[pl.BlockSpec((tm, KV_W), lambda i: (i, 0)), part_p, part_s, part_p, part_s],
        out_shape=[jax.ShapeDtypeStruct((m, KV_W), F32), cache(n_p), cache(m - n_p), cache(n_p), cache(m - n_p)],
        compiler_params=_cparams(("arbitrary",)),
        name="shared_kv",
    )(kv, *tables)


def _dil_prompt_kernel(*refs, seq, n_batch):
    os_ref, o_ref = refs[B_GROUPS + 2:B_GROUPS + 4]
    _prompt_then_sample(n_batch, os_ref, o_ref, functools.partial(_dil_sequence, *refs, seq=seq), axis=1)


def _dil_sequence(*refs, seq):
    q_refs = refs[:B_GROUPS]
    k_ref, v_ref, _, o_ref, m_ref, l_ref, acc_ref = refs[B_GROUPS:]
    blk = B_BLOCK
    base = (lax.broadcasted_iota(jnp.int32, (blk, blk), 0) - lax.broadcasted_iota(jnp.int32, (blk, blk), 1))
    for g, (win, dil) in enumerate(B_PATTERNS):
        span = win // dil
        n_blk = seq // dil // blk
        assert span <= blk

        def body(it, carry, g=g, dil=dil, span=span, n_blk=n_blk):
            own, prev, prev_ok = [], [], []
            for j in range(DIL_BATCH):
                t = it * DIL_BATCH + j
                r, n = t % dil, t // dil

                def rows(nn, r=r):
                    start = nn * (blk * dil) + r
                    return pl.ds(pl.multiple_of(start, blk), blk) if dil == 1 else pl.ds(start, blk, stride=dil)

                own.append(rows(n))
                prev.append(rows(jnp.maximum(n - 1, 0)))
                prev_ok.append(base <= jnp.where(n > 0, span - blk, -blk - 1))
            load = lambda ref, idx: jnp.stack([ref[i, :].astype(BF16) for i in idx])
            scores = lambda a, b: jnp.einsum('bqd,bkd->bqk', a, b, preferred_element_type=F32) * ATTN_SCALE
            weighted = lambda p, v: jnp.einsum('bqk,bkd->bqd', p.astype(BF16), v, preferred_element_type=F32)
            q = load(q_refs[g], own)
            s = jnp.where((base >= 0)[None], scores(q, load(k_ref, own)), NEG_BIG)
            m_b = jnp.max(s, axis=-1, keepdims=True)
            if n_blk > 1:
                s_prev = jnp.where(jnp.stack(prev_ok), scores(q, load(k_ref, prev)), NEG_BIG)
                m_b = jnp.maximum(m_b, jnp.max(s_prev, axis=-1, keepdims=True))
            p = jnp.exp(s - m_b)
            l_b = jnp.sum(p, axis=-1, keepdims=True)
            acc_b = weighted(p, load(v_ref, own))
            if n_blk > 1:
                p_prev = jnp.exp(s_prev - m_b)
                l_b = l_b + jnp.sum(p_prev, axis=-1, keepdims=True)
                acc_b = acc_b + weighted(p_prev, load(v_ref, prev))
            for j, idx in enumerate(own):
                if g == 0:
                    m_ref[idx, :] = jnp.broadcast_to(m_b[j], (blk, HEAD_DIM))
                    l_ref[idx, :] = jnp.broadcast_to(l_b[j], (blk, HEAD_DIM))
                    acc_ref[idx, :] = acc_b[j]
                else:
                    m_old = m_ref[idx, :]
                    m_new = jnp.maximum(m_old, m_b[j])
                    a_old = jnp.exp(m_old - m_new)
                    a_b = jnp.exp(m_b[j] - m_new)
                    m_ref[idx, :] = m_new
                    l_ref[idx, :] = l_ref[idx, :] * a_old + l_b[j] * a_b
                    acc_ref[idx, :] = acc_ref[idx, :] * a_old + acc_b[j] * a_b
            return carry

        assert (dil * n_blk) % DIL_BATCH == 0
        lax.fori_loop(0, dil * n_blk // DIL_BATCH, body, 0)
    o_ref[...] = (acc_ref[...] / l_ref[...]).astype(o_ref.dtype)


def _dil_attn_prompt(q, k, kv, o_sample, n_batch, seq):
    ns = o_sample.shape[0]
    assert ns <= seq and q.shape[0] == n_batch * seq + ns
    last = n_batch - 1
    col = lambda c0: pl.BlockSpec((seq, HEAD_DIM), functools.partial(lambda h, b, c0: (jnp.minimum(b, last), c0 + h), c0=c0))
    return pl.pallas_call(
        functools.partial(_dil_prompt_kernel, seq=seq, n_batch=n_batch),
        grid=(B_SLOTS, n_batch + 1),
        in_specs=[col(g * B_SLOTS) for g in range(B_GROUPS)] + [col(0), col(B_SLOTS),
                                                                 pl.BlockSpec((ns, HEAD_DIM), lambda h, b: (0, h))],
        out_specs=pl.BlockSpec((seq, HEAD_DIM), lambda h, b: (b, h)),
        out_shape=jax.ShapeDtypeStruct((q.shape[0], KV_W), BF16),
        scratch_shapes=[pltpu.VMEM((seq, HEAD_DIM), F32)] * 3,
        compiler_params=_cparams(("arbitrary", "arbitrary"), V7X_VMEM_LIMIT_BYTES),
        name="dil_attn_prompt",
    )(*([q] * B_GROUPS), k, kv, o_sample)


def _mem_decode_kernel(q_ref, k_ref, v_ref, o_ref):
    n_seq, rows, _ = k_ref.shape
    nh = q_ref.shape[1]
    own = (lax.broadcasted_iota(jnp.int32, (nh, rows), 1) % nh) == lax.broadcasted_iota(jnp.int32, (nh, rows), 0)
    for i in range(n_seq):
        s = _dot_nt(q_ref[i].astype(BF16), k_ref[i].astype(BF16)) * ATTN_SCALE
        o_ref[i] = _softmax_av(jnp.where(own, s, NEG_BIG), v_ref[i].astype(BF16)).astype(o_ref.dtype)


def _mem_attn_decode(q, cache_k, cache_v, layer):
    ns, nh, hd = q.shape
    n_layers, _, mem_len = cache_k.shape[:3]
    g = MEM_DECODE_SEQS
    assert ns % g == 0
    rows = mem_len * nh
    cspec = pl.BlockSpec((None, g, rows, hd), lambda b: (layer, b, 0, 0))
    qspec = pl.BlockSpec((g, nh, hd), lambda b: (b, 0, 0))
    return pl.pallas_call(
        _mem_decode_kernel,
        grid=(ns // g,),
        in_specs=[qspec, cspec, cspec],
        out_specs=qspec,
        out_shape=jax.ShapeDtypeStruct((ns, nh, hd), BF16),
        compiler_params=_cparams(("parallel",)),
        name="mem_attn_decode",
    )(q, cache_k.reshape(n_layers, ns, rows, hd), cache_v.reshape(n_layers, ns, rows, hd))


def _dil_decode_kernel(q_ref, kn_ref, vn_ref, *refs):
    o_ref = refs[-1]
    n_slots = kn_ref.shape[1]
    for i in range(q_ref.shape[0]):
        kn, vn = kn_ref[i], vn_ref[i]
        scores, values = [], []
        for g in range(B_GROUPS):
            k_ref, v_ref = refs[2 * g], refs[2 * g + 1]
            rows = k_ref.shape[1] * (k_ref.shape[2] if len(k_ref.shape) == 4 else 1)
            period = k_ref.shape[2] if len(k_ref.shape) == 4 else n_slots
            k = k_ref[i].reshape(rows, HEAD_DIM).astype(BF16)
            v = v_ref[i].reshape(rows, HEAD_DIM).astype(BF16)
            q = q_ref[i, g]
            own = (lax.broadcasted_iota(jnp.int32, (n_slots, rows), 1) % period
                   == lax.broadcasted_iota(jnp.int32, (n_slots, rows), 0))
            scores.append(jnp.where(own, _dot_nt(q.astype(BF16), k) * ATTN_SCALE, NEG_BIG))
            values.append(v)
            scores.append(jnp.sum(q * kn, axis=-1, keepdims=True) * ATTN_SCALE)
            values.append(None)
        m = functools.reduce(jnp.maximum, [jnp.max(s, axis=-1, keepdims=True) for s in scores])
        l = jnp.zeros_like(m)
        acc = jnp.zeros((n_slots, HEAD_DIM), F32)
        for s, v in zip(scores, values):
            p = jnp.exp(s - m)
            l = l + jnp.sum(p, axis=-1, keepdims=True)
            acc = acc + (p * vn if v is None else _dot(p.astype(BF16), v))
        o_ref[i] = (acc / l).astype(o_ref.dtype)


def _dil_attn_decode(q, k_new, v_new, cache_k, cache_v):
    ns, w_buf, ns_slots, hd = cache_k.shape
    n_seq = DIL_DECODE_SEQS
    assert ns % n_seq == 0
    in_specs = [pl.BlockSpec((n_seq, B_GROUPS, ns_slots, hd), lambda b: (b, 0, 0, 0)),
                pl.BlockSpec((n_seq, ns_slots, hd), lambda b: (b, 0, 0)),
                pl.BlockSpec((n_seq, ns_slots, hd), lambda b: (b, 0, 0))]
    args = [q, k_new, v_new]
    for win, dil in B_PATTERNS:
        span = win // dil
        assert w_buf % dil == 0 and (w_buf // dil) % span == 0
        last = w_buf // dil // span - 1
        if dil == 1:
            view = (ns, w_buf * ns_slots, hd)
            spec = pl.BlockSpec((n_seq, span * ns_slots, hd), functools.partial(lambda b, last: (b, last, 0), last=last))
        else:
            assert dil * ns_slots >= SUBLANES and ns_slots <= SUBLANES
            view = (ns, w_buf // dil, dil * ns_slots, hd)
            spec = pl.BlockSpec((n_seq, span, SUBLANES, hd), functools.partial(lambda b, last: (b, last, 0, 0), last=last))
        in_specs += [spec, spec]
        args += [cache_k.reshape(view), cache_v.reshape(view)]
    return pl.pallas_call(
        _dil_decode_kernel,
        grid=(ns // n_seq,),
        in_specs=in_specs,
        out_specs=pl.BlockSpec((n_seq, ns_slots, hd), lambda b: (b, 0, 0)),
        out_shape=jax.ShapeDtypeStruct((ns, ns_slots, hd), BF16),
        compiler_params=_cparams(("parallel",)),
        name="dil_attn_decode",
    )(*args)


def _hgrn_step_kernel(p_ref, s_ref, lb_ref, gn_ref, o_ref, so_ref, rows_ref):
    h = A_HEADS
    hk = h * HEAD_DIM
    gn = gn_ref[...]
    rows_ref[...] = jnp.zeros_like(rows_ref)
    for i in range(h):
        fp = p_ref[:, hk + i * HEAD_DIM:hk + (i + 1) * HEAD_DIM]
        lb = lb_ref[i:i + 1, :]
        rows_ref[i:i + 1, :] = lb + (1.0 - lb) * _sigmoid(fp)
    cols = rows_ref[...].T
    first = lax.broadcasted_iota(jnp.int32, (SUBLANES, HEAD_DIM), 0) == 0
    for i in range(h):
        sl = slice(i * HEAD_DIM, (i + 1) * HEAD_DIM)
        qp = p_ref[:, sl]
        v = p_ref[:, 2 * hk + i * HEAD_DIM:2 * hk + (i + 1) * HEAD_DIM]
        gp = p_ref[:, 3 * hk + i * HEAD_DIM:3 * hk + (i + 1) * HEAD_DIM]
        k = 1.0 - rows_ref[i:i + 1, :]
        k8 = jnp.where(first, k, 0.0).astype(BF16)
        v8 = jnp.broadcast_to(v, (SUBLANES, HEAD_DIM)).astype(BF16)
        kv = lax.dot_general(k8, v8, (((0,), (0,)), ((), ())), preferred_element_type=F32)
        s_new = cols[:, i:i + 1] * s_ref[i] + kv
        so_ref[i] = s_new
        q8 = jnp.broadcast_to(qp * _sigmoid(qp), (SUBLANES, HEAD_DIM)).astype(BF16)
        o = _dot(q8, s_new.astype(BF16))[0:1]
        o = o * lax.rsqrt(jnp.mean(o * o, axis=-1, keepdims=True) + RMS_EPS) * gn
        o_ref[:, sl] = (o * (gp * _sigmoid(gp))).astype(o_ref.dtype)


def _hgrn_step(proj, state, layer, lb, g_norm):
    ns = proj.shape[0]
    h = A_HEADS
    sspec = pl.BlockSpec((None, h, HEAD_DIM, HEAD_DIM), lambda b: (b, 0, 0, 0))
    return pl.pallas_call(
        _hgrn_step_kernel,
        grid=(ns,),
        in_specs=[pl.BlockSpec((None, 1, proj.shape[2]), lambda b: (b, 0, 0)),
                  pl.BlockSpec((None, None, h, HEAD_DIM, HEAD_DIM), lambda b: (layer, b, 0, 0, 0)),
                  pl.BlockSpec((h, HEAD_DIM), lambda b: (0, 0)),
                  pl.BlockSpec((1, HEAD_DIM), lambda b: (0, 0))],
        out_specs=[pl.BlockSpec((None, 1, h * HEAD_DIM), lambda b: (b, 0, 0)), sspec],
        out_shape=[jax.ShapeDtypeStruct((ns, 1, h * HEAD_DIM), BF16),
                   jax.ShapeDtypeStruct((ns, h, HEAD_DIM, HEAD_DIM), F32)],
        scratch_shapes=[pltpu.VMEM((HEAD_DIM, HEAD_DIM), F32)],
        compiler_params=_cparams(("parallel",)),
        name="hgrn_step",
    )(proj, state, lb.reshape(h, HEAD_DIM), g_norm.reshape(1, HEAD_DIM))


def _ffn_kernel(be_ref, nv_ref, od_ref, nu_ref, x_ref, wg_ref, wu_ref, wd_ref, o_ref, wgb_ref, wub_ref, wdb_ref, *, sub):
    i = pl.program_id(0)
    n_valid = nv_ref[i]

    @pl.when((n_valid > 0) & ((i == 0) | (be_ref[i] != be_ref[jnp.maximum(i - 1, 0)])))
    def _():
        wgb_ref[...] = wg_ref[...].astype(BF16)
        wub_ref[...] = wu_ref[...].astype(BF16)
        wdb_ref[...] = wd_ref[...].astype(BF16)

    for j in range(x_ref.shape[0] // sub):
        rows = slice(j * sub, (j + 1) * sub)

        @pl.when(n_valid > j * sub)
        def _(j=j, rows=rows):
            lo, hi = _unpack_halves(x_ref[rows, :])
            c = lo.shape[1]
            keep = lax.broadcasted_iota(jnp.int32, lo.shape, 0) < n_valid - j * sub
            lo = jnp.where(keep, lo, 0.0).astype(BF16)
            hi = jnp.where(keep, hi, 0.0).astype(BF16)
            ff = wgb_ref.shape[1]
            y = None
            for part in range(FFN_COL_SPLIT):
                cs = slice(part * ff // FFN_COL_SPLIT, (part + 1) * ff // FFN_COL_SPLIT)
                hg = _dot(lo, wgb_ref[:c, cs]) + _dot(hi, wgb_ref[c:, cs])
                hu = _dot(lo, wub_ref[:c, cs]) + _dot(hi, wub_ref[c:, cs])
                hid = (hg * _sigmoid(hg) * hu).astype(BF16)
                yp = _dot(hid, wdb_ref[cs, :])
                y = yp if y is None else y + yp
            o_ref[rows, :] = _pack_halves(y)

        @pl.when(n_valid <= j * sub)
        def _(rows=rows):
            o_ref[rows, :] = jnp.zeros((sub, o_ref.shape[1]), o_ref.dtype)


def _expert_ffn(x, blk_exp, blk_valid, blk_order, n_used, w_gate, w_up, w_down, layer, tm, sub, name):
    r = x.shape[0]
    d, ff = w_gate.shape[2:]
    assert tm % sub == 0
    w_spec = lambda shape: pl.BlockSpec((None, None) + shape, lambda i, be, nv, od, nu: (layer, be[i], 0, 0))
    return pl.pallas_call(
        functools.partial(_ffn_kernel, sub=sub),
        grid_spec=pltpu.PrefetchScalarGridSpec(
            num_scalar_prefetch=4,
            grid=(r // tm,),
            in_specs=[pl.BlockSpec((tm, d // 2), lambda i, be, nv, od, nu: (od[jnp.minimum(i, nu[0] - 1)], 0)),
                      w_spec((d, ff)), w_spec((d, ff)), w_spec((ff, d))],
            out_specs=pl.BlockSpec((tm, d // 2), lambda i, be, nv, od, nu: (od[i], 0)),
            scratch_shapes=[pltpu.VMEM((d, ff), BF16), pltpu.VMEM((d, ff), BF16), pltpu.VMEM((ff, d), BF16)],
        ),
        out_shape=jax.ShapeDtypeStruct((r, d // 2), jnp.int32),
        compiler_params=_cparams(("arbitrary",), V7X_VMEM_LIMIT_BYTES),
        name=name,
    )(blk_exp, blk_valid, blk_order, n_used, x, w_gate, w_up, w_down)


def _first_argmax(val, idx, sentinel):
    m = jnp.max(val, axis=0, keepdims=True)
    i = jnp.min(jnp.where(val == m, idx, sentinel), axis=0, keepdims=True)
    return m, i


def _route_tile(x, w, b):
    t = x.shape[0]
    xh = x.astype(BF16)
    xl = (x - xh.astype(F32)).astype(BF16)
    wh = w.astype(BF16)
    wl = (w - wh.astype(F32)).astype(BF16)
    logits = _dot_nt(wh, xh) + (_dot_nt(wl, xh) + _dot_nt(wh, xl))
    scores = _sigmoid(logits)
    biased = scores + b
    gs = N_EXPERTS // N_GROUPS
    neg = -jnp.inf
    eid = lax.broadcasted_iota(jnp.int32, (N_EXPERTS, t), 0)
    sub = lax.broadcasted_iota(jnp.int32, (gs, t), 0)
    grow = lax.broadcasted_iota(jnp.int32, (N_GROUPS, t), 0)
    grp = jnp.zeros((N_GROUPS, t), F32)
    for g in range(N_GROUPS):
        bg = biased[g * gs:(g + 1) * gs]
        m1, i1 = _first_argmax(bg, sub, gs)
        m2 = jnp.max(jnp.where(sub == i1, neg, bg), axis=0, keepdims=True)
        grp = jnp.where(grow == g, m1 + m2, grp)
    chosen = jnp.zeros((N_GROUPS, t), F32)
    for _ in range(TOPK_GROUPS):
        _, gi = _first_argmax(grp, grow, N_GROUPS)
        hit = grow == gi
        chosen = jnp.where(hit, 1.0, chosen)
        grp = jnp.where(hit, neg, grp)
    chosen_e = jnp.concatenate([jnp.broadcast_to(chosen[g:g + 1], (gs, t)) for g in range(N_GROUPS)], axis=0)
    masked = jnp.where(chosen_e > 0.0, biased, neg)
    krow = lax.broadcasted_iota(jnp.int32, (TOP_K, t), 0)
    e_out = jnp.zeros((TOP_K, t), jnp.int32)
    g_out = jnp.zeros((TOP_K, t), F32)
    member = jnp.zeros((N_EXPERTS, t), F32)
    for k in range(TOP_K):
        _, idx = _first_argmax(masked, eid, N_EXPERTS)
        hit = eid == idx
        gk = jnp.sum(jnp.where(hit, scores, 0.0), axis=0, keepdims=True)
        masked = jnp.where(hit, neg, masked)
        member = jnp.where(hit, 1.0, member)
        e_out = jnp.where(krow == k, idx, e_out)
        g_out = jnp.where(krow == k, gk, g_out)
    g_out = g_out / jnp.sum(g_out, axis=0, keepdims=True) * ROUTED_SCALE
    return e_out, g_out, jnp.sum(member, axis=1, keepdims=True).astype(jnp.int32)


def _slots_kernel(e_ref, base_ref, o_ref):
    e = e_ref[...]
    t = e.shape[1]
    eid = lax.broadcasted_iota(jnp.int32, (N_EXPERTS, t), 0)
    member = jnp.zeros((N_EXPERTS, t), F32)
    for k in range(TOP_K):
        member = jnp.where(eid == e[k:k + 1], 1.0, member)
    before = (lax.broadcasted_iota(jnp.int32, (t, t), 0) < lax.broadcasted_iota(jnp.int32, (t, t), 1))
    rank = _dot(member.astype(BF16), jnp.where(before, 1.0, 0.0).astype(BF16))
    slot = rank + base_ref[...].astype(F32)
    krow = lax.broadcasted_iota(jnp.int32, (TOP_K, t), 0)
    out = jnp.zeros((TOP_K, t), F32)
    for k in range(TOP_K):
        sk = jnp.sum(jnp.where(eid == e[k:k + 1], slot, 0.0), axis=0, keepdims=True)
        out = jnp.where(krow == k, sk, out)
    o_ref[...] = out.astype(jnp.int32)


def _dispatch_plan(e_idx, counts, tm, tile):
    k, n = e_idx.shape
    n_blocks = -(-(n * k) // tm) + N_EXPERTS
    counts = counts.reshape(n // tile, N_EXPERTS)
    total = jnp.sum(counts, axis=0)
    padded = (total + tm - 1) // tm * tm
    pad_end = jnp.cumsum(padded)
    tile_base = (pad_end - padded)[None, :] + jnp.cumsum(counts, axis=0) - counts
    slot_of = pl.pallas_call(
        _slots_kernel,
        grid=(n // tile,),
        in_specs=[pl.BlockSpec((k, tile), lambda i: (0, i)),
                  pl.BlockSpec((None, N_EXPERTS, 1), lambda i: (i, 0, 0))],
        out_specs=pl.BlockSpec((k, tile), lambda i: (0, i)),
        out_shape=jax.ShapeDtypeStruct((k, n), jnp.int32),
        compiler_params=_cparams(("parallel",)),
        name="slots",
    )(e_idx, tile_base.astype(jnp.int32).reshape(n // tile, N_EXPERTS, 1))
    pos = jnp.arange(n_blocks, dtype=jnp.int32)
    blk_exp = jnp.minimum(jnp.sum((pad_end[None, :] <= (pos * tm)[:, None]).astype(jnp.int32), axis=1), N_EXPERTS - 1)
    is_exp = blk_exp[:, None] == jnp.arange(N_EXPERTS, dtype=blk_exp.dtype)[None, :]
    per_blk = lambda v: jnp.sum(jnp.where(is_exp, v[None, :], 0), axis=1)
    n_used = jnp.maximum(pad_end[-1:] // tm, 1).astype(jnp.int32)
    first, count = per_blk((pad_end - padded) // tm), jnp.maximum(per_blk(padded // tm), 1)
    blk_order = jnp.where(pos < n_used[0], first + (pos - first + count - 1) % count, pos)
    blk_valid = jnp.clip(per_blk(pad_end - padded + total) - blk_order * tm, 0, tm)
    return (slot_of, blk_exp.astype(jnp.int32), blk_valid.astype(jnp.int32), blk_order.astype(jnp.int32), n_used,
            n_blocks)


def _sc_mesh():
    return plsc.VectorSubcoreMesh(core_axis_name="c", subcore_axis_name="s",
                                  num_cores=V7X_SC_CORES, num_subcores=V7X_SC_SUBCORES)


def _sc_worker_id():
    return lax.axis_index("s") * V7X_SC_CORES + lax.axis_index("c")


def _sc_gather_rows(table, slot_of):
    k, n = slot_of.shape
    w = table.shape[1]
    n_rows = k * n
    ch = SC_GATHER_CHUNK
    per_worker = n_rows // (SC_WORKERS * ch)
    assert per_worker * SC_WORKERS * ch == n_rows and per_worker % 2 == 0
    idx = slot_of.reshape(SC_WORKERS, per_worker, ch)

    def body(table_hbm, idx_hbm, out_hbm, idx_v, rows_v, sems):
        wid = _sc_worker_id()
        pltpu.sync_copy(idx_hbm.at[wid], idx_v)

        def gather(c, slot):
            return pltpu.make_async_copy(table_hbm.at[idx_v.at[c]], rows_v.at[slot], sems.at[slot])

        gather(0, 0).start()

        @pl.loop(0, per_worker, step=2)
        def _(c0):
            for slot in range(2):
                c = c0 + slot
                gather(c, slot).wait()

                @pl.when(c + 1 < per_worker)
                def _():
                    gather(c + 1, 1 - slot).start()

                row0 = pl.multiple_of((wid * per_worker + c) * ch, 8)
                pltpu.sync_copy(rows_v.at[slot], out_hbm.at[pl.ds(row0, ch)])

    return pl.kernel(
        body, out_type=jax.ShapeDtypeStruct((n_rows, w), table.dtype), mesh=_sc_mesh(),
        scratch_types=[pltpu.VMEM((per_worker, ch), jnp.int32), pltpu.VMEM((2, ch, w), table.dtype),
                       pltpu.SemaphoreType.DMA((2,))],
        name="sc_gather_rows",
    )(table, idx)


def _sc_scatter_rows(x, slot_of, n_slots):
    k, n = slot_of.shape
    w = x.shape[1]
    n_chunks = n // SC_CHUNK
    assert n_chunks * SC_CHUNK == n
    idx = slot_of.reshape(k, n_chunks, SC_CHUNK).transpose(1, 0, 2)
    rounds = -(-n_chunks // SC_WORKERS)

    def body(x_hbm, idx_hbm, out_hbm, idx_v, rows_v, sem):
        wid = _sc_worker_id()

        @pl.loop(0, rounds)
        def _(j):
            q = j * SC_WORKERS + wid

            @pl.when(q < n_chunks)
            def _():
                pltpu.sync_copy(idx_hbm.at[q], idx_v)
                pltpu.sync_copy(x_hbm.at[pl.ds(pl.multiple_of(q * SC_CHUNK, SC_CHUNK), SC_CHUNK)], rows_v)
                copies = [pltpu.make_async_copy(rows_v, out_hbm.at[idx_v.at[kk]], sem) for kk in range(k)]
                for cp in copies:
                    cp.start()
                for cp in copies:
                    cp.wait()

    return pl.kernel(
        body, out_type=jax.ShapeDtypeStruct((n_slots, w), x.dtype), mesh=_sc_mesh(),
        scratch_types=[pltpu.VMEM((k, SC_CHUNK), jnp.int32), pltpu.VMEM((SC_CHUNK, w), x.dtype),
                       pltpu.SemaphoreType.DMA],
        name="sc_scatter_rows",
    )(x, idx)


def _moe_postnorm_kernel(x_ref, y_ref, gate_ref, sh_ref, g_ref, b_ref, ofp_ref, ofs_ref, ob_ref, *, alpha, tiles_p):
    def part(which):
        of_ref = (ofp_ref, ofs_ref)[which]
        gate = gate_ref[...]
        lo, hi = _unpack_halves(sh_ref[...])
        for k in range(TOP_K):
            lo_k, hi_k = _unpack_halves(y_ref[k])
            lo = lo + gate[:, k:k + 1] * lo_k
            hi = hi + gate[:, k:k + 1] * hi_k
        z = alpha * x_ref[...] + jnp.concatenate([lo, hi], axis=1)
        out = _layer_norm(z, g_ref[...], b_ref[...])
        of_ref[...] = out
        ob_ref[...] = out.astype(BF16)

    _on_part(tiles_p, part)


def _moe_postnorm(x, n_p, y_tok, gate, shared, g, b, alpha):
    m, d = x.shape
    tm = PART_TILE
    row = pl.BlockSpec((tm, d), lambda i: (i, 0))
    vec = pl.BlockSpec((1, d), lambda i: (0, 0))
    part_p, part_s = _part_specs(n_p, (tm, d))
    return pl.pallas_call(
        functools.partial(_moe_postnorm_kernel, alpha=alpha, tiles_p=n_p // tm),
        grid=(m // tm,),
        in_specs=[row, pl.BlockSpec((TOP_K, tm, d // 2), lambda i: (0, i, 0)),
                  pl.BlockSpec((tm, TOP_K), lambda i: (i, 0)), pl.BlockSpec((tm, d // 2), lambda i: (i, 0)), vec, vec],
        out_specs=[part_p, part_s, row],
        out_shape=[jax.ShapeDtypeStruct((n_p, d), F32), jax.ShapeDtypeStruct((m - n_p, d), F32),
                   jax.ShapeDtypeStruct((m, d), BF16)],
        compiler_params=_cparams(("arbitrary",), V7X_VMEM_LIMIT_BYTES),
        name="postnorm_moe",
    )(x, y_tok, gate, shared, g.reshape(1, d), b.reshape(1, d))


def _moe(xf, n_p, xp, routing, layer, w_gate, w_up, w_down, sw_gate, sw_up, sw_down, ln_g, ln_b, alpha):
    n, d = xf.shape
    e_idx, gate, counts = routing
    counts = jnp.sum(counts.reshape(n // ROW_TILE, ROW_TILE // PART_TILE, N_EXPERTS), axis=1)
    slot_of, blk_exp, blk_valid, blk_order, n_used, n_blocks = _dispatch_plan(e_idx, counts, MOE_BLOCK, ROW_TILE)
    x_sorted = _sc_scatter_rows(xp, slot_of, n_blocks * MOE_BLOCK)
    y_sorted = _expert_ffn(x_sorted, blk_exp, blk_valid, blk_order, n_used, w_gate, w_up, w_down, layer,
                           MOE_BLOCK, MOE_SUB, "routed_ffn")
    y_tok = _sc_gather_rows(y_sorted, slot_of).reshape(TOP_K, n, d // 2)
    n_sh = n // ROW_TILE
    shared = _expert_ffn(xp, jnp.zeros((n_sh,), jnp.int32), jnp.full((n_sh,), ROW_TILE, jnp.int32),
                         jnp.arange(n_sh, dtype=jnp.int32), jnp.full((1,), n_sh, jnp.int32),
                         sw_gate[:, None], sw_up[:, None], sw_down[:, None], layer, ROW_TILE, ROW_TILE, "shared_ffn")
    return _moe_postnorm(xf, n_p, y_tok, gate.T, shared, ln_g, ln_b, alpha)


def kernel(x_prompt, x_sample, state_hgrn, cache_win_k, cache_win_v, cache_mem_k, cache_mem_v, mem_prompt,
           w_in_a, lb_logits, g_norm_a, w_out_a, w_in_b, w_out_b, w_kv_shared, w_mem_kv, ln_g, ln_b,
           router_w, router_b, exp_w_gate, exp_w_up, exp_w_down, sh_w_gate, sh_w_up, sh_w_down):
    bp, sp, d = x_prompt.shape
    ns = x_sample.shape[0]
    assert x_sample.shape[1] == 1
    depth = ln_g.shape[0]
    n_a = w_in_a.shape[0]
    alpha = (2 * depth) ** 0.25
    n_p = bp * sp
    mem_len = mem_prompt.shape[1]
    a_mix = 4 * A_HEADS * HEAD_DIM

    xf_p, xf_s = x_prompt.reshape(n_p, d), x_sample.reshape(ns, d)
    xb = jnp.concatenate([xf_p, xf_s], axis=0).astype(BF16)
    lower_bounds = jnp.cumsum(jax.nn.softmax(lb_logits.astype(F32), axis=0), axis=0)
    mem_flat = mem_prompt.reshape(bp * mem_len, d)
    pos_all = jnp.concatenate([jnp.tile(jnp.arange(sp, dtype=jnp.int32), bp),
                               jnp.full((ns,), PAST_LEN, jnp.int32)])
    tables = _rope_tables(pos_all)

    hgrn_p, hgrn_s, mem_k_p, mem_v_p = [], [], [], []
    for layer in range(depth):
        kvm = _proj([mem_flat], w_mem_kv, layer, F32, 2 * mem_len, 1024, "mem_kv")
        mem_k_p.append(kvm[:, :MEM_W].reshape(bp, mem_len, MEM_HEADS, HEAD_DIM))
        mem_v_p.append(kvm[:, MEM_W:].reshape(bp, mem_len, MEM_HEADS, HEAD_DIM))
        if layer < n_a:
            a = layer
            proj = _proj([xb], w_in_a, a, F32, ROW_TILE, 1664, "proj_in_a")
            proj_s = proj[n_p:]
            o_x_s, st_s = _hgrn_step(proj_s.reshape(ns, 1, -1), state_hgrn, a, lower_bounds[a], g_norm_a[a])
            o_m_s = _mem_attn_decode(proj_s[:, a_mix:].reshape(ns, MEM_HEADS, HEAD_DIM), cache_mem_k, cache_mem_v, layer)
            o_x, st_p = _hgrn_prompt(proj, lower_bounds[a], g_norm_a[a], o_x_s.reshape(ns, -1), bp, sp)
            o_m = _mem_attn_prompt(proj, a_mix // MEM_W, kvm, o_m_s.reshape(ns, -1), bp, sp, 512)
            hgrn_p.append(st_p)
            hgrn_s.append(st_s)
            w_out, w_out_layer = w_out_a, a
        else:
            bl = layer - n_a
            if layer == n_a:
                kv = _proj([xb], w_kv_shared[None], 0, F32, ROW_TILE, 1024, "proj_kv")
                k_r, k_p, k_s, v_p, v_s = _shared_kv(kv, tables, n_p)
            proj = _proj([xb], w_in_b, bl, F32, ROW_TILE, 1024, "proj_in_b", tables, B_QHEADS)
            q_s = proj[n_p:, :B_QHEADS * HEAD_DIM].reshape(ns, B_GROUPS, B_SLOTS, HEAD_DIM)
            o_x_s = _dil_attn_decode(q_s, k_s, v_s, cache_win_k, cache_win_v)
            o_m_s = _mem_attn_decode(proj[n_p:, B_QHEADS * HEAD_DIM:].reshape(ns, MEM_HEADS, HEAD_DIM),
                                     cache_mem_k, cache_mem_v, layer)
            o_x = _dil_attn_prompt(proj, k_r, kv, o_x_s.reshape(ns, -1), bp, sp)
            o_m = _mem_attn_prompt(proj, B_QHEADS * HEAD_DIM // MEM_W, kvm, o_m_s.reshape(ns, -1), bp, sp, 512)
            w_out, w_out_layer = w_out_b, bl
        y = _proj([o_x, o_m], w_out, w_out_layer, F32, ROW_TILE, 1024, "proj_out")
        xf, xp, *routing = _postnorm_route(xf_p, xf_s, y, ln_g[layer, 0], ln_b[layer, 0], alpha,
                                           router_w, layer, router_b, "postnorm_route")
        xf_p, xf_s, xb = _moe(xf, n_p, xp, routing, layer, exp_w_gate, exp_w_up, exp_w_down,
                              sh_w_gate, sh_w_up, sh_w_down, ln_g[layer, 1], ln_b[layer, 1], alpha)

    w_p = min(max(w for w, _ in B_PATTERNS), sp)
    k_p = k_p.reshape(bp, sp, B_SLOTS, HEAD_DIM)
    v_p = v_p.reshape(bp, sp, B_SLOTS, HEAD_DIM)
    return (xf_p.reshape(bp, sp, d), xf_s.reshape(ns, 1, d),
            jnp.stack(hgrn_p), jnp.stack(hgrn_s),
            k_p[:, sp - w_p:], v_p[:, sp - w_p:],
            k_s.reshape(ns, 1, B_SLOTS, HEAD_DIM), v_s.reshape(ns, 1, B_SLOTS, HEAD_DIM),
            jnp.stack(mem_k_p), jnp.stack(mem_v_p))
```
